```python
import jax, jax.numpy as jnp
from jax import lax
import numpy as np

D_MODEL = 4096
BATCH = 2
SEQ = 8192
DEPTH = 2

HEAD_DIM = 128
N_BRANCH = 4
MIX_WIDTH = D_MODEL // N_BRANCH
N_HEADS = MIX_WIDTH // HEAD_DIM
CONV_WIDTH = 4
MLSTM_CHUNK = 128
RET_CHUNK = 128
NSA_KV_GROUPS = 2
NSA_GROUP_SIZE = N_HEADS // NSA_KV_GROUPS
KV_WIDTH = NSA_KV_GROUPS * HEAD_DIM
CMP_BLOCK = 32
CMP_STRIDE = 16
SEL_BLOCK = 64
SEL_TOPN = 16
WINDOW = 512
Q_BLOCK = 128
SGU_CHUNK = 128
N_GROUPS = 8
EXPERTS_PER_GROUP = 8
N_EXPERTS = N_GROUPS * EXPERTS_PER_GROUP
TOP_K_IN_GROUP = 2
EXPERT_FF = 256
MOE_BLOCK = 128
RMS_EPS = 1e-6
NEG_INF = -1e30
FORCE_SCORE = 1e9

COL_WIDTHS = (
    MIX_WIDTH, MIX_WIDTH, MIX_WIDTH, MIX_WIDTH, N_HEADS, N_HEADS,
    MIX_WIDTH, MIX_WIDTH, MIX_WIDTH, MIX_WIDTH,
    MIX_WIDTH, KV_WIDTH, KV_WIDTH, KV_WIDTH, KV_WIDTH, KV_WIDTH, KV_WIDTH, 3 * N_HEADS,
    MIX_WIDTH, MIX_WIDTH,
    N_BRANCH * D_MODEL,
)
C_IN = sum(COL_WIDTHS)

kernel_name = "hybrid_mlstm_retnet_nsa_sgu_hiermoe"


def rmsnorm(x, g):
    xf = x.astype(jnp.float32)
    y = xf * lax.rsqrt(jnp.mean(xf * xf, axis=-1, keepdims=True) + RMS_EPS)
    return (y * g.astype(jnp.float32)).astype(x.dtype)


def head_norm(h, g, dtype):
    B, T, H, d = h.shape
    hf = h.astype(jnp.float32)
    hf = hf * lax.rsqrt(jnp.mean(hf * hf, axis=-1, keepdims=True) + RMS_EPS)
    return (hf.reshape(B, T, H * d) * g.astype(jnp.float32)).astype(dtype)


def alibi_slopes(n):
    return 2.0 ** (-8.0 * (jnp.arange(n, dtype=jnp.float32) + 1.0) / n)


def split_columns(z):
    idx, off = [], 0
    for w in COL_WIDTHS[:-1]:
        off += w
        idx.append(off)
    return jnp.split(z, idx, axis=-1)


def causal_conv(x, w):
    K, C = w.shape
    return lax.conv_general_dilated(x, w[:, None, :].astype(x.dtype), window_strides=(1,),
                                    padding=[(K - 1, 0)], dimension_numbers=('NWC', 'WIO', 'NWC'),
                                    feature_group_count=C)


def mlstm(q, k, v, i_pre, f_pre):
    B, T, _ = q.shape
    H, d, L = N_HEADS, HEAD_DIM, MLSTM_CHUNK
    nc = T // L
    f32 = jnp.float32

    def heads(a):
        return a.astype(f32).reshape(B, nc, L, H, d).transpose(1, 0, 3, 2, 4)

    def gate(a):
        return a.astype(f32).reshape(B, nc, L, H).transpose(1, 0, 3, 2)

    qc, kc, vc = heads(q) * (d ** -0.5), heads(k), heads(v)
    ic, fc = gate(i_pre), jax.nn.log_sigmoid(gate(f_pre))
    causal = jnp.tril(jnp.ones((L, L), dtype=bool))

    def step(carry, inp):
        c_mem, n_mem, m_mem = carry
        qb, kb, vb, ib, fb = inp
        bcum = jnp.cumsum(fb, axis=-1)
        log_d = bcum[..., :, None] - bcum[..., None, :] + ib[..., None, :]
        log_d = jnp.where(causal, log_d, NEG_INF)
        log_inter = bcum + m_mem[..., None]
        m_row = jnp.maximum(jnp.max(log_d, axis=-1), log_inter)
        s = jnp.einsum('bhld,bhsd->bhls', qb, kb) * jnp.exp(log_d - m_row[..., None])
        w_inter = jnp.exp(log_inter - m_row)
        num = (jnp.einsum('bhls,bhsd->bhld', s, vb)
               + w_inter[..., None] * jnp.einsum('bhvk,bhlk->bhlv', c_mem, qb))
        den = jnp.sum(s, axis=-1) + w_inter * jnp.einsum('bhk,bhlk->bhl', n_mem, qb)
        h = num / jnp.maximum(jnp.abs(den), jnp.exp(-m_row))[..., None]
        b_last = bcum[..., -1]
        log_w = b_last[..., None] - bcum + ib
        m_new = jnp.maximum(b_last + m_mem, jnp.max(log_w, axis=-1))
        w_upd = jnp.exp(log_w - m_new[..., None])
        carry_decay = jnp.exp(b_last + m_mem - m_new)
        c_new = carry_decay[..., None, None] * c_mem + jnp.einsum('bhl,bhlv,bhlk->bhvk', w_upd, vb, kb)
        n_new = carry_decay[..., None] * n_mem + jnp.einsum('bhl,bhlk->bhk', w_upd, kb)
        return (c_new, n_new, m_new), h

    init = (jnp.zeros((B, H, d, d), f32), jnp.zeros((B, H, d), f32), jnp.zeros((B, H), f32))
    _, h = lax.scan(step, init, (qc, kc, vc, ic, fc))
    return h.transpose(1, 0, 3, 2, 4).reshape(B, T, H, d)


def retention(q, k, v):
    B, T, _ = q.shape
    H, d, L = N_HEADS, HEAD_DIM, RET_CHUNK
    nc = T // L
    f32 = jnp.float32

    def heads(a):
        return a.astype(f32).reshape(B, nc, L, H, d).transpose(1, 0, 3, 2, 4)

    qc, kc, vc = heads(q), heads(k) * (d ** -0.5), heads(v)
    log_gamma = jnp.log(1.0 - 2.0 ** (-5.0 - jnp.arange(H, dtype=f32)))
    pos = jnp.arange(L, dtype=f32)
    diff = pos[:, None] - pos[None, :]
    decay = jnp.where(diff >= 0, jnp.exp(log_gamma[:, None, None] * jnp.maximum(diff, 0.0)), 0.0)
    q_decay = jnp.exp(log_gamma[:, None] * (pos + 1.0))[:, :, None]
    k_decay = jnp.exp(log_gamma[:, None] * (L - 1.0 - pos))[:, :, None]
    chunk_decay = jnp.exp(log_gamma * L)[:, None, None]

    def step(r_state, inp):
        qb, kb, vb = inp
        s = jnp.einsum('bhld,bhsd->bhls', qb, kb) * decay
        o = jnp.einsum('bhls,bhsv->bhlv', s, vb) + jnp.einsum('bhld,bhdv->bhlv', qb * q_decay, r_state)
        r_state = chunk_decay * r_state + jnp.einsum('bhsd,bhsv->bhdv', kb * k_decay, vb)
        return r_state, o

    _, o = lax.scan(step, jnp.zeros((B, H, d, d), f32), (qc, kc, vc))
    return o.transpose(1, 0, 3, 2, 4).reshape(B, T, H, d)


def nsa_attention(q, k_c, v_c, k_s, v_s, k_w, v_w, gates, cmp_w, cmp_pe):
    B, T, _ = q.shape
    f32 = jnp.float32
    G, J, d = NSA_KV_GROUPS, NSA_GROUP_SIZE, HEAD_DIM
    qh = q.astype(f32).reshape(B, T, G, J, d).transpose(0, 2, 3, 1, 4) * (d ** -0.5)

    def kv_heads(a):
        return a.astype(f32).reshape(B, T, G, d).transpose(0, 2, 1, 3)

    kc, vc, ks, vs, kw, vw = (kv_heads(a) for a in (k_c, v_c, k_s, v_s, k_w, v_w))
    slopes = alibi_slopes(N_HEADS).reshape(G, J)[None, :, :, None, None]
    t_pos = jnp.arange(T)
    nq = T // Q_BLOCK
    tq_b = t_pos.reshape(nq, Q_BLOCK)

    n_cmp = T // CMP_STRIDE - 1

    def compress(a, w, pe):
        ar = a.reshape(B, G, T // CMP_STRIDE, CMP_STRIDE, d)
        blocks = jnp.concatenate([ar[:, :, :-1], ar[:, :, 1:]], axis=3)
        return jnp.einsum('bgnld,lde->bgne', blocks + pe.astype(f32), w.astype(f32))

    k_cmp = compress(kc, cmp_w[0], cmp_pe[0])
    v_cmp = compress(vc, cmp_w[1], cmp_pe[1])
    cmp_end = jnp.arange(n_cmp) * CMP_STRIDE + CMP_BLOCK - 1
    dist_c = (t_pos[:, None] - cmp_end[None, :]).astype(f32)
    valid_c = dist_c >= 0
    s_c = jnp.einsum('bgjtd,bgnd->bgjtn', qh, k_cmp) - slopes * dist_c
    s_c = jnp.where(valid_c, s_c, NEG_INF)
    has_c = jnp.any(valid_c, axis=-1).astype(f32)[:, None]
    p_c = jax.nn.softmax(s_c, axis=-1) * has_c
    o_cmp = jnp.einsum('bgjtn,bgnd->bgjtd', p_c, v_cmp)

    n_sel = T // SEL_BLOCK
    topn = min(SEL_TOPN, n_sel)
    c_start = jnp.arange(n_cmp) * CMP_STRIDE
    s_start = jnp.arange(n_sel) * SEL_BLOCK
    overlap = ((c_start[:, None] < s_start[None, :] + SEL_BLOCK)
               & (c_start[:, None] + CMP_BLOCK > s_start[None, :])).astype(f32)
    importance = jnp.einsum('bgjtn,nm->bgtm', p_c, overlap)
    cur = t_pos // SEL_BLOCK
    blk = jnp.arange(n_sel)
    valid_s = blk[None, :] <= cur[:, None]
    forced = (blk[None, :] == 0) | (blk[None, :] == cur[:, None]) | (blk[None, :] == cur[:, None] - 1)
    score = jnp.where(valid_s, importance + jnp.where(forced, FORCE_SCORE, 0.0), NEG_INF)
    top_s, top_i = lax.top_k(score, topn)
    top_valid = top_s > 0.5 * NEG_INF

    ks_blk = ks.reshape(B, G, n_sel, SEL_BLOCK, d)
    vs_blk = vs.reshape(B, G, n_sel, SEL_BLOCK, d)
    q_b = qh.reshape(B, G, J, nq, Q_BLOCK, d).transpose(3, 0, 1, 2, 4, 5)
    i_b = top_i.reshape(B, G, nq, Q_BLOCK, topn).transpose(2, 0, 1, 3, 4)
    v_b = top_valid.reshape(B, G, nq, Q_BLOCK, topn).transpose(2, 0, 1, 3, 4)
    gather = jax.vmap(jax.vmap(lambda blocks, ix: blocks[ix]))
    within = jnp.arange(SEL_BLOCK)

    def sel_step(args):
        qb, ib, vb, tq = args
        kg = gather(ks_blk, ib)
        vg = gather(vs_blk, ib)
        kpos = ib[..., None] * SEL_BLOCK + within
        dist = (tq[None, None, :, None, None] - kpos).astype(f32)
        mask = vb[..., None] & (dist >= 0)
        s = jnp.einsum('bgjqd,bgqnsd->bgjqns', qb, kg) - slopes[..., None] * dist[:, :, None]
        s = jnp.where(mask[:, :, None], s, NEG_INF).reshape(B, G, J, Q_BLOCK, topn * SEL_BLOCK)
        p = jax.nn.softmax(s, axis=-1).reshape(B, G, J, Q_BLOCK, topn, SEL_BLOCK)
        return jnp.einsum('bgjqns,bgqnsd->bgjqd', p, vg)

    o_sel = lax.map(sel_step, (q_b, i_b, v_b, tq_b))
    o_sel = o_sel.transpose(1, 2, 3, 0, 4, 5).reshape(B, G, J, T, d)

    n_back = WINDOW // Q_BLOCK
    span = (n_back + 1) * Q_BLOCK

    def band(a):
        ap = jnp.pad(a, ((0, 0), (0, 0), (WINDOW, 0), (0, 0))).reshape(B, G, nq + n_back, Q_BLOCK, d)
        return jnp.concatenate([ap[:, :, i:i + nq] for i in range(n_back + 1)], axis=3)

    kwb, vwb = band(kw), band(vw)
    qw = qh.reshape(B, G, J, nq, Q_BLOCK, d)
    kpos_w = jnp.arange(nq)[:, None] * Q_BLOCK - WINDOW + jnp.arange(span)[None, :]
    dist_w = (tq_b[:, :, None] - kpos_w[:, None, :]).astype(f32)
    mask_w = (dist_w >= 0) & (dist_w < WINDOW) & (kpos_w[:, None, :] >= 0)
    s_w = jnp.einsum('bgjnqd,bgnkd->bgjnqk', qw, kwb) - slopes[..., None] * dist_w
    s_w = jnp.where(mask_w, s_w, NEG_INF)
    p_w = jax.nn.softmax(s_w, axis=-1)
    o_win = jnp.einsum('bgjnqk,bgnkd->bgjnqd', p_w, vwb).reshape(B, G, J, T, d)

    g = jax.nn.sigmoid(gates.astype(f32)).reshape(B, T, 3, G, J).transpose(2, 0, 3, 4, 1)[..., None]
    o = g[0] * o_cmp + g[1] * o_sel + g[2] * o_win
    return o.transpose(0, 3, 1, 2, 4).reshape(B, T, N_HEADS * d).astype(q.dtype)


def spatial_gating(u, v, norm_g, w_s, b_s):
    B, T, W = u.shape
    L, gr = SGU_CHUNK, N_HEADS
    nc = T // L
    u = jax.nn.gelu(u)
    v = rmsnorm(jax.nn.gelu(v), norm_g).reshape(B, nc, L, gr, W // gr)
    w_m = w_s * jnp.tril(jnp.ones((L, L), dtype=w_s.dtype))
    mixed = jnp.einsum('gts,bcsge->bctge', w_m, v) + b_s.T[:, :, None]
    return u * mixed.reshape(B, T, W)


def mixing_block(h, w_in, conv_w, gate_bias, mlstm_g, ret_g, cmp_w, cmp_pe,
                 sgu_g, sgu_w, sgu_b, w_branch, w_out):
    z = h @ w_in
    (aq, ak, av, ao, ai, af, bq, bk, bv, bg, cq, ckc, cvc, cks, cvs, ckw, cvw, cg,
     du, dv, merge_logits) = split_columns(z)
    qk = jax.nn.silu(causal_conv(jnp.concatenate([aq, ak], axis=-1), conv_w))
    aq2, ak2 = jnp.split(qk, 2, axis=-1)
    ha = mlstm(aq2, ak2, av, ai + gate_bias[0], af + gate_bias[1])
    ya = head_norm(ha, mlstm_g, h.dtype) * jax.nn.sigmoid(ao)
    hb = retention(bq, bk, bv)
    yb = head_norm(hb, ret_g, h.dtype) * jax.nn.silu(bg)
    yc = nsa_attention(cq, ckc, cvc, cks, cvs, ckw, cvw, cg, cmp_w, cmp_pe)
    yd = spatial_gating(du, dv, sgu_g, sgu_w, sgu_b)
    merged = None
    for n, y in enumerate((ya, yb, yc, yd)):
        term = jax.nn.sigmoid(merge_logits[..., n * D_MODEL:(n + 1) * D_MODEL]) * (y @ w_branch[n])
        merged = term if merged is None else merged + term
    return merged @ w_out


def hier_moe(h, wg, bg, we, be, w1, w3, w2):
    B, T, D = h.shape
    N = B * T
    K = TOP_K_IN_GROUP
    f32 = jnp.float32
    hf = h.reshape(N, D)
    grp_logits = (hf @ wg).astype(f32) + bg.astype(f32)
    grp_prob = jax.nn.softmax(grp_logits, axis=-1)
    g_idx = jnp.argmax(grp_logits, axis=-1).astype(jnp.int32)
    g_w = jnp.take_along_axis(grp_prob, g_idx[:, None], axis=-1)
    exp_logits = ((hf @ we).astype(f32) + be.astype(f32)).reshape(N, N_GROUPS, EXPERTS_PER_GROUP)
    in_grp = jnp.take_along_axis(exp_logits, g_idx[:, None, None], axis=1)[:, 0]
    top_v, top_i = lax.top_k(in_grp, K)
    comb = (jax.nn.softmax(top_v, axis=-1) * g_w).astype(h.dtype)
    e_idx = g_idx[:, None] * EXPERTS_PER_GROUP + top_i.astype(jnp.int32)

    nk = N * K
    flat_e = e_idx.reshape(-1)
    flat_tok = jnp.repeat(jnp.arange(N, dtype=jnp.int32), K)
    flat_w = comb.reshape(-1)
    order = jnp.argsort(flat_e)
    se, stok, sw = flat_e[order], flat_tok[order], flat_w[order]
    counts = jnp.bincount(flat_e, length=N_EXPERTS)
    padded = (counts + MOE_BLOCK - 1) // MOE_BLOCK * MOE_BLOCK
    pad_end = jnp.cumsum(padded)
    pad_start = pad_end - padded
    start = jnp.cumsum(counts) - counts
    dest = pad_start[se] + jnp.arange(nk, dtype=jnp.int32) - start[se]
    n_blk = (nk + MOE_BLOCK - 1) // MOE_BLOCK + N_EXPERTS
    P = n_blk * MOE_BLOCK
    tok_pad = jnp.full((P,), N, dtype=jnp.int32).at[dest].set(stok)
    w_pad = jnp.zeros((P,), dtype=h.dtype).at[dest].set(sw)
    blk_exp = jnp.minimum(jnp.searchsorted(pad_end, jnp.arange(n_blk) * MOE_BLOCK, side='right'),
                          N_EXPERTS - 1)
    h_ext = jnp.concatenate([hf, jnp.zeros((1, D), hf.dtype)], axis=0)

    def expert_block(args):
        tok, wt, e = args
        xb = h_ext[tok]
        a = jax.nn.silu(xb @ w1[e]) * (xb @ w3[e])
        return (a @ w2[e]) * wt[:, None]

    y = lax.map(expert_block, (tok_pad.reshape(n_blk, MOE_BLOCK), w_pad.reshape(n_blk, MOE_BLOCK), blk_exp))
    out = jax.ops.segment_sum(y.reshape(P, D), tok_pad, num_segments=N + 1)[:N]
    return out.reshape(B, T, D)


def setup_inputs(seed: int = 0) -> dict:
    key = jax.random.key(seed)
    ks = jax.random.split(key, 24)
    f32 = jnp.float32

    def nrm(k, shape, scale):
        return jax.random.normal(k, shape, f32) * scale

    def gain(k, shape):
        return 1.0 + 0.02 * jax.random.normal(k, shape, f32)

    f_bias = jnp.linspace(3.0, 6.0, N_HEADS, dtype=f32)
    return {
        "x": nrm(ks[0], (BATCH, SEQ, D_MODEL), 1.0),
        "w_in": nrm(ks[1], (DEPTH, D_MODEL, C_IN), D_MODEL ** -0.5),
        "mlstm_conv": nrm(ks[2], (DEPTH, CONV_WIDTH, 2 * MIX_WIDTH), CONV_WIDTH ** -0.5),
        "mlstm_gate_bias": jnp.stack([nrm(ks[3], (DEPTH, N_HEADS), 0.1),
                                      f_bias + nrm(ks[4], (DEPTH, N_HEADS), 0.1)], axis=1),
        "mlstm_norm": gain(ks[5], (DEPTH, MIX_WIDTH)),
        "ret_norm": gain(ks[6], (DEPTH, MIX_WIDTH)),
        "nsa_cmp_w": nrm(ks[7], (DEPTH, 2, CMP_BLOCK, HEAD_DIM, HEAD_DIM), (CMP_BLOCK * HEAD_DIM) ** -0.5),
        "nsa_cmp_pe": nrm(ks[8], (DEPTH, 2, CMP_BLOCK, HEAD_DIM), 0.1),
        "sgu_norm": gain(ks[9], (DEPTH, MIX_WIDTH)),
        "sgu_w": nrm(ks[10], (DEPTH, N_HEADS, SGU_CHUNK, SGU_CHUNK), SGU_CHUNK ** -0.5),
        "sgu_b": 1.0 + nrm(ks[11], (DEPTH, N_HEADS, SGU_CHUNK), 0.1),
        "w_branch": nrm(ks[12], (DEPTH, N_BRANCH, MIX_WIDTH, D_MODEL), MIX_WIDTH ** -0.5),
        "w_out": nrm(ks[13], (DEPTH, D_MODEL, D_MODEL), D_MODEL ** -0.5),
        "norm_mix": gain(ks[14], (DEPTH, D_MODEL)),
        "norm_ffn": gain(ks[15], (DEPTH, D_MODEL)),
        "router_group_w": nrm(ks[16], (DEPTH, D_MODEL, N_GROUPS), D_MODEL ** -0.5),
        "router_group_b": nrm(ks[17], (DEPTH, N_GROUPS), 0.01),
        "router_expert_w": nrm(ks[18], (DEPTH, D_MODEL, N_EXPERTS), D_MODEL ** -0.5),
        "router_expert_b": nrm(ks[19], (DEPTH, N_EXPERTS), 0.01),
        "expert_w1": nrm(ks[20], (DEPTH, N_EXPERTS, D_MODEL, EXPERT_FF), D_MODEL ** -0.5),
        "expert_w3": nrm(ks[21], (DEPTH, N_EXPERTS, D_MODEL, EXPERT_FF), D_MODEL ** -0.5),
        "expert_w2": nrm(ks[22], (DEPTH, N_EXPERTS, EXPERT_FF, D_MODEL), EXPERT_FF ** -0.5),
        "norm_final": gain(ks[23], (D_MODEL,)),
    }


def reference(x, w_in, mlstm_conv, mlstm_gate_bias, mlstm_norm, ret_norm, nsa_cmp_w, nsa_cmp_pe,
              sgu_norm, sgu_w, sgu_b, w_branch, w_out, norm_mix, norm_ffn, router_group_w,
              router_group_b, router_expert_w, router_expert_b, expert_w1, expert_w3, expert_w2,
              norm_final):
    for l in range(DEPTH):
        h = rmsnorm(x, norm_mix[l])
        x = x + mixing_block(h, w_in[l], mlstm_conv[l], mlstm_gate_bias[l], mlstm_norm[l], ret_norm[l],
                             nsa_cmp_w[l], nsa_cmp_pe[l], sgu_norm[l], sgu_w[l], sgu_b[l],
                             w_branch[l], w_out[l])
        h = rmsnorm(x, norm_ffn[l])
        x = x + hier_moe(h, router_group_w[l], router_group_b[l], router_expert_w[l], router_expert_b[l],
                         expert_w1[l], expert_w3[l], expert_w2[l])
    return rmsnorm(x, norm_final)
```

```python
import functools

import numpy as np
import jax
import jax.numpy as jnp
from jax import lax
from jax.experimental import pallas as pl
from jax.experimental.pallas import tpu as pltpu

F32 = jnp.float32
BF16 = jnp.bfloat16

D_MODEL = 4096
DEPTH = 2
HEAD_DIM = 128
N_BRANCH = 4
MIX_WIDTH = D_MODEL // N_BRANCH
N_HEADS = MIX_WIDTH // HEAD_DIM
CONV_WIDTH = 4
MLSTM_CHUNK = 128
RET_CHUNK = 128
NSA_KV_GROUPS = 2
NSA_GROUP_SIZE = N_HEADS // NSA_KV_GROUPS
KV_WIDTH = NSA_KV_GROUPS * HEAD_DIM
CMP_BLOCK = 32
CMP_STRIDE = 16
SEL_BLOCK = 64
SEL_TOPN = 16
WINDOW = 512
Q_BLOCK = 128
SGU_CHUNK = 128
N_GROUPS = 8
EXPERTS_PER_GROUP = 8
N_EXPERTS = N_GROUPS * EXPERTS_PER_GROUP
TOP_K_IN_GROUP = 2
EXPERT_FF = 256
MOE_BLOCK = 128
RMS_EPS = 1e-6
NEG_INF = -1e30
FORCE_SCORE = 1e9

COL_WIDTHS = (
    MIX_WIDTH, MIX_WIDTH, MIX_WIDTH, MIX_WIDTH, N_HEADS, N_HEADS,
    MIX_WIDTH, MIX_WIDTH, MIX_WIDTH, MIX_WIDTH,
    MIX_WIDTH, KV_WIDTH, KV_WIDTH, KV_WIDTH, KV_WIDTH, KV_WIDTH, KV_WIDTH, 3 * N_HEADS,
    MIX_WIDTH, MIX_WIDTH,
    N_BRANCH * D_MODEL,
)
COL_OFFSETS = tuple(int(v) for v in np.concatenate([[0], np.cumsum(COL_WIDTHS)]))
(I_AQ, I_AK, I_AV, I_AO, I_AI, I_AF, I_BQ, I_BK, I_BV, I_BG, I_CQ, I_CKC, I_CVC, I_CKS, I_CVS,
 I_CKW, I_CVW, I_CG, I_DU, I_DV, I_MERGE) = range(21)

MAIN_ORDER = (I_AQ, I_AK, I_AV, I_AO, I_BQ, I_BK, I_BV, I_BG, I_CQ, I_DU, I_DV,
              I_CKC, I_CVC, I_CKS, I_CVS, I_CKW, I_CVW)
MAIN_WIDTH = sum(COL_WIDTHS[i] for i in MAIN_ORDER)
LANES = 128
BLK_AQ, BLK_AK, BLK_AV, BLK_AO, BLK_BQ, BLK_BK, BLK_BV, BLK_BG, BLK_CQ, BLK_DU, BLK_DV = range(11)
KV_BASE = 11 * MIX_WIDTH // HEAD_DIM
VMEM_LIMIT = 48 * 1024 * 1024


def _cparams(sem, vmem=VMEM_LIMIT):
    return pltpu.CompilerParams(dimension_semantics=sem, vmem_limit_bytes=vmem)


def _dot(a, b):
    return jnp.dot(a, b, preferred_element_type=F32)


def _dot_nt(a, b):
    return lax.dot_general(a, b, (((1,), (1,)), ((), ())), preferred_element_type=F32)


def _split3(x):
    hi = x.astype(BF16)
    r1 = x - hi.astype(F32)
    mid = r1.astype(BF16)
    lo = (r1 - mid.astype(F32)).astype(BF16)
    return hi, mid, lo


def _dot_exact_rhs(x, m_bf16):
    hi, mid, lo = _split3(x)
    return _dot(hi, m_bf16) + _dot(mid, m_bf16) + _dot(lo, m_bf16)


def _log_sigmoid(x):
    return jnp.minimum(x, 0.0) - jnp.log1p(jnp.exp(-jnp.abs(x)))


def _norm_kernel(*refs, n_add, emit_sum, router, final):
    adds = refs[:n_add]
    g_ref = refs[n_add]
    pos = n_add + 1
    if router:
        whi_ref, wlo_ref, rb_ref = refs[pos:pos + 3]
        pos += 3
    outs = refs[pos:]
    x = adds[0][...]
    if n_add == 2:
        x = x + adds[1][...]
    elif n_add == 3:
        x = x + (adds[1][...] + adds[2][...])
    y = x * lax.rsqrt(jnp.mean(x * x, axis=-1, keepdims=True) + RMS_EPS) * g_ref[...]
    o = 0
    if emit_sum:
        outs[o][...] = x
        o += 1
    if final:
        outs[o][...] = y
    else:
        outs[o][...] = y.astype(BF16)
    o += 1
    if router:
        y_hi = y.astype(BF16)
        y_lo = (y - y_hi.astype(F32)).astype(BF16)
        lg = _dot(y_hi, whi_ref[...]) + (_dot(y_hi, wlo_ref[...]) + _dot(y_lo, whi_ref[...]))
        outs[o][...] = lg + rb_ref[...]


def _norm(adds, gain, *, router_w=None, final=False, rows=128):
    n, d = adds[0].shape
    rows = min(rows, n)
    n_add = len(adds)
    emit_sum = n_add > 1
    row_spec = pl.BlockSpec((rows, d), lambda i: (i, 0))
    in_specs = [row_spec] * n_add + [pl.BlockSpec((1, d), lambda i: (0, 0))]
    args = list(adds) + [gain.reshape(1, d)]
    out_shape, out_specs = [], []
    if emit_sum:
        out_shape.append(jax.ShapeDtypeStruct((n, d), F32))
        out_specs.append(row_spec)
    out_shape.append(jax.ShapeDtypeStruct((n, d), F32 if final else BF16))
    out_specs.append(row_spec)
    if router_w is not None:
        whi, wlo, rb = router_w
        in_specs += [pl.BlockSpec((d, LANES), lambda i: (0, 0))] * 2 + [pl.BlockSpec((1, LANES), lambda i: (0, 0))]
        args += [whi, wlo, rb]
        out_shape.append(jax.ShapeDtypeStruct((n, LANES), F32))
        out_specs.append(pl.BlockSpec((rows, LANES), lambda i: (i, 0)))
    return pl.pallas_call(
        functools.partial(_norm_kernel, n_add=n_add, emit_sum=emit_sum, router=router_w is not None, final=final),
        grid=(n // rows,), in_specs=in_specs, out_specs=out_specs, out_shape=out_shape,
        compiler_params=_cparams(("parallel",)), name="rmsnorm",
    )(*args)


def _mm_kernel(a_ref, w_ref, *rest, has_res):
    acc = _dot(a_ref[...], w_ref[...])
    if has_res:
        acc = acc + rest[0][...]
    rest[-1][...] = acc.astype(rest[-1].dtype)


def _matmul(a, w, *, out_dtype, residual=None, tm=1024, tn=512, name="matmul"):
    m, k = a.shape
    nc = w.shape[1]
    tm, tn = min(tm, m), min(tn, nc)
    in_specs = [pl.BlockSpec((tm, k), lambda i, j: (i, 0)), pl.BlockSpec((k, tn), lambda i, j: (0, j))]
    args = [a, w]
    if residual is not None:
        in_specs.append(pl.BlockSpec((tm, tn), lambda i, j: (i, j)))
        args.append(residual)
    return pl.pallas_call(
        functools.partial(_mm_kernel, has_res=residual is not None),
        grid=(m // tm, nc // tn), in_specs=in_specs,
        out_specs=pl.BlockSpec((tm, tn), lambda i, j: (i, j)),
        out_shape=jax.ShapeDtypeStruct((m, nc), out_dtype),
        compiler_params=_cparams(("parallel", "parallel")), name=name,
    )(*args)


def _mlstm_kernel(aq_ref, ak_ref, av_ref, ao_ref, zs_ref, zst_ref, cw_ref, gbc_ref, gbr_ref, gain_ref,
                  tril_ref, triu_ref, o_ref, xbuf, qk_s, ct_s, n_s, m_s):
    L, d, H, W = MLSTM_CHUNK, HEAD_DIM, N_HEADS, MIX_WIDTH
    c = pl.program_id(1)

    @pl.when(c == 0)
    def _():
        xbuf[0:8, :] = jnp.zeros((8, 2 * W), F32)
        ct_s[...] = jnp.zeros_like(ct_s)
        n_s[...] = jnp.zeros_like(n_s)
        m_s[...] = jnp.zeros_like(m_s)

    @pl.when(c > 0)
    def _():
        xbuf[0:8, :] = xbuf[L:L + 8, :]

    xbuf[8:L + 8, 0:W] = aq_ref[...].astype(F32)
    xbuf[8:L + 8, W:2 * W] = ak_ref[...].astype(F32)
    base = 8 - (CONV_WIDTH - 1)
    conv = cw_ref[0:1, :] * xbuf[base:base + L, :]
    for j in range(1, CONV_WIDTH):
        conv = conv + cw_ref[j:j + 1, :] * xbuf[base + j:base + j + L, :]
    qk_s[...] = conv * jax.nn.sigmoid(conv)

    pre_c = zs_ref[...] + gbc_ref[...]
    ls_c = _log_sigmoid(pre_c)
    hi, mid, lo = _split3(ls_c)
    tril = tril_ref[...]
    bc = _dot(tril, hi) + _dot(tril, mid) + _dot(tril, lo)
    pre_r = zst_ref[...] + gbr_ref[...]
    br = _dot_exact_rhs(_log_sigmoid(pre_r), triu_ref[...])

    row = lax.broadcasted_iota(jnp.int32, (L, L), 0)
    col = lax.broadcasted_iota(jnp.int32, (L, L), 1)
    causal = row >= col
    scale = d ** -0.5
    for h in range(H):
        sl = slice(h * d, (h + 1) * d)
        q = qk_s[:, sl] * scale
        k = qk_s[:, W + h * d:W + (h + 1) * d]
        v = av_ref[:, sl]
        i_col, b_col = pre_c[:, h:h + 1], bc[:, H + h:H + h + 1]
        i_row, b_row = pre_r[h:h + 1, :], br[H + h:H + h + 1, :]
        m_prev = m_s[h:h + 1, 0:1]
        log_d = jnp.where(causal, b_col - b_row + i_row, NEG_INF)
        log_inter = b_col + m_prev
        m_row = jnp.maximum(jnp.max(log_d, axis=-1, keepdims=True), log_inter)
        qb, kb = q.astype(BF16), k.astype(BF16)
        s = _dot_nt(qb, kb) * jnp.exp(log_d - m_row)
        w_inter = jnp.exp(log_inter - m_row)
        num = _dot(s.astype(BF16), v) + w_inter * _dot(qb, ct_s[h].astype(BF16))
        den = jnp.sum(s, axis=-1, keepdims=True) + w_inter * jnp.sum(q * n_s[h:h + 1, :], axis=-1, keepdims=True)
        hh = num / jnp.maximum(jnp.abs(den), jnp.exp(-m_row))
        b_last = b_row[:, L - 1:L]
        log_w_row = b_last - b_row + i_row
        m_new = jnp.maximum(b_last + m_prev, jnp.max(log_w_row, axis=-1, keepdims=True))
        w_col = jnp.exp(b_last - b_col + i_col - m_new)
        decay = jnp.exp(b_last + m_prev - m_new)
        vw = (v.astype(F32) * w_col).astype(BF16)
        ct_s[h] = decay * ct_s[h] + _dot(k.T.astype(BF16), vw)
        n_s[h:h + 1, :] = decay * n_s[h:h + 1, :] + jnp.sum(k * w_col, axis=0, keepdims=True)
        m_s[h:h + 1, :] = jnp.broadcast_to(m_new, (1, LANES))
        hn = hh * lax.rsqrt(jnp.mean(hh * hh, axis=-1, keepdims=True) + RMS_EPS)
        o_ref[:, sl] = (hn * gain_ref[:, sl] * jax.nn.sigmoid(ao_ref[:, sl].astype(F32))).astype(BF16)


def _tri_consts(L):
    r = np.arange(L)
    tril = (r[:, None] >= r[None, :]).astype(np.float32)
    return jnp.asarray(tril, BF16), jnp.asarray(tril.T, BF16)


def _mlstm(zm, zs, zst, conv_w, gate_bias, gain, B, T):
    L, W = MLSTM_CHUNK, MIX_WIDTH
    nc = T // L
    n = B * T
    gbc = jnp.zeros((1, LANES), F32).at[0, :2 * N_HEADS].set(gate_bias.reshape(-1))
    gbr = jnp.broadcast_to(gate_bias.reshape(2 * N_HEADS, 1), (2 * N_HEADS, L)).astype(F32)
    tril, triu = _tri_consts(L)

    def blk(b_idx):
        return pl.BlockSpec((L, W), lambda b, c: (b * nc + c, b_idx))

    const = lambda shape: pl.BlockSpec(shape, lambda b, c: (0,) * len(shape))
    return pl.pallas_call(
        _mlstm_kernel, grid=(B, nc),
        in_specs=[blk(BLK_AQ), blk(BLK_AK), blk(BLK_AV), blk(BLK_AO),
                  pl.BlockSpec((L, LANES), lambda b, c: (b * nc + c, 0)),
                  pl.BlockSpec((2 * N_HEADS, L), lambda b, c: (0, b * nc + c)),
                  const((CONV_WIDTH, 2 * W)), const((1, LANES)), const((2 * N_HEADS, L)), const((1, W)),
                  const((L, L)), const((L, L))],
        out_specs=pl.BlockSpec((L, W), lambda b, c: (b * nc + c, 0)),
        out_shape=jax.ShapeDtypeStruct((n, W), BF16),
        scratch_shapes=[pltpu.VMEM((L + 8, 2 * W), F32), pltpu.VMEM((L, 2 * W), F32),
                        pltpu.VMEM((N_HEADS, HEAD_DIM, HEAD_DIM), F32), pltpu.VMEM((N_HEADS, HEAD_DIM), F32),
                        pltpu.VMEM((N_HEADS, LANES), F32)],
        compiler_params=_cparams(("parallel", "arbitrary")), name="mlstm",
    )(zm, zm, zm, zm, zs, zst, conv_w, gbc, gbr, gain.reshape(1, W), tril, triu)


def _ret_kernel(q_ref, k_ref, v_ref, g_ref, dec_ref, qd_ref, kd_ref, gain_ref, o_ref, r_s, *, chunk_decay):
    L, d, H = RET_CHUNK, HEAD_DIM, N_HEADS
    c = pl.program_id(1)

    @pl.when(c == 0)
    def _():
        r_s[...] = jnp.zeros_like(r_s)

    scale = d ** -0.5
    for h in range(H):
        sl = slice(h * d, (h + 1) * d)
        q = q_ref[:, sl].astype(F32)
        k = k_ref[:, sl].astype(F32) * scale
        v = v_ref[:, sl]
        s = _dot_nt(q.astype(BF16), k.astype(BF16)) * dec_ref[h]
        o = _dot(s.astype(BF16), v) + _dot((q * qd_ref[:, sl]).astype(BF16), r_s[h].astype(BF16))
        kd = (k * kd_ref[:, sl]).T.astype(BF16)
        r_s[h] = chunk_decay[h] * r_s[h] + _dot(kd, v)
        on = o * lax.rsqrt(jnp.mean(o * o, axis=-1, keepdims=True) + RMS_EPS)
        gate = g_ref[:, sl].astype(F32)
        o_ref[:, sl] = (on * gain_ref[:, sl] * (gate * jax.nn.sigmoid(gate))).astype(BF16)


def _retention(zm, gain, B, T):
    L, W, H, d = RET_CHUNK, MIX_WIDTH, N_HEADS, HEAD_DIM
    nc = T // L
    log_gamma = np.log(np.float32(1.0) - np.float32(2.0) ** (-5.0 - np.arange(H, dtype=np.float32))).astype(np.float32)
    pos = np.arange(L, dtype=np.float32)
    diff = pos[:, None] - pos[None, :]
    decay = np.where(diff >= 0, np.exp(log_gamma[:, None, None] * np.maximum(diff, 0.0)), 0.0).astype(np.float32)
    q_decay = np.exp(log_gamma[:, None] * (pos + 1.0)).astype(np.float32)
    k_decay = np.exp(log_gamma[:, None] * (L - 1.0 - pos)).astype(np.float32)
    chunk_decay = tuple(float(v) for v in np.exp(log_gamma * np.float32(L)).astype(np.float32))
    qd = jnp.asarray(np.repeat(q_decay.T, d, axis=1))
    kd = jnp.asarray(np.repeat(k_decay.T, d, axis=1))

    def blk(b_idx):
        return pl.BlockSpec((L, W), lambda b, c: (b * nc + c, b_idx))

    const = lambda shape: pl.BlockSpec(shape, lambda b, c: (0,) * len(shape))
    return pl.pallas_call(
        functools.partial(_ret_kernel, chunk_decay=chunk_decay), grid=(B, nc),
        in_specs=[blk(BLK_BQ), blk(BLK_BK), blk(BLK_BV), blk(BLK_BG),
                  const((H, L, L)), const((L, W)), const((L, W)), const((1, W))],
        out_specs=pl.BlockSpec((L, W), lambda b, c: (b * nc + c, 0)),
        out_shape=jax.ShapeDtypeStruct((B * T, W), BF16),
        scratch_shapes=[pltpu.VMEM((H, d, d), F32)],
        compiler_params=_cparams(("parallel", "arbitrary")), name="retention",
    )(zm, zm, zm, zm, jnp.asarray(decay), qd, kd, gain.reshape(1, W))


def _gelu(x):
    return 0.5 * x * (1.0 + jnp.tanh(0.7978845608028654 * (x + 0.044715 * (x * x * x))))


def _sgu_kernel(u_ref, v_ref, ng_ref, w_ref, b_ref, o_ref):
    L, d, H = SGU_CHUNK, HEAD_DIM, N_HEADS
    v = _gelu(v_ref[...].astype(F32))
    vn = (v * lax.rsqrt(jnp.mean(v * v, axis=-1, keepdims=True) + RMS_EPS) * ng_ref[...]).astype(BF16)
    u = _gelu(u_ref[...].astype(F32))
    row = lax.broadcasted_iota(jnp.int32, (L, L), 0)
    col = lax.broadcasted_iota(jnp.int32, (L, L), 1)
    for g in range(H):
        sl = slice(g * d, (g + 1) * d)
        wm = jnp.where(row >= col, w_ref[g], 0.0).astype(BF16)
        mixed = _dot(wm, vn[:, sl]) + b_ref[:, sl]
        o_ref[:, sl] = (u[:, sl] * mixed).astype(BF16)


def _sgu(zm, norm_g, w_s, b_s, n):
    L, W, H, d = SGU_CHUNK, MIX_WIDTH, N_HEADS, HEAD_DIM
    bsb = jnp.repeat(b_s.T, d, axis=1)
    const = lambda shape: pl.BlockSpec(shape, lambda i: (0,) * len(shape))
    return pl.pallas_call(
        _sgu_kernel, grid=(n // L,),
        in_specs=[pl.BlockSpec((L, W), lambda i: (i, BLK_DU)), pl.BlockSpec((L, W), lambda i: (i, BLK_DV)),
                  const((1, W)), const((H, L, L)), const((L, W))],
        out_specs=pl.BlockSpec((L, W), lambda i: (i, 0)),
        out_shape=jax.ShapeDtypeStruct((n, W), BF16),
        compiler_params=_cparams(("parallel",)), name="sgu",
    )(zm, zm, norm_g.reshape(1, W), w_s, bsb)


def _compress_kernel(k_ref, v_ref, w_ref, pe_ref, ko_ref, vo_ref, buf):
    t = k_ref.shape[0]
    n16 = t // CMP_STRIDE
    buf[t:t + CMP_STRIDE, :] = jnp.zeros((CMP_STRIDE, HEAD_DIM), F32)
    for which, (src, dst) in enumerate(((k_ref, ko_ref), (v_ref, vo_ref))):
        buf[0:t, :] = src[...].astype(F32)
        acc = jnp.zeros((n16, HEAD_DIM), F32)
        for l in range(CMP_BLOCK):
            x = buf[pl.ds(l, n16, stride=CMP_STRIDE), :]
            acc = acc + _dot((x + pe_ref[which, l:l + 1, :]).astype(BF16), w_ref[which, l].astype(BF16))
        dst[...] = acc.astype(BF16)


def _compress(zm, cmp_w, cmp_pe, B, T):
    G, d = NSA_KV_GROUPS, HEAD_DIM
    n16 = T // CMP_STRIDE
    out = jax.ShapeDtypeStruct((B, G, n16, d), BF16)
    kblk = KV_BASE
    vblk = KV_BASE + G
    return pl.pallas_call(
        _compress_kernel, grid=(B, G),
        in_specs=[pl.BlockSpec((T, d), lambda b, g: (b, kblk + g)), pl.BlockSpec((T, d), lambda b, g: (b, vblk + g)),
                  pl.BlockSpec((2, CMP_BLOCK, d, d), lambda b, g: (0, 0, 0, 0)),
                  pl.BlockSpec((2, CMP_BLOCK, d), lambda b, g: (0, 0, 0))],
        out_specs=[pl.BlockSpec((None, None, n16, d), lambda b, g: (b, g, 0, 0))] * 2,
        out_shape=[out, out],
        scratch_shapes=[pltpu.VMEM((T + CMP_STRIDE, d), F32)],
        compiler_params=_cparams(("parallel", "parallel")), name="nsa_compress",
    )(zm, zm, cmp_w, cmp_pe)


def _softmax_rows(s):
    m = jnp.max(s, axis=-1, keepdims=True)
    p = jnp.exp(s - m)
    return p, jnp.sum(p, axis=-1, keepdims=True)


def _nsa_a_kernel(slope_ref, q_ref, kc_ref, vc_ref, *rest, n_back):
    nw = n_back + 1
    kw_refs, vw_refs = rest[:nw], rest[nw:2 * nw]
    cg_ref, ov_ref, part_ref, sel_ref = rest[2 * nw:]
    tq, d, J = Q_BLOCK, HEAD_DIM, NSA_GROUP_SIZE
    g, qi = pl.program_id(1), pl.program_id(2)
    t0 = qi * tq
    ncmp = kc_ref.shape[0]
    scale = d ** -0.5
    qs = [(q_ref[:, j * d:(j + 1) * d].astype(F32) * scale).astype(BF16) for j in range(J)]
    sg = jax.nn.sigmoid(cg_ref[...])

    r = lax.broadcasted_iota(jnp.int32, (tq, ncmp), 0)
    n_idx = lax.broadcasted_iota(jnp.int32, (tq, ncmp), 1)
    dist_i = t0 + r - CMP_STRIDE * n_idx - (CMP_BLOCK - 1)
    valid_c = dist_i >= 0
    dist_c = dist_i.astype(F32)
    t_col = t0 + lax.broadcasted_iota(jnp.int32, (tq, 1), 0)
    has_c = (t_col >= CMP_BLOCK - 1).astype(F32)
    kc, vc = kc_ref[...], vc_ref[...]
    psum = jnp.zeros((tq, ncmp), F32)
    o_cmp = []
    for j in range(J):
        slope = slope_ref[g * J + j]
        s = jnp.where(valid_c, _dot_nt(qs[j], kc) - slope * dist_c, NEG_INF)
        p, l = _softmax_rows(s)
        p = p * (has_c / l)
        o_cmp.append(_dot(p.astype(BF16), vc))
        psum = psum + p

    imp = _dot_exact_rhs(psum, ov_ref[...])
    blk = lax.broadcasted_iota(jnp.int32, (tq, LANES), 1)
    cur = (t0 + lax.broadcasted_iota(jnp.int32, (tq, LANES), 0)) // SEL_BLOCK
    valid_s = blk <= cur
    forced = (blk == 0) | (blk == cur) | (blk == cur - 1)
    key = jnp.where(valid_s, imp + jnp.where(forced, FORCE_SCORE, 0.0), NEG_INF)
    chosen = jnp.zeros((tq, LANES), jnp.int32)
    for _ in range(SEL_TOPN):
        mx = jnp.max(key, axis=-1, keepdims=True)
        first = jnp.min(jnp.where(key == mx, blk, LANES), axis=-1, keepdims=True)
        hit = blk == first
        chosen = jnp.where(hit, 1, chosen)
        key = jnp.where(hit, -3.0e38, key)
    sel_ref[...] = jnp.where((chosen > 0) & valid_s, 1.0, 0.0).astype(BF16)

    kw = jnp.concatenate([kr[...] for kr in kw_refs], axis=0)
    vw = jnp.concatenate([vr[...] for vr in vw_refs], axis=0)
    span = nw * tq
    rw = lax.broadcasted_iota(jnp.int32, (tq, span), 0)
    cw = lax.broadcasted_iota(jnp.int32, (tq, span), 1)
    dist_wi = rw + n_back * tq - cw
    mask_w = (dist_wi >= 0) & (dist_wi < WINDOW) & (t0 - n_back * tq + cw >= 0)
    dist_w = dist_wi.astype(F32)
    for j in range(J):
        slope = slope_ref[g * J + j]
        s = jnp.where(mask_w, _dot_nt(qs[j], kw) - slope * dist_w, NEG_INF)
        p, l = _softmax_rows(s)
        o_win = _dot((p / l).astype(BF16), vw)
        part_ref[:, j * d:(j + 1) * d] = sg[:, j:j + 1] * o_cmp[j] + sg[:, 2 * J + j:2 * J + j + 1] * o_win


def _nsa_consts(T):
    n16 = T // CMP_STRIDE
    n_sel = T // SEL_BLOCK
    n = np.arange(n16)[:, None]
    m = np.arange(LANES)[None, :]
    c_start, s_start = n * CMP_STRIDE, m * SEL_BLOCK
    overlap = ((c_start < s_start + SEL_BLOCK) & (c_start + CMP_BLOCK > s_start) & (n < n16 - 1) & (m < n_sel))
    return jnp.asarray(overlap.astype(np.float32), BF16)


def _nsa_a(zm, k_cmp, v_cmp, cg_g, slopes, B, T):
    G, J, d, tq = NSA_KV_GROUPS, NSA_GROUP_SIZE, HEAD_DIM, Q_BLOCK
    nq = T // tq
    n16 = T // CMP_STRIDE
    n_back = WINDOW // tq
    qblk = BLK_CQ * (MIX_WIDTH // (J * d))
    kwblk = KV_BASE + 4 * G
    vwblk = KV_BASE + 5 * G

    def win_spec(base, i):
        return pl.BlockSpec((tq, d), lambda b, g, qi: (b * nq + jnp.maximum(qi - n_back + i, 0), base + g))

    in_specs = ([pl.BlockSpec(memory_space=pltpu.SMEM),
                 pl.BlockSpec((tq, J * d), lambda b, g, qi: (b * nq + qi, qblk + g)),
                 pl.BlockSpec((None, None, n16, d), lambda b, g, qi: (b, g, 0, 0)),
                 pl.BlockSpec((None, None, n16, d), lambda b, g, qi: (b, g, 0, 0))]
                + [win_spec(kwblk, i) for i in range(n_back + 1)]
                + [win_spec(vwblk, i) for i in range(n_back + 1)]
                + [pl.BlockSpec((None, tq, LANES), lambda b, g, qi: (g, b * nq + qi, 0)),
                   pl.BlockSpec((n16, LANES), lambda b, g, qi: (0, 0))])
    return pl.pallas_call(
        functools.partial(_nsa_a_kernel, n_back=n_back), grid=(B, G, nq), in_specs=in_specs,
        out_specs=[pl.BlockSpec((tq, J * d), lambda b, g, qi: (b * nq + qi, g)),
                   pl.BlockSpec((None, None, tq, LANES), lambda b, g, qi: (b, g, qi, 0))],
        out_shape=[jax.ShapeDtypeStruct((B * T, MIX_WIDTH), F32), jax.ShapeDtypeStruct((B, G, T, LANES), BF16)],
        compiler_params=_cparams(("parallel", "parallel", "parallel")), name="nsa_cmp_win",
    )(slopes, zm, k_cmp, v_cmp, *([zm] * (2 * (n_back + 1))), cg_g, _nsa_consts(T))


def _nsa_b_kernel(slope_ref, q_ref, ks_ref, vs_ref, sel_ref, e_ref, cg_ref, part_ref, o_ref, q_s, m_s, l_s, acc_s, *, tk):
    tq, d, J = Q_BLOCK, HEAD_DIM, NSA_GROUP_SIZE
    g, qi = pl.program_id(1), pl.program_id(2)
    t0 = qi * tq
    scale = d ** -0.5
    for j in range(J):
        q_s[j] = (q_ref[:, j * d:(j + 1) * d].astype(F32) * scale).astype(BF16)
    m_s[...] = jnp.full_like(m_s, NEG_INF)
    l_s[...] = jnp.zeros_like(l_s)
    acc_s[...] = jnp.zeros_like(acc_s)
    rc = lax.broadcasted_iota(jnp.int32, (tq, tk), 0) - lax.broadcasted_iota(jnp.int32, (tq, tk), 1)
    sel = sel_ref[...]
    n_kt = (t0 + tq - 1) // tk + 1

    def body(kt, carry):
        start = pl.multiple_of(kt * tk, tk)
        k_t = ks_ref[pl.ds(start, tk), :]
        v_t = vs_ref[pl.ds(start, tk), :]
        picked = _dot(sel, e_ref[kt])
        dist_i = rc + (t0 - kt * tk)
        bias = jnp.where((picked > 0.5) & (dist_i >= 0), 0.0, NEG_INF)
        dist = dist_i.astype(F32)
        for j in range(J):
            slope = slope_ref[g * J + j]
            s = _dot_nt(q_s[j], k_t) - slope * dist + bias
            m_prev = m_s[j]
            m_new = jnp.maximum(m_prev, jnp.max(s, axis=-1, keepdims=True))
            alpha = jnp.exp(m_prev - m_new)
            p = jnp.exp(s - m_new)
            l_s[j] = alpha * l_s[j] + jnp.sum(p, axis=-1, keepdims=True)
            acc_s[j] = alpha * acc_s[j] + _dot(p.astype(BF16), v_t)
            m_s[j] = m_new
        return carry

    lax.fori_loop(0, n_kt, body, 0)
    sg = jax.nn.sigmoid(cg_ref[...])
    for j in range(J):
        o_sel = acc_s[j] / l_s[j]
        o_ref[:, j * d:(j + 1) * d] = (part_ref[:, j * d:(j + 1) * d] + sg[:, J + j:J + j + 1] * o_sel).astype(BF16)


def _nsa_b(zm, sel, cg_g, part, slopes, B, T, tk=512):
    G, J, d, tq = NSA_KV_GROUPS, NSA_GROUP_SIZE, HEAD_DIM, Q_BLOCK
    tk = min(tk, T)
    nq, nk = T // tq, T // tk
    qblk = BLK_CQ * (MIX_WIDTH // (J * d))
    ksblk = KV_BASE + 2 * G
    vsblk = KV_BASE + 3 * G
    per_tile = tk // SEL_BLOCK
    kt = np.arange(nk)[:, None, None]
    m = np.arange(LANES)[None, :, None]
    cc = np.arange(tk)[None, None, :]
    expand = jnp.asarray((m == kt * per_tile + cc // SEL_BLOCK).astype(np.float32), BF16)
    return pl.pallas_call(
        functools.partial(_nsa_b_kernel, tk=tk), grid=(B, G, nq),
        in_specs=[pl.BlockSpec(memory_space=pltpu.SMEM),
                  pl.BlockSpec((tq, J * d), lambda b, g, qi: (b * nq + qi, qblk + g)),
                  pl.BlockSpec((T, d), lambda b, g, qi: (b, ksblk + g)),
                  pl.BlockSpec((T, d), lambda b, g, qi: (b, vsblk + g)),
                  pl.BlockSpec((None, None, tq, LANES), lambda b, g, qi: (b, g, qi, 0)),
                  pl.BlockSpec((nk, LANES, tk), lambda b, g, qi: (0, 0, 0)),
                  pl.BlockSpec((None, tq, LANES), lambda b, g, qi: (g, b * nq + qi, 0)),
                  pl.BlockSpec((tq, J * d), lambda b, g, qi: (b * nq + qi, g))],
        out_specs=pl.BlockSpec((tq, J * d), lambda b, g, qi: (b * nq + qi, g)),
        out_shape=jax.ShapeDtypeStruct((B * T, MIX_WIDTH), BF16),
        scratch_shapes=[pltpu.VMEM((J, tq, d), BF16), pltpu.VMEM((J, tq, 1), F32), pltpu.VMEM((J, tq, 1), F32),
                        pltpu.VMEM((J, tq, d), F32)],
        compiler_params=_cparams(("parallel", "parallel", "parallel")), name="nsa_selected",
    )(slopes, zm, zm, zm, sel, expand, cg_g, part)


def _nsa(zm, zs, cmp_w, cmp_pe, B, T):
    G, J = NSA_KV_GROUPS, NSA_GROUP_SIZE
    n = B * T
    cg = zs[:, 2 * N_HEADS:2 * N_HEADS + 3 * N_HEADS].reshape(n, 3, G, J)
    cg_g = jnp.pad(cg.transpose(2, 0, 1, 3).reshape(G, n, 3 * J), ((0, 0), (0, 0), (0, LANES - 3 * J)))
    slopes = jnp.asarray(2.0 ** (-8.0 * (np.arange(N_HEADS, dtype=np.float32) + 1.0) / N_HEADS), F32)
    k_cmp, v_cmp = _compress(zm, cmp_w, cmp_pe, B, T)
    part, sel = _nsa_a(zm, k_cmp, v_cmp, cg_g, slopes, B, T)
    return _nsa_b(zm, sel, cg_g, part, slopes, B, T)


def _merge_kernel(h_ref, y0, y1, y2, y3, g0, g1, g2, g3, b0, b1, b2, b3, o_ref):
    h = h_ref[...]
    acc = None
    for y, gw, bw in ((y0, g0, b0), (y1, g1, b1), (y2, g2, b2), (y3, g3, b3)):
        term = jax.nn.sigmoid(_dot(h, gw[...])) * _dot(y[...], bw[...])
        acc = term if acc is None else acc + term
    o_ref[...] = acc.astype(BF16)


def _merge(h, ys, w_gate, w_branch, tm=512, tn=256):
    n, dm = h.shape
    tm = min(tm, n)
    nj = dm // tn
    y_spec = pl.BlockSpec((tm, MIX_WIDTH), lambda i, j: (i, 0))
    gate_specs = [pl.BlockSpec((dm, tn), functools.partial(lambda i, j, b: (0, b * nj + j), b=b)) for b in range(N_BRANCH)]
    br_specs = [pl.BlockSpec((None, MIX_WIDTH, tn), functools.partial(lambda i, j, b: (b, 0, j), b=b)) for b in range(N_BRANCH)]
    return pl.pallas_call(
        _merge_kernel, grid=(n // tm, nj),
        in_specs=[pl.BlockSpec((tm, dm), lambda i, j: (i, 0))] + [y_spec] * N_BRANCH + gate_specs + br_specs,
        out_specs=pl.BlockSpec((tm, tn), lambda i, j: (i, j)),
        out_shape=jax.ShapeDtypeStruct((n, dm), BF16),
        compiler_params=_cparams(("parallel", "parallel")), name="gated_merge",
    )(h, *ys, *([w_gate] * N_BRANCH), *([w_branch] * N_BRANCH))


def _expert_kernel(be_ref, nact_ref, x_ref, w1_ref, w3_ref, w2_ref, wt_ref, o_ref, w1b, w3b, w2b):
    i = pl.program_id(0)
    e = be_ref[i]
    prev = be_ref[jnp.maximum(i - 1, 0)]

    @pl.when((i == 0) | (prev != e))
    def _():
        w1b[...] = w1_ref[...].astype(BF16)
        w3b[...] = w3_ref[...].astype(BF16)
        w2b[...] = w2_ref[...].astype(BF16)

    @pl.when(i < nact_ref[0])
    def _():
        x = x_ref[...]
        a1 = _dot(x, w1b[...])
        a = (a1 * jax.nn.sigmoid(a1)) * _dot(x, w3b[...])
        o_ref[...] = _dot(a.astype(BF16), w2b[...]) * wt_ref[...]

    @pl.when(i >= nact_ref[0])
    def _():
        o_ref[...] = jnp.zeros_like(o_ref)


def _experts(xg, w_pad, blk_exp, n_active, w1, w3, w2):
    p, dm = xg.shape
    n_blk = p // MOE_BLOCK
    ff = w1.shape[-1]
    grid_spec = pltpu.PrefetchScalarGridSpec(
        num_scalar_prefetch=2, grid=(n_blk,),
        in_specs=[pl.BlockSpec((MOE_BLOCK, dm), lambda i, be, na: (i, 0)),
                  pl.BlockSpec((None, dm, ff), lambda i, be, na: (be[i], 0, 0)),
                  pl.BlockSpec((None, dm, ff), lambda i, be, na: (be[i], 0, 0)),
                  pl.BlockSpec((None, ff, dm), lambda i, be, na: (be[i], 0, 0)),
                  pl.BlockSpec((MOE_BLOCK, 1), lambda i, be, na: (i, 0))],
        out_specs=pl.BlockSpec((MOE_BLOCK, dm), lambda i, be, na: (i, 0)),
        scratch_shapes=[pltpu.VMEM((dm, ff), BF16), pltpu.VMEM((dm, ff), BF16), pltpu.VMEM((ff, dm), BF16)])
    return pl.pallas_call(
        _expert_kernel, grid_spec=grid_spec,
        out_shape=jax.ShapeDtypeStruct((p, dm), F32),
        compiler_params=_cparams(("arbitrary",)), name="moe_experts",
    )(blk_exp, n_active, xg, w1, w3, w2, w_pad.reshape(p, 1))


def _route(logits, bg_unused=None):
    n = logits.shape[0]
    K = TOP_K_IN_GROUP
    grp_logits = logits[:, :N_GROUPS]
    grp_prob = jax.nn.softmax(grp_logits, axis=-1)
    g_idx = jnp.argmax(grp_logits, axis=-1).astype(jnp.int32)
    g_w = jnp.take_along_axis(grp_prob, g_idx[:, None], axis=-1)
    exp_logits = logits[:, N_GROUPS:N_GROUPS + N_EXPERTS].reshape(n, N_GROUPS, EXPERTS_PER_GROUP)
    in_grp = jnp.take_along_axis(exp_logits, g_idx[:, None, None], axis=1)[:, 0]
    top_v, top_i = lax.top_k(in_grp, K)
    comb = jax.nn.softmax(top_v, axis=-1) * g_w
    e_idx = g_idx[:, None] * EXPERTS_PER_GROUP + top_i.astype(jnp.int32)

    nk = n * K
    flat_e = e_idx.reshape(-1)
    flat_tok = jnp.repeat(jnp.arange(n, dtype=jnp.int32), K)
    flat_w = comb.reshape(-1)
    order = jnp.argsort(flat_e)
    se, stok, sw = flat_e[order], flat_tok[order], flat_w[order]
    counts = jnp.bincount(flat_e, length=N_EXPERTS)
    padded = (counts + MOE_BLOCK - 1) // MOE_BLOCK * MOE_BLOCK
    pad_end = jnp.cumsum(padded)
    pad_start = pad_end - padded
    start = jnp.cumsum(counts) - counts
    dest = (pad_start[se] + jnp.arange(nk, dtype=jnp.int32) - start[se]).astype(jnp.int32)
    n_blk = (nk + MOE_BLOCK - 1) // MOE_BLOCK + N_EXPERTS
    p = n_blk * MOE_BLOCK
    tok_pad = jnp.full((p,), n, dtype=jnp.int32).at[dest].set(stok)
    w_pad = jnp.zeros((p,), dtype=F32).at[dest].set(sw)
    blk_exp = jnp.minimum(jnp.searchsorted(pad_end, jnp.arange(n_blk) * MOE_BLOCK, side='right'),
                          N_EXPERTS - 1).astype(jnp.int32)
    n_active = (pad_end[-1] // MOE_BLOCK).astype(jnp.int32).reshape(1)
    slot = jnp.zeros((nk,), jnp.int32).at[order].set(dest)
    return tok_pad, w_pad, blk_exp, n_active, slot.reshape(n, K)


def _moe(h, logits, w1, w3, w2):
    n, dm = h.shape
    tok_pad, w_pad, blk_exp, n_active, slot = _route(logits)
    xg = jnp.take(h, jnp.minimum(tok_pad, n - 1), axis=0)
    y = _experts(xg, w_pad, blk_exp, n_active, w1, w3, w2)
    return jnp.take(y, slot[:, 0], axis=0), jnp.take(y, slot[:, 1], axis=0)


def _pack_w_in(w):
    cols = lambda i: w[:, COL_OFFSETS[i]:COL_OFFSETS[i + 1]]
    w_main = jnp.concatenate([cols(i) for i in MAIN_ORDER], axis=1).astype(BF16)
    small = jnp.concatenate([cols(I_AI), cols(I_AF), cols(I_CG)], axis=1)
    w_small = jnp.pad(small, ((0, 0), (0, LANES - small.shape[1]))).astype(BF16)
    w_gate = cols(I_MERGE).astype(BF16)
    return w_main, w_small, w_gate


def _router_weights(wg, bg, we, be):
    w = jnp.concatenate([wg, we], axis=1)
    w = jnp.pad(w, ((0, 0), (0, LANES - w.shape[1])))
    hi = w.astype(BF16)
    lo = (w - hi.astype(F32)).astype(BF16)
    b = jnp.concatenate([bg, be])
    b = jnp.pad(b, (0, LANES - b.shape[0])).reshape(1, LANES).astype(F32)
    return hi, lo, b


def kernel(x, w_in, mlstm_conv, mlstm_gate_bias, mlstm_norm, ret_norm, nsa_cmp_w, nsa_cmp_pe, sgu_norm, sgu_w, sgu_b, w_branch, w_out, norm_mix, norm_ffn, router_group_w, router_group_b, router_expert_w, router_expert_b, expert_w1, expert_w3, expert_w2, norm_final):
    B, T, D = x.shape
    n = B * T
    adds = [x.reshape(n, D)]
    for l in range(DEPTH):
        w_main, w_small, w_gate = _pack_w_in(w_in[l])
        res = _norm(adds, norm_mix[l])
        xs, h = (res[0], res[1]) if len(adds) > 1 else (adds[0], res[0])
        zm = _matmul(h, w_main, out_dtype=BF16, name="proj_main")
        zs = _matmul(h, w_small, out_dtype=F32, name="proj_small")
        zst = zs[:, :2 * N_HEADS].T
        ya = _mlstm(zm, zs, zst, mlstm_conv[l], mlstm_gate_bias[l], mlstm_norm[l], B, T)
        yb = _retention(zm, ret_norm[l], B, T)
        yc = _nsa(zm, zs, nsa_cmp_w[l], nsa_cmp_pe[l], B, T)
        yd = _sgu(zm, sgu_norm[l], sgu_w[l], sgu_b[l], n)
        merged = _merge(h, (ya, yb, yc, yd), w_gate, w_branch[l].astype(BF16))
        x_mid = _matmul(merged, w_out[l].astype(BF16), out_dtype=F32, residual=xs, name="proj_out")
        h2, logits = _norm([x_mid], norm_ffn[l],
                           router_w=_router_weights(router_group_w[l], router_group_b[l],
                                                    router_expert_w[l], router_expert_b[l]))
        m0, m1 = _moe(h2, logits, expert_w1[l], expert_w3[l], expert_w2[l])
        adds = [x_mid, m0, m1]
    _, out = _norm(adds, norm_final, final=True)
    return out.reshape(B, T, D)
```

```python
import functools

import numpy as np
import jax
import jax.numpy as jnp
from jax import lax
from jax.experimental import pallas as pl
from jax.experimental.pallas import tpu as pltpu

F32 = jnp.float32
BF16 = jnp.bfloat16

D_MODEL = 4096
DEPTH = 2
HEAD_DIM = 128
N_BRANCH = 4
MIX_WIDTH = D_MODEL // N_BRANCH
N_HEADS = MIX_WIDTH // HEAD_DIM
CONV_WIDTH = 4
MLSTM_CHUNK = 128
RET_CHUNK = 128
NSA_KV_GROUPS = 2
NSA_GROUP_SIZE = N_HEADS // NSA_KV_GROUPS
KV_WIDTH = NSA_KV_GROUPS * HEAD_DIM
CMP_BLOCK = 32
CMP_STRIDE = 16
SEL_BLOCK = 64
SEL_TOPN = 16
WINDOW = 512
Q_BLOCK = 128
SGU_CHUNK = 128
N_GROUPS = 8
EXPERTS_PER_GROUP = 8
N_EXPERTS = N_GROUPS * EXPERTS_PER_GROUP
TOP_K_IN_GROUP = 2
EXPERT_FF = 256
MOE_BLOCK = 128
RMS_EPS = 1e-6
NEG_INF = -1e30
FORCE_SCORE = 1e9

COL_WIDTHS = (
    MIX_WIDTH, MIX_WIDTH, MIX_WIDTH, MIX_WIDTH, N_HEADS, N_HEADS,
    MIX_WIDTH, MIX_WIDTH, MIX_WIDTH, MIX_WIDTH,
    MIX_WIDTH, KV_WIDTH, KV_WIDTH, KV_WIDTH, KV_WIDTH, KV_WIDTH, KV_WIDTH, 3 * N_HEADS,
    MIX_WIDTH, MIX_WIDTH,
    N_BRANCH * D_MODEL,
)
COL_OFFSETS = tuple(int(v) for v in np.concatenate([[0], np.cumsum(COL_WIDTHS)]))
(I_AQ, I_AK, I_AV, I_AO, I_AI, I_AF, I_BQ, I_BK, I_BV, I_BG, I_CQ, I_CKC, I_CVC, I_CKS, I_CVS,
 I_CKW, I_CVW, I_CG, I_DU, I_DV, I_MERGE) = range(21)

MAIN_ORDER = (I_AQ, I_AK, I_AV, I_AO, I_BQ, I_BK, I_BV, I_BG, I_CQ, I_DU, I_DV,
              I_CKC, I_CVC, I_CKS, I_CVS, I_CKW, I_CVW)
MAIN_WIDTH = sum(COL_WIDTHS[i] for i in MAIN_ORDER)
LANES = 128
BLK_AQ, BLK_AK, BLK_AV, BLK_AO, BLK_BQ, BLK_BK, BLK_BV, BLK_BG, BLK_CQ, BLK_DU, BLK_DV = range(11)
KV_BASE = 11 * MIX_WIDTH // HEAD_DIM
VMEM_LIMIT = 48 * 1024 * 1024


def _cparams(sem, vmem=VMEM_LIMIT):
    return pltpu.CompilerParams(dimension_semantics=sem, vmem_limit_bytes=vmem)


def _dot(a, b):
    return jnp.dot(a, b, preferred_element_type=F32)


def _dot_nt(a, b):
    return lax.dot_general(a, b, (((1,), (1,)), ((), ())), preferred_element_type=F32)


def _split3(x):
    hi = x.astype(BF16)
    r1 = x - hi.astype(F32)
    mid = r1.astype(BF16)
    lo = (r1 - mid.astype(F32)).astype(BF16)
    return hi, mid, lo


def _dot_exact_rhs(x, m_bf16):
    hi, mid, lo = _split3(x)
    return _dot(hi, m_bf16) + _dot(mid, m_bf16) + _dot(lo, m_bf16)


def _log_sigmoid(x):
    return jnp.minimum(x, 0.0) - jnp.log1p(jnp.exp(-jnp.abs(x)))


def _norm_kernel(*refs, n_add, emit_sum, router, final):
    adds = refs[:n_add]
    g_ref = refs[n_add]
    pos = n_add + 1
    if router:
        whi_ref, wlo_ref, rb_ref = refs[pos:pos + 3]
        pos += 3
    outs = refs[pos:]
    x = adds[0][...]
    if n_add == 2:
        x = x + adds[1][...]
    elif n_add == 3:
        x = x + (adds[1][...] + adds[2][...])
    y = x * lax.rsqrt(jnp.mean(x * x, axis=-1, keepdims=True) + RMS_EPS) * g_ref[...]
    o = 0
    if emit_sum:
        outs[o][...] = x
        o += 1
    if final:
        outs[o][...] = y
    else:
        outs[o][...] = y.astype(BF16)
    o += 1
    if router:
        y_hi = y.astype(BF16)
        y_lo = (y - y_hi.astype(F32)).astype(BF16)
        lg = _dot(y_hi, whi_ref[...]) + (_dot(y_hi, wlo_ref[...]) + _dot(y_lo, whi_ref[...]))
        outs[o][...] = lg + rb_ref[...]


def _norm(adds, gain, *, router_w=None, final=False, rows=128):
    n, d = adds[0].shape
    rows = min(rows, n)
    n_add = len(adds)
    emit_sum = n_add > 1
    row_spec = pl.BlockSpec((rows, d), lambda i: (i, 0))
    in_specs = [row_spec] * n_add + [pl.BlockSpec((1, d), lambda i: (0, 0))]
    args = list(adds) + [gain.reshape(1, d)]
    out_shape, out_specs = [], []
    if emit_sum:
        out_shape.append(jax.ShapeDtypeStruct((n, d), F32))
        out_specs.append(row_spec)
    out_shape.append(jax.ShapeDtypeStruct((n, d), F32 if final else BF16))
    out_specs.append(row_spec)
    if router_w is not None:
        whi, wlo, rb = router_w
        in_specs += [pl.BlockSpec((d, LANES), lambda i: (0, 0))] * 2 + [pl.BlockSpec((1, LANES), lambda i: (0, 0))]
        args += [whi, wlo, rb]
        out_shape.append(jax.ShapeDtypeStruct((n, LANES), F32))
        out_specs.append(pl.BlockSpec((rows, LANES), lambda i: (i, 0)))
    return pl.pallas_call(
        functools.partial(_norm_kernel, n_add=n_add, emit_sum=emit_sum, router=router_w is not None, final=final),
        grid=(n // rows,), in_specs=in_specs, out_specs=out_specs, out_shape=out_shape,
        compiler_params=_cparams(("parallel",)), name="rmsnorm",
    )(*args)


def _mm_kernel(a_ref, w_ref, *rest, has_res):
    acc = _dot(a_ref[...], w_ref[...])
    if has_res:
        acc = acc + rest[0][...]
    rest[-1][...] = acc.astype(rest[-1].dtype)


def _matmul(a, w, *, out_dtype, residual=None, tm=1024, tn=512, name="matmul"):
    m, k = a.shape
    nc = w.shape[1]
    tm, tn = min(tm, m), min(tn, nc)
    in_specs = [pl.BlockSpec((tm, k), lambda i, j: (i, 0)), pl.BlockSpec((k, tn), lambda i, j: (0, j))]
    args = [a, w]
    if residual is not None:
        in_specs.append(pl.BlockSpec((tm, tn), lambda i, j: (i, j)))
        args.append(residual)
    return pl.pallas_call(
        functools.partial(_mm_kernel, has_res=residual is not None),
        grid=(m // tm, nc // tn), in_specs=in_specs,
        out_specs=pl.BlockSpec((tm, tn), lambda i, j: (i, j)),
        out_shape=jax.ShapeDtypeStruct((m, nc), out_dtype),
        compiler_params=_cparams(("parallel", "parallel")), name=name,
    )(*args)


def _mlstm_kernel(aq_ref, ak_ref, av_ref, ao_ref, zs_ref, zst_ref, cw_ref, gbc_ref, gbr_ref, gain_ref,
                  tril_ref, triu_ref, o_ref, xbuf, qk_s, ct_s, n_s, m_s):
    L, d, H, W = MLSTM_CHUNK, HEAD_DIM, N_HEADS, MIX_WIDTH
    c = pl.program_id(1)

    @pl.when(c == 0)
    def _():
        xbuf[0:8, :] = jnp.zeros((8, 2 * W), F32)
        ct_s[...] = jnp.zeros_like(ct_s)
        n_s[...] = jnp.zeros_like(n_s)
        m_s[...] = jnp.zeros_like(m_s)

    @pl.when(c > 0)
    def _():
        xbuf[0:8, :] = xbuf[L:L + 8, :]

    xbuf[8:L + 8, 0:W] = aq_ref[...].astype(F32)
    xbuf[8:L + 8, W:2 * W] = ak_ref[...].astype(F32)
    base = 8 - (CONV_WIDTH - 1)
    conv = cw_ref[0:1, :] * xbuf[base:base + L, :]
    for j in range(1, CONV_WIDTH):
        conv = conv + cw_ref[j:j + 1, :] * xbuf[base + j:base + j + L, :]
    qk_s[...] = conv * jax.nn.sigmoid(conv)

    pre_c = zs_ref[...] + gbc_ref[...]
    ls_c = _log_sigmoid(pre_c)
    hi, mid, lo = _split3(ls_c)
    tril = tril_ref[...]
    bc = _dot(tril, hi) + _dot(tril, mid) + _dot(tril, lo)
    pre_r = zst_ref[...] + gbr_ref[...]
    br = _dot_exact_rhs(_log_sigmoid(pre_r), triu_ref[...])

    row = lax.broadcasted_iota(jnp.int32, (L, L), 0)
    col = lax.broadcasted_iota(jnp.int32, (L, L), 1)
    causal = row >= col
    scale = d ** -0.5
    for h in range(H):
        sl = slice(h * d, (h + 1) * d)
        q = qk_s[:, sl] * scale
        k = qk_s[:, W + h * d:W + (h + 1) * d]
        v = av_ref[:, sl]
        i_col, b_col = pre_c[:, h:h + 1], bc[:, H + h:H + h + 1]
        i_row, b_row = pre_r[h:h + 1, :], br[H + h:H + h + 1, :]
        m_prev = m_s[h:h + 1, 0:1]
        log_d = jnp.where(causal, b_col - b_row + i_row, NEG_INF)
        log_inter = b_col + m_prev
        m_row = jnp.maximum(jnp.max(log_d, axis=-1, keepdims=True), log_inter)
        qb, kb = q.astype(BF16), k.astype(BF16)
        s = _dot_nt(qb, kb) * jnp.exp(log_d - m_row)
        w_inter = jnp.exp(log_inter - m_row)
        num = _dot(s.astype(BF16), v) + w_inter * _dot(qb, ct_s[h].astype(BF16))
        den = jnp.sum(s, axis=-1, keepdims=True) + w_inter * jnp.sum(q * n_s[h:h + 1, :], axis=-1, keepdims=True)
        hh = num / jnp.maximum(jnp.abs(den), jnp.exp(-m_row))
        b_last = b_row[:, L - 1:L]
        log_w_row = b_last - b_row + i_row
        m_new = jnp.maximum(b_last + m_prev, jnp.max(log_w_row, axis=-1, keepdims=True))
        w_col = jnp.exp(b_last - b_col + i_col - m_new)
        decay = jnp.exp(b_last + m_prev - m_new)
        vw = (v.astype(F32) * w_col).astype(BF16)
        ct_s[h] = decay * ct_s[h] + _dot(k.T.astype(BF16), vw)
        n_s[h:h + 1, :] = decay * n_s[h:h + 1, :] + jnp.sum(k * w_col, axis=0, keepdims=True)
        m_s[h:h + 1, :] = jnp.broadcast_to(m_new, (1, LANES))
        hn = hh * lax.rsqrt(jnp.mean(hh * hh, axis=-1, keepdims=True) + RMS_EPS)
        o_ref[:, sl] = (hn * gain_ref[:, sl] * jax.nn.sigmoid(ao_ref[:, sl].astype(F32))).astype(BF16)


def _tri_consts(L):
    r = np.arange(L)
    tril = (r[:, None] >= r[None, :]).astype(np.float32)
    return jnp.asarray(tril, BF16), jnp.asarray(tril.T, BF16)


def _mlstm(zm, zs, zst, conv_w, gate_bias, gain, B, T):
    L, W = MLSTM_CHUNK, MIX_WIDTH
    nc = T // L
    n = B * T
    gbc = jnp.zeros((1, LANES), F32).at[0, :2 * N_HEADS].set(gate_bias.reshape(-1))
    gbr = jnp.broadcast_to(gate_bias.reshape(2 * N_HEADS, 1), (2 * N_HEADS, L)).astype(F32)
    tril, triu = _tri_consts(L)

    def blk(b_idx):
        return pl.BlockSpec((L, W), lambda b, c: (b * nc + c, b_idx))

    const = lambda shape: pl.BlockSpec(shape, lambda b, c: (0,) * len(shape))
    return pl.pallas_call(
        _mlstm_kernel, grid=(B, nc),
        in_specs=[blk(BLK_AQ), blk(BLK_AK), blk(BLK_AV), blk(BLK_AO),
                  pl.BlockSpec((L, LANES), lambda b, c: (b * nc + c, 0)),
                  pl.BlockSpec((2 * N_HEADS, L), lambda b, c: (0, b * nc + c)),
                  const((CONV_WIDTH, 2 * W)), const((1, LANES)), const((2 * N_HEADS, L)), const((1, W)),
                  const((L, L)), const((L, L))],
        out_specs=pl.BlockSpec((L, W), lambda b, c: (b * nc + c, 0)),
        out_shape=jax.ShapeDtypeStruct((n, W), BF16),
        scratch_shapes=[pltpu.VMEM((L + 8, 2 * W), F32), pltpu.VMEM((L, 2 * W), F32),
                        pltpu.VMEM((N_HEADS, HEAD_DIM, HEAD_DIM), F32), pltpu.VMEM((N_HEADS, HEAD_DIM), F32),
                        pltpu.VMEM((N_HEADS, LANES), F32)],
        compiler_params=_cparams(("parallel", "arbitrary")), name="mlstm",
    )(zm, zm, zm, zm, zs, zst, conv_w, gbc, gbr, gain.reshape(1, W), tril, triu)


def _ret_kernel(q_ref, k_ref, v_ref, g_ref, dec_ref, qd_ref, kd_ref, gain_ref, o_ref, r_s, *, chunk_decay):
    L, d, H = RET_CHUNK, HEAD_DIM, N_HEADS
    c = pl.program_id(1)

    @pl.when(c == 0)
    def _():
        r_s[...] = jnp.zeros_like(r_s)

    scale = d ** -0.5
    for h in range(H):
        sl = slice(h * d, (h + 1) * d)
        q = q_ref[:, sl].astype(F32)
        k = k_ref[:, sl].astype(F32) * scale
        v = v_ref[:, sl]
        s = _dot_nt(q.astype(BF16), k.astype(BF16)) * dec_ref[h]
        o = _dot(s.astype(BF16), v) + _dot((q * qd_ref[:, sl]).astype(BF16), r_s[h].astype(BF16))
        kd = (k * kd_ref[:, sl]).T.astype(BF16)
        r_s[h] = chunk_decay[h] * r_s[h] + _dot(kd, v)
        on = o * lax.rsqrt(jnp.mean(o * o, axis=-1, keepdims=True) + RMS_EPS)
        gate = g_ref[:, sl].astype(F32)
        o_ref[:, sl] = (on * gain_ref[:, sl] * (gate * jax.nn.sigmoid(gate))).astype(BF16)


def _retention(zm, gain, B, T):
    L, W, H, d = RET_CHUNK, MIX_WIDTH, N_HEADS, HEAD_DIM
    nc = T // L
    log_gamma = np.log(np.float32(1.0) - np.float32(2.0) ** (-5.0 - np.arange(H, dtype=np.float32))).astype(np.float32)
    pos = np.arange(L, dtype=np.float32)
    diff = pos[:, None] - pos[None, :]
    decay = np.where(diff >= 0, np.exp(log_gamma[:, None, None] * np.maximum(diff, 0.0)), 0.0).astype(np.float32)
    q_decay = np.exp(log_gamma[:, None] * (pos + 1.0)).astype(np.float32)
    k_decay = np.exp(log_gamma[:, None] * (L - 1.0 - pos)).astype(np.float32)
    chunk_decay = tuple(float(v) for v in np.exp(log_gamma * np.float32(L)).astype(np.float32))
    qd = jnp.asarray(np.repeat(q_decay.T, d, axis=1))
    kd = jnp.asarray(np.repeat(k_decay.T, d, axis=1))

    def blk(b_idx):
        return pl.BlockSpec((L, W), lambda b, c: (b * nc + c, b_idx))

    const = lambda shape: pl.BlockSpec(shape, lambda b, c: (0,) * len(shape))
    return pl.pallas_call(
        functools.partial(_ret_kernel, chunk_decay=chunk_decay), grid=(B, nc),
        in_specs=[blk(BLK_BQ), blk(BLK_BK), blk(BLK_BV), blk(BLK_BG),
                  const((H, L, L)), const((L, W)), const((L, W)), const((1, W))],
        out_specs=pl.BlockSpec((L, W), lambda b, c: (b * nc + c, 0)),
        out_shape=jax.ShapeDtypeStruct((B * T, W), BF16),
        scratch_shapes=[pltpu.VMEM((H, d, d), F32)],
        compiler_params=_cparams(("parallel", "arbitrary")), name="retention",
    )(zm, zm, zm, zm, jnp.asarray(decay), qd, kd, gain.reshape(1, W))


def _gelu(x):
    return 0.5 * x * (1.0 + jnp.tanh(0.7978845608028654 * (x + 0.044715 * (x * x * x))))


def _sgu_kernel(u_ref, v_ref, ng_ref, w_ref, b_ref, o_ref):
    L, d, H = SGU_CHUNK, HEAD_DIM, N_HEADS
    v = _gelu(v_ref[...].astype(F32))
    vn = (v * lax.rsqrt(jnp.mean(v * v, axis=-1, keepdims=True) + RMS_EPS) * ng_ref[...]).astype(BF16)
    u = _gelu(u_ref[...].astype(F32))
    row = lax.broadcasted_iota(jnp.int32, (L, L), 0)
    col = lax.broadcasted_iota(jnp.int32, (L, L), 1)
    for g in range(H):
        sl = slice(g * d, (g + 1) * d)
        wm = jnp.where(row >= col, w_ref[g], 0.0).astype(BF16)
        mixed = _dot(wm, vn[:, sl]) + b_ref[:, sl]
        o_ref[:, sl] = (u[:, sl] * mixed).astype(BF16)


def _sgu(zm, norm_g, w_s, b_s, n):
    L, W, H, d = SGU_CHUNK, MIX_WIDTH, N_HEADS, HEAD_DIM
    bsb = jnp.repeat(b_s.T, d, axis=1)
    const = lambda shape: pl.BlockSpec(shape, lambda i: (0,) * len(shape))
    return pl.pallas_call(
        _sgu_kernel, grid=(n // L,),
        in_specs=[pl.BlockSpec((L, W), lambda i: (i, BLK_DU)), pl.BlockSpec((L, W), lambda i: (i, BLK_DV)),
                  const((1, W)), const((H, L, L)), const((L, W))],
        out_specs=pl.BlockSpec((L, W), lambda i: (i, 0)),
        out_shape=jax.ShapeDtypeStruct((n, W), BF16),
        compiler_params=_cparams(("parallel",)), name="sgu",
    )(zm, zm, norm_g.reshape(1, W), w_s, bsb)


def _compress_kernel(k_ref, v_ref, w_ref, pe_ref, ko_ref, vo_ref, buf):
    t = k_ref.shape[0]
    n16 = t // CMP_STRIDE
    buf[t:t + CMP_STRIDE, :] = jnp.zeros((CMP_STRIDE, HEAD_DIM), F32)
    for which, (src, dst) in enumerate(((k_ref, ko_ref), (v_ref, vo_ref))):
        buf[0:t, :] = src[...].astype(F32)
        acc = jnp.zeros((n16, HEAD_DIM), F32)
        for l in range(CMP_BLOCK):
            x = buf[pl.ds(l, n16, stride=CMP_STRIDE), :]
            acc = acc + _dot((x + pe_ref[which, l:l + 1, :]).astype(BF16), w_ref[which, l].astype(BF16))
        dst[...] = acc.astype(BF16)


def _compress(zm, cmp_w, cmp_pe, B, T):
    G, d = NSA_KV_GROUPS, HEAD_DIM
    n16 = T // CMP_STRIDE
    out = jax.ShapeDtypeStruct((B, G, n16, d), BF16)
    kblk = KV_BASE
    vblk = KV_BASE + G
    return pl.pallas_call(
        _compress_kernel, grid=(B, G),
        in_specs=[pl.BlockSpec((T, d), lambda b, g: (b, kblk + g)), pl.BlockSpec((T, d), lambda b, g: (b, vblk + g)),
                  pl.BlockSpec((2, CMP_BLOCK, d, d), lambda b, g: (0, 0, 0, 0)),
                  pl.BlockSpec((2, CMP_BLOCK, d), lambda b, g: (0, 0, 0))],
        out_specs=[pl.BlockSpec((None, None, n16, d), lambda b, g: (b, g, 0, 0))] * 2,
        out_shape=[out, out],
        scratch_shapes=[pltpu.VMEM((T + CMP_STRIDE, d), F32)],
        compiler_params=_cparams(("parallel", "parallel")), name="nsa_compress",
    )(zm, zm, cmp_w, cmp_pe)


def _softmax_rows(s):
    m = jnp.max(s, axis=-1, keepdims=True)
    p = jnp.exp(s - m)
    return p, jnp.sum(p, axis=-1, keepdims=True)


def _nsa_a_kernel(slope_ref, q_ref, kc_ref, vc_ref, *rest, n_back):
    nw = n_back + 1
    kw_refs, vw_refs = rest[:nw], rest[nw:2 * nw]
    cg_ref, ov_ref, part_ref, sel_ref, key_s = rest[2 * nw:]
    tq, d, J = Q_BLOCK, HEAD_DIM, NSA_GROUP_SIZE
    g, qi = pl.program_id(1), pl.program_id(2)
    t0 = qi * tq
    ncmp = kc_ref.shape[0]
    scale = d ** -0.5
    qs = [(q_ref[:, j * d:(j + 1) * d].astype(F32) * scale).astype(BF16) for j in range(J)]
    sg = jax.nn.sigmoid(cg_ref[...])

    r = lax.broadcasted_iota(jnp.int32, (tq, ncmp), 0)
    n_idx = lax.broadcasted_iota(jnp.int32, (tq, ncmp), 1)
    dist_i = t0 + r - CMP_STRIDE * n_idx - (CMP_BLOCK - 1)
    valid_c = dist_i >= 0
    dist_c = dist_i.astype(F32)
    t_col = t0 + lax.broadcasted_iota(jnp.int32, (tq, 1), 0)
    has_c = (t_col >= CMP_BLOCK - 1).astype(F32)
    kc, vc = kc_ref[...], vc_ref[...]
    psum = jnp.zeros((tq, ncmp), F32)
    o_cmp = []
    for j in range(J):
        slope = slope_ref[g * J + j]
        s = jnp.where(valid_c, _dot_nt(qs[j], kc) - slope * dist_c, NEG_INF)
        p, l = _softmax_rows(s)
        p = p * (has_c / l)
        o_cmp.append(_dot(p.astype(BF16), vc))
        psum = psum + p

    imp = _dot_exact_rhs(psum, ov_ref[...])
    blk = lax.broadcasted_iota(jnp.int32, (tq, LANES), 1)
    cur = (t0 + lax.broadcasted_iota(jnp.int32, (tq, LANES), 0)) // SEL_BLOCK
    valid_s = blk <= cur
    forced = (blk == 0) | (blk == cur) | (blk == cur - 1)
    key = jnp.where(valid_s, imp + jnp.where(forced, FORCE_SCORE, 0.0), NEG_INF)
    key_s[...] = key.T
    sub = lax.broadcasted_iota(jnp.int32, (8, tq), 0)
    groups = LANES // 8
    key_g = [key_s[8 * i:8 * i + 8, :] for i in range(groups)]
    rank = [jnp.zeros((8, tq), F32) for _ in range(groups)]
    for mp in range(LANES):
        rowv = jnp.broadcast_to(key_s[mp:mp + 1, :], (8, tq))
        for i in range(groups):
            if 8 * i > mp:
                ahead = rowv >= key_g[i]
            elif 8 * i + 7 <= mp:
                ahead = rowv > key_g[i]
            else:
                ahead = (rowv > key_g[i]) | ((sub > mp - 8 * i) & (rowv == key_g[i]))
            rank[i] = rank[i] + jnp.where(ahead, 1.0, 0.0)
    rank_t = jnp.concatenate(rank, axis=0)
    blk_t = lax.broadcasted_iota(jnp.int32, (LANES, tq), 0)
    cur_t = (t0 + lax.broadcasted_iota(jnp.int32, (LANES, tq), 1)) // SEL_BLOCK
    sel_ref[...] = jnp.where((rank_t < float(SEL_TOPN)) & (blk_t <= cur_t), 1.0, 0.0).astype(BF16)

    kw = jnp.concatenate([kr[...] for kr in kw_refs], axis=0)
    vw = jnp.concatenate([vr[...] for vr in vw_refs], axis=0)
    span = nw * tq
    rw = lax.broadcasted_iota(jnp.int32, (tq, span), 0)
    cw = lax.broadcasted_iota(jnp.int32, (tq, span), 1)
    dist_wi = rw + n_back * tq - cw
    mask_w = (dist_wi >= 0) & (dist_wi < WINDOW) & (t0 - n_back * tq + cw >= 0)
    dist_w = dist_wi.astype(F32)
    for j in range(J):
        slope = slope_ref[g * J + j]
        s = jnp.where(mask_w, _dot_nt(qs[j], kw) - slope * dist_w, NEG_INF)
        p, l = _softmax_rows(s)
        o_win = _dot((p / l).astype(BF16), vw)
        part_ref[:, j * d:(j + 1) * d] = sg[:, j:j + 1] * o_cmp[j] + sg[:, 2 * J + j:2 * J + j + 1] * o_win


def _nsa_consts(T):
    n16 = T // CMP_STRIDE
    n_sel = T // SEL_BLOCK
    n = np.arange(n16)[:, None]
    m = np.arange(LANES)[None, :]
    c_start, s_start = n * CMP_STRIDE, m * SEL_BLOCK
    overlap = ((c_start < s_start + SEL_BLOCK) & (c_start + CMP_BLOCK > s_start) & (n < n16 - 1) & (m < n_sel))
    return jnp.asarray(overlap.astype(np.float32), BF16)


def _nsa_a(zm, k_cmp, v_cmp, cg_g, slopes, B, T):
    G, J, d, tq = NSA_KV_GROUPS, NSA_GROUP_SIZE, HEAD_DIM, Q_BLOCK
    nq = T // tq
    n16 = T // CMP_STRIDE
    n_back = WINDOW // tq
    qblk = BLK_CQ * (MIX_WIDTH // (J * d))
    kwblk = KV_BASE + 4 * G
    vwblk = KV_BASE + 5 * G

    def win_spec(base, i):
        return pl.BlockSpec((tq, d), lambda b, g, qi: (b * nq + jnp.maximum(qi - n_back + i, 0), base + g))

    in_specs = ([pl.BlockSpec(memory_space=pltpu.SMEM),
                 pl.BlockSpec((tq, J * d), lambda b, g, qi: (b * nq + qi, qblk + g)),
                 pl.BlockSpec((None, None, n16, d), lambda b, g, qi: (b, g, 0, 0)),
                 pl.BlockSpec((None, None, n16, d), lambda b, g, qi: (b, g, 0, 0))]
                + [win_spec(kwblk, i) for i in range(n_back + 1)]
                + [win_spec(vwblk, i) for i in range(n_back + 1)]
                + [pl.BlockSpec((None, tq, LANES), lambda b, g, qi: (g, b * nq + qi, 0)),
                   pl.BlockSpec((n16, LANES), lambda b, g, qi: (0, 0))])
    return pl.pallas_call(
        functools.partial(_nsa_a_kernel, n_back=n_back), grid=(B, G, nq), in_specs=in_specs,
        out_specs=[pl.BlockSpec((tq, J * d), lambda b, g, qi: (b * nq + qi, g)),
                   pl.BlockSpec((None, None, LANES, tq), lambda b, g, qi: (b, g, 0, qi))],
        out_shape=[jax.ShapeDtypeStruct((B * T, MIX_WIDTH), F32), jax.ShapeDtypeStruct((B, G, LANES, T), BF16)],
        scratch_shapes=[pltpu.VMEM((LANES, tq), F32)],
        compiler_params=_cparams(("parallel", "parallel", "parallel")), name="nsa_cmp_win",
    )(slopes, zm, k_cmp, v_cmp, *([zm] * (2 * (n_back + 1))), cg_g, _nsa_consts(T))


def _nsa_b_kernel(cnt_ref, lst_ref, q_ref, ks_ref, vt_ref, sel_ref, e_ref, a_ref, srow_ref, cg_ref, part_ref, o_ref,
                  q_s, acc_s, *, tk, nk):
    tq, d, J = Q_BLOCK, HEAD_DIM, NSA_GROUP_SIZE
    b, g, qi = pl.program_id(0), pl.program_id(1), pl.program_id(2)
    lin = (b * pl.num_programs(1) + g) * pl.num_programs(2) + qi
    t0 = qi * tq
    scale = d ** -0.5
    for j in range(J):
        q_s[j * tq:(j + 1) * tq, :] = (q_ref[:, j * d:(j + 1) * d].astype(F32) * scale).astype(BF16)
    acc_s[...] = jnp.zeros_like(acc_s)
    r_minus_c = lax.broadcasted_iota(jnp.int32, (tk, tq), 1) - lax.broadcasted_iota(jnp.int32, (tk, tq), 0)

    def body(i, carry):
        m_prev, l_prev = carry
        kt = lst_ref[lin * nk + i]
        k_t = ks_ref[pl.ds(pl.multiple_of(kt * tk, tk), tk), :]
        off = t0 - kt * tk
        picked = _dot(e_ref[kt], sel_ref[...])
        bias = jnp.where((picked > 0.5) & (r_minus_c + off >= 0), 0.0, NEG_INF)
        s = _dot_nt(k_t, q_s[...]) - a_ref[...] + jnp.concatenate([bias] * J, axis=1)
        shift = srow_ref[...] * off.astype(F32)
        m_new = jnp.maximum(m_prev, jnp.max(s, axis=0, keepdims=True) - shift)
        p = jnp.exp(s - (m_new + shift))
        alpha = jnp.exp(m_prev - m_new)
        acc_s[...] = alpha * acc_s[...] + _dot(vt_ref[kt], p.astype(BF16))
        return m_new, alpha * l_prev + jnp.sum(p, axis=0, keepdims=True)

    init = (jnp.full((1, J * tq), NEG_INF, F32), jnp.zeros((1, J * tq), F32))
    _, l = lax.fori_loop(0, cnt_ref[lin], body, init)
    inv_l = 1.0 / l
    sg = jax.nn.sigmoid(cg_ref[...])
    for j in range(J):
        cols = slice(j * tq, (j + 1) * tq)
        o_sel = (acc_s[:, cols] * inv_l[:, cols]).T
        o_ref[:, j * d:(j + 1) * d] = (part_ref[:, j * d:(j + 1) * d] + sg[:, J + j:J + j + 1] * o_sel).astype(BF16)


def _nsa_b(zm, sel_t, cg_g, part, B, T, tk=512):
    G, J, d, tq = NSA_KV_GROUPS, NSA_GROUP_SIZE, HEAD_DIM, Q_BLOCK
    tk = min(tk, T)
    nq, nk = T // tq, T // tk
    qblk = BLK_CQ * (MIX_WIDTH // (J * d))
    ksblk = KV_BASE + 2 * G
    vs_col = (KV_BASE + 3 * G) * d
    per_tile = tk // SEL_BLOCK
    kt = np.arange(nk)[:, None, None]
    cc = np.arange(tk)[None, :, None]
    m = np.arange(LANES)[None, None, :]
    expand_t = jnp.asarray((m == kt * per_tile + cc // SEL_BLOCK).astype(np.float32), BF16)
    slopes_np = (2.0 ** (-8.0 * (np.arange(N_HEADS, dtype=np.float32) + 1.0) / N_HEADS)).astype(np.float32)
    r_minus_c = (np.arange(tq)[None, :] - np.arange(tk)[:, None]).astype(np.float32)
    alibi_t = jnp.asarray((slopes_np.reshape(G, 1, J, 1) * r_minus_c[None, :, None, :]).reshape(G, tk, J * tq))
    srow = jnp.asarray(np.repeat(slopes_np.reshape(G, J), tq, axis=1).reshape(G, 1, J * tq))
    vs_t = zm[:, vs_col:vs_col + G * d].reshape(B, nk, tk, G, d).transpose(0, 3, 1, 4, 2)

    active = sel_t[:, :, :nk * per_tile].reshape(B, G, nk, per_tile, nq, tq).max(axis=(3, 5)) > 0
    active = active.transpose(0, 1, 3, 2)
    tiles = jnp.arange(nk, dtype=jnp.int32)
    order = jnp.sort(jnp.where(active, tiles, tiles + nk), axis=-1) % nk
    counts = active.sum(axis=-1).astype(jnp.int32)

    grid_spec = pltpu.PrefetchScalarGridSpec(
        num_scalar_prefetch=2, grid=(B, G, nq),
        in_specs=[pl.BlockSpec((tq, J * d), lambda b, g, qi, c, o: (b * nq + qi, qblk + g)),
                  pl.BlockSpec((T, d), lambda b, g, qi, c, o: (b, ksblk + g)),
                  pl.BlockSpec((None, None, nk, d, tk), lambda b, g, qi, c, o: (b, g, 0, 0, 0)),
                  pl.BlockSpec((None, None, LANES, tq), lambda b, g, qi, c, o: (b, g, 0, qi)),
                  pl.BlockSpec((nk, tk, LANES), lambda b, g, qi, c, o: (0, 0, 0)),
                  pl.BlockSpec((None, tk, J * tq), lambda b, g, qi, c, o: (g, 0, 0)),
                  pl.BlockSpec((None, 1, J * tq), lambda b, g, qi, c, o: (g, 0, 0)),
                  pl.BlockSpec((None, tq, LANES), lambda b, g, qi, c, o: (g, b * nq + qi, 0)),
                  pl.BlockSpec((tq, J * d), lambda b, g, qi, c, o: (b * nq + qi, g))],
        out_specs=pl.BlockSpec((tq, J * d), lambda b, g, qi, c, o: (b * nq + qi, g)),
        scratch_shapes=[pltpu.VMEM((J * tq, d), BF16), pltpu.VMEM((d, J * tq), F32)])
    return pl.pallas_call(
        functools.partial(_nsa_b_kernel, tk=tk, nk=nk), grid_spec=grid_spec,
        out_shape=jax.ShapeDtypeStruct((B * T, MIX_WIDTH), BF16),
        compiler_params=_cparams(("parallel", "parallel", "parallel")), name="nsa_selected",
    )(counts.reshape(-1), order.reshape(-1).astype(jnp.int32), zm, zm, vs_t, sel_t, expand_t, alibi_t, srow, cg_g, part)


def _nsa(zm, zs, cmp_w, cmp_pe, B, T):
    G, J = NSA_KV_GROUPS, NSA_GROUP_SIZE
    n = B * T
    cg = zs[:, 2 * N_HEADS:2 * N_HEADS + 3 * N_HEADS].reshape(n, 3, G, J)
    cg_g = jnp.pad(cg.transpose(2, 0, 1, 3).reshape(G, n, 3 * J), ((0, 0), (0, 0), (0, LANES - 3 * J)))
    slopes = jnp.asarray(2.0 ** (-8.0 * (np.arange(N_HEADS, dtype=np.float32) + 1.0) / N_HEADS), F32)
    k_cmp, v_cmp = _compress(zm, cmp_w, cmp_pe, B, T)
    part, sel_t = _nsa_a(zm, k_cmp, v_cmp, cg_g, slopes, B, T)
    return _nsa_b(zm, sel_t, cg_g, part, B, T)


def _merge_kernel(h_ref, y0, y1, y2, y3, g0, g1, g2, g3, b0, b1, b2, b3, o_ref):
    h = h_ref[...]
    acc = None
    for y, gw, bw in ((y0, g0, b0), (y1, g1, b1), (y2, g2, b2), (y3, g3, b3)):
        term = jax.nn.sigmoid(_dot(h, gw[...])) * _dot(y[...], bw[...])
        acc = term if acc is None else acc + term
    o_ref[...] = acc.astype(BF16)


def _merge(h, ys, w_gate, w_branch, tm=512, tn=256):
    n, dm = h.shape
    tm = min(tm, n)
    nj = dm // tn
    y_spec = pl.BlockSpec((tm, MIX_WIDTH), lambda i, j: (i, 0))
    gate_specs = [pl.BlockSpec((dm, tn), functools.partial(lambda i, j, b: (0, b * nj + j), b=b)) for b in range(N_BRANCH)]
    br_specs = [pl.BlockSpec((None, MIX_WIDTH, tn), functools.partial(lambda i, j, b: (b, 0, j), b=b)) for b in range(N_BRANCH)]
    return pl.pallas_call(
        _merge_kernel, grid=(n // tm, nj),
        in_specs=[pl.BlockSpec((tm, dm), lambda i, j: (i, 0))] + [y_spec] * N_BRANCH + gate_specs + br_specs,
        out_specs=pl.BlockSpec((tm, tn), lambda i, j: (i, j)),
        out_shape=jax.ShapeDtypeStruct((n, dm), BF16),
        compiler_params=_cparams(("parallel", "parallel")), name="gated_merge",
    )(h, *ys, *([w_gate] * N_BRANCH), *([w_branch] * N_BRANCH))


def _expert_kernel(be_ref, nact_ref, x_ref, w1_ref, w3_ref, w2_ref, wt_ref, o_ref, w1b, w3b, w2b):
    i = pl.program_id(0)
    e = be_ref[i]
    prev = be_ref[jnp.maximum(i - 1, 0)]

    @pl.when((i == 0) | (prev != e))
    def _():
        w1b[...] = w1_ref[...].astype(BF16)
        w3b[...] = w3_ref[...].astype(BF16)
        w2b[...] = w2_ref[...].astype(BF16)

    @pl.when(i < nact_ref[0])
    def _():
        x = x_ref[...]
        a1 = _dot(x, w1b[...])
        a = (a1 * jax.nn.sigmoid(a1)) * _dot(x, w3b[...])
        o_ref[...] = _dot(a.astype(BF16), w2b[...]) * wt_ref[...]

    @pl.when(i >= nact_ref[0])
    def _():
        o_ref[...] = jnp.zeros_like(o_ref)


def _experts(xg, w_pad, blk_exp, n_active, w1, w3, w2):
    p, dm = xg.shape
    n_blk = p // MOE_BLOCK
    ff = w1.shape[-1]
    grid_spec = pltpu.PrefetchScalarGridSpec(
        num_scalar_prefetch=2, grid=(n_blk,),
        in_specs=[pl.BlockSpec((MOE_BLOCK, dm), lambda i, be, na: (i, 0)),
                  pl.BlockSpec((None, dm, ff), lambda i, be, na: (be[i], 0, 0)),
                  pl.BlockSpec((None, dm, ff), lambda i, be, na: (be[i], 0, 0)),
                  pl.BlockSpec((None, ff, dm), lambda i, be, na: (be[i], 0, 0)),
                  pl.BlockSpec((MOE_BLOCK, 1), lambda i, be, na: (i, 0))],
        out_specs=pl.BlockSpec((MOE_BLOCK, dm), lambda i, be, na: (i, 0)),
        scratch_shapes=[pltpu.VMEM((dm, ff), BF16), pltpu.VMEM((dm, ff), BF16), pltpu.VMEM((ff, dm), BF16)])
    return pl.pallas_call(
        _expert_kernel, grid_spec=grid_spec,
        out_shape=jax.ShapeDtypeStruct((p, dm), F32),
        compiler_params=_cparams(("arbitrary",)), name="moe_experts",
    )(blk_exp, n_active, xg, w1, w3, w2, w_pad.reshape(p, 1))


def _route(logits, bg_unused=None):
    n = logits.shape[0]
    K = TOP_K_IN_GROUP
    grp_logits = logits[:, :N_GROUPS]
    grp_prob = jax.nn.softmax(grp_logits, axis=-1)
    g_idx = jnp.argmax(grp_logits, axis=-1).astype(jnp.int32)
    g_w = jnp.take_along_axis(grp_prob, g_idx[:, None], axis=-1)
    exp_logits = logits[:, N_GROUPS:N_GROUPS + N_EXPERTS].reshape(n, N_GROUPS, EXPERTS_PER_GROUP)
    in_grp = jnp.take_along_axis(exp_logits, g_idx[:, None, None], axis=1)[:, 0]
    top_v, top_i = lax.top_k(in_grp, K)
    comb = jax.nn.softmax(top_v, axis=-1) * g_w
    e_idx = g_idx[:, None] * EXPERTS_PER_GROUP + top_i.astype(jnp.int32)

    nk = n * K
    flat_e = e_idx.reshape(-1)
    flat_tok = jnp.repeat(jnp.arange(n, dtype=jnp.int32), K)
    flat_w = comb.reshape(-1)
    order = jnp.argsort(flat_e)
    se, stok, sw = flat_e[order], flat_tok[order], flat_w[order]
    counts = jnp.bincount(flat_e, length=N_EXPERTS)
    padded = (counts + MOE_BLOCK - 1) // MOE_BLOCK * MOE_BLOCK
    pad_end = jnp.cumsum(padded)
    pad_start = pad_end - padded
    start = jnp.cumsum(counts) - counts
    dest = (pad_start[se] + jnp.arange(nk, dtype=jnp.int32) - start[se]).astype(jnp.int32)
    n_blk = (nk + MOE_BLOCK - 1) // MOE_BLOCK + N_EXPERTS
    p = n_blk * MOE_BLOCK
    tok_pad = jnp.full((p,), n, dtype=jnp.int32).at[dest].set(stok)
    w_pad = jnp.zeros((p,), dtype=F32).at[dest].set(sw)
    blk_exp = jnp.minimum(jnp.searchsorted(pad_end, jnp.arange(n_blk) * MOE_BLOCK, side='right'),
                          N_EXPERTS - 1).astype(jnp.int32)
    n_active = (pad_end[-1] // MOE_BLOCK).astype(jnp.int32).reshape(1)
    slot = jnp.zeros((nk,), jnp.int32).at[order].set(dest)
    return tok_pad, w_pad, blk_exp, n_active, slot.reshape(n, K)


def _moe(h, logits, w1, w3, w2):
    n, dm = h.shape
    tok_pad, w_pad, blk_exp, n_active, slot = _route(logits)
    xg = jnp.take(h, jnp.minimum(tok_pad, n - 1), axis=0, mode="clip")
    y = _experts(xg, w_pad, blk_exp, n_active, w1, w3, w2)
    return jnp.take(y, slot[:, 0], axis=0, mode="clip"), jnp.take(y, slot[:, 1], axis=0, mode="clip")


def _pack_w_in(w):
    cols = lambda i: w[:, COL_OFFSETS[i]:COL_OFFSETS[i + 1]]
    w_main = jnp.concatenate([cols(i) for i in MAIN_ORDER], axis=1).astype(BF16)
    small = jnp.concatenate([cols(I_AI), cols(I_AF), cols(I_CG)], axis=1)
    w_small = jnp.pad(small, ((0, 0), (0, LANES - small.shape[1]))).astype(BF16)
    w_gate = cols(I_MERGE).astype(BF16)
    return w_main, w_small, w_gate


def _router_weights(wg, bg, we, be):
    w = jnp.concatenate([wg, we], axis=1)
    w = jnp.pad(w, ((0, 0), (0, LANES - w.shape[1])))
    hi = w.astype(BF16)
    lo = (w - hi.astype(F32)).astype(BF16)
    b = jnp.concatenate([bg, be])
    b = jnp.pad(b, (0, LANES - b.shape[0])).reshape(1, LANES).astype(F32)
    return hi, lo, b


def kernel(x, w_in, mlstm_conv, mlstm_gate_bias, mlstm_norm, ret_norm, nsa_cmp_w, nsa_cmp_pe, sgu_norm, sgu_w, sgu_b, w_branch, w_out, norm_mix, norm_ffn, router_group_w, router_group_b, router_expert_w, router_expert_b, expert_w1, expert_w3, expert_w2, norm_final):
    B, T, D = x.shape
    n = B * T
    adds = [x.reshape(n, D)]
    for l in range(DEPTH):
        w_main, w_small, w_gate = _pack_w_in(w_in[l])
        res = _norm(adds, norm_mix[l])
        xs, h = (res[0], res[1]) if len(adds) > 1 else (adds[0], res[0])
        zm = _matmul(h, w_main, out_dtype=BF16, name="proj_main")
        zs = _matmul(h, w_small, out_dtype=F32, name="proj_small")
        zst = zs[:, :2 * N_HEADS].T
        ya = _mlstm(zm, zs, zst, mlstm_conv[l], mlstm_gate_bias[l], mlstm_norm[l], B, T)
        yb = _retention(zm, ret_norm[l], B, T)
        yc = _nsa(zm, zs, nsa_cmp_w[l], nsa_cmp_pe[l], B, T)
        yd = _sgu(zm, sgu_norm[l], sgu_w[l], sgu_b[l], n)
        merged = _merge(h, (ya, yb, yc, yd), w_gate, w_branch[l].astype(BF16))
        x_mid = _matmul(merged, w_out[l].astype(BF16), out_dtype=F32, residual=xs, name="proj_out")
        h2, logits = _norm([x_mid], norm_ffn[l],
                           router_w=_router_weights(router_group_w[l], router_group_b[l],
                                                    router_expert_w[l], router_expert_b[l]))
        m0, m1 = _moe(h2, logits, expert_w1[l], expert_w3[l], expert_w2[l])
        adds = [x_mid, m0, m1]
    _, out = _norm(adds, norm_final, final=True)
    return out.reshape(B, T, D)
```

```python
import functools

import numpy as np
import jax
import jax.numpy as jnp
from jax import lax
from jax.experimental import pallas as pl
from jax.experimental.pallas import tpu as pltpu

F32 = jnp.float32
BF16 = jnp.bfloat16

D_MODEL = 4096
DEPTH = 2
HEAD_DIM = 128
N_BRANCH = 4
MIX_WIDTH = D_MODEL // N_BRANCH
N_HEADS = MIX_WIDTH // HEAD_DIM
CONV_WIDTH = 4
MLSTM_CHUNK = 128
RET_CHUNK = 128
NSA_KV_GROUPS = 2
NSA_GROUP_SIZE = N_HEADS // NSA_KV_GROUPS
KV_WIDTH = NSA_KV_GROUPS * HEAD_DIM
CMP_BLOCK = 32
CMP_STRIDE = 16
SEL_BLOCK = 64
SEL_TOPN = 16
WINDOW = 512
Q_BLOCK = 128
SGU_CHUNK = 128
N_GROUPS = 8
EXPERTS_PER_GROUP = 8
N_EXPERTS = N_GROUPS * EXPERTS_PER_GROUP
TOP_K_IN_GROUP = 2
EXPERT_FF = 256
MOE_BLOCK = 128
RMS_EPS = 1e-6
NEG_INF = -1e30
FORCE_SCORE = 1e9

COL_WIDTHS = (
    MIX_WIDTH, MIX_WIDTH, MIX_WIDTH, MIX_WIDTH, N_HEADS, N_HEADS,
    MIX_WIDTH, MIX_WIDTH, MIX_WIDTH, MIX_WIDTH,
    MIX_WIDTH, KV_WIDTH, KV_WIDTH, KV_WIDTH, KV_WIDTH, KV_WIDTH, KV_WIDTH, 3 * N_HEADS,
    MIX_WIDTH, MIX_WIDTH,
    N_BRANCH * D_MODEL,
)
COL_OFFSETS = tuple(int(v) for v in np.concatenate([[0], np.cumsum(COL_WIDTHS)]))
(I_AQ, I_AK, I_AV, I_AO, I_AI, I_AF, I_BQ, I_BK, I_BV, I_BG, I_CQ, I_CKC, I_CVC, I_CKS, I_CVS,
 I_CKW, I_CVW, I_CG, I_DU, I_DV, I_MERGE) = range(21)

MAIN_ORDER = (I_AQ, I_AK, I_AV, I_AO, I_BQ, I_BK, I_BV, I_BG, I_CQ, I_DU, I_DV,
              I_CKC, I_CVC, I_CKS, I_CVS, I_CKW, I_CVW)
MAIN_WIDTH = sum(COL_WIDTHS[i] for i in MAIN_ORDER)
LANES = 128
BLK_AQ, BLK_AK, BLK_AV, BLK_AO, BLK_BQ, BLK_BK, BLK_BV, BLK_BG, BLK_CQ, BLK_DU, BLK_DV = range(11)
KV_BASE = 11 * MIX_WIDTH // HEAD_DIM
VMEM_LIMIT = 48 * 1024 * 1024


def _cparams(sem, vmem=VMEM_LIMIT):
    return pltpu.CompilerParams(dimension_semantics=sem, vmem_limit_bytes=vmem)


def _dot(a, b):
    return jnp.dot(a, b, preferred_element_type=F32)


def _dot_nt(a, b):
    return lax.dot_general(a, b, (((1,), (1,)), ((), ())), preferred_element_type=F32)


def _split3(x):
    hi = x.astype(BF16)
    r1 = x - hi.astype(F32)
    mid = r1.astype(BF16)
    lo = (r1 - mid.astype(F32)).astype(BF16)
    return hi, mid, lo


def _dot_exact_rhs(x, m_bf16):
    hi, mid, lo = _split3(x)
    return _dot(hi, m_bf16) + _dot(mid, m_bf16) + _dot(lo, m_bf16)


def _log_sigmoid(x):
    return jnp.minimum(x, 0.0) - jnp.log1p(jnp.exp(-jnp.abs(x)))


def _norm_kernel(*refs, moe, router, final):
    x = refs[0][...]
    pos = 1
    if moe:
        m0_ref, m1_ref, rt_ref = refs[pos:pos + 3]
        pos += 3
        x = x + (rt_ref[:, 2:3] * m0_ref[...] + rt_ref[:, 3:4] * m1_ref[...])
    g_ref = refs[pos]
    pos += 1
    if router:
        whi_ref, wlo_ref, rb_ref = refs[pos:pos + 3]
        pos += 3
    outs = refs[pos:]
    y = x * lax.rsqrt(jnp.mean(x * x, axis=-1, keepdims=True) + RMS_EPS) * g_ref[...]
    o = 0
    if moe and not final:
        outs[o][...] = x
        o += 1
    if final:
        outs[o][...] = y
    else:
        outs[o][...] = y.astype(BF16)
    o += 1
    if router:
        y_hi = y.astype(BF16)
        y_lo = (y - y_hi.astype(F32)).astype(BF16)
        lg = _dot(y_hi, whi_ref[...]) + (_dot(y_hi, wlo_ref[...]) + _dot(y_lo, whi_ref[...])) + rb_ref[...]
        outs[o][...] = _route_rows(lg)


def _first_max(v, lane):
    mx = jnp.max(v, axis=-1, keepdims=True)
    return mx, jnp.min(jnp.where(v == mx, lane, LANES), axis=-1, keepdims=True)


def _route_rows(lg):
    low = -3.0e38
    lane = lax.broadcasted_iota(jnp.int32, lg.shape, 1)
    is_grp = lane < N_GROUPS
    gmax, g_idx = _first_max(jnp.where(is_grp, lg, low), lane)
    g_w = 1.0 / jnp.sum(jnp.where(is_grp, jnp.exp(lg - gmax), 0.0), axis=-1, keepdims=True)
    assert EXPERTS_PER_GROUP & (EXPERTS_PER_GROUP - 1) == 0
    grp_of_lane = lax.shift_right_arithmetic(lane - N_GROUPS, EXPERTS_PER_GROUP.bit_length() - 1)
    in_grp = (lane >= N_GROUPS) & (lane < N_GROUPS + N_EXPERTS) & (grp_of_lane == g_idx)
    el = jnp.where(in_grp, lg, low)
    v0, i0 = _first_max(el, lane)
    v1, i1 = _first_max(jnp.where(lane == i0, low, el), lane)
    e1 = jnp.exp(v1 - v0)
    p0 = g_w / (1.0 + e1)
    vals = (i0 - N_GROUPS).astype(F32), (i1 - N_GROUPS).astype(F32), p0, p0 * e1
    out = jnp.zeros(lg.shape, F32)
    for k, v in enumerate(vals):
        out = jnp.where(lane == k, v, out)
    return out


def _norm(x, gain, *, moe=None, router_w=None, final=False, rows=128):
    n, d = x.shape
    rows = min(rows, n)
    row_spec = pl.BlockSpec((rows, d), lambda i: (i, 0))
    lane_spec = pl.BlockSpec((rows, LANES), lambda i: (i, 0))
    in_specs, args = [row_spec], [x]
    if moe is not None:
        in_specs += [row_spec, row_spec, lane_spec]
        args += list(moe)
    in_specs.append(pl.BlockSpec((1, d), lambda i: (0, 0)))
    args.append(gain.reshape(1, d))
    out_shape, out_specs = [], []
    if moe is not None and not final:
        out_shape.append(jax.ShapeDtypeStruct((n, d), F32))
        out_specs.append(row_spec)
    out_shape.append(jax.ShapeDtypeStruct((n, d), F32 if final else BF16))
    out_specs.append(row_spec)
    if router_w is not None:
        whi, wlo, rb = router_w
        in_specs += [pl.BlockSpec((d, LANES), lambda i: (0, 0))] * 2 + [pl.BlockSpec((1, LANES), lambda i: (0, 0))]
        args += [whi, wlo, rb]
        out_shape.append(jax.ShapeDtypeStruct((n, LANES), F32))
        out_specs.append(lane_spec)
    return pl.pallas_call(
        functools.partial(_norm_kernel, moe=moe is not None, router=router_w is not None, final=final),
        grid=(n // rows,), in_specs=in_specs, out_specs=out_specs, out_shape=out_shape,
        compiler_params=_cparams(("parallel",)), name="rmsnorm",
    )(*args)


def _mm_kernel(a_ref, w_ref, *rest, has_res):
    acc = _dot(a_ref[...], w_ref[...])
    if has_res:
        acc = acc + rest[0][...]
    rest[-1][...] = acc.astype(rest[-1].dtype)


def _matmul(a, w, *, out_dtype, residual=None, tm=1024, tn=512, name="matmul"):
    m, k = a.shape
    nc = w.shape[1]
    tm, tn = min(tm, m), min(tn, nc)
    in_specs = [pl.BlockSpec((tm, k), lambda i, j: (i, 0)), pl.BlockSpec((k, tn), lambda i, j: (0, j))]
    args = [a, w]
    if residual is not None:
        in_specs.append(pl.BlockSpec((tm, tn), lambda i, j: (i, j)))
        args.append(residual)
    return pl.pallas_call(
        functools.partial(_mm_kernel, has_res=residual is not None),
        grid=(m // tm, nc // tn), in_specs=in_specs,
        out_specs=pl.BlockSpec((tm, tn), lambda i, j: (i, j)),
        out_shape=jax.ShapeDtypeStruct((m, nc), out_dtype),
        compiler_params=_cparams(("parallel", "parallel")), name=name,
    )(*args)


def _mlstm_kernel(aq_ref, ak_ref, av_ref, ao_ref, zs_ref, zst_ref, cw_ref, gbc_ref, gbr_ref, gain_ref,
                  tril_ref, triu_ref, o_ref, xbuf, qk_s, ct_s, n_s, m_s):
    L, d, H, W = MLSTM_CHUNK, HEAD_DIM, N_HEADS, MIX_WIDTH
    c = pl.program_id(1)

    @pl.when(c == 0)
    def _():
        xbuf[0:8, :] = jnp.zeros((8, 2 * W), F32)
        ct_s[...] = jnp.zeros_like(ct_s)
        n_s[...] = jnp.zeros_like(n_s)
        m_s[...] = jnp.zeros_like(m_s)

    @pl.when(c > 0)
    def _():
        xbuf[0:8, :] = xbuf[L:L + 8, :]

    xbuf[8:L + 8, 0:W] = aq_ref[...].astype(F32)
    xbuf[8:L + 8, W:2 * W] = ak_ref[...].astype(F32)
    base = 8 - (CONV_WIDTH - 1)
    conv = cw_ref[0:1, :] * xbuf[base:base + L, :]
    for j in range(1, CONV_WIDTH):
        conv = conv + cw_ref[j:j + 1, :] * xbuf[base + j:base + j + L, :]
    qk_s[...] = conv * jax.nn.sigmoid(conv)

    pre_c = zs_ref[...] + gbc_ref[...]
    ls_c = _log_sigmoid(pre_c)
    hi, mid, lo = _split3(ls_c)
    tril = tril_ref[...]
    bc = _dot(tril, hi) + _dot(tril, mid) + _dot(tril, lo)
    pre_r = zst_ref[...] + gbr_ref[...]
    br = _dot_exact_rhs(_log_sigmoid(pre_r), triu_ref[...])

    row = lax.broadcasted_iota(jnp.int32, (L, L), 0)
    col = lax.broadcasted_iota(jnp.int32, (L, L), 1)
    causal = row >= col
    scale = d ** -0.5
    for h in range(H):
        sl = slice(h * d, (h + 1) * d)
        q = qk_s[:, sl] * scale
        k = qk_s[:, W + h * d:W + (h + 1) * d]
        v = av_ref[:, sl]
        i_col, b_col = pre_c[:, h:h + 1], bc[:, H + h:H + h + 1]
        i_row, b_row = pre_r[h:h + 1, :], br[H + h:H + h + 1, :]
        m_prev = m_s[h:h + 1, 0:1]
        log_d = jnp.where(causal, b_col - b_row + i_row, NEG_INF)
        log_inter = b_col + m_prev
        m_row = jnp.maximum(jnp.max(log_d, axis=-1, keepdims=True), log_inter)
        qb, kb = q.astype(BF16), k.astype(BF16)
        s = _dot_nt(qb, kb) * jnp.exp(log_d - m_row)
        w_inter = jnp.exp(log_inter - m_row)
        num = _dot(s.astype(BF16), v) + w_inter * _dot(qb, ct_s[h].astype(BF16))
        den = jnp.sum(s, axis=-1, keepdims=True) + w_inter * jnp.sum(q * n_s[h:h + 1, :], axis=-1, keepdims=True)
        hh = num / jnp.maximum(jnp.abs(den), jnp.exp(-m_row))
        b_last = b_row[:, L - 1:L]
        log_w_row = b_last - b_row + i_row
        m_new = jnp.maximum(b_last + m_prev, jnp.max(log_w_row, axis=-1, keepdims=True))
        w_col = jnp.exp(b_last - b_col + i_col - m_new)
        decay = jnp.exp(b_last + m_prev - m_new)
        vw = (v.astype(F32) * w_col).astype(BF16)
        ct_s[h] = decay * ct_s[h] + _dot(k.T.astype(BF16), vw)
        n_s[h:h + 1, :] = decay * n_s[h:h + 1, :] + jnp.sum(k * w_col, axis=0, keepdims=True)
        m_s[h:h + 1, :] = jnp.broadcast_to(m_new, (1, LANES))
        hn = hh * lax.rsqrt(jnp.mean(hh * hh, axis=-1, keepdims=True) + RMS_EPS)
        o_ref[:, sl] = (hn * gain_ref[:, sl] * jax.nn.sigmoid(ao_ref[:, sl].astype(F32))).astype(BF16)


def _tri_consts(L):
    r = np.arange(L)
    tril = (r[:, None] >= r[None, :]).astype(np.float32)
    return jnp.asarray(tril, BF16), jnp.asarray(tril.T, BF16)


def _mlstm(zm, zs, zst, conv_w, gate_bias, gain, B, T):
    L, W = MLSTM_CHUNK, MIX_WIDTH
    nc = T // L
    n = B * T
    gbc = jnp.zeros((1, LANES), F32).at[0, :2 * N_HEADS].set(gate_bias.reshape(-1))
    gbr = jnp.broadcast_to(gate_bias.reshape(2 * N_HEADS, 1), (2 * N_HEADS, L)).astype(F32)
    tril, triu = _tri_consts(L)

    def blk(b_idx):
        return pl.BlockSpec((L, W), lambda b, c: (b * nc + c, b_idx))

    const = lambda shape: pl.BlockSpec(shape, lambda b, c: (0,) * len(shape))
    return pl.pallas_call(
        _mlstm_kernel, grid=(B, nc),
        in_specs=[blk(BLK_AQ), blk(BLK_AK), blk(BLK_AV), blk(BLK_AO),
                  pl.BlockSpec((L, LANES), lambda b, c: (b * nc + c, 0)),
                  pl.BlockSpec((2 * N_HEADS, L), lambda b, c: (0, b * nc + c)),
                  const((CONV_WIDTH, 2 * W)), const((1, LANES)), const((2 * N_HEADS, L)), const((1, W)),
                  const((L, L)), const((L, L))],
        out_specs=pl.BlockSpec((L, W), lambda b, c: (b * nc + c, 0)),
        out_shape=jax.ShapeDtypeStruct((n, W), BF16),
        scratch_shapes=[pltpu.VMEM((L + 8, 2 * W), F32), pltpu.VMEM((L, 2 * W), F32),
                        pltpu.VMEM((N_HEADS, HEAD_DIM, HEAD_DIM), F32), pltpu.VMEM((N_HEADS, HEAD_DIM), F32),
                        pltpu.VMEM((N_HEADS, LANES), F32)],
        compiler_params=_cparams(("parallel", "arbitrary")), name="mlstm",
    )(zm, zm, zm, zm, zs, zst, conv_w, gbc, gbr, gain.reshape(1, W), tril, triu)


def _ret_kernel(q_ref, k_ref, v_ref, g_ref, dec_ref, qd_ref, kd_ref, gain_ref, o_ref, r_s, *, chunk_decay):
    L, d, H = RET_CHUNK, HEAD_DIM, N_HEADS
    c = pl.program_id(1)

    @pl.when(c == 0)
    def _():
        r_s[...] = jnp.zeros_like(r_s)

    scale = d ** -0.5
    for h in range(H):
        sl = slice(h * d, (h + 1) * d)
        q = q_ref[:, sl].astype(F32)
        k = k_ref[:, sl].astype(F32) * scale
        v = v_ref[:, sl]
        s = _dot_nt(q.astype(BF16), k.astype(BF16)) * dec_ref[h]
        o = _dot(s.astype(BF16), v) + _dot((q * qd_ref[:, sl]).astype(BF16), r_s[h].astype(BF16))
        kd = (k * kd_ref[:, sl]).T.astype(BF16)
        r_s[h] = chunk_decay[h] * r_s[h] + _dot(kd, v)
        on = o * lax.rsqrt(jnp.mean(o * o, axis=-1, keepdims=True) + RMS_EPS)
        gate = g_ref[:, sl].astype(F32)
        o_ref[:, sl] = (on * gain_ref[:, sl] * (gate * jax.nn.sigmoid(gate))).astype(BF16)


def _retention(zm, gain, B, T):
    L, W, H, d = RET_CHUNK, MIX_WIDTH, N_HEADS, HEAD_DIM
    nc = T // L
    log_gamma = np.log(np.float32(1.0) - np.float32(2.0) ** (-5.0 - np.arange(H, dtype=np.float32))).astype(np.float32)
    pos = np.arange(L, dtype=np.float32)
    diff = pos[:, None] - pos[None, :]
    decay = np.where(diff >= 0, np.exp(log_gamma[:, None, None] * np.maximum(diff, 0.0)), 0.0).astype(np.float32)
    q_decay = np.exp(log_gamma[:, None] * (pos + 1.0)).astype(np.float32)
    k_decay = np.exp(log_gamma[:, None] * (L - 1.0 - pos)).astype(np.float32)
    chunk_decay = tuple(float(v) for v in np.exp(log_gamma * np.float32(L)).astype(np.float32))
    qd = jnp.asarray(np.repeat(q_decay.T, d, axis=1))
    kd = jnp.asarray(np.repeat(k_decay.T, d, axis=1))

    def blk(b_idx):
        return pl.BlockSpec((L, W), lambda b, c: (b * nc + c, b_idx))

    const = lambda shape: pl.BlockSpec(shape, lambda b, c: (0,) * len(shape))
    return pl.pallas_call(
        functools.partial(_ret_kernel, chunk_decay=chunk_decay), grid=(B, nc),
        in_specs=[blk(BLK_BQ), blk(BLK_BK), blk(BLK_BV), blk(BLK_BG),
                  const((H, L, L)), const((L, W)), const((L, W)), const((1, W))],
        out_specs=pl.BlockSpec((L, W), lambda b, c: (b * nc + c, 0)),
        out_shape=jax.ShapeDtypeStruct((B * T, W), BF16),
        scratch_shapes=[pltpu.VMEM((H, d, d), F32)],
        compiler_params=_cparams(("parallel", "arbitrary")), name="retention",
    )(zm, zm, zm, zm, jnp.asarray(decay), qd, kd, gain.reshape(1, W))


def _gelu(x):
    return 0.5 * x * (1.0 + jnp.tanh(0.7978845608028654 * (x + 0.044715 * (x * x * x))))


def _sgu_kernel(u_ref, v_ref, ng_ref, w_ref, b_ref, o_ref):
    L, d, H = SGU_CHUNK, HEAD_DIM, N_HEADS
    v = _gelu(v_ref[...].astype(F32))
    vn = (v * lax.rsqrt(jnp.mean(v * v, axis=-1, keepdims=True) + RMS_EPS) * ng_ref[...]).astype(BF16)
    u = _gelu(u_ref[...].astype(F32))
    row = lax.broadcasted_iota(jnp.int32, (L, L), 0)
    col = lax.broadcasted_iota(jnp.int32, (L, L), 1)
    for g in range(H):
        sl = slice(g * d, (g + 1) * d)
        wm = jnp.where(row >= col, w_ref[g], 0.0).astype(BF16)
        mixed = _dot(wm, vn[:, sl]) + b_ref[:, sl]
        o_ref[:, sl] = (u[:, sl] * mixed).astype(BF16)


def _sgu(zm, norm_g, w_s, b_s, n):
    L, W, H, d = SGU_CHUNK, MIX_WIDTH, N_HEADS, HEAD_DIM
    bsb = jnp.repeat(b_s.T, d, axis=1)
    const = lambda shape: pl.BlockSpec(shape, lambda i: (0,) * len(shape))
    return pl.pallas_call(
        _sgu_kernel, grid=(n // L,),
        in_specs=[pl.BlockSpec((L, W), lambda i: (i, BLK_DU)), pl.BlockSpec((L, W), lambda i: (i, BLK_DV)),
                  const((1, W)), const((H, L, L)), const((L, W))],
        out_specs=pl.BlockSpec((L, W), lambda i: (i, 0)),
        out_shape=jax.ShapeDtypeStruct((n, W), BF16),
        compiler_params=_cparams(("parallel",)), name="sgu",
    )(zm, zm, norm_g.reshape(1, W), w_s, bsb)


def _compress_kernel(k_ref, v_ref, w_ref, pe_ref, ko_ref, vo_ref, buf):
    t = k_ref.shape[0]
    n16 = t // CMP_STRIDE
    buf[t:t + CMP_STRIDE, :] = jnp.zeros((CMP_STRIDE, HEAD_DIM), F32)
    for which, (src, dst) in enumerate(((k_ref, ko_ref), (v_ref, vo_ref))):
        buf[0:t, :] = src[...].astype(F32)
        acc = jnp.zeros((n16, HEAD_DIM), F32)
        for l in range(CMP_BLOCK):
            x = buf[pl.ds(l, n16, stride=CMP_STRIDE), :]
            acc = acc + _dot((x + pe_ref[which, l:l + 1, :]).astype(BF16), w_ref[which, l].astype(BF16))
        dst[...] = acc.astype(BF16)


def _compress(zm, cmp_w, cmp_pe, B, T):
    G, d = NSA_KV_GROUPS, HEAD_DIM
    n16 = T // CMP_STRIDE
    out = jax.ShapeDtypeStruct((B, G, n16, d), BF16)
    kblk = KV_BASE
    vblk = KV_BASE + G
    return pl.pallas_call(
        _compress_kernel, grid=(B, G),
        in_specs=[pl.BlockSpec((T, d), lambda b, g: (b, kblk + g)), pl.BlockSpec((T, d), lambda b, g: (b, vblk + g)),
                  pl.BlockSpec((2, CMP_BLOCK, d, d), lambda b, g: (0, 0, 0, 0)),
                  pl.BlockSpec((2, CMP_BLOCK, d), lambda b, g: (0, 0, 0))],
        out_specs=[pl.BlockSpec((None, None, n16, d), lambda b, g: (b, g, 0, 0))] * 2,
        out_shape=[out, out],
        scratch_shapes=[pltpu.VMEM((T + CMP_STRIDE, d), F32)],
        compiler_params=_cparams(("parallel", "parallel")), name="nsa_compress",
    )(zm, zm, cmp_w, cmp_pe)


def _softmax_rows(s):
    m = jnp.max(s, axis=-1, keepdims=True)
    p = jnp.exp(s - m)
    return p, jnp.sum(p, axis=-1, keepdims=True)


def _nsa_a_kernel(slope_ref, q_ref, kc_ref, vc_ref, *rest, n_back):
    nw = n_back + 1
    kw_refs, vw_refs = rest[:nw], rest[nw:2 * nw]
    cg_ref, ov_ref, part_ref, sel_ref, key_s = rest[2 * nw:]
    tq, d, J = Q_BLOCK, HEAD_DIM, NSA_GROUP_SIZE
    g, qi = pl.program_id(1), pl.program_id(2)
    t0 = qi * tq
    ncmp = kc_ref.shape[0]
    scale = d ** -0.5
    qs = [(q_ref[:, j * d:(j + 1) * d].astype(F32) * scale).astype(BF16) for j in range(J)]
    sg = jax.nn.sigmoid(cg_ref[...])

    r = lax.broadcasted_iota(jnp.int32, (tq, ncmp), 0)
    n_idx = lax.broadcasted_iota(jnp.int32, (tq, ncmp), 1)
    dist_i = t0 + r - CMP_STRIDE * n_idx - (CMP_BLOCK - 1)
    valid_c = dist_i >= 0
    dist_c = dist_i.astype(F32)
    t_col = t0 + lax.broadcasted_iota(jnp.int32, (tq, 1), 0)
    has_c = (t_col >= CMP_BLOCK - 1).astype(F32)
    kc, vc = kc_ref[...], vc_ref[...]
    psum = jnp.zeros((tq, ncmp), F32)
    o_cmp = []
    for j in range(J):
        slope = slope_ref[g * J + j]
        s = jnp.where(valid_c, _dot_nt(qs[j], kc) - slope * dist_c, NEG_INF)
        p, l = _softmax_rows(s)
        p = p * (has_c / l)
        o_cmp.append(_dot(p.astype(BF16), vc))
        psum = psum + p

    imp = _dot_exact_rhs(psum, ov_ref[...])
    blk = lax.broadcasted_iota(jnp.int32, (tq, LANES), 1)
    cur = (t0 + lax.broadcasted_iota(jnp.int32, (tq, LANES), 0)) // SEL_BLOCK
    valid_s = blk <= cur
    forced = (blk == 0) | (blk == cur) | (blk == cur - 1)
    key = jnp.where(valid_s, imp + jnp.where(forced, FORCE_SCORE, 0.0), NEG_INF)
    key_s[...] = key.T
    sub = lax.broadcasted_iota(jnp.int32, (8, tq), 0)
    groups = LANES // 8
    key_g = [key_s[8 * i:8 * i + 8, :] for i in range(groups)]
    rank = [jnp.zeros((8, tq), F32) for _ in range(groups)]
    for mp in range(LANES):
        rowv = jnp.broadcast_to(key_s[mp:mp + 1, :], (8, tq))
        for i in range(groups):
            if 8 * i > mp:
                ahead = rowv >= key_g[i]
            elif 8 * i + 7 <= mp:
                ahead = rowv > key_g[i]
            else:
                ahead = (rowv > key_g[i]) | ((sub > mp - 8 * i) & (rowv == key_g[i]))
            rank[i] = rank[i] + jnp.where(ahead, 1.0, 0.0)
    rank_t = jnp.concatenate(rank, axis=0)
    blk_t = lax.broadcasted_iota(jnp.int32, (LANES, tq), 0)
    cur_t = (t0 + lax.broadcasted_iota(jnp.int32, (LANES, tq), 1)) // SEL_BLOCK
    sel_ref[...] = jnp.where((rank_t < float(SEL_TOPN)) & (blk_t <= cur_t), 1.0, 0.0).astype(BF16)

    kw = jnp.concatenate([kr[...] for kr in kw_refs], axis=0)
    vw = jnp.concatenate([vr[...] for vr in vw_refs], axis=0)
    span = nw * tq
    rw = lax.broadcasted_iota(jnp.int32, (tq, span), 0)
    cw = lax.broadcasted_iota(jnp.int32, (tq, span), 1)
    dist_wi = rw + n_back * tq - cw
    mask_w = (dist_wi >= 0) & (dist_wi < WINDOW) & (t0 - n_back * tq + cw >= 0)
    dist_w = dist_wi.astype(F32)
    for j in range(J):
        slope = slope_ref[g * J + j]
        s = jnp.where(mask_w, _dot_nt(qs[j], kw) - slope * dist_w, NEG_INF)
        p, l = _softmax_rows(s)
        o_win = _dot((p / l).astype(BF16), vw)
        part_ref[:, j * d:(j + 1) * d] = sg[:, j:j + 1] * o_cmp[j] + sg[:, 2 * J + j:2 * J + j + 1] * o_win


def _nsa_consts(T):
    n16 = T // CMP_STRIDE
    n_sel = T // SEL_BLOCK
    n = np.arange(n16)[:, None]
    m = np.arange(LANES)[None, :]
    c_start, s_start = n * CMP_STRIDE, m * SEL_BLOCK
    overlap = ((c_start < s_start + SEL_BLOCK) & (c_start + CMP_BLOCK > s_start) & (n < n16 - 1) & (m < n_sel))
    return jnp.asarray(overlap.astype(np.float32), BF16)


def _nsa_a(zm, k_cmp, v_cmp, cg_g, slopes, B, T):
    G, J, d, tq = NSA_KV_GROUPS, NSA_GROUP_SIZE, HEAD_DIM, Q_BLOCK
    nq = T // tq
    n16 = T // CMP_STRIDE
    n_back = WINDOW // tq
    qblk = BLK_CQ * (MIX_WIDTH // (J * d))
    kwblk = KV_BASE + 4 * G
    vwblk = KV_BASE + 5 * G

    def win_spec(base, i):
        return pl.BlockSpec((tq, d), lambda b, g, qi: (b * nq + jnp.maximum(qi - n_back + i, 0), base + g))

    in_specs = ([pl.BlockSpec(memory_space=pltpu.SMEM),
                 pl.BlockSpec((tq, J * d), lambda b, g, qi: (b * nq + qi, qblk + g)),
                 pl.BlockSpec((None, None, n16, d), lambda b, g, qi: (b, g, 0, 0)),
                 pl.BlockSpec((None, None, n16, d), lambda b, g, qi: (b, g, 0, 0))]
                + [win_spec(kwblk, i) for i in range(n_back + 1)]
                + [win_spec(vwblk, i) for i in range(n_back + 1)]
                + [pl.BlockSpec((None, tq, LANES), lambda b, g, qi: (g, b * nq + qi, 0)),
                   pl.BlockSpec((n16, LANES), lambda b, g, qi: (0, 0))])
    return pl.pallas_call(
        functools.partial(_nsa_a_kernel, n_back=n_back), grid=(B, G, nq), in_specs=in_specs,
        out_specs=[pl.BlockSpec((tq, J * d), lambda b, g, qi: (b * nq + qi, g)),
                   pl.BlockSpec((None, None, LANES, tq), lambda b, g, qi: (b, g, 0, qi))],
        out_shape=[jax.ShapeDtypeStruct((B * T, MIX_WIDTH), F32), jax.ShapeDtypeStruct((B, G, LANES, T), BF16)],
        scratch_shapes=[pltpu.VMEM((LANES, tq), F32)],
        compiler_params=_cparams(("parallel", "parallel", "parallel")), name="nsa_cmp_win",
    )(slopes, zm, k_cmp, v_cmp, *([zm] * (2 * (n_back + 1))), cg_g, _nsa_consts(T))


def _nsa_b_kernel(cnt_ref, lst_ref, q_ref, ks_ref, vt_ref, sel_ref, e_ref, a_ref, srow_ref, cg_ref, part_ref, o_ref,
                  q_s, acc_s, *, tk, nk):
    tq, d, J = Q_BLOCK, HEAD_DIM, NSA_GROUP_SIZE
    b, g, qi = pl.program_id(0), pl.program_id(1), pl.program_id(2)
    lin = (b * pl.num_programs(1) + g) * pl.num_programs(2) + qi
    t0 = qi * tq
    scale = d ** -0.5
    for j in range(J):
        q_s[j * tq:(j + 1) * tq, :] = (q_ref[:, j * d:(j + 1) * d].astype(F32) * scale).astype(BF16)
    acc_s[...] = jnp.zeros_like(acc_s)
    r_minus_c = lax.broadcasted_iota(jnp.int32, (tk, tq), 1) - lax.broadcasted_iota(jnp.int32, (tk, tq), 0)

    def body(i, carry):
        m_prev, l_prev = carry
        kt = lst_ref[lin * nk + i]
        k_t = ks_ref[pl.ds(pl.multiple_of(kt * tk, tk), tk), :]
        off = t0 - kt * tk
        picked = _dot(e_ref[kt], sel_ref[...])
        bias = jnp.where((picked > 0.5) & (r_minus_c + off >= 0), 0.0, NEG_INF)
        s = _dot_nt(k_t, q_s[...]) - a_ref[...] + jnp.concatenate([bias] * J, axis=1)
        shift = srow_ref[...] * off.astype(F32)
        m_new = jnp.maximum(m_prev, jnp.max(s, axis=0, keepdims=True) - shift)
        p = jnp.exp(s - (m_new + shift))
        alpha = jnp.exp(m_prev - m_new)
        acc_s[...] = alpha * acc_s[...] + _dot(vt_ref[kt], p.astype(BF16))
        return m_new, alpha * l_prev + jnp.sum(p, axis=0, keepdims=True)

    init = (jnp.full((1, J * tq), NEG_INF, F32), jnp.zeros((1, J * tq), F32))
    _, l = lax.fori_loop(0, cnt_ref[lin], body, init)
    inv_l = 1.0 / l
    sg = jax.nn.sigmoid(cg_ref[...])
    for j in range(J):
        cols = slice(j * tq, (j + 1) * tq)
        o_sel = (acc_s[:, cols] * inv_l[:, cols]).T
        o_ref[:, j * d:(j + 1) * d] = (part_ref[:, j * d:(j + 1) * d] + sg[:, J + j:J + j + 1] * o_sel).astype(BF16)


def _nsa_b(zm, sel_t, cg_g, part, B, T, tk=512):
    G, J, d, tq = NSA_KV_GROUPS, NSA_GROUP_SIZE, HEAD_DIM, Q_BLOCK
    tk = min(tk, T)
    nq, nk = T // tq, T // tk
    qblk = BLK_CQ * (MIX_WIDTH // (J * d))
    ksblk = KV_BASE + 2 * G
    vs_col = (KV_BASE + 3 * G) * d
    per_tile = tk // SEL_BLOCK
    kt = np.arange(nk)[:, None, None]
    cc = np.arange(tk)[None, :, None]
    m = np.arange(LANES)[None, None, :]
    expand_t = jnp.asarray((m == kt * per_tile + cc // SEL_BLOCK).astype(np.float32), BF16)
    slopes_np = (2.0 ** (-8.0 * (np.arange(N_HEADS, dtype=np.float32) + 1.0) / N_HEADS)).astype(np.float32)
    r_minus_c = (np.arange(tq)[None, :] - np.arange(tk)[:, None]).astype(np.float32)
    alibi_t = jnp.asarray((slopes_np.reshape(G, 1, J, 1) * r_minus_c[None, :, None, :]).reshape(G, tk, J * tq))
    srow = jnp.asarray(np.repeat(slopes_np.reshape(G, J), tq, axis=1).reshape(G, 1, J * tq))
    vs_t = zm[:, vs_col:vs_col + G * d].reshape(B, nk, tk, G, d).transpose(0, 3, 1, 4, 2)

    active = sel_t[:, :, :nk * per_tile].reshape(B, G, nk, per_tile, nq, tq).max(axis=(3, 5)) > 0
    active = active.transpose(0, 1, 3, 2)
    tiles = jnp.arange(nk, dtype=jnp.int32)
    order = jnp.sort(jnp.where(active, tiles, tiles + nk), axis=-1) % nk
    counts = active.sum(axis=-1).astype(jnp.int32)

    grid_spec = pltpu.PrefetchScalarGridSpec(
        num_scalar_prefetch=2, grid=(B, G, nq),
        in_specs=[pl.BlockSpec((tq, J * d), lambda b, g, qi, c, o: (b * nq + qi, qblk + g)),
                  pl.BlockSpec((T, d), lambda b, g, qi, c, o: (b, ksblk + g)),
                  pl.BlockSpec((None, None, nk, d, tk), lambda b, g, qi, c, o: (b, g, 0, 0, 0)),
                  pl.BlockSpec((None, None, LANES, tq), lambda b, g, qi, c, o: (b, g, 0, qi)),
                  pl.BlockSpec((nk, tk, LANES), lambda b, g, qi, c, o: (0, 0, 0)),
                  pl.BlockSpec((None, tk, J * tq), lambda b, g, qi, c, o: (g, 0, 0)),
                  pl.BlockSpec((None, 1, J * tq), lambda b, g, qi, c, o: (g, 0, 0)),
                  pl.BlockSpec((None, tq, LANES), lambda b, g, qi, c, o: (g, b * nq + qi, 0)),
                  pl.BlockSpec((tq, J * d), lambda b, g, qi, c, o: (b * nq + qi, g))],
        out_specs=pl.BlockSpec((tq, J * d), lambda b, g, qi, c, o: (b * nq + qi, g)),
        scratch_shapes=[pltpu.VMEM((J * tq, d), BF16), pltpu.VMEM((d, J * tq), F32)])
    return pl.pallas_call(
        functools.partial(_nsa_b_kernel, tk=tk, nk=nk), grid_spec=grid_spec,
        out_shape=jax.ShapeDtypeStruct((B * T, MIX_WIDTH), BF16),
        compiler_params=_cparams(("parallel", "parallel", "parallel")), name="nsa_selected",
    )(counts.reshape(-1), order.reshape(-1).astype(jnp.int32), zm, zm, vs_t, sel_t, expand_t, alibi_t, srow, cg_g, part)


def _nsa(zm, zs, cmp_w, cmp_pe, B, T):
    G, J = NSA_KV_GROUPS, NSA_GROUP_SIZE
    n = B * T
    cg = zs[:, 2 * N_HEADS:2 * N_HEADS + 3 * N_HEADS].reshape(n, 3, G, J)
    cg_g = jnp.pad(cg.transpose(2, 0, 1, 3).reshape(G, n, 3 * J), ((0, 0), (0, 0), (0, LANES - 3 * J)))
    slopes = jnp.asarray(2.0 ** (-8.0 * (np.arange(N_HEADS, dtype=np.float32) + 1.0) / N_HEADS), F32)
    k_cmp, v_cmp = _compress(zm, cmp_w, cmp_pe, B, T)
    part, sel_t = _nsa_a(zm, k_cmp, v_cmp, cg_g, slopes, B, T)
    return _nsa_b(zm, sel_t, cg_g, part, B, T)


def _merge_kernel(h_ref, y0, y1, y2, y3, g0, g1, g2, g3, b0, b1, b2, b3, o_ref):
    h = h_ref[...]
    acc = None
    for y, gw, bw in ((y0, g0, b0), (y1, g1, b1), (y2, g2, b2), (y3, g3, b3)):
        term = jax.nn.sigmoid(_dot(h, gw[...])) * _dot(y[...], bw[...])
        acc = term if acc is None else acc + term
    o_ref[...] = acc.astype(BF16)


def _merge(h, ys, w_gate, w_branch, tm=512, tn=256):
    n, dm = h.shape
    tm = min(tm, n)
    nj = dm // tn
    y_spec = pl.BlockSpec((tm, MIX_WIDTH), lambda i, j: (i, 0))
    gate_specs = [pl.BlockSpec((dm, tn), functools.partial(lambda i, j, b: (0, b * nj + j), b=b)) for b in range(N_BRANCH)]
    br_specs = [pl.BlockSpec((None, MIX_WIDTH, tn), functools.partial(lambda i, j, b: (b, 0, j), b=b)) for b in range(N_BRANCH)]
    return pl.pallas_call(
        _merge_kernel, grid=(n // tm, nj),
        in_specs=[pl.BlockSpec((tm, dm), lambda i, j: (i, 0))] + [y_spec] * N_BRANCH + gate_specs + br_specs,
        out_specs=pl.BlockSpec((tm, tn), lambda i, j: (i, j)),
        out_shape=jax.ShapeDtypeStruct((n, dm), BF16),
        compiler_params=_cparams(("parallel", "parallel")), name="gated_merge",
    )(h, *ys, *([w_gate] * N_BRANCH), *([w_branch] * N_BRANCH))


def _expert_kernel(be_ref, nact_ref, x_ref, w1_ref, w3_ref, w2_ref, o_ref, w1b, w3b, w2b):
    i = pl.program_id(0)
    e = be_ref[i]
    prev = be_ref[jnp.maximum(i - 1, 0)]

    @pl.when((i == 0) | (prev != e))
    def _():
        w1b[...] = w1_ref[...].astype(BF16)
        w3b[...] = w3_ref[...].astype(BF16)
        w2b[...] = w2_ref[...].astype(BF16)

    @pl.when(i < nact_ref[0])
    def _():
        x = x_ref[...]
        a1 = _dot(x, w1b[...])
        a = (a1 * jax.nn.sigmoid(a1)) * _dot(x, w3b[...])
        o_ref[...] = _dot(a.astype(BF16), w2b[...])

    @pl.when(i >= nact_ref[0])
    def _():
        o_ref[...] = jnp.zeros_like(o_ref)


def _experts(xg, blk_exp, n_active, w1, w3, w2):
    p, dm = xg.shape
    n_blk = p // MOE_BLOCK
    ff = w1.shape[-1]
    grid_spec = pltpu.PrefetchScalarGridSpec(
        num_scalar_prefetch=2, grid=(n_blk,),
        in_specs=[pl.BlockSpec((MOE_BLOCK, dm), lambda i, be, na: (i, 0)),
                  pl.BlockSpec((None, dm, ff), lambda i, be, na: (be[i], 0, 0)),
                  pl.BlockSpec((None, dm, ff), lambda i, be, na: (be[i], 0, 0)),
                  pl.BlockSpec((None, ff, dm), lambda i, be, na: (be[i], 0, 0))],
        out_specs=pl.BlockSpec((MOE_BLOCK, dm), lambda i, be, na: (i, 0)),
        scratch_shapes=[pltpu.VMEM((dm, ff), BF16), pltpu.VMEM((dm, ff), BF16), pltpu.VMEM((ff, dm), BF16)])
    return pl.pallas_call(
        _expert_kernel, grid_spec=grid_spec,
        out_shape=jax.ShapeDtypeStruct((p, dm), F32),
        compiler_params=_cparams(("arbitrary",)), name="moe_experts",
    )(blk_exp, n_active, xg, w1, w3, w2)


def _dispatch_plan(e_idx):
    n, K = e_idx.shape
    nk = n * K
    i32 = jnp.int32
    flat_e = e_idx.reshape(-1)
    order = jnp.argsort(flat_e).astype(i32)
    rank = jnp.argsort(order).astype(i32)
    experts = jnp.arange(N_EXPERTS, dtype=i32)
    counts = jnp.sum((flat_e[:, None] == experts[None, :]).astype(i32), axis=0)
    padded = (counts + MOE_BLOCK - 1) // MOE_BLOCK * MOE_BLOCK
    pad_end = jnp.cumsum(padded)
    pad_start = pad_end - padded
    start = jnp.cumsum(counts) - counts
    dest = (pad_start - start)[flat_e] + rank
    n_blk = (nk + MOE_BLOCK - 1) // MOE_BLOCK + N_EXPERTS
    blk_first = jnp.arange(n_blk, dtype=i32) * MOE_BLOCK
    blk_exp = jnp.minimum(jnp.sum((pad_end[None, :] <= blk_first[:, None]).astype(i32), axis=1), N_EXPERTS - 1)
    rows = jnp.arange(n_blk * MOE_BLOCK, dtype=i32)
    e_row = jnp.repeat(blk_exp, MOE_BLOCK)
    j = rows - pad_start[e_row]
    src = order[jnp.clip(start[e_row] + j, 0, nk - 1)] // K
    tok_pad = jnp.where(j < counts[e_row], src, n - 1)
    n_active = (pad_end[-1] // MOE_BLOCK).astype(i32).reshape(1)
    return tok_pad, blk_exp.astype(i32), n_active, dest.reshape(n, K)


def _moe(h, route, w1, w3, w2):
    e_idx = route[:, :TOP_K_IN_GROUP].astype(jnp.int32)
    tok_pad, blk_exp, n_active, dest = _dispatch_plan(e_idx)
    xg = jnp.take(h, tok_pad, axis=0, mode="clip")
    y = _experts(xg, blk_exp, n_active, w1, w3, w2)
    return jnp.take(y, dest[:, 0], axis=0, mode="clip"), jnp.take(y, dest[:, 1], axis=0, mode="clip")


def _pack_kernel(w_ref, main_ref, gate_ref):
    off = 0
    for i in MAIN_ORDER:
        a, b = COL_OFFSETS[i], COL_OFFSETS[i + 1]
        main_ref[:, off:off + b - a] = w_ref[:, a:b].astype(BF16)
        off += b - a
    a, b = COL_OFFSETS[I_MERGE], COL_OFFSETS[I_MERGE + 1]
    gate_ref[...] = w_ref[:, a:b].astype(BF16)


def _pack_w_in(w, rows=64):
    k, c = w.shape
    gate_w = COL_WIDTHS[I_MERGE]
    w_main, w_gate = pl.pallas_call(
        _pack_kernel, grid=(k // rows,),
        in_specs=[pl.BlockSpec((rows, c), lambda i: (i, 0))],
        out_specs=[pl.BlockSpec((rows, MAIN_WIDTH), lambda i: (i, 0)), pl.BlockSpec((rows, gate_w), lambda i: (i, 0))],
        out_shape=[jax.ShapeDtypeStruct((k, MAIN_WIDTH), BF16), jax.ShapeDtypeStruct((k, gate_w), BF16)],
        compiler_params=_cparams(("parallel",)), name="pack_w_in",
    )(w)
    cols = lambda i: w[:, COL_OFFSETS[i]:COL_OFFSETS[i + 1]]
    small = jnp.concatenate([cols(I_AI), cols(I_AF), cols(I_CG)], axis=1)
    w_small = jnp.pad(small, ((0, 0), (0, LANES - small.shape[1]))).astype(BF16)
    return w_main, w_small, w_gate


def _router_weights(wg, bg, we, be):
    w = jnp.concatenate([wg, we], axis=1)
    w = jnp.pad(w, ((0, 0), (0, LANES - w.shape[1])))
    hi = w.astype(BF16)
    lo = (w - hi.astype(F32)).astype(BF16)
    b = jnp.concatenate([bg, be])
    b = jnp.pad(b, (0, LANES - b.shape[0])).reshape(1, LANES).astype(F32)
    return hi, lo, b


def kernel(x, w_in, mlstm_conv, mlstm_gate_bias, mlstm_norm, ret_norm, nsa_cmp_w, nsa_cmp_pe, sgu_norm, sgu_w, sgu_b, w_branch, w_out, norm_mix, norm_ffn, router_group_w, router_group_b, router_expert_w, router_expert_b, expert_w1, expert_w3, expert_w2, norm_final):
    B, T, D = x.shape
    n = B * T
    xs, moe = x.reshape(n, D), None
    for l in range(DEPTH):
        w_main, w_small, w_gate = _pack_w_in(w_in[l])
        if moe is None:
            h, = _norm(xs, norm_mix[l])
        else:
            xs, h = _norm(xs, norm_mix[l], moe=moe)
        zm = _matmul(h, w_main, out_dtype=BF16, name="proj_main")
        zs = _matmul(h, w_small, out_dtype=F32, name="proj_small")
        zst = zs[:, :2 * N_HEADS].T
        ya = _mlstm(zm, zs, zst, mlstm_conv[l], mlstm_gate_bias[l], mlstm_norm[l], B, T)
        yb = _retention(zm, ret_norm[l], B, T)
        yc = _nsa(zm, zs, nsa_cmp_w[l], nsa_cmp_pe[l], B, T)
        yd = _sgu(zm, sgu_norm[l], sgu_w[l], sgu_b[l], n)
        merged = _merge(h, (ya, yb, yc, yd), w_gate, w_branch[l].astype(BF16))
        x_mid = _matmul(merged, w_out[l].astype(BF16), out_dtype=F32, residual=xs, name="proj_out")
        h2, route = _norm(x_mid, norm_ffn[l],
                          router_w=_router_weights(router_group_w[l], router_group_b[l],
                                                   router_expert_w[l], router_expert_b[l]))
        m0, m1 = _moe(h2, route, expert_w1[l], expert_w3[l], expert_w2[l])
        xs, moe = x_mid, (m0, m1, route)
    out, = _norm(xs, norm_final, moe=moe, final=True)
    return out.reshape(B, T, D)
```

```python
import functools

import numpy as np
import jax
import jax.numpy as jnp
from jax import lax
from jax.experimental import pallas as pl
from jax.experimental.pallas import tpu as pltpu

F32 = jnp.float32
BF16 = jnp.bfloat16

D_MODEL = 4096
DEPTH = 2
HEAD_DIM = 128
N_BRANCH = 4
MIX_WIDTH = D_MODEL // N_BRANCH
N_HEADS = MIX_WIDTH // HEAD_DIM
CONV_WIDTH = 4
MLSTM_CHUNK = 128
RET_CHUNK = 128
NSA_KV_GROUPS = 2
NSA_GROUP_SIZE = N_HEADS // NSA_KV_GROUPS
KV_WIDTH = NSA_KV_GROUPS * HEAD_DIM
CMP_BLOCK = 32
CMP_STRIDE = 16
SEL_BLOCK = 64
SEL_TOPN = 16
WINDOW = 512
Q_BLOCK = 128
SGU_CHUNK = 128
N_GROUPS = 8
EXPERTS_PER_GROUP = 8
N_EXPERTS = N_GROUPS * EXPERTS_PER_GROUP
TOP_K_IN_GROUP = 2
EXPERT_FF = 256
MOE_BLOCK = 128
RMS_EPS = 1e-6
NEG_INF = -1e30
FORCE_SCORE = 1e9

COL_WIDTHS = (
    MIX_WIDTH, MIX_WIDTH, MIX_WIDTH, MIX_WIDTH, N_HEADS, N_HEADS,
    MIX_WIDTH, MIX_WIDTH, MIX_WIDTH, MIX_WIDTH,
    MIX_WIDTH, KV_WIDTH, KV_WIDTH, KV_WIDTH, KV_WIDTH, KV_WIDTH, KV_WIDTH, 3 * N_HEADS,
    MIX_WIDTH, MIX_WIDTH,
    N_BRANCH * D_MODEL,
)
COL_OFFSETS = tuple(int(v) for v in np.concatenate([[0], np.cumsum(COL_WIDTHS)]))
(I_AQ, I_AK, I_AV, I_AO, I_AI, I_AF, I_BQ, I_BK, I_BV, I_BG, I_CQ, I_CKC, I_CVC, I_CKS, I_CVS,
 I_CKW, I_CVW, I_CG, I_DU, I_DV, I_MERGE) = range(21)

MAIN_ORDER = (I_AQ, I_AK, I_AV, I_AO, I_BQ, I_BK, I_BV, I_BG, I_CQ, I_DU, I_DV,
              I_CKC, I_CVC, I_CKS, I_CVS, I_CKW, I_CVW)
MAIN_WIDTH = sum(COL_WIDTHS[i] for i in MAIN_ORDER)
LANES = 128
BLK_AQ, BLK_AK, BLK_AV, BLK_AO, BLK_BQ, BLK_BK, BLK_BV, BLK_BG, BLK_CQ, BLK_DU, BLK_DV = range(11)
KV_BASE = 11 * MIX_WIDTH // HEAD_DIM
VMEM_LIMIT = 48 * 1024 * 1024


def _cparams(sem, vmem=VMEM_LIMIT):
    return pltpu.CompilerParams(dimension_semantics=sem, vmem_limit_bytes=vmem)


def _dot(a, b):
    return jnp.dot(a, b, preferred_element_type=F32)


def _dot_nt(a, b):
    return lax.dot_general(a, b, (((1,), (1,)), ((), ())), preferred_element_type=F32)


def _split3(x):
    hi = x.astype(BF16)
    r1 = x - hi.astype(F32)
    mid = r1.astype(BF16)
    lo = (r1 - mid.astype(F32)).astype(BF16)
    return hi, mid, lo


def _dot_exact_rhs(x, m_bf16):
    hi, mid, lo = _split3(x)
    return _dot(hi, m_bf16) + _dot(mid, m_bf16) + _dot(lo, m_bf16)


def _log_sigmoid(x):
    return jnp.minimum(x, 0.0) - jnp.log1p(jnp.exp(-jnp.abs(x)))


def _norm_kernel(*refs, moe, router, final):
    x = refs[0][...]
    pos = 1
    if moe:
        m0_ref, m1_ref, rt_ref = refs[pos:pos + 3]
        pos += 3
        x = x + (rt_ref[:, 2:3] * m0_ref[...] + rt_ref[:, 3:4] * m1_ref[...])
    g_ref = refs[pos]
    pos += 1
    if router:
        whi_ref, wlo_ref, rb_ref = refs[pos:pos + 3]
        pos += 3
    outs = refs[pos:]
    y = x * lax.rsqrt(jnp.mean(x * x, axis=-1, keepdims=True) + RMS_EPS) * g_ref[...]
    o = 0
    if moe and not final:
        outs[o][...] = x
        o += 1
    if final:
        outs[o][...] = y
    else:
        outs[o][...] = y.astype(BF16)
    o += 1
    if router:
        y_hi = y.astype(BF16)
        y_lo = (y - y_hi.astype(F32)).astype(BF16)
        lg = _dot(y_hi, whi_ref[...]) + (_dot(y_hi, wlo_ref[...]) + _dot(y_lo, whi_ref[...])) + rb_ref[...]
        outs[o][...] = _route_rows(lg)


def _first_max(v, lane):
    mx = jnp.max(v, axis=-1, keepdims=True)
    return mx, jnp.min(jnp.where(v == mx, lane, LANES), axis=-1, keepdims=True)


def _route_rows(lg):
    low = -3.0e38
    lane = lax.broadcasted_iota(jnp.int32, lg.shape, 1)
    is_grp = lane < N_GROUPS
    gmax, g_idx = _first_max(jnp.where(is_grp, lg, low), lane)
    g_w = 1.0 / jnp.sum(jnp.where(is_grp, jnp.exp(lg - gmax), 0.0), axis=-1, keepdims=True)
    assert EXPERTS_PER_GROUP & (EXPERTS_PER_GROUP - 1) == 0
    grp_of_lane = lax.shift_right_arithmetic(lane - N_GROUPS, EXPERTS_PER_GROUP.bit_length() - 1)
    in_grp = (lane >= N_GROUPS) & (lane < N_GROUPS + N_EXPERTS) & (grp_of_lane == g_idx)
    el = jnp.where(in_grp, lg, low)
    v0, i0 = _first_max(el, lane)
    v1, i1 = _first_max(jnp.where(lane == i0, low, el), lane)
    e1 = jnp.exp(v1 - v0)
    p0 = g_w / (1.0 + e1)
    vals = (i0 - N_GROUPS).astype(F32), (i1 - N_GROUPS).astype(F32), p0, p0 * e1
    out = jnp.zeros(lg.shape, F32)
    for k, v in enumerate(vals):
        out = jnp.where(lane == k, v, out)
    return out


def _norm(x, gain, *, moe=None, router_w=None, final=False, rows=128):
    n, d = x.shape
    rows = min(rows, n)
    row_spec = pl.BlockSpec((rows, d), lambda i: (i, 0))
    lane_spec = pl.BlockSpec((rows, LANES), lambda i: (i, 0))
    in_specs, args = [row_spec], [x]
    if moe is not None:
        in_specs += [row_spec, row_spec, lane_spec]
        args += list(moe)
    in_specs.append(pl.BlockSpec((1, d), lambda i: (0, 0)))
    args.append(gain.reshape(1, d))
    out_shape, out_specs = [], []
    if moe is not None and not final:
        out_shape.append(jax.ShapeDtypeStruct((n, d), F32))
        out_specs.append(row_spec)
    out_shape.append(jax.ShapeDtypeStruct((n, d), F32 if final else BF16))
    out_specs.append(row_spec)
    if router_w is not None:
        whi, wlo, rb = router_w
        in_specs += [pl.BlockSpec((d, LANES), lambda i: (0, 0))] * 2 + [pl.BlockSpec((1, LANES), lambda i: (0, 0))]
        args += [whi, wlo, rb]
        out_shape.append(jax.ShapeDtypeStruct((n, LANES), F32))
        out_specs.append(lane_spec)
    return pl.pallas_call(
        functools.partial(_norm_kernel, moe=moe is not None, router=router_w is not None, final=final),
        grid=(n // rows,), in_specs=in_specs, out_specs=out_specs, out_shape=out_shape,
        compiler_params=_cparams(("parallel",)), name="rmsnorm",
    )(*args)


def _mm_kernel(a_ref, w_ref, *rest, has_res, w_rows_are_outputs):
    w = w_ref[...].astype(BF16)
    acc = _dot_nt(a_ref[...], w) if w_rows_are_outputs else _dot(a_ref[...], w)
    if has_res:
        acc = acc + rest[0][...]
    rest[-1][...] = acc.astype(rest[-1].dtype)


def _matmul(a, w, *, out_dtype, residual=None, tm=1024, tn=512, w_rows_are_outputs=False, name="matmul"):
    m, k = a.shape
    nc = w.shape[0] if w_rows_are_outputs else w.shape[1]
    tm, tn = min(tm, m), min(tn, nc)
    w_spec = pl.BlockSpec((tn, k), lambda i, j: (j, 0)) if w_rows_are_outputs else pl.BlockSpec((k, tn), lambda i, j: (0, j))
    in_specs = [pl.BlockSpec((tm, k), lambda i, j: (i, 0)), w_spec]
    args = [a, w]
    if residual is not None:
        in_specs.append(pl.BlockSpec((tm, tn), lambda i, j: (i, j)))
        args.append(residual)
    return pl.pallas_call(
        functools.partial(_mm_kernel, has_res=residual is not None, w_rows_are_outputs=w_rows_are_outputs),
        grid=(m // tm, nc // tn), in_specs=in_specs,
        out_specs=pl.BlockSpec((tm, tn), lambda i, j: (i, j)),
        out_shape=jax.ShapeDtypeStruct((m, nc), out_dtype),
        compiler_params=_cparams(("parallel", "parallel")), name=name,
    )(*args)


def _mlstm_kernel(aq_ref, ak_ref, av_ref, ao_ref, zs_ref, zst_ref, cw_ref, gbc_ref, gbr_ref, gain_ref,
                  tril_ref, triu_ref, o_ref, xbuf, qk_s, ct_s, n_s, m_s):
    L, d, H, W = MLSTM_CHUNK, HEAD_DIM, N_HEADS, MIX_WIDTH
    c = pl.program_id(1)

    @pl.when(c == 0)
    def _():
        xbuf[0:8, :] = jnp.zeros((8, 2 * W), F32)
        ct_s[...] = jnp.zeros_like(ct_s)
        n_s[...] = jnp.zeros_like(n_s)
        m_s[...] = jnp.zeros_like(m_s)

    @pl.when(c > 0)
    def _():
        xbuf[0:8, :] = xbuf[L:L + 8, :]

    xbuf[8:L + 8, 0:W] = aq_ref[...].astype(F32)
    xbuf[8:L + 8, W:2 * W] = ak_ref[...].astype(F32)
    base = 8 - (CONV_WIDTH - 1)
    conv = cw_ref[0:1, :] * xbuf[base:base + L, :]
    for j in range(1, CONV_WIDTH):
        conv = conv + cw_ref[j:j + 1, :] * xbuf[base + j:base + j + L, :]
    qk_s[...] = conv * jax.nn.sigmoid(conv)

    pre_c = zs_ref[...] + gbc_ref[...]
    ls_c = _log_sigmoid(pre_c)
    hi, mid, lo = _split3(ls_c)
    tril = tril_ref[...]
    bc = _dot(tril, hi) + _dot(tril, mid) + _dot(tril, lo)
    pre_r = zst_ref[...] + gbr_ref[...]
    br = _dot_exact_rhs(_log_sigmoid(pre_r), triu_ref[...])

    row = lax.broadcasted_iota(jnp.int32, (L, L), 0)
    col = lax.broadcasted_iota(jnp.int32, (L, L), 1)
    causal = row >= col
    scale = d ** -0.5
    for h in range(H):
        sl = slice(h * d, (h + 1) * d)
        q = qk_s[:, sl] * scale
        k = qk_s[:, W + h * d:W + (h + 1) * d]
        v = av_ref[:, sl]
        i_col, b_col = pre_c[:, h:h + 1], bc[:, H + h:H + h + 1]
        i_row, b_row = pre_r[h:h + 1, :], br[H + h:H + h + 1, :]
        m_prev = m_s[h:h + 1, 0:1]
        log_d = jnp.where(causal, b_col - b_row + i_row, NEG_INF)
        log_inter = b_col + m_prev
        m_row = jnp.maximum(jnp.max(log_d, axis=-1, keepdims=True), log_inter)
        qb, kb = q.astype(BF16), k.astype(BF16)
        s = _dot_nt(qb, kb) * jnp.exp(log_d - m_row)
        w_inter = jnp.exp(log_inter - m_row)
        num = _dot(s.astype(BF16), v) + w_inter * _dot(qb, ct_s[h].astype(BF16))
        den = jnp.sum(s, axis=-1, keepdims=True) + w_inter * jnp.sum(q * n_s[h:h + 1, :], axis=-1, keepdims=True)
        hh = num / jnp.maximum(jnp.abs(den), jnp.exp(-m_row))
        b_last = b_row[:, L - 1:L]
        log_w_row = b_last - b_row + i_row
        m_new = jnp.maximum(b_last + m_prev, jnp.max(log_w_row, axis=-1, keepdims=True))
        w_col = jnp.exp(b_last - b_col + i_col - m_new)
        decay = jnp.exp(b_last + m_prev - m_new)
        vw = (v.astype(F32) * w_col).astype(BF16)
        ct_s[h] = decay * ct_s[h] + _dot(k.T.astype(BF16), vw)
        n_s[h:h + 1, :] = decay * n_s[h:h + 1, :] + jnp.sum(k * w_col, axis=0, keepdims=True)
        m_s[h:h + 1, :] = jnp.broadcast_to(m_new, (1, LANES))
        hn = hh * lax.rsqrt(jnp.mean(hh * hh, axis=-1, keepdims=True) + RMS_EPS)
        o_ref[:, sl] = (hn * gain_ref[:, sl] * jax.nn.sigmoid(ao_ref[:, sl].astype(F32))).astype(BF16)


def _tri_consts(L):
    r = np.arange(L)
    tril = (r[:, None] >= r[None, :]).astype(np.float32)
    return jnp.asarray(tril, BF16), jnp.asarray(tril.T, BF16)


def _mlstm(zm, zs, zst, conv_w, gate_bias, gain, B, T):
    L, W = MLSTM_CHUNK, MIX_WIDTH
    nc = T // L
    n = B * T
    gbc = jnp.zeros((1, LANES), F32).at[0, :2 * N_HEADS].set(gate_bias.reshape(-1))
    gbr = jnp.broadcast_to(gate_bias.reshape(2 * N_HEADS, 1), (2 * N_HEADS, L)).astype(F32)
    tril, triu = _tri_consts(L)

    def blk(b_idx):
        return pl.BlockSpec((L, W), lambda b, c: (b * nc + c, b_idx))

    const = lambda shape: pl.BlockSpec(shape, lambda b, c: (0,) * len(shape))
    return pl.pallas_call(
        _mlstm_kernel, grid=(B, nc),
        in_specs=[blk(BLK_AQ), blk(BLK_AK), blk(BLK_AV), blk(BLK_AO),
                  pl.BlockSpec((L, LANES), lambda b, c: (b * nc + c, 0)),
                  pl.BlockSpec((2 * N_HEADS, L), lambda b, c: (0, b * nc + c)),
                  const((CONV_WIDTH, 2 * W)), const((1, LANES)), const((2 * N_HEADS, L)), const((1, W)),
                  const((L, L)), const((L, L))],
        out_specs=pl.BlockSpec((L, W), lambda b, c: (b * nc + c, 0)),
        out_shape=jax.ShapeDtypeStruct((n, W), BF16),
        scratch_shapes=[pltpu.VMEM((L + 8, 2 * W), F32), pltpu.VMEM((L, 2 * W), F32),
                        pltpu.VMEM((N_HEADS, HEAD_DIM, HEAD_DIM), F32), pltpu.VMEM((N_HEADS, HEAD_DIM), F32),
                        pltpu.VMEM((N_HEADS, LANES), F32)],
        compiler_params=_cparams(("parallel", "arbitrary")), name="mlstm",
    )(zm, zm, zm, zm, zs, zst, conv_w, gbc, gbr, gain.reshape(1, W), tril, triu)


def _ret_kernel(q_ref, k_ref, v_ref, g_ref, dec_ref, qd_ref, kd_ref, gain_ref, o_ref, r_s, *, chunk_decay):
    L, d, H = RET_CHUNK, HEAD_DIM, N_HEADS
    c = pl.program_id(1)

    @pl.when(c == 0)
    def _():
        r_s[...] = jnp.zeros_like(r_s)

    scale = d ** -0.5
    for h in range(H):
        sl = slice(h * d, (h + 1) * d)
        q = q_ref[:, sl].astype(F32)
        k = k_ref[:, sl].astype(F32) * scale
        v = v_ref[:, sl]
        s = _dot_nt(q.astype(BF16), k.astype(BF16)) * dec_ref[h]
        o = _dot(s.astype(BF16), v) + _dot((q * qd_ref[:, sl]).astype(BF16), r_s[h].astype(BF16))
        kd = (k * kd_ref[:, sl]).T.astype(BF16)
        r_s[h] = chunk_decay[h] * r_s[h] + _dot(kd, v)
        on = o * lax.rsqrt(jnp.mean(o * o, axis=-1, keepdims=True) + RMS_EPS)
        gate = g_ref[:, sl].astype(F32)
        o_ref[:, sl] = (on * gain_ref[:, sl] * (gate * jax.nn.sigmoid(gate))).astype(BF16)


def _retention(zm, gain, B, T):
    L, W, H, d = RET_CHUNK, MIX_WIDTH, N_HEADS, HEAD_DIM
    nc = T // L
    log_gamma = np.log(np.float32(1.0) - np.float32(2.0) ** (-5.0 - np.arange(H, dtype=np.float32))).astype(np.float32)
    pos = np.arange(L, dtype=np.float32)
    diff = pos[:, None] - pos[None, :]
    decay = np.where(diff >= 0, np.exp(log_gamma[:, None, None] * np.maximum(diff, 0.0)), 0.0).astype(np.float32)
    q_decay = np.exp(log_gamma[:, None] * (pos + 1.0)).astype(np.float32)
    k_decay = np.exp(log_gamma[:, None] * (L - 1.0 - pos)).astype(np.float32)
    chunk_decay = tuple(float(v) for v in np.exp(log_gamma * np.float32(L)).astype(np.float32))
    qd = jnp.asarray(np.repeat(q_decay.T, d, axis=1))
    kd = jnp.asarray(np.repeat(k_decay.T, d, axis=1))

    def blk(b_idx):
        return pl.BlockSpec((L, W), lambda b, c: (b * nc + c, b_idx))

    const = lambda shape: pl.BlockSpec(shape, lambda b, c: (0,) * len(shape))
    return pl.pallas_call(
        functools.partial(_ret_kernel, chunk_decay=chunk_decay), grid=(B, nc),
        in_specs=[blk(BLK_BQ), blk(BLK_BK), blk(BLK_BV), blk(BLK_BG),
                  const((H, L, L)), const((L, W)), const((L, W)), const((1, W))],
        out_specs=pl.BlockSpec((L, W), lambda b, c: (b * nc + c, 0)),
        out_shape=jax.ShapeDtypeStruct((B * T, W), BF16),
        scratch_shapes=[pltpu.VMEM((H, d, d), F32)],
        compiler_params=_cparams(("parallel", "arbitrary")), name="retention",
    )(zm, zm, zm, zm, jnp.asarray(decay), qd, kd, gain.reshape(1, W))


def _gelu(x):
    return 0.5 * x * (1.0 + jnp.tanh(0.7978845608028654 * (x + 0.044715 * (x * x * x))))


def _sgu_kernel(u_ref, v_ref, ng_ref, w_ref, b_ref, o_ref):
    L, d, H = SGU_CHUNK, HEAD_DIM, N_HEADS
    v = _gelu(v_ref[...].astype(F32))
    vn = (v * lax.rsqrt(jnp.mean(v * v, axis=-1, keepdims=True) + RMS_EPS) * ng_ref[...]).astype(BF16)
    u = _gelu(u_ref[...].astype(F32))
    row = lax.broadcasted_iota(jnp.int32, (L, L), 0)
    col = lax.broadcasted_iota(jnp.int32, (L, L), 1)
    for g in range(H):
        sl = slice(g * d, (g + 1) * d)
        wm = jnp.where(row >= col, w_ref[g], 0.0).astype(BF16)
        mixed = _dot(wm, vn[:, sl]) + b_ref[:, sl]
        o_ref[:, sl] = (u[:, sl] * mixed).astype(BF16)


def _sgu(zm, norm_g, w_s, b_s, n):
    L, W, H, d = SGU_CHUNK, MIX_WIDTH, N_HEADS, HEAD_DIM
    bsb = jnp.repeat(b_s.T, d, axis=1)
    const = lambda shape: pl.BlockSpec(shape, lambda i: (0,) * len(shape))
    return pl.pallas_call(
        _sgu_kernel, grid=(n // L,),
        in_specs=[pl.BlockSpec((L, W), lambda i: (i, BLK_DU)), pl.BlockSpec((L, W), lambda i: (i, BLK_DV)),
                  const((1, W)), const((H, L, L)), const((L, W))],
        out_specs=pl.BlockSpec((L, W), lambda i: (i, 0)),
        out_shape=jax.ShapeDtypeStruct((n, W), BF16),
        compiler_params=_cparams(("parallel",)), name="sgu",
    )(zm, zm, norm_g.reshape(1, W), w_s, bsb)


def _compress_kernel(k_ref, v_ref, w_ref, pe_ref, ko_ref, vo_ref, buf):
    t = k_ref.shape[0]
    n16 = t // CMP_STRIDE
    buf[t:t + CMP_STRIDE, :] = jnp.zeros((CMP_STRIDE, HEAD_DIM), F32)
    for which, (src, dst) in enumerate(((k_ref, ko_ref), (v_ref, vo_ref))):
        buf[0:t, :] = src[...].astype(F32)
        acc = jnp.zeros((n16, HEAD_DIM), F32)
        for l in range(CMP_BLOCK):
            x = buf[pl.ds(l, n16, stride=CMP_STRIDE), :]
            acc = acc + _dot((x + pe_ref[which, l:l + 1, :]).astype(BF16), w_ref[which, l].astype(BF16))
        dst[...] = acc.astype(BF16)


def _compress(zm, cmp_w, cmp_pe, B, T):
    G, d = NSA_KV_GROUPS, HEAD_DIM
    n16 = T // CMP_STRIDE
    out = jax.ShapeDtypeStruct((B, G, n16, d), BF16)
    kblk = KV_BASE
    vblk = KV_BASE + G
    return pl.pallas_call(
        _compress_kernel, grid=(B, G),
        in_specs=[pl.BlockSpec((T, d), lambda b, g: (b, kblk + g)), pl.BlockSpec((T, d), lambda b, g: (b, vblk + g)),
                  pl.BlockSpec((2, CMP_BLOCK, d, d), lambda b, g: (0, 0, 0, 0)),
                  pl.BlockSpec((2, CMP_BLOCK, d), lambda b, g: (0, 0, 0))],
        out_specs=[pl.BlockSpec((None, None, n16, d), lambda b, g: (b, g, 0, 0))] * 2,
        out_shape=[out, out],
        scratch_shapes=[pltpu.VMEM((T + CMP_STRIDE, d), F32)],
        compiler_params=_cparams(("parallel", "parallel")), name="nsa_compress",
    )(zm, zm, cmp_w, cmp_pe)


def _softmax_rows(s):
    m = jnp.max(s, axis=-1, keepdims=True)
    p = jnp.exp(s - m)
    return p, jnp.sum(p, axis=-1, keepdims=True)


def _nsa_a_kernel(slope_ref, q_ref, kc_ref, vc_ref, *rest, n_back):
    nw = n_back + 1
    kw_refs, vw_refs = rest[:nw], rest[nw:2 * nw]
    cg_ref, ov_ref, part_ref, sel_ref, key_s = rest[2 * nw:]
    tq, d, J = Q_BLOCK, HEAD_DIM, NSA_GROUP_SIZE
    g, qi = pl.program_id(1), pl.program_id(2)
    t0 = qi * tq
    ncmp = kc_ref.shape[0]
    scale = d ** -0.5
    qs = [(q_ref[:, j * d:(j + 1) * d].astype(F32) * scale).astype(BF16) for j in range(J)]
    sg = jax.nn.sigmoid(cg_ref[...])

    r = lax.broadcasted_iota(jnp.int32, (tq, ncmp), 0)
    n_idx = lax.broadcasted_iota(jnp.int32, (tq, ncmp), 1)
    dist_i = t0 + r - CMP_STRIDE * n_idx - (CMP_BLOCK - 1)
    valid_c = dist_i >= 0
    dist_c = dist_i.astype(F32)
    t_col = t0 + lax.broadcasted_iota(jnp.int32, (tq, 1), 0)
    has_c = (t_col >= CMP_BLOCK - 1).astype(F32)
    kc, vc = kc_ref[...], vc_ref[...]
    psum = jnp.zeros((tq, ncmp), F32)
    o_cmp = []
    for j in range(J):
        slope = slope_ref[g * J + j]
        s = jnp.where(valid_c, _dot_nt(qs[j], kc) - slope * dist_c, NEG_INF)
        p, l = _softmax_rows(s)
        p = p * (has_c / l)
        o_cmp.append(_dot(p.astype(BF16), vc))
        psum = psum + p

    imp = _dot_exact_rhs(psum, ov_ref[...])
    blk = lax.broadcasted_iota(jnp.int32, (tq, LANES), 1)
    cur = (t0 + lax.broadcasted_iota(jnp.int32, (tq, LANES), 0)) // SEL_BLOCK
    valid_s = blk <= cur
    forced = (blk == 0) | (blk == cur) | (blk == cur - 1)
    key = jnp.where(valid_s, imp + jnp.where(forced, FORCE_SCORE, 0.0), NEG_INF)
    key_s[...] = key.T
    sub = lax.broadcasted_iota(jnp.int32, (8, tq), 0)
    groups = LANES // 8
    key_g = [key_s[8 * i:8 * i + 8, :] for i in range(groups)]
    rank = [jnp.zeros((8, tq), F32) for _ in range(groups)]
    for mp in range(LANES):
        rowv = jnp.broadcast_to(key_s[mp:mp + 1, :], (8, tq))
        for i in range(groups):
            if 8 * i > mp:
                ahead = rowv >= key_g[i]
            elif 8 * i + 7 <= mp:
                ahead = rowv > key_g[i]
            else:
                ahead = (rowv > key_g[i]) | ((sub > mp - 8 * i) & (rowv == key_g[i]))
            rank[i] = rank[i] + jnp.where(ahead, 1.0, 0.0)
    rank_t = jnp.concatenate(rank, axis=0)
    blk_t = lax.broadcasted_iota(jnp.int32, (LANES, tq), 0)
    cur_t = (t0 + lax.broadcasted_iota(jnp.int32, (LANES, tq), 1)) // SEL_BLOCK
    sel_ref[...] = jnp.where((rank_t < float(SEL_TOPN)) & (blk_t <= cur_t), 1.0, 0.0).astype(BF16)

    kw = jnp.concatenate([kr[...] for kr in kw_refs], axis=0)
    vw = jnp.concatenate([vr[...] for vr in vw_refs], axis=0)
    span = nw * tq
    rw = lax.broadcasted_iota(jnp.int32, (tq, span), 0)
    cw = lax.broadcasted_iota(jnp.int32, (tq, span), 1)
    dist_wi = rw + n_back * tq - cw
    mask_w = (dist_wi >= 0) & (dist_wi < WINDOW) & (t0 - n_back * tq + cw >= 0)
    dist_w = dist_wi.astype(F32)
    for j in range(J):
        slope = slope_ref[g * J + j]
        s = jnp.where(mask_w, _dot_nt(qs[j], kw) - slope * dist_w, NEG_INF)
        p, l = _softmax_rows(s)
        o_win = _dot((p / l).astype(BF16), vw)
        part_ref[:, j * d:(j + 1) * d] = sg[:, j:j + 1] * o_cmp[j] + sg[:, 2 * J + j:2 * J + j + 1] * o_win


def _nsa_consts(T):
    n16 = T // CMP_STRIDE
    n_sel = T // SEL_BLOCK
    n = np.arange(n16)[:, None]
    m = np.arange(LANES)[None, :]
    c_start, s_start = n * CMP_STRIDE, m * SEL_BLOCK
    overlap = ((c_start < s_start + SEL_BLOCK) & (c_start + CMP_BLOCK > s_start) & (n < n16 - 1) & (m < n_sel))
    return jnp.asarray(overlap.astype(np.float32), BF16)


def _nsa_a(zm, k_cmp, v_cmp, cg_g, slopes, B, T):
    G, J, d, tq = NSA_KV_GROUPS, NSA_GROUP_SIZE, HEAD_DIM, Q_BLOCK
    nq = T // tq
    n16 = T // CMP_STRIDE
    n_back = WINDOW // tq
    qblk = BLK_CQ * (MIX_WIDTH // (J * d))
    kwblk = KV_BASE + 4 * G
    vwblk = KV_BASE + 5 * G

    def win_spec(base, i):
        return pl.BlockSpec((tq, d), lambda b, g, qi: (b * nq + jnp.maximum(qi - n_back + i, 0), base + g))

    in_specs = ([pl.BlockSpec(memory_space=pltpu.SMEM),
                 pl.BlockSpec((tq, J * d), lambda b, g, qi: (b * nq + qi, qblk + g)),
                 pl.BlockSpec((None, None, n16, d), lambda b, g, qi: (b, g, 0, 0)),
                 pl.BlockSpec((None, None, n16, d), lambda b, g, qi: (b, g, 0, 0))]
                + [win_spec(kwblk, i) for i in range(n_back + 1)]
                + [win_spec(vwblk, i) for i in range(n_back + 1)]
                + [pl.BlockSpec((None, tq, LANES), lambda b, g, qi: (g, b * nq + qi, 0)),
                   pl.BlockSpec((n16, LANES), lambda b, g, qi: (0, 0))])
    return pl.pallas_call(
        functools.partial(_nsa_a_kernel, n_back=n_back), grid=(B, G, nq), in_specs=in_specs,
        out_specs=[pl.BlockSpec((tq, J * d), lambda b, g, qi: (b * nq + qi, g)),
                   pl.BlockSpec((None, None, LANES, tq), lambda b, g, qi: (b, g, 0, qi))],
        out_shape=[jax.ShapeDtypeStruct((B * T, MIX_WIDTH), F32), jax.ShapeDtypeStruct((B, G, LANES, T), BF16)],
        scratch_shapes=[pltpu.VMEM((LANES, tq), F32)],
        compiler_params=_cparams(("parallel", "parallel", "parallel")), name="nsa_cmp_win",
    )(slopes, zm, k_cmp, v_cmp, *([zm] * (2 * (n_back + 1))), cg_g, _nsa_consts(T))


def _nsa_b_kernel(cnt_ref, lst_ref, q_ref, ks_ref, vt_ref, sel_ref, e_ref, a_ref, srow_ref, cg_ref, part_ref, o_ref,
                  q_s, acc_s, *, tk, nk):
    tq, d, J = Q_BLOCK, HEAD_DIM, NSA_GROUP_SIZE
    b, g, qi = pl.program_id(0), pl.program_id(1), pl.program_id(2)
    lin = (b * pl.num_programs(1) + g) * pl.num_programs(2) + qi
    t0 = qi * tq
    scale = d ** -0.5
    for j in range(J):
        q_s[j * tq:(j + 1) * tq, :] = (q_ref[:, j * d:(j + 1) * d].astype(F32) * scale).astype(BF16)
    acc_s[...] = jnp.zeros_like(acc_s)
    r_minus_c = lax.broadcasted_iota(jnp.int32, (tk, tq), 1) - lax.broadcasted_iota(jnp.int32, (tk, tq), 0)

    def body(i, carry):
        m_prev, l_prev = carry
        kt = lst_ref[lin * nk + i]
        k_t = ks_ref[pl.ds(pl.multiple_of(kt * tk, tk), tk), :]
        off = t0 - kt * tk
        picked = _dot(e_ref[kt], sel_ref[...])
        bias = jnp.where((picked > 0.5) & (r_minus_c + off >= 0), 0.0, NEG_INF)
        s = _dot_nt(k_t, q_s[...]) - a_ref[...] + jnp.concatenate([bias] * J, axis=1)
        shift = srow_ref[...] * off.astype(F32)
        m_new = jnp.maximum(m_prev, jnp.max(s, axis=0, keepdims=True) - shift)
        p = jnp.exp(s - (m_new + shift))
        alpha = jnp.exp(m_prev - m_new)
        acc_s[...] = alpha * acc_s[...] + _dot(vt_ref[kt], p.astype(BF16))
        return m_new, alpha * l_prev + jnp.sum(p, axis=0, keepdims=True)

    init = (jnp.full((1, J * tq), NEG_INF, F32), jnp.zeros((1, J * tq), F32))
    _, l = lax.fori_loop(0, cnt_ref[lin], body, init)
    inv_l = 1.0 / l
    sg = jax.nn.sigmoid(cg_ref[...])
    for j in range(J):
        cols = slice(j * tq, (j + 1) * tq)
        o_sel = (acc_s[:, cols] * inv_l[:, cols]).T
        o_ref[:, j * d:(j + 1) * d] = (part_ref[:, j * d:(j + 1) * d] + sg[:, J + j:J + j + 1] * o_sel).astype(BF16)


def _nsa_b(zm, sel_t, cg_g, part, B, T, tk=512):
    G, J, d, tq = NSA_KV_GROUPS, NSA_GROUP_SIZE, HEAD_DIM, Q_BLOCK
    tk = min(tk, T)
    nq, nk = T // tq, T // tk
    qblk = BLK_CQ * (MIX_WIDTH // (J * d))
    ksblk = KV_BASE + 2 * G
    vs_col = (KV_BASE + 3 * G) * d
    per_tile = tk // SEL_BLOCK
    kt = np.arange(nk)[:, None, None]
    cc = np.arange(tk)[None, :, None]
    m = np.arange(LANES)[None, None, :]
    expand_t = jnp.asarray((m == kt * per_tile + cc // SEL_BLOCK).astype(np.float32), BF16)
    slopes_np = (2.0 ** (-8.0 * (np.arange(N_HEADS, dtype=np.float32) + 1.0) / N_HEADS)).astype(np.float32)
    r_minus_c = (np.arange(tq)[None, :] - np.arange(tk)[:, None]).astype(np.float32)
    alibi_t = jnp.asarray((slopes_np.reshape(G, 1, J, 1) * r_minus_c[None, :, None, :]).reshape(G, tk, J * tq))
    srow = jnp.asarray(np.repeat(slopes_np.reshape(G, J), tq, axis=1).reshape(G, 1, J * tq))
    vs_t = zm[:, vs_col:vs_col + G * d].reshape(B, nk, tk, G, d).transpose(0, 3, 1, 4, 2)

    active = sel_t[:, :, :nk * per_tile].reshape(B, G, nk, per_tile, nq, tq).max(axis=(3, 5)) > 0
    active = active.transpose(0, 1, 3, 2)
    tiles = jnp.arange(nk, dtype=jnp.int32)
    order = jnp.sort(jnp.where(active, tiles, tiles + nk), axis=-1) % nk
    counts = active.sum(axis=-1).astype(jnp.int32)

    grid_spec = pltpu.PrefetchScalarGridSpec(
        num_scalar_prefetch=2, grid=(B, G, nq),
        in_specs=[pl.BlockSpec((tq, J * d), lambda b, g, qi, c, o: (b * nq + qi, qblk + g)),
                  pl.BlockSpec((T, d), lambda b, g, qi, c, o: (b, ksblk + g)),
                  pl.BlockSpec((None, None, nk, d, tk), lambda b, g, qi, c, o: (b, g, 0, 0, 0)),
                  pl.BlockSpec((None, None, LANES, tq), lambda b, g, qi, c, o: (b, g, 0, qi)),
                  pl.BlockSpec((nk, tk, LANES), lambda b, g, qi, c, o: (0, 0, 0)),
                  pl.BlockSpec((None, tk, J * tq), lambda b, g, qi, c, o: (g, 0, 0)),
                  pl.BlockSpec((None, 1, J * tq), lambda b, g, qi, c, o: (g, 0, 0)),
                  pl.BlockSpec((None, tq, LANES), lambda b, g, qi, c, o: (g, b * nq + qi, 0)),
                  pl.BlockSpec((tq, J * d), lambda b, g, qi, c, o: (b * nq + qi, g))],
        out_specs=pl.BlockSpec((tq, J * d), lambda b, g, qi, c, o: (b * nq + qi, g)),
        scratch_shapes=[pltpu.VMEM((J * tq, d), BF16), pltpu.VMEM((d, J * tq), F32)])
    return pl.pallas_call(
        functools.partial(_nsa_b_kernel, tk=tk, nk=nk), grid_spec=grid_spec,
        out_shape=jax.ShapeDtypeStruct((B * T, MIX_WIDTH), BF16),
        compiler_params=_cparams(("parallel", "parallel", "parallel")), name="nsa_selected",
    )(counts.reshape(-1), order.reshape(-1).astype(jnp.int32), zm, zm, vs_t, sel_t, expand_t, alibi_t, srow, cg_g, part)


def _nsa(zm, zs, cmp_w, cmp_pe, B, T):
    G, J = NSA_KV_GROUPS, NSA_GROUP_SIZE
    n = B * T
    cg = zs[:, 2 * N_HEADS:2 * N_HEADS + 3 * N_HEADS].reshape(n, 3, G, J)
    cg_g = jnp.pad(cg.transpose(2, 0, 1, 3).reshape(G, n, 3 * J), ((0, 0), (0, 0), (0, LANES - 3 * J)))
    slopes = jnp.asarray(2.0 ** (-8.0 * (np.arange(N_HEADS, dtype=np.float32) + 1.0) / N_HEADS), F32)
    k_cmp, v_cmp = _compress(zm, cmp_w, cmp_pe, B, T)
    part, sel_t = _nsa_a(zm, k_cmp, v_cmp, cg_g, slopes, B, T)
    return _nsa_b(zm, sel_t, cg_g, part, B, T)


def _merge_kernel(h_ref, y0, y1, y2, y3, g0, g1, g2, g3, b0, b1, b2, b3, o_ref):
    h = h_ref[...]
    acc = None
    for y, gw, bw in ((y0, g0, b0), (y1, g1, b1), (y2, g2, b2), (y3, g3, b3)):
        term = jax.nn.sigmoid(_dot(h, gw[...])) * _dot(y[...], bw[...])
        acc = term if acc is None else acc + term
    o_ref[...] = acc.astype(BF16)


def _merge(h, ys, w_gate, w_branch, tm=512, tn=256):
    n, dm = h.shape
    tm = min(tm, n)
    nj = dm // tn
    y_spec = pl.BlockSpec((tm, MIX_WIDTH), lambda i, j: (i, 0))
    gate_specs = [pl.BlockSpec((dm, tn), functools.partial(lambda i, j, b: (0, b * nj + j), b=b)) for b in range(N_BRANCH)]
    br_specs = [pl.BlockSpec((None, MIX_WIDTH, tn), functools.partial(lambda i, j, b: (b, 0, j), b=b)) for b in range(N_BRANCH)]
    return pl.pallas_call(
        _merge_kernel, grid=(n // tm, nj),
        in_specs=[pl.BlockSpec((tm, dm), lambda i, j: (i, 0))] + [y_spec] * N_BRANCH + gate_specs + br_specs,
        out_specs=pl.BlockSpec((tm, tn), lambda i, j: (i, j)),
        out_shape=jax.ShapeDtypeStruct((n, dm), BF16),
        compiler_params=_cparams(("parallel", "parallel")), name="gated_merge",
    )(h, *ys, *([w_gate] * N_BRANCH), *([w_branch] * N_BRANCH))


def _expert_kernel(be_ref, nact_ref, x_ref, w1_ref, w3_ref, w2_ref, o_ref, w1b, w3b, w2b):
    i = pl.program_id(0)
    e = be_ref[i]
    prev = be_ref[jnp.maximum(i - 1, 0)]

    @pl.when((i == 0) | (prev != e))
    def _():
        w1b[...] = w1_ref[...].astype(BF16)
        w3b[...] = w3_ref[...].astype(BF16)
        w2b[...] = w2_ref[...].astype(BF16)

    @pl.when(i < nact_ref[0])
    def _():
        x = x_ref[...]
        a1 = _dot(x, w1b[...])
        a = (a1 * jax.nn.sigmoid(a1)) * _dot(x, w3b[...])
        o_ref[...] = _dot(a.astype(BF16), w2b[...])

    @pl.when(i >= nact_ref[0])
    def _():
        o_ref[...] = jnp.zeros_like(o_ref)


def _experts(xg, blk_exp, n_active, w1, w3, w2, layer):
    p, dm = xg.shape
    n_blk = p // MOE_BLOCK
    ff = w1.shape[-1]
    grid_spec = pltpu.PrefetchScalarGridSpec(
        num_scalar_prefetch=2, grid=(n_blk,),
        in_specs=[pl.BlockSpec((MOE_BLOCK, dm), lambda i, be, na: (i, 0)),
                  pl.BlockSpec((None, None, dm, ff), lambda i, be, na: (layer, be[i], 0, 0)),
                  pl.BlockSpec((None, None, dm, ff), lambda i, be, na: (layer, be[i], 0, 0)),
                  pl.BlockSpec((None, None, ff, dm), lambda i, be, na: (layer, be[i], 0, 0))],
        out_specs=pl.BlockSpec((MOE_BLOCK, dm), lambda i, be, na: (i, 0)),
        scratch_shapes=[pltpu.VMEM((dm, ff), BF16), pltpu.VMEM((dm, ff), BF16), pltpu.VMEM((ff, dm), BF16)])
    return pl.pallas_call(
        _expert_kernel, grid_spec=grid_spec,
        out_shape=jax.ShapeDtypeStruct((p, dm), F32),
        compiler_params=_cparams(("arbitrary",)), name="moe_experts",
    )(blk_exp, n_active, xg, w1, w3, w2)


def _dispatch_plan(e_idx):
    n, K = e_idx.shape
    nk = n * K
    i32 = jnp.int32
    flat_e = e_idx.reshape(-1)
    order = jnp.argsort(flat_e).astype(i32)
    rank = jnp.argsort(order).astype(i32)
    experts = jnp.arange(N_EXPERTS, dtype=i32)
    hot = flat_e[:, None] == experts[None, :]
    counts = jnp.sum(hot.astype(i32), axis=0)
    padded = (counts + MOE_BLOCK - 1) // MOE_BLOCK * MOE_BLOCK
    pad_end = jnp.cumsum(padded)
    pad_start = pad_end - padded
    start = jnp.cumsum(counts) - counts
    lookup = lambda table, onehot: jnp.sum(jnp.where(onehot, table[None, :], 0), axis=1)
    dest = lookup(pad_start - start, hot) + rank
    n_blk = (nk + MOE_BLOCK - 1) // MOE_BLOCK + N_EXPERTS
    blk_first = jnp.arange(n_blk, dtype=i32) * MOE_BLOCK
    blk_exp = jnp.minimum(jnp.sum((pad_end[None, :] <= blk_first[:, None]).astype(i32), axis=1), N_EXPERTS - 1)
    blk_hot = blk_exp[:, None] == experts[None, :]
    within = jnp.arange(MOE_BLOCK, dtype=i32)[None, :]
    j = (blk_first - lookup(pad_start, blk_hot))[:, None] + within
    sorted_pos = jnp.clip(lookup(start, blk_hot)[:, None] + j, 0, nk - 1)
    src = jnp.take(order, sorted_pos.reshape(-1), mode="clip") // K
    tok_pad = jnp.where((j < lookup(counts, blk_hot)[:, None]).reshape(-1), src, n - 1)
    n_active = (pad_end[-1] // MOE_BLOCK).astype(i32).reshape(1)
    return tok_pad, blk_exp.astype(i32), n_active, dest.reshape(n, K)


def _moe(h, route, w1, w3, w2, layer):
    e_idx = route[:, :TOP_K_IN_GROUP].astype(jnp.int32)
    tok_pad, blk_exp, n_active, dest = _dispatch_plan(e_idx)
    xg = jnp.take(h, tok_pad, axis=0, mode="clip")
    y = _experts(xg, blk_exp, n_active, w1, w3, w2, layer)
    return jnp.take(y, dest[:, 0], axis=0, mode="clip"), jnp.take(y, dest[:, 1], axis=0, mode="clip")


PACK_TILE = 512


def _pack_kernel(offs_ref, wt_ref, o_ref):
    del offs_ref
    o_ref[...] = wt_ref[0].T.astype(BF16)


def _pack_columns(w_in_t, layer, col_offsets):
    _, _, k = w_in_t.shape
    nt = len(col_offsets)
    grid_spec = pltpu.PrefetchScalarGridSpec(
        num_scalar_prefetch=1, grid=(nt,),
        in_specs=[pl.BlockSpec((pl.Element(1), pl.Element(PACK_TILE), pl.Element(k)),
                               lambda j, offs: (layer, pl.multiple_of(offs[j], 8), 0))],
        out_specs=pl.BlockSpec((k, PACK_TILE), lambda j, offs: (0, j)))
    return pl.pallas_call(
        _pack_kernel, grid_spec=grid_spec,
        out_shape=jax.ShapeDtypeStruct((k, nt * PACK_TILE), BF16),
        compiler_params=_cparams(("parallel",)), name="pack_w_in",
    )(jnp.asarray(col_offsets, jnp.int32), w_in_t)


def _tile_offsets(groups):
    offs = []
    for i in groups:
        assert COL_WIDTHS[i] % PACK_TILE == 0 or COL_WIDTHS[i] < PACK_TILE
        offs += list(range(COL_OFFSETS[i], COL_OFFSETS[i + 1], PACK_TILE))
    return offs


def _pack_w_in(w_in_t, layer):
    wide = [i for i in MAIN_ORDER if COL_WIDTHS[i] >= PACK_TILE]
    narrow = [i for i in MAIN_ORDER if COL_WIDTHS[i] < PACK_TILE]
    assert MAIN_ORDER == tuple(wide + narrow) and narrow == list(range(narrow[0], narrow[-1] + 1))
    kv_lo, kv_hi = COL_OFFSETS[narrow[0]], COL_OFFSETS[narrow[-1] + 1]
    assert (kv_hi - kv_lo) % PACK_TILE == 0
    main_offs = _tile_offsets(wide) + list(range(kv_lo, kv_hi, PACK_TILE))
    w_main = _pack_columns(w_in_t, layer, main_offs)
    w_gate = _pack_columns(w_in_t, layer, _tile_offsets([I_MERGE]))
    rows = lambda i: w_in_t[layer, COL_OFFSETS[i]:COL_OFFSETS[i + 1]]
    small_t = jnp.concatenate([rows(I_AI), rows(I_AF), rows(I_CG)], axis=0)
    small_t = jnp.pad(small_t, ((0, LANES - small_t.shape[0]), (0, 0)))
    return w_main, small_t, w_gate


def _router_weights(wg, bg, we, be):
    w = jnp.concatenate([wg, we], axis=1)
    w = jnp.pad(w, ((0, 0), (0, LANES - w.shape[1])))
    hi = w.astype(BF16)
    lo = (w - hi.astype(F32)).astype(BF16)
    b = jnp.concatenate([bg, be])
    b = jnp.pad(b, (0, LANES - b.shape[0])).reshape(1, LANES).astype(F32)
    return hi, lo, b


def kernel(x, w_in, mlstm_conv, mlstm_gate_bias, mlstm_norm, ret_norm, nsa_cmp_w, nsa_cmp_pe, sgu_norm, sgu_w, sgu_b, w_branch, w_out, norm_mix, norm_ffn, router_group_w, router_group_b, router_expert_w, router_expert_b, expert_w1, expert_w3, expert_w2, norm_final):
    B, T, D = x.shape
    n = B * T
    xs, moe = x.reshape(n, D), None
    w_in_t = jnp.swapaxes(w_in, 1, 2)
    for l in range(DEPTH):
        w_main, w_small_t, w_gate = _pack_w_in(w_in_t, l)
        if moe is None:
            h, = _norm(xs, norm_mix[l])
        else:
            xs, h = _norm(xs, norm_mix[l], moe=moe)
        zm = _matmul(h, w_main, out_dtype=BF16, name="proj_main")
        zs = _matmul(h, w_small_t, out_dtype=F32, w_rows_are_outputs=True, name="proj_small")
        zst = zs[:, :2 * N_HEADS].T
        ya = _mlstm(zm, zs, zst, mlstm_conv[l], mlstm_gate_bias[l], mlstm_norm[l], B, T)
        yb = _retention(zm, ret_norm[l], B, T)
        yc = _nsa(zm, zs, nsa_cmp_w[l], nsa_cmp_pe[l], B, T)
        yd = _sgu(zm, sgu_norm[l], sgu_w[l], sgu_b[l], n)
        merged = _merge(h, (ya, yb, yc, yd), w_gate, w_branch[l].astype(BF16))
        x_mid = _matmul(merged, w_out[l].astype(BF16), out_dtype=F32, residual=xs, name="proj_out")
        h2, route = _norm(x_mid, norm_ffn[l],
                          router_w=_router_weights(router_group_w[l], router_group_b[l],
                                                   router_expert_w[l], router_expert_b[l]))
        m0, m1 = _moe(h2, route, expert_w1, expert_w3, expert_w2, l)
        xs, moe = x_mid, (m0, m1, route)
    out, = _norm(xs, norm_final, moe=moe, final=True)
    return out.reshape(B, T, D)
```

```python
import functools

import numpy as np
import jax
import jax.numpy as jnp
from jax import lax
from jax.experimental import pallas as pl
from jax.experimental.pallas import tpu as pltpu

F32 = jnp.float32
BF16 = jnp.bfloat16

D_MODEL = 4096
DEPTH = 2
HEAD_DIM = 128
N_BRANCH = 4
MIX_WIDTH = D_MODEL // N_BRANCH
N_HEADS = MIX_WIDTH // HEAD_DIM
CONV_WIDTH = 4
MLSTM_CHUNK = 128
RET_CHUNK = 128
NSA_KV_GROUPS = 2
NSA_GROUP_SIZE = N_HEADS // NSA_KV_GROUPS
KV_WIDTH = NSA_KV_GROUPS * HEAD_DIM
CMP_BLOCK = 32
CMP_STRIDE = 16
SEL_BLOCK = 64
SEL_TOPN = 16
WINDOW = 512
Q_BLOCK = 128
SGU_CHUNK = 128
N_GROUPS = 8
EXPERTS_PER_GROUP = 8
N_EXPERTS = N_GROUPS * EXPERTS_PER_GROUP
TOP_K_IN_GROUP = 2
EXPERT_FF = 256
MOE_BLOCK = 128
RMS_EPS = 1e-6
NEG_INF = -1e30
FORCE_SCORE = 1e9

COL_WIDTHS = (
    MIX_WIDTH, MIX_WIDTH, MIX_WIDTH, MIX_WIDTH, N_HEADS, N_HEADS,
    MIX_WIDTH, MIX_WIDTH, MIX_WIDTH, MIX_WIDTH,
    MIX_WIDTH, KV_WIDTH, KV_WIDTH, KV_WIDTH, KV_WIDTH, KV_WIDTH, KV_WIDTH, 3 * N_HEADS,
    MIX_WIDTH, MIX_WIDTH,
    N_BRANCH * D_MODEL,
)
COL_OFFSETS = tuple(int(v) for v in np.concatenate([[0], np.cumsum(COL_WIDTHS)]))
(I_AQ, I_AK, I_AV, I_AO, I_AI, I_AF, I_BQ, I_BK, I_BV, I_BG, I_CQ, I_CKC, I_CVC, I_CKS, I_CVS,
 I_CKW, I_CVW, I_CG, I_DU, I_DV, I_MERGE) = range(21)

MAIN_ORDER = (I_AQ, I_AK, I_AV, I_AO, I_BQ, I_BK, I_BV, I_BG, I_CQ, I_DU, I_DV,
              I_CKC, I_CVC, I_CKS, I_CVS, I_CKW, I_CVW)
MAIN_WIDTH = sum(COL_WIDTHS[i] for i in MAIN_ORDER)
LANES = 128
BLK_AQ, BLK_AK, BLK_AV, BLK_AO, BLK_BQ, BLK_BK, BLK_BV, BLK_BG, BLK_CQ, BLK_DU, BLK_DV = range(11)
KV_BASE = 11 * MIX_WIDTH // HEAD_DIM
VMEM_LIMIT = 48 * 1024 * 1024


def _cparams(sem, vmem=VMEM_LIMIT):
    return pltpu.CompilerParams(dimension_semantics=sem, vmem_limit_bytes=vmem)


def _dot(a, b):
    return jnp.dot(a, b, preferred_element_type=F32)


def _dot_nt(a, b):
    return lax.dot_general(a, b, (((1,), (1,)), ((), ())), preferred_element_type=F32)


def _split3(x):
    hi = x.astype(BF16)
    r1 = x - hi.astype(F32)
    mid = r1.astype(BF16)
    lo = (r1 - mid.astype(F32)).astype(BF16)
    return hi, mid, lo


def _dot_exact_rhs(x, m_bf16):
    hi, mid, lo = _split3(x)
    return _dot(hi, m_bf16) + _dot(mid, m_bf16) + _dot(lo, m_bf16)


def _log_sigmoid(x):
    return jnp.minimum(x, 0.0) - jnp.log1p(jnp.exp(-jnp.abs(x)))


def _row_gather(idx_ref, src_hbm, dst, sem, count):
    def issue(r, carry):
        pltpu.make_async_copy(src_hbm.at[pl.ds(idx_ref[0, r], 1)], dst.at[pl.ds(r, 1)], sem).start()
        return carry
    lax.fori_loop(0, count, issue, 0)


def _row_gather_wait(src_hbm, dst, sem, count):
    def wait(r, carry):
        pltpu.make_async_copy(src_hbm.at[pl.ds(0, 1)], dst.at[pl.ds(r, 1)], sem).wait()
        return carry
    lax.fori_loop(0, count, wait, 0)


def _norm_kernel(*refs, moe, router, final):
    x = refs[0][...]
    rows = x.shape[0]
    pos = 1
    n_scratch = 0
    if moe:
        idx_ref, idx_next_ref, y_hbm, rt_ref = refs[pos:pos + 4]
        pos += 4
        gbuf, sems = refs[-2:]
        n_scratch = 2
        i, last = pl.program_id(0), pl.num_programs(0) - 1
        slot = lax.rem(i, 2)

        @pl.when(i == 0)
        def _():
            _row_gather(idx_ref, y_hbm, gbuf.at[0], sems.at[0], 2 * rows)

        @pl.when(i < last)
        def _():
            _row_gather(idx_next_ref, y_hbm, gbuf.at[1 - slot], sems.at[1 - slot], 2 * rows)

        _row_gather_wait(y_hbm, gbuf.at[slot], sems.at[slot], 2 * rows)
        x = x + (rt_ref[:, 2:3] * gbuf[slot, 0:rows, :] + rt_ref[:, 3:4] * gbuf[slot, rows:2 * rows, :])
    g_ref = refs[pos]
    pos += 1
    if router:
        whi_ref, wlo_ref, rb_ref = refs[pos:pos + 3]
        pos += 3
    outs = refs[pos:len(refs) - n_scratch]
    y = x * lax.rsqrt(jnp.mean(x * x, axis=-1, keepdims=True) + RMS_EPS) * g_ref[...]
    o = 0
    if moe and not final:
        outs[o][...] = x
        o += 1
    if final:
        outs[o][...] = y
    elif router:
        outs[o][...] = _pack_bf16_pairs(y)
    else:
        outs[o][...] = y.astype(BF16)
    o += 1
    if router:
        y_hi = y.astype(BF16)
        y_lo = (y - y_hi.astype(F32)).astype(BF16)
        lg = _dot(y_hi, whi_ref[...]) + (_dot(y_hi, wlo_ref[...]) + _dot(y_lo, whi_ref[...])) + rb_ref[...]
        outs[o][...] = _route_rows(lg)


def _first_max(v, lane):
    mx = jnp.max(v, axis=-1, keepdims=True)
    return mx, jnp.min(jnp.where(v == mx, lane, LANES), axis=-1, keepdims=True)


def _route_rows(lg):
    low = -3.0e38
    lane = lax.broadcasted_iota(jnp.int32, lg.shape, 1)
    is_grp = lane < N_GROUPS
    gmax, g_idx = _first_max(jnp.where(is_grp, lg, low), lane)
    g_w = 1.0 / jnp.sum(jnp.where(is_grp, jnp.exp(lg - gmax), 0.0), axis=-1, keepdims=True)
    assert EXPERTS_PER_GROUP & (EXPERTS_PER_GROUP - 1) == 0
    grp_of_lane = lax.shift_right_arithmetic(lane - N_GROUPS, EXPERTS_PER_GROUP.bit_length() - 1)
    in_grp = (lane >= N_GROUPS) & (lane < N_GROUPS + N_EXPERTS) & (grp_of_lane == g_idx)
    el = jnp.where(in_grp, lg, low)
    v0, i0 = _first_max(el, lane)
    v1, i1 = _first_max(jnp.where(lane == i0, low, el), lane)
    e1 = jnp.exp(v1 - v0)
    p0 = g_w / (1.0 + e1)
    vals = (i0 - N_GROUPS).astype(F32), (i1 - N_GROUPS).astype(F32), p0, p0 * e1
    out = jnp.zeros(lg.shape, F32)
    for k, v in enumerate(vals):
        out = jnp.where(lane == k, v, out)
    return out


def _norm(x, gain, *, moe=None, router_w=None, final=False, rows=128):
    n, d = x.shape
    rows = min(rows, n)
    steps = n // rows
    row_spec = pl.BlockSpec((rows, d), lambda i: (i, 0))
    lane_spec = pl.BlockSpec((rows, LANES), lambda i: (i, 0))
    in_specs, args, scratch = [row_spec], [x], []
    if moe is not None:
        y, dest, route = moe
        K = dest.shape[1]
        idx = dest.reshape(steps, rows, K).transpose(0, 2, 1).reshape(steps, 1, K * rows)
        smem_blk = lambda f: pl.BlockSpec((None, 1, K * rows), f, memory_space=pltpu.SMEM)
        in_specs += [smem_blk(lambda i: (i, 0, 0)), smem_blk(lambda i: (jnp.minimum(i + 1, steps - 1), 0, 0)),
                     pl.BlockSpec(memory_space=pl.ANY), lane_spec]
        args += [idx, idx, y, route]
        scratch = [pltpu.VMEM((2, K * rows, d), F32), pltpu.SemaphoreType.DMA((2,))]
    in_specs.append(pl.BlockSpec((1, d), lambda i: (0, 0)))
    args.append(gain.reshape(1, d))
    out_shape, out_specs = [], []
    if moe is not None and not final:
        out_shape.append(jax.ShapeDtypeStruct((n, d), F32))
        out_specs.append(row_spec)
    if router_w is not None:
        out_shape.append(jax.ShapeDtypeStruct((n, d // 2), jnp.int32))
        out_specs.append(pl.BlockSpec((rows, d // 2), lambda i: (i, 0)))
    else:
        out_shape.append(jax.ShapeDtypeStruct((n, d), F32 if final else BF16))
        out_specs.append(row_spec)
    if router_w is not None:
        whi, wlo, rb = router_w
        in_specs += [pl.BlockSpec((d, LANES), lambda i: (0, 0))] * 2 + [pl.BlockSpec((1, LANES), lambda i: (0, 0))]
        args += [whi, wlo, rb]
        out_shape.append(jax.ShapeDtypeStruct((n, LANES), F32))
        out_specs.append(lane_spec)
    return pl.pallas_call(
        functools.partial(_norm_kernel, moe=moe is not None, router=router_w is not None, final=final),
        grid=(steps,), in_specs=in_specs, out_specs=out_specs, out_shape=out_shape, scratch_shapes=scratch,
        compiler_params=_cparams(("arbitrary",)), name="rmsnorm",
    )(*args)


def _mm_kernel(a_ref, w_ref, *rest, has_res, w_rows_are_outputs):
    w = w_ref[...].astype(BF16)
    acc = _dot_nt(a_ref[...], w) if w_rows_are_outputs else _dot(a_ref[...], w)
    if has_res:
        acc = acc + rest[0][...]
    rest[-1][...] = acc.astype(rest[-1].dtype)


def _matmul(a, w, *, out_dtype, residual=None, tm=1024, tn=512, w_rows_are_outputs=False, name="matmul"):
    m, k = a.shape
    nc = w.shape[0] if w_rows_are_outputs else w.shape[1]
    tm, tn = min(tm, m), min(tn, nc)
    w_spec = pl.BlockSpec((tn, k), lambda i, j: (j, 0)) if w_rows_are_outputs else pl.BlockSpec((k, tn), lambda i, j: (0, j))
    in_specs = [pl.BlockSpec((tm, k), lambda i, j: (i, 0)), w_spec]
    args = [a, w]
    if residual is not None:
        in_specs.append(pl.BlockSpec((tm, tn), lambda i, j: (i, j)))
        args.append(residual)
    return pl.pallas_call(
        functools.partial(_mm_kernel, has_res=residual is not None, w_rows_are_outputs=w_rows_are_outputs),
        grid=(m // tm, nc // tn), in_specs=in_specs,
        out_specs=pl.BlockSpec((tm, tn), lambda i, j: (i, j)),
        out_shape=jax.ShapeDtypeStruct((m, nc), out_dtype),
        compiler_params=_cparams(("parallel", "parallel")), name=name,
    )(*args)


def _mlstm_kernel(aq_ref, ak_ref, av_ref, ao_ref, zs_ref, zst_ref, cw_ref, gbc_ref, gbr_ref, gain_ref,
                  tril_ref, triu_ref, o_ref, xbuf, qk_s, ct_s, n_s, m_s):
    L, d, H, W = MLSTM_CHUNK, HEAD_DIM, N_HEADS, MIX_WIDTH
    c = pl.program_id(1)

    @pl.when(c == 0)
    def _():
        xbuf[0:8, :] = jnp.zeros((8, 2 * W), F32)
        ct_s[...] = jnp.zeros_like(ct_s)
        n_s[...] = jnp.zeros_like(n_s)
        m_s[...] = jnp.zeros_like(m_s)

    @pl.when(c > 0)
    def _():
        xbuf[0:8, :] = xbuf[L:L + 8, :]

    xbuf[8:L + 8, 0:W] = aq_ref[...].astype(F32)
    xbuf[8:L + 8, W:2 * W] = ak_ref[...].astype(F32)
    base = 8 - (CONV_WIDTH - 1)
    conv = cw_ref[0:1, :] * xbuf[base:base + L, :]
    for j in range(1, CONV_WIDTH):
        conv = conv + cw_ref[j:j + 1, :] * xbuf[base + j:base + j + L, :]
    qk_s[...] = conv * jax.nn.sigmoid(conv)

    pre_c = zs_ref[...] + gbc_ref[...]
    ls_c = _log_sigmoid(pre_c)
    hi, mid, lo = _split3(ls_c)
    tril = tril_ref[...]
    bc = _dot(tril, hi) + _dot(tril, mid) + _dot(tril, lo)
    pre_r = zst_ref[...] + gbr_ref[...]
    br = _dot_exact_rhs(_log_sigmoid(pre_r), triu_ref[...])

    row = lax.broadcasted_iota(jnp.int32, (L, L), 0)
    col = lax.broadcasted_iota(jnp.int32, (L, L), 1)
    causal = row >= col
    scale = d ** -0.5
    for h in range(H):
        sl = slice(h * d, (h + 1) * d)
        q = qk_s[:, sl] * scale
        k = qk_s[:, W + h * d:W + (h + 1) * d]
        v = av_ref[:, sl]
        i_col, b_col = pre_c[:, h:h + 1], bc[:, H + h:H + h + 1]
        i_row, b_row = pre_r[h:h + 1, :], br[H + h:H + h + 1, :]
        m_prev = m_s[h:h + 1, 0:1]
        log_d = jnp.where(causal, b_col - b_row + i_row, NEG_INF)
        log_inter = b_col + m_prev
        m_row = jnp.maximum(jnp.max(log_d, axis=-1, keepdims=True), log_inter)
        qb, kb = q.astype(BF16), k.astype(BF16)
        s = _dot_nt(qb, kb) * jnp.exp(log_d - m_row)
        w_inter = jnp.exp(log_inter - m_row)
        num = _dot(s.astype(BF16), v) + w_inter * _dot(qb, ct_s[h].astype(BF16))
        den = jnp.sum(s, axis=-1, keepdims=True) + w_inter * jnp.sum(q * n_s[h:h + 1, :], axis=-1, keepdims=True)
        hh = num / jnp.maximum(jnp.abs(den), jnp.exp(-m_row))
        b_last = b_row[:, L - 1:L]
        log_w_row = b_last - b_row + i_row
        m_new = jnp.maximum(b_last + m_prev, jnp.max(log_w_row, axis=-1, keepdims=True))
        w_col = jnp.exp(b_last - b_col + i_col - m_new)
        decay = jnp.exp(b_last + m_prev - m_new)
        vw = (v.astype(F32) * w_col).astype(BF16)
        ct_s[h] = decay * ct_s[h] + _dot(k.T.astype(BF16), vw)
        n_s[h:h + 1, :] = decay * n_s[h:h + 1, :] + jnp.sum(k * w_col, axis=0, keepdims=True)
        m_s[h:h + 1, :] = jnp.broadcast_to(m_new, (1, LANES))
        hn = hh * lax.rsqrt(jnp.mean(hh * hh, axis=-1, keepdims=True) + RMS_EPS)
        o_ref[:, sl] = (hn * gain_ref[:, sl] * jax.nn.sigmoid(ao_ref[:, sl].astype(F32))).astype(BF16)


def _tri_consts(L):
    r = np.arange(L)
    tril = (r[:, None] >= r[None, :]).astype(np.float32)
    return jnp.asarray(tril, BF16), jnp.asarray(tril.T, BF16)


def _mlstm(zm, zs, zst, conv_w, gate_bias, gain, B, T):
    L, W = MLSTM_CHUNK, MIX_WIDTH
    nc = T // L
    n = B * T
    gbc = jnp.zeros((1, LANES), F32).at[0, :2 * N_HEADS].set(gate_bias.reshape(-1))
    gbr = jnp.broadcast_to(gate_bias.reshape(2 * N_HEADS, 1), (2 * N_HEADS, L)).astype(F32)
    tril, triu = _tri_consts(L)

    def blk(b_idx):
        return pl.BlockSpec((L, W), lambda b, c: (b * nc + c, b_idx))

    const = lambda shape: pl.BlockSpec(shape, lambda b, c: (0,) * len(shape))
    return pl.pallas_call(
        _mlstm_kernel, grid=(B, nc),
        in_specs=[blk(BLK_AQ), blk(BLK_AK), blk(BLK_AV), blk(BLK_AO),
                  pl.BlockSpec((L, LANES), lambda b, c: (b * nc + c, 0)),
                  pl.BlockSpec((2 * N_HEADS, L), lambda b, c: (0, b * nc + c)),
                  const((CONV_WIDTH, 2 * W)), const((1, LANES)), const((2 * N_HEADS, L)), const((1, W)),
                  const((L, L)), const((L, L))],
        out_specs=pl.BlockSpec((L, W), lambda b, c: (b * nc + c, 0)),
        out_shape=jax.ShapeDtypeStruct((n, W), BF16),
        scratch_shapes=[pltpu.VMEM((L + 8, 2 * W), F32), pltpu.VMEM((L, 2 * W), F32),
                        pltpu.VMEM((N_HEADS, HEAD_DIM, HEAD_DIM), F32), pltpu.VMEM((N_HEADS, HEAD_DIM), F32),
                        pltpu.VMEM((N_HEADS, LANES), F32)],
        compiler_params=_cparams(("parallel", "arbitrary")), name="mlstm",
    )(zm, zm, zm, zm, zs, zst, conv_w, gbc, gbr, gain.reshape(1, W), tril, triu)


def _ret_kernel(q_ref, k_ref, v_ref, g_ref, dec_ref, qd_ref, kd_ref, gain_ref, o_ref, r_s, *, chunk_decay):
    L, d, H = RET_CHUNK, HEAD_DIM, N_HEADS
    c = pl.program_id(1)

    @pl.when(c == 0)
    def _():
        r_s[...] = jnp.zeros_like(r_s)

    scale = d ** -0.5
    for h in range(H):
        sl = slice(h * d, (h + 1) * d)
        q = q_ref[:, sl].astype(F32)
        k = k_ref[:, sl].astype(F32) * scale
        v = v_ref[:, sl]
        s = _dot_nt(q.astype(BF16), k.astype(BF16)) * dec_ref[h]
        o = _dot(s.astype(BF16), v) + _dot((q * qd_ref[:, sl]).astype(BF16), r_s[h].astype(BF16))
        kd = (k * kd_ref[:, sl]).T.astype(BF16)
        r_s[h] = chunk_decay[h] * r_s[h] + _dot(kd, v)
        on = o * lax.rsqrt(jnp.mean(o * o, axis=-1, keepdims=True) + RMS_EPS)
        gate = g_ref[:, sl].astype(F32)
        o_ref[:, sl] = (on * gain_ref[:, sl] * (gate * jax.nn.sigmoid(gate))).astype(BF16)


def _retention(zm, gain, B, T):
    L, W, H, d = RET_CHUNK, MIX_WIDTH, N_HEADS, HEAD_DIM
    nc = T // L
    log_gamma = np.log(np.float32(1.0) - np.float32(2.0) ** (-5.0 - np.arange(H, dtype=np.float32))).astype(np.float32)
    pos = np.arange(L, dtype=np.float32)
    diff = pos[:, None] - pos[None, :]
    decay = np.where(diff >= 0, np.exp(log_gamma[:, None, None] * np.maximum(diff, 0.0)), 0.0).astype(np.float32)
    q_decay = np.exp(log_gamma[:, None] * (pos + 1.0)).astype(np.float32)
    k_decay = np.exp(log_gamma[:, None] * (L - 1.0 - pos)).astype(np.float32)
    chunk_decay = tuple(float(v) for v in np.exp(log_gamma * np.float32(L)).astype(np.float32))
    qd = jnp.asarray(np.repeat(q_decay.T, d, axis=1))
    kd = jnp.asarray(np.repeat(k_decay.T, d, axis=1))

    def blk(b_idx):
        return pl.BlockSpec((L, W), lambda b, c: (b * nc + c, b_idx))

    const = lambda shape: pl.BlockSpec(shape, lambda b, c: (0,) * len(shape))
    return pl.pallas_call(
        functools.partial(_ret_kernel, chunk_decay=chunk_decay), grid=(B, nc),
        in_specs=[blk(BLK_BQ), blk(BLK_BK), blk(BLK_BV), blk(BLK_BG),
                  const((H, L, L)), const((L, W)), const((L, W)), const((1, W))],
        out_specs=pl.BlockSpec((L, W), lambda b, c: (b * nc + c, 0)),
        out_shape=jax.ShapeDtypeStruct((B * T, W), BF16),
        scratch_shapes=[pltpu.VMEM((H, d, d), F32)],
        compiler_params=_cparams(("parallel", "arbitrary")), name="retention",
    )(zm, zm, zm, zm, jnp.asarray(decay), qd, kd, gain.reshape(1, W))


def _gelu(x):
    return 0.5 * x * (1.0 + jnp.tanh(0.7978845608028654 * (x + 0.044715 * (x * x * x))))


def _sgu_kernel(u_ref, v_ref, ng_ref, w_ref, b_ref, o_ref):
    L, d, H = SGU_CHUNK, HEAD_DIM, N_HEADS
    v = _gelu(v_ref[...].astype(F32))
    vn = (v * lax.rsqrt(jnp.mean(v * v, axis=-1, keepdims=True) + RMS_EPS) * ng_ref[...]).astype(BF16)
    u = _gelu(u_ref[...].astype(F32))
    row = lax.broadcasted_iota(jnp.int32, (L, L), 0)
    col = lax.broadcasted_iota(jnp.int32, (L, L), 1)
    for g in range(H):
        sl = slice(g * d, (g + 1) * d)
        wm = jnp.where(row >= col, w_ref[g], 0.0).astype(BF16)
        mixed = _dot(wm, vn[:, sl]) + b_ref[:, sl]
        o_ref[:, sl] = (u[:, sl] * mixed).astype(BF16)


def _sgu(zm, norm_g, w_s, b_s, n):
    L, W, H, d = SGU_CHUNK, MIX_WIDTH, N_HEADS, HEAD_DIM
    bsb = jnp.repeat(b_s.T, d, axis=1)
    const = lambda shape: pl.BlockSpec(shape, lambda i: (0,) * len(shape))
    return pl.pallas_call(
        _sgu_kernel, grid=(n // L,),
        in_specs=[pl.BlockSpec((L, W), lambda i: (i, BLK_DU)), pl.BlockSpec((L, W), lambda i: (i, BLK_DV)),
                  const((1, W)), const((H, L, L)), const((L, W))],
        out_specs=pl.BlockSpec((L, W), lambda i: (i, 0)),
        out_shape=jax.ShapeDtypeStruct((n, W), BF16),
        compiler_params=_cparams(("parallel",)), name="sgu",
    )(zm, zm, norm_g.reshape(1, W), w_s, bsb)


def _compress_kernel(k_ref, v_ref, w_ref, pe_ref, ko_ref, vo_ref, buf):
    t = k_ref.shape[0]
    n16 = t // CMP_STRIDE
    buf[t:t + CMP_STRIDE, :] = jnp.zeros((CMP_STRIDE, HEAD_DIM), F32)
    for which, (src, dst) in enumerate(((k_ref, ko_ref), (v_ref, vo_ref))):
        buf[0:t, :] = src[...].astype(F32)
        acc = jnp.zeros((n16, HEAD_DIM), F32)
        for l in range(CMP_BLOCK):
            x = buf[pl.ds(l, n16, stride=CMP_STRIDE), :]
            acc = acc + _dot((x + pe_ref[which, l:l + 1, :]).astype(BF16), w_ref[which, l].astype(BF16))
        dst[...] = acc.astype(BF16)


def _compress(zm, cmp_w, cmp_pe, B, T):
    G, d = NSA_KV_GROUPS, HEAD_DIM
    n16 = T // CMP_STRIDE
    out = jax.ShapeDtypeStruct((B, G, n16, d), BF16)
    kblk = KV_BASE
    vblk = KV_BASE + G
    return pl.pallas_call(
        _compress_kernel, grid=(B, G),
        in_specs=[pl.BlockSpec((T, d), lambda b, g: (b, kblk + g)), pl.BlockSpec((T, d), lambda b, g: (b, vblk + g)),
                  pl.BlockSpec((2, CMP_BLOCK, d, d), lambda b, g: (0, 0, 0, 0)),
                  pl.BlockSpec((2, CMP_BLOCK, d), lambda b, g: (0, 0, 0))],
        out_specs=[pl.BlockSpec((None, None, n16, d), lambda b, g: (b, g, 0, 0))] * 2,
        out_shape=[out, out],
        scratch_shapes=[pltpu.VMEM((T + CMP_STRIDE, d), F32)],
        compiler_params=_cparams(("parallel", "parallel")), name="nsa_compress",
    )(zm, zm, cmp_w, cmp_pe)


def _softmax_rows(s):
    m = jnp.max(s, axis=-1, keepdims=True)
    p = jnp.exp(s - m)
    return p, jnp.sum(p, axis=-1, keepdims=True)


def _nsa_a_kernel(slope_ref, q_ref, kc_ref, vc_ref, *rest, n_back):
    nw = n_back + 1
    kw_refs, vw_refs = rest[:nw], rest[nw:2 * nw]
    cg_ref, ov_ref, part_ref, sel_ref, key_s = rest[2 * nw:]
    tq, d, J = Q_BLOCK, HEAD_DIM, NSA_GROUP_SIZE
    g, qi = pl.program_id(1), pl.program_id(2)
    t0 = qi * tq
    ncmp = kc_ref.shape[0]
    scale = d ** -0.5
    qs = [(q_ref[:, j * d:(j + 1) * d].astype(F32) * scale).astype(BF16) for j in range(J)]
    sg = jax.nn.sigmoid(cg_ref[...])

    r = lax.broadcasted_iota(jnp.int32, (tq, ncmp), 0)
    n_idx = lax.broadcasted_iota(jnp.int32, (tq, ncmp), 1)
    dist_i = t0 + r - CMP_STRIDE * n_idx - (CMP_BLOCK - 1)
    valid_c = dist_i >= 0
    dist_c = dist_i.astype(F32)
    t_col = t0 + lax.broadcasted_iota(jnp.int32, (tq, 1), 0)
    has_c = (t_col >= CMP_BLOCK - 1).astype(F32)
    kc, vc = kc_ref[...], vc_ref[...]
    psum = jnp.zeros((tq, ncmp), F32)
    o_cmp = []
    for j in range(J):
        slope = slope_ref[g * J + j]
        s = jnp.where(valid_c, _dot_nt(qs[j], kc) - slope * dist_c, NEG_INF)
        p, l = _softmax_rows(s)
        p = p * (has_c / l)
        o_cmp.append(_dot(p.astype(BF16), vc))
        psum = psum + p

    imp = _dot_exact_rhs(psum, ov_ref[...])
    blk = lax.broadcasted_iota(jnp.int32, (tq, LANES), 1)
    cur = (t0 + lax.broadcasted_iota(jnp.int32, (tq, LANES), 0)) // SEL_BLOCK
    valid_s = blk <= cur
    forced = (blk == 0) | (blk == cur) | (blk == cur - 1)
    key = jnp.where(valid_s, imp + jnp.where(forced, FORCE_SCORE, 0.0), NEG_INF)
    key_s[...] = key.T
    sub = lax.broadcasted_iota(jnp.int32, (8, tq), 0)
    groups = LANES // 8
    key_g = [key_s[8 * i:8 * i + 8, :] for i in range(groups)]
    rank = [jnp.zeros((8, tq), F32) for _ in range(groups)]
    for mp in range(LANES):
        rowv = jnp.broadcast_to(key_s[mp:mp + 1, :], (8, tq))
        for i in range(groups):
            if 8 * i > mp:
                ahead = rowv >= key_g[i]
            elif 8 * i + 7 <= mp:
                ahead = rowv > key_g[i]
            else:
                ahead = (rowv > key_g[i]) | ((sub > mp - 8 * i) & (rowv == key_g[i]))
            rank[i] = rank[i] + jnp.where(ahead, 1.0, 0.0)
    rank_t = jnp.concatenate(rank, axis=0)
    blk_t = lax.broadcasted_iota(jnp.int32, (LANES, tq), 0)
    cur_t = (t0 + lax.broadcasted_iota(jnp.int32, (LANES, tq), 1)) // SEL_BLOCK
    sel_ref[...] = jnp.where((rank_t < float(SEL_TOPN)) & (blk_t <= cur_t), 1.0, 0.0).astype(BF16)

    kw = jnp.concatenate([kr[...] for kr in kw_refs], axis=0)
    vw = jnp.concatenate([vr[...] for vr in vw_refs], axis=0)
    span = nw * tq
    rw = lax.broadcasted_iota(jnp.int32, (tq, span), 0)
    cw = lax.broadcasted_iota(jnp.int32, (tq, span), 1)
    dist_wi = rw + n_back * tq - cw
    mask_w = (dist_wi >= 0) & (dist_wi < WINDOW) & (t0 - n_back * tq + cw >= 0)
    dist_w = dist_wi.astype(F32)
    for j in range(J):
        slope = slope_ref[g * J + j]
        s = jnp.where(mask_w, _dot_nt(qs[j], kw) - slope * dist_w, NEG_INF)
        p, l = _softmax_rows(s)
        o_win = _dot((p / l).astype(BF16), vw)
        part_ref[:, j * d:(j + 1) * d] = sg[:, j:j + 1] * o_cmp[j] + sg[:, 2 * J + j:2 * J + j + 1] * o_win


def _nsa_consts(T):
    n16 = T // CMP_STRIDE
    n_sel = T // SEL_BLOCK
    n = np.arange(n16)[:, None]
    m = np.arange(LANES)[None, :]
    c_start, s_start = n * CMP_STRIDE, m * SEL_BLOCK
    overlap = ((c_start < s_start + SEL_BLOCK) & (c_start + CMP_BLOCK > s_start) & (n < n16 - 1) & (m < n_sel))
    return jnp.asarray(overlap.astype(np.float32), BF16)


def _nsa_a(zm, k_cmp, v_cmp, cg_g, slopes, B, T):
    G, J, d, tq = NSA_KV_GROUPS, NSA_GROUP_SIZE, HEAD_DIM, Q_BLOCK
    nq = T // tq
    n16 = T // CMP_STRIDE
    n_back = WINDOW // tq
    qblk = BLK_CQ * (MIX_WIDTH // (J * d))
    kwblk = KV_BASE + 4 * G
    vwblk = KV_BASE + 5 * G

    def win_spec(base, i):
        return pl.BlockSpec((tq, d), lambda b, g, qi: (b * nq + jnp.maximum(qi - n_back + i, 0), base + g))

    in_specs = ([pl.BlockSpec(memory_space=pltpu.SMEM),
                 pl.BlockSpec((tq, J * d), lambda b, g, qi: (b * nq + qi, qblk + g)),
                 pl.BlockSpec((None, None, n16, d), lambda b, g, qi: (b, g, 0, 0)),
                 pl.BlockSpec((None, None, n16, d), lambda b, g, qi: (b, g, 0, 0))]
                + [win_spec(kwblk, i) for i in range(n_back + 1)]
                + [win_spec(vwblk, i) for i in range(n_back + 1)]
                + [pl.BlockSpec((None, tq, LANES), lambda b, g, qi: (g, b * nq + qi, 0)),
                   pl.BlockSpec((n16, LANES), lambda b, g, qi: (0, 0))])
    return pl.pallas_call(
        functools.partial(_nsa_a_kernel, n_back=n_back), grid=(B, G, nq), in_specs=in_specs,
        out_specs=[pl.BlockSpec((tq, J * d), lambda b, g, qi: (b * nq + qi, g)),
                   pl.BlockSpec((None, None, LANES, tq), lambda b, g, qi: (b, g, 0, qi))],
        out_shape=[jax.ShapeDtypeStruct((B * T, MIX_WIDTH), F32), jax.ShapeDtypeStruct((B, G, LANES, T), BF16)],
        scratch_shapes=[pltpu.VMEM((LANES, tq), F32)],
        compiler_params=_cparams(("parallel", "parallel", "parallel")), name="nsa_cmp_win",
    )(slopes, zm, k_cmp, v_cmp, *([zm] * (2 * (n_back + 1))), cg_g, _nsa_consts(T))


def _nsa_b_kernel(cnt_ref, lst_ref, q_ref, ks_ref, vt_ref, sel_ref, e_ref, a_ref, srow_ref, cg_ref, part_ref, o_ref,
                  q_s, acc_s, *, tk, nk):
    tq, d, J = Q_BLOCK, HEAD_DIM, NSA_GROUP_SIZE
    b, g, qi = pl.program_id(0), pl.program_id(1), pl.program_id(2)
    lin = (b * pl.num_programs(1) + g) * pl.num_programs(2) + qi
    t0 = qi * tq
    scale = d ** -0.5
    for j in range(J):
        q_s[j * tq:(j + 1) * tq, :] = (q_ref[:, j * d:(j + 1) * d].astype(F32) * scale).astype(BF16)
    acc_s[...] = jnp.zeros_like(acc_s)
    r_minus_c = lax.broadcasted_iota(jnp.int32, (tk, tq), 1) - lax.broadcasted_iota(jnp.int32, (tk, tq), 0)

    def body(i, carry):
        m_prev, l_prev = carry
        kt = lst_ref[lin * nk + i]
        k_t = ks_ref[pl.ds(pl.multiple_of(kt * tk, tk), tk), :]
        off = t0 - kt * tk
        picked = _dot(e_ref[kt], sel_ref[...])
        bias = jnp.where((picked > 0.5) & (r_minus_c + off >= 0), 0.0, NEG_INF)
        s = _dot_nt(k_t, q_s[...]) - a_ref[...] + jnp.concatenate([bias] * J, axis=1)
        shift = srow_ref[...] * off.astype(F32)
        m_new = jnp.maximum(m_prev, jnp.max(s, axis=0, keepdims=True) - shift)
        p = jnp.exp(s - (m_new + shift))
        alpha = jnp.exp(m_prev - m_new)
        acc_s[...] = alpha * acc_s[...] + _dot(vt_ref[kt], p.astype(BF16))
        return m_new, alpha * l_prev + jnp.sum(p, axis=0, keepdims=True)

    init = (jnp.full((1, J * tq), NEG_INF, F32), jnp.zeros((1, J * tq), F32))
    _, l = lax.fori_loop(0, cnt_ref[lin], body, init)
    inv_l = 1.0 / l
    sg = jax.nn.sigmoid(cg_ref[...])
    for j in range(J):
        cols = slice(j * tq, (j + 1) * tq)
        o_sel = (acc_s[:, cols] * inv_l[:, cols]).T
        o_ref[:, j * d:(j + 1) * d] = (part_ref[:, j * d:(j + 1) * d] + sg[:, J + j:J + j + 1] * o_sel).astype(BF16)


def _nsa_b(zm, sel_t, cg_g, part, B, T, tk=512):
    G, J, d, tq = NSA_KV_GROUPS, NSA_GROUP_SIZE, HEAD_DIM, Q_BLOCK
    tk = min(tk, T)
    nq, nk = T // tq, T // tk
    qblk = BLK_CQ * (MIX_WIDTH // (J * d))
    ksblk = KV_BASE + 2 * G
    vs_col = (KV_BASE + 3 * G) * d
    per_tile = tk // SEL_BLOCK
    kt = np.arange(nk)[:, None, None]
    cc = np.arange(tk)[None, :, None]
    m = np.arange(LANES)[None, None, :]
    expand_t = jnp.asarray((m == kt * per_tile + cc // SEL_BLOCK).astype(np.float32), BF16)
    slopes_np = (2.0 ** (-8.0 * (np.arange(N_HEADS, dtype=np.float32) + 1.0) / N_HEADS)).astype(np.float32)
    r_minus_c = (np.arange(tq)[None, :] - np.arange(tk)[:, None]).astype(np.float32)
    alibi_t = jnp.asarray((slopes_np.reshape(G, 1, J, 1) * r_minus_c[None, :, None, :]).reshape(G, tk, J * tq))
    srow = jnp.asarray(np.repeat(slopes_np.reshape(G, J), tq, axis=1).reshape(G, 1, J * tq))
    vs_t = zm[:, vs_col:vs_col + G * d].reshape(B, nk, tk, G, d).transpose(0, 3, 1, 4, 2)

    active = sel_t[:, :, :nk * per_tile].reshape(B, G, nk, per_tile, nq, tq).max(axis=(3, 5)) > 0
    active = active.transpose(0, 1, 3, 2)
    tiles = jnp.arange(nk, dtype=jnp.int32)
    order = jnp.sort(jnp.where(active, tiles, tiles + nk), axis=-1) % nk
    counts = active.sum(axis=-1).astype(jnp.int32)

    grid_spec = pltpu.PrefetchScalarGridSpec(
        num_scalar_prefetch=2, grid=(B, G, nq),
        in_specs=[pl.BlockSpec((tq, J * d), lambda b, g, qi, c, o: (b * nq + qi, qblk + g)),
                  pl.BlockSpec((T, d), lambda b, g, qi, c, o: (b, ksblk + g)),
                  pl.BlockSpec((None, None, nk, d, tk), lambda b, g, qi, c, o: (b, g, 0, 0, 0)),
                  pl.BlockSpec((None, None, LANES, tq), lambda b, g, qi, c, o: (b, g, 0, qi)),
                  pl.BlockSpec((nk, tk, LANES), lambda b, g, qi, c, o: (0, 0, 0)),
                  pl.BlockSpec((None, tk, J * tq), lambda b, g, qi, c, o: (g, 0, 0)),
                  pl.BlockSpec((None, 1, J * tq), lambda b, g, qi, c, o: (g, 0, 0)),
                  pl.BlockSpec((None, tq, LANES), lambda b, g, qi, c, o: (g, b * nq + qi, 0)),
                  pl.BlockSpec((tq, J * d), lambda b, g, qi, c, o: (b * nq + qi, g))],
        out_specs=pl.BlockSpec((tq, J * d), lambda b, g, qi, c, o: (b * nq + qi, g)),
        scratch_shapes=[pltpu.VMEM((J * tq, d), BF16), pltpu.VMEM((d, J * tq), F32)])
    return pl.pallas_call(
        functools.partial(_nsa_b_kernel, tk=tk, nk=nk), grid_spec=grid_spec,
        out_shape=jax.ShapeDtypeStruct((B * T, MIX_WIDTH), BF16),
        compiler_params=_cparams(("parallel", "parallel", "parallel")), name="nsa_selected",
    )(counts.reshape(-1), order.reshape(-1).astype(jnp.int32), zm, zm, vs_t, sel_t, expand_t, alibi_t, srow, cg_g, part)


def _nsa(zm, zs, cmp_w, cmp_pe, B, T):
    G, J = NSA_KV_GROUPS, NSA_GROUP_SIZE
    n = B * T
    cg = zs[:, 2 * N_HEADS:2 * N_HEADS + 3 * N_HEADS].reshape(n, 3, G, J)
    cg_g = jnp.pad(cg.transpose(2, 0, 1, 3).reshape(G, n, 3 * J), ((0, 0), (0, 0), (0, LANES - 3 * J)))
    slopes = jnp.asarray(2.0 ** (-8.0 * (np.arange(N_HEADS, dtype=np.float32) + 1.0) / N_HEADS), F32)
    k_cmp, v_cmp = _compress(zm, cmp_w, cmp_pe, B, T)
    part, sel_t = _nsa_a(zm, k_cmp, v_cmp, cg_g, slopes, B, T)
    return _nsa_b(zm, sel_t, cg_g, part, B, T)


def _merge_kernel(h_ref, y0, y1, y2, y3, g0, g1, g2, g3, b0, b1, b2, b3, o_ref):
    h = h_ref[...]
    acc = None
    for y, gw, bw in ((y0, g0, b0), (y1, g1, b1), (y2, g2, b2), (y3, g3, b3)):
        term = jax.nn.sigmoid(_dot(h, gw[...])) * _dot(y[...], bw[...])
        acc = term if acc is None else acc + term
    o_ref[...] = acc.astype(BF16)


def _merge(h, ys, w_gate, w_branch, tm=512, tn=256):
    n, dm = h.shape
    tm = min(tm, n)
    nj = dm // tn
    y_spec = pl.BlockSpec((tm, MIX_WIDTH), lambda i, j: (i, 0))
    gate_specs = [pl.BlockSpec((dm, tn), functools.partial(lambda i, j, b: (0, b * nj + j), b=b)) for b in range(N_BRANCH)]
    br_specs = [pl.BlockSpec((None, MIX_WIDTH, tn), functools.partial(lambda i, j, b: (b, 0, j), b=b)) for b in range(N_BRANCH)]
    return pl.pallas_call(
        _merge_kernel, grid=(n // tm, nj),
        in_specs=[pl.BlockSpec((tm, dm), lambda i, j: (i, 0))] + [y_spec] * N_BRANCH + gate_specs + br_specs,
        out_specs=pl.BlockSpec((tm, tn), lambda i, j: (i, j)),
        out_shape=jax.ShapeDtypeStruct((n, dm), BF16),
        compiler_params=_cparams(("parallel", "parallel")), name="gated_merge",
    )(h, *ys, *([w_gate] * N_BRANCH), *([w_branch] * N_BRANCH))


def _pack_bf16_pairs(y):
    half = y.shape[1] // 2
    bits = lax.bitcast_convert_type(y.astype(BF16).astype(F32), jnp.int32)
    return bits[:, :half] | lax.shift_right_logical(bits[:, half:], 16)


def _unpack_bf16_pairs(w):
    first = lax.bitcast_convert_type(w & jnp.int32(-65536), F32).astype(BF16)
    second = lax.bitcast_convert_type(lax.shift_left(w, 16), F32).astype(BF16)
    return first, second


def _expert_kernel(be_ref, nact_ref, idx_ref, idx_next_ref, hp_hbm, w1_ref, w3_ref, w2_ref, o_ref,
                   xbuf, sems, w1b, w3b, w2b):
    i = pl.program_id(0)
    n_act = nact_ref[0]
    slot = lax.rem(i, 2)
    e = be_ref[i]
    prev = be_ref[jnp.maximum(i - 1, 0)]
    half = hp_hbm.shape[1]

    @pl.when((i == 0) & (n_act > 0))
    def _():
        _row_gather(idx_ref, hp_hbm, xbuf.at[0], sems.at[0], MOE_BLOCK)

    @pl.when(i + 1 < n_act)
    def _():
        _row_gather(idx_next_ref, hp_hbm, xbuf.at[1 - slot], sems.at[1 - slot], MOE_BLOCK)

    @pl.when((i == 0) | (prev != e))
    def _():
        w1b[...] = w1_ref[...].astype(BF16)
        w3b[...] = w3_ref[...].astype(BF16)
        w2b[...] = w2_ref[...].astype(BF16)

    @pl.when(i < n_act)
    def _():
        _row_gather_wait(hp_hbm, xbuf.at[slot], sems.at[slot], MOE_BLOCK)
        x_a, x_b = _unpack_bf16_pairs(xbuf[slot])
        a1 = _dot(x_a, w1b[0:half, :]) + _dot(x_b, w1b[half:, :])
        a3 = _dot(x_a, w3b[0:half, :]) + _dot(x_b, w3b[half:, :])
        a = (a1 * jax.nn.sigmoid(a1)) * a3
        o_ref[...] = _dot(a.astype(BF16), w2b[...])

    @pl.when(i >= n_act)
    def _():
        o_ref[...] = jnp.zeros_like(o_ref)


def _experts(hp, tok_pad, blk_exp, n_active, w1, w3, w2, layer):
    half = hp.shape[1]
    dm = 2 * half
    n_blk = tok_pad.shape[0] // MOE_BLOCK
    ff = w1.shape[-1]
    idx = tok_pad.reshape(n_blk, 1, MOE_BLOCK)
    smem_blk = lambda f: pl.BlockSpec((None, 1, MOE_BLOCK), f, memory_space=pltpu.SMEM)
    grid_spec = pltpu.PrefetchScalarGridSpec(
        num_scalar_prefetch=2, grid=(n_blk,),
        in_specs=[smem_blk(lambda i, be, na: (i, 0, 0)),
                  smem_blk(lambda i, be, na: (jnp.minimum(i + 1, n_blk - 1), 0, 0)),
                  pl.BlockSpec(memory_space=pl.ANY),
                  pl.BlockSpec((None, None, dm, ff), lambda i, be, na: (layer, be[i], 0, 0)),
                  pl.BlockSpec((None, None, dm, ff), lambda i, be, na: (layer, be[i], 0, 0)),
                  pl.BlockSpec((None, None, ff, dm), lambda i, be, na: (layer, be[i], 0, 0))],
        out_specs=pl.BlockSpec((MOE_BLOCK, dm), lambda i, be, na: (i, 0)),
        scratch_shapes=[pltpu.VMEM((2, MOE_BLOCK, half), jnp.int32), pltpu.SemaphoreType.DMA((2,)),
                        pltpu.VMEM((dm, ff), BF16), pltpu.VMEM((dm, ff), BF16), pltpu.VMEM((ff, dm), BF16)])
    return pl.pallas_call(
        _expert_kernel, grid_spec=grid_spec,
        out_shape=jax.ShapeDtypeStruct((n_blk * MOE_BLOCK, dm), F32),
        compiler_params=_cparams(("arbitrary",)), name="moe_experts",
    )(blk_exp, n_active, idx, idx, hp, w1, w3, w2)


def _dispatch_plan(e_idx):
    n, K = e_idx.shape
    nk = n * K
    i32 = jnp.int32
    flat_e = e_idx.reshape(-1)
    order = jnp.argsort(flat_e).astype(i32)
    rank = jnp.argsort(order).astype(i32)
    experts = jnp.arange(N_EXPERTS, dtype=i32)
    hot = flat_e[:, None] == experts[None, :]
    counts = jnp.sum(hot.astype(i32), axis=0)
    padded = (counts + MOE_BLOCK - 1) // MOE_BLOCK * MOE_BLOCK
    pad_end = jnp.cumsum(padded)
    pad_start = pad_end - padded
    start = jnp.cumsum(counts) - counts
    lookup = lambda table, onehot: jnp.sum(jnp.where(onehot, table[None, :], 0), axis=1)
    dest = lookup(pad_start - start, hot) + rank
    n_blk = (nk + MOE_BLOCK - 1) // MOE_BLOCK + N_EXPERTS
    blk_first = jnp.arange(n_blk, dtype=i32) * MOE_BLOCK
    blk_exp = jnp.minimum(jnp.sum((pad_end[None, :] <= blk_first[:, None]).astype(i32), axis=1), N_EXPERTS - 1)
    blk_hot = blk_exp[:, None] == experts[None, :]
    within = jnp.arange(MOE_BLOCK, dtype=i32)[None, :]
    j = (blk_first - lookup(pad_start, blk_hot))[:, None] + within
    sorted_pos = jnp.clip(lookup(start, blk_hot)[:, None] + j, 0, nk - 1)
    src = jnp.take(order, sorted_pos.reshape(-1), mode="clip") // K
    tok_pad = jnp.where((j < lookup(counts, blk_hot)[:, None]).reshape(-1), src, n - 1)
    n_active = (pad_end[-1] // MOE_BLOCK).astype(i32).reshape(1)
    return tok_pad, blk_exp.astype(i32), n_active, dest.reshape(n, K)


def _moe(h, route, w1, w3, w2, layer):
    e_idx = route[:, :TOP_K_IN_GROUP].astype(jnp.int32)
    tok_pad, blk_exp, n_active, dest = _dispatch_plan(e_idx)
    y = _experts(h, tok_pad, blk_exp, n_active, w1, w3, w2, layer)
    return y, dest


PACK_TILE = 512


def _pack_kernel(offs_ref, wt_ref, o_ref):
    del offs_ref
    o_ref[...] = wt_ref[0].T.astype(BF16)


def _pack_columns(w_in_t, layer, col_offsets):
    _, _, k = w_in_t.shape
    nt = len(col_offsets)
    grid_spec = pltpu.PrefetchScalarGridSpec(
        num_scalar_prefetch=1, grid=(nt,),
        in_specs=[pl.BlockSpec((pl.Element(1), pl.Element(PACK_TILE), pl.Element(k)),
                               lambda j, offs: (layer, pl.multiple_of(offs[j], 8), 0))],
        out_specs=pl.BlockSpec((k, PACK_TILE), lambda j, offs: (0, j)))
    return pl.pallas_call(
        _pack_kernel, grid_spec=grid_spec,
        out_shape=jax.ShapeDtypeStruct((k, nt * PACK_TILE), BF16),
        compiler_params=_cparams(("parallel",)), name="pack_w_in",
    )(jnp.asarray(col_offsets, jnp.int32), w_in_t)


def _tile_offsets(groups):
    offs = []
    for i in groups:
        assert COL_WIDTHS[i] % PACK_TILE == 0 or COL_WIDTHS[i] < PACK_TILE
        offs += list(range(COL_OFFSETS[i], COL_OFFSETS[i + 1], PACK_TILE))
    return offs


def _pack_w_in(w_in_t, layer):
    wide = [i for i in MAIN_ORDER if COL_WIDTHS[i] >= PACK_TILE]
    narrow = [i for i in MAIN_ORDER if COL_WIDTHS[i] < PACK_TILE]
    assert MAIN_ORDER == tuple(wide + narrow) and narrow == list(range(narrow[0], narrow[-1] + 1))
    kv_lo, kv_hi = COL_OFFSETS[narrow[0]], COL_OFFSETS[narrow[-1] + 1]
    assert (kv_hi - kv_lo) % PACK_TILE == 0
    main_offs = _tile_offsets(wide) + list(range(kv_lo, kv_hi, PACK_TILE))
    w_main = _pack_columns(w_in_t, layer, main_offs)
    w_gate = _pack_columns(w_in_t, layer, _tile_offsets([I_MERGE]))
    rows = lambda i: w_in_t[layer, COL_OFFSETS[i]:COL_OFFSETS[i + 1]]
    small_t = jnp.concatenate([rows(I_AI), rows(I_AF), rows(I_CG)], axis=0)
    small_t = jnp.pad(small_t, ((0, LANES - small_t.shape[0]), (0, 0)))
    return w_main, small_t, w_gate


def _router_weights(wg, bg, we, be):
    w = jnp.concatenate([wg, we], axis=1)
    w = jnp.pad(w, ((0, 0), (0, LANES - w.shape[1])))
    hi = w.astype(BF16)
    lo = (w - hi.astype(F32)).astype(BF16)
    b = jnp.concatenate([bg, be])
    b = jnp.pad(b, (0, LANES - b.shape[0])).reshape(1, LANES).astype(F32)
    return hi, lo, b


def kernel(x, w_in, mlstm_conv, mlstm_gate_bias, mlstm_norm, ret_norm, nsa_cmp_w, nsa_cmp_pe, sgu_norm, sgu_w, sgu_b, w_branch, w_out, norm_mix, norm_ffn, router_group_w, router_group_b, router_expert_w, router_expert_b, expert_w1, expert_w3, expert_w2, norm_final):
    B, T, D = x.shape
    n = B * T
    xs, moe = x.reshape(n, D), None
    w_in_t = jnp.swapaxes(w_in, 1, 2)
    for l in range(DEPTH):
        w_main, w_small_t, w_gate = _pack_w_in(w_in_t, l)
        if moe is None:
            h, = _norm(xs, norm_mix[l])
        else:
            xs, h = _norm(xs, norm_mix[l], moe=moe)
        zm = _matmul(h, w_main, out_dtype=BF16, name="proj_main")
        zs = _matmul(h, w_small_t, out_dtype=F32, w_rows_are_outputs=True, name="proj_small")
        zst = zs[:, :2 * N_HEADS].T
        ya = _mlstm(zm, zs, zst, mlstm_conv[l], mlstm_gate_bias[l], mlstm_norm[l], B, T)
        yb = _retention(zm, ret_norm[l], B, T)
        yc = _nsa(zm, zs, nsa_cmp_w[l], nsa_cmp_pe[l], B, T)
        yd = _sgu(zm, sgu_norm[l], sgu_w[l], sgu_b[l], n)
        merged = _merge(h, (ya, yb, yc, yd), w_gate, w_branch[l].astype(BF16))
        x_mid = _matmul(merged, w_out[l].astype(BF16), out_dtype=F32, residual=xs, name="proj_out")
        h2, route = _norm(x_mid, norm_ffn[l],
                          router_w=_router_weights(router_group_w[l], router_group_b[l],
                                                   router_expert_w[l], router_expert_b[l]))
        y_exp, dest = _moe(h2, route, expert_w1, expert_w3, expert_w2, l)
        xs, moe = x_mid, (y_exp, dest, route)
    out, = _norm(xs, norm_final, moe=moe, final=True)
    return out.reshape(B, T, D)
```

```python
import functools

import numpy as np
import jax
import jax.numpy as jnp
from jax import lax
from jax.experimental import pallas as pl
from jax.experimental.pallas import tpu as pltpu

F32 = jnp.float32
BF16 = jnp.bfloat16

D_MODEL = 4096
DEPTH = 2
HEAD_DIM = 128
N_BRANCH = 4
MIX_WIDTH = D_MODEL // N_BRANCH
N_HEADS = MIX_WIDTH // HEAD_DIM
CONV_WIDTH = 4
MLSTM_CHUNK = 128
RET_CHUNK = 128
NSA_KV_GROUPS = 2
NSA_GROUP_SIZE = N_HEADS // NSA_KV_GROUPS
KV_WIDTH = NSA_KV_GROUPS * HEAD_DIM
CMP_BLOCK = 32
CMP_STRIDE = 16
SEL_BLOCK = 64
SEL_TOPN = 16
WINDOW = 512
Q_BLOCK = 128
SGU_CHUNK = 128
N_GROUPS = 8
EXPERTS_PER_GROUP = 8
N_EXPERTS = N_GROUPS * EXPERTS_PER_GROUP
TOP_K_IN_GROUP = 2
EXPERT_FF = 256
MOE_BLOCK = 256
GATHER_UNROLL = 8
RMS_EPS = 1e-6
NEG_INF = -1e30
FORCE_SCORE = 1e9

COL_WIDTHS = (
    MIX_WIDTH, MIX_WIDTH, MIX_WIDTH, MIX_WIDTH, N_HEADS, N_HEADS,
    MIX_WIDTH, MIX_WIDTH, MIX_WIDTH, MIX_WIDTH,
    MIX_WIDTH, KV_WIDTH, KV_WIDTH, KV_WIDTH, KV_WIDTH, KV_WIDTH, KV_WIDTH, 3 * N_HEADS,
    MIX_WIDTH, MIX_WIDTH,
    N_BRANCH * D_MODEL,
)
COL_OFFSETS = tuple(int(v) for v in np.concatenate([[0], np.cumsum(COL_WIDTHS)]))
(I_AQ, I_AK, I_AV, I_AO, I_AI, I_AF, I_BQ, I_BK, I_BV, I_BG, I_CQ, I_CKC, I_CVC, I_CKS, I_CVS,
 I_CKW, I_CVW, I_CG, I_DU, I_DV, I_MERGE) = range(21)

MAIN_ORDER = (I_AQ, I_AK, I_AV, I_AO, I_BQ, I_BK, I_BV, I_BG, I_CQ, I_DU, I_DV,
              I_CKC, I_CVC, I_CKS, I_CVS, I_CKW, I_CVW)
MAIN_WIDTH = sum(COL_WIDTHS[i] for i in MAIN_ORDER)
LANES = 128
BLK_AQ, BLK_AK, BLK_AV, BLK_AO, BLK_BQ, BLK_BK, BLK_BV, BLK_BG, BLK_CQ, BLK_DU, BLK_DV = range(11)
KV_BASE = 11 * MIX_WIDTH // HEAD_DIM
VMEM_LIMIT = 48 * 1024 * 1024
MOE_VMEM_LIMIT = 56 * 1024 * 1024


def _cparams(sem, vmem=VMEM_LIMIT):
    return pltpu.CompilerParams(dimension_semantics=sem, vmem_limit_bytes=vmem)


def _dot(a, b):
    return jnp.dot(a, b, preferred_element_type=F32)


def _dot_nt(a, b):
    return lax.dot_general(a, b, (((1,), (1,)), ((), ())), preferred_element_type=F32)


def _split3(x):
    hi = x.astype(BF16)
    r1 = x - hi.astype(F32)
    mid = r1.astype(BF16)
    lo = (r1 - mid.astype(F32)).astype(BF16)
    return hi, mid, lo


def _dot_exact_rhs(x, m_bf16):
    hi, mid, lo = _split3(x)
    return _dot(hi, m_bf16) + _dot(mid, m_bf16) + _dot(lo, m_bf16)


def _log_sigmoid(x):
    return jnp.minimum(x, 0.0) - jnp.log1p(jnp.exp(-jnp.abs(x)))


def _row_gather(idx_ref, src_hbm, dst, sem, count):
    def issue(c, carry):
        for u in range(GATHER_UNROLL):
            r = c * GATHER_UNROLL + u
            pltpu.make_async_copy(src_hbm.at[pl.ds(idx_ref[0, r], 1)], dst.at[pl.ds(r, 1)], sem).start(priority=u % 2)
        return carry
    lax.fori_loop(0, count // GATHER_UNROLL, issue, 0)


def _row_gather_wait(src_hbm, dst, sem, count):
    for r in range(count):
        pltpu.make_async_copy(src_hbm.at[pl.ds(0, 1)], dst.at[pl.ds(r, 1)], sem).wait()


def _norm_kernel(*refs, moe, router, final):
    x = refs[0][...]
    rows = x.shape[0]
    pos = 1
    n_scratch = 0
    if moe:
        idx_ref, idx_next_ref, y_hbm, rt_ref = refs[pos:pos + 4]
        pos += 4
        gbuf, sems = refs[-2:]
        n_scratch = 2
        i, last = pl.program_id(0), pl.num_programs(0) - 1
        slot = lax.rem(i, 2)

        @pl.when(i == 0)
        def _():
            _row_gather(idx_ref, y_hbm, gbuf.at[0], sems.at[0], 2 * rows)

        @pl.when(i < last)
        def _():
            _row_gather(idx_next_ref, y_hbm, gbuf.at[1 - slot], sems.at[1 - slot], 2 * rows)

        _row_gather_wait(y_hbm, gbuf.at[slot], sems.at[slot], 2 * rows)
        x = x + (rt_ref[:, 2:3] * gbuf[slot, 0:rows, :] + rt_ref[:, 3:4] * gbuf[slot, rows:2 * rows, :])
    g_ref = refs[pos]
    pos += 1
    if router:
        whi_ref, wlo_ref, rb_ref = refs[pos:pos + 3]
        pos += 3
    outs = refs[pos:len(refs) - n_scratch]
    y = x * lax.rsqrt(jnp.mean(x * x, axis=-1, keepdims=True) + RMS_EPS) * g_ref[...]
    o = 0
    if moe and not final:
        outs[o][...] = x
        o += 1
    if final:
        outs[o][...] = y
    elif router:
        outs[o][...] = _pack_bf16_pairs(y)
    else:
        outs[o][...] = y.astype(BF16)
    o += 1
    if router:
        y_hi = y.astype(BF16)
        y_lo = (y - y_hi.astype(F32)).astype(BF16)
        lg = _dot(y_hi, whi_ref[...]) + (_dot(y_hi, wlo_ref[...]) + _dot(y_lo, whi_ref[...])) + rb_ref[...]
        outs[o][...] = _route_rows(lg)


def _first_max(v, lane):
    mx = jnp.max(v, axis=-1, keepdims=True)
    return mx, jnp.min(jnp.where(v == mx, lane, LANES), axis=-1, keepdims=True)


def _route_rows(lg):
    low = -3.0e38
    lane = lax.broadcasted_iota(jnp.int32, lg.shape, 1)
    is_grp = lane < N_GROUPS
    gmax, g_idx = _first_max(jnp.where(is_grp, lg, low), lane)
    g_w = 1.0 / jnp.sum(jnp.where(is_grp, jnp.exp(lg - gmax), 0.0), axis=-1, keepdims=True)
    assert EXPERTS_PER_GROUP & (EXPERTS_PER_GROUP - 1) == 0
    grp_of_lane = lax.shift_right_arithmetic(lane - N_GROUPS, EXPERTS_PER_GROUP.bit_length() - 1)
    in_grp = (lane >= N_GROUPS) & (lane < N_GROUPS + N_EXPERTS) & (grp_of_lane == g_idx)
    el = jnp.where(in_grp, lg, low)
    v0, i0 = _first_max(el, lane)
    v1, i1 = _first_max(jnp.where(lane == i0, low, el), lane)
    e1 = jnp.exp(v1 - v0)
    p0 = g_w / (1.0 + e1)
    vals = (i0 - N_GROUPS).astype(F32), (i1 - N_GROUPS).astype(F32), p0, p0 * e1
    out = jnp.zeros(lg.shape, F32)
    for k, v in enumerate(vals):
        out = jnp.where(lane == k, v, out)
    return out


def _norm(x, gain, *, moe=None, router_w=None, final=False, rows=128):
    n, d = x.shape
    rows = min(rows, n)
    steps = n // rows
    row_spec = pl.BlockSpec((rows, d), lambda i: (i, 0))
    lane_spec = pl.BlockSpec((rows, LANES), lambda i: (i, 0))
    in_specs, args, scratch = [row_spec], [x], []
    if moe is not None:
        y, dest, route = moe
        K = dest.shape[1]
        idx = dest.reshape(steps, rows, K).transpose(0, 2, 1).reshape(steps, 1, K * rows)
        smem_blk = lambda f: pl.BlockSpec((None, 1, K * rows), f, memory_space=pltpu.SMEM)
        in_specs += [smem_blk(lambda i: (i, 0, 0)), smem_blk(lambda i: (jnp.minimum(i + 1, steps - 1), 0, 0)),
                     pl.BlockSpec(memory_space=pl.ANY), lane_spec]
        args += [idx, idx, y, route]
        scratch = [pltpu.VMEM((2, K * rows, d), F32), pltpu.SemaphoreType.DMA((2,))]
    in_specs.append(pl.BlockSpec((1, d), lambda i: (0, 0)))
    args.append(gain.reshape(1, d))
    out_shape, out_specs = [], []
    if moe is not None and not final:
        out_shape.append(jax.ShapeDtypeStruct((n, d), F32))
        out_specs.append(row_spec)
    if router_w is not None:
        out_shape.append(jax.ShapeDtypeStruct((n, d // 2), jnp.int32))
        out_specs.append(pl.BlockSpec((rows, d // 2), lambda i: (i, 0)))
    else:
        out_shape.append(jax.ShapeDtypeStruct((n, d), F32 if final else BF16))
        out_specs.append(row_spec)
    if router_w is not None:
        whi, wlo, rb = router_w
        in_specs += [pl.BlockSpec((d, LANES), lambda i: (0, 0))] * 2 + [pl.BlockSpec((1, LANES), lambda i: (0, 0))]
        args += [whi, wlo, rb]
        out_shape.append(jax.ShapeDtypeStruct((n, LANES), F32))
        out_specs.append(lane_spec)
    return pl.pallas_call(
        functools.partial(_norm_kernel, moe=moe is not None, router=router_w is not None, final=final),
        grid=(steps,), in_specs=in_specs, out_specs=out_specs, out_shape=out_shape, scratch_shapes=scratch,
        compiler_params=_cparams(("arbitrary",)), name="rmsnorm",
    )(*args)


def _mm_kernel(a_ref, w_ref, *rest, has_res, w_rows_are_outputs):
    w = w_ref[...].astype(BF16)
    acc = _dot_nt(a_ref[...], w) if w_rows_are_outputs else _dot(a_ref[...], w)
    if has_res:
        acc = acc + rest[0][...]
    rest[-1][...] = acc.astype(rest[-1].dtype)


def _matmul(a, w, *, out_dtype, residual=None, tm=1024, tn=512, w_rows_are_outputs=False, name="matmul"):
    m, k = a.shape
    nc = w.shape[0] if w_rows_are_outputs else w.shape[1]
    tm, tn = min(tm, m), min(tn, nc)
    w_spec = pl.BlockSpec((tn, k), lambda i, j: (j, 0)) if w_rows_are_outputs else pl.BlockSpec((k, tn), lambda i, j: (0, j))
    in_specs = [pl.BlockSpec((tm, k), lambda i, j: (i, 0)), w_spec]
    args = [a, w]
    if residual is not None:
        in_specs.append(pl.BlockSpec((tm, tn), lambda i, j: (i, j)))
        args.append(residual)
    return pl.pallas_call(
        functools.partial(_mm_kernel, has_res=residual is not None, w_rows_are_outputs=w_rows_are_outputs),
        grid=(m // tm, nc // tn), in_specs=in_specs,
        out_specs=pl.BlockSpec((tm, tn), lambda i, j: (i, j)),
        out_shape=jax.ShapeDtypeStruct((m, nc), out_dtype),
        compiler_params=_cparams(("parallel", "parallel")), name=name,
    )(*args)


def _mlstm_kernel(aq_ref, ak_ref, av_ref, ao_ref, zs_ref, zst_ref, cw_ref, gbc_ref, gbr_ref, gain_ref,
                  tril_ref, triu_ref, o_ref, xbuf, qk_s, ct_s, n_s, m_s):
    L, d, H, W = MLSTM_CHUNK, HEAD_DIM, N_HEADS, MIX_WIDTH
    c = pl.program_id(1)

    @pl.when(c == 0)
    def _():
        xbuf[0:8, :] = jnp.zeros((8, 2 * W), F32)
        ct_s[...] = jnp.zeros_like(ct_s)
        n_s[...] = jnp.zeros_like(n_s)
        m_s[...] = jnp.zeros_like(m_s)

    @pl.when(c > 0)
    def _():
        xbuf[0:8, :] = xbuf[L:L + 8, :]

    xbuf[8:L + 8, 0:W] = aq_ref[...].astype(F32)
    xbuf[8:L + 8, W:2 * W] = ak_ref[...].astype(F32)
    base = 8 - (CONV_WIDTH - 1)
    conv = cw_ref[0:1, :] * xbuf[base:base + L, :]
    for j in range(1, CONV_WIDTH):
        conv = conv + cw_ref[j:j + 1, :] * xbuf[base + j:base + j + L, :]
    qk_s[...] = conv * jax.nn.sigmoid(conv)

    pre_c = zs_ref[...] + gbc_ref[...]
    ls_c = _log_sigmoid(pre_c)
    hi, mid, lo = _split3(ls_c)
    tril = tril_ref[...]
    bc = _dot(tril, hi) + _dot(tril, mid) + _dot(tril, lo)
    pre_r = zst_ref[...] + gbr_ref[...]
    br = _dot_exact_rhs(_log_sigmoid(pre_r), triu_ref[...])

    row = lax.broadcasted_iota(jnp.int32, (L, L), 0)
    col = lax.broadcasted_iota(jnp.int32, (L, L), 1)
    causal = row >= col
    scale = d ** -0.5
    for h in range(H):
        sl = slice(h * d, (h + 1) * d)
        q = qk_s[:, sl] * scale
        k = qk_s[:, W + h * d:W + (h + 1) * d]
        v = av_ref[:, sl]
        i_col, b_col = pre_c[:, h:h + 1], bc[:, H + h:H + h + 1]
        i_row, b_row = pre_r[h:h + 1, :], br[H + h:H + h + 1, :]
        m_prev = m_s[h:h + 1, 0:1]
        log_d = jnp.where(causal, b_col - b_row + i_row, NEG_INF)
        log_inter = b_col + m_prev
        m_row = jnp.maximum(jnp.max(log_d, axis=-1, keepdims=True), log_inter)
        qb, kb = q.astype(BF16), k.astype(BF16)
        s = _dot_nt(qb, kb) * jnp.exp(log_d - m_row)
        w_inter = jnp.exp(log_inter - m_row)
        num = _dot(s.astype(BF16), v) + w_inter * _dot(qb, ct_s[h].astype(BF16))
        den = jnp.sum(s, axis=-1, keepdims=True) + w_inter * jnp.sum(q * n_s[h:h + 1, :], axis=-1, keepdims=True)
        hh = num / jnp.maximum(jnp.abs(den), jnp.exp(-m_row))
        b_last = b_row[:, L - 1:L]
        log_w_row = b_last - b_row + i_row
        m_new = jnp.maximum(b_last + m_prev, jnp.max(log_w_row, axis=-1, keepdims=True))
        w_col = jnp.exp(b_last - b_col + i_col - m_new)
        decay = jnp.exp(b_last + m_prev - m_new)
        vw = (v.astype(F32) * w_col).astype(BF16)
        ct_s[h] = decay * ct_s[h] + _dot(k.T.astype(BF16), vw)
        n_s[h:h + 1, :] = decay * n_s[h:h + 1, :] + jnp.sum(k * w_col, axis=0, keepdims=True)
        m_s[h:h + 1, :] = jnp.broadcast_to(m_new, (1, LANES))
        hn = hh * lax.rsqrt(jnp.mean(hh * hh, axis=-1, keepdims=True) + RMS_EPS)
        o_ref[:, sl] = (hn * gain_ref[:, sl] * jax.nn.sigmoid(ao_ref[:, sl].astype(F32))).astype(BF16)


def _tri_consts(L):
    r = np.arange(L)
    tril = (r[:, None] >= r[None, :]).astype(np.float32)
    return jnp.asarray(tril, BF16), jnp.asarray(tril.T, BF16)


def _mlstm(zm, zs, zst, conv_w, gate_bias, gain, B, T):
    L, W = MLSTM_CHUNK, MIX_WIDTH
    nc = T // L
    n = B * T
    gbc = jnp.zeros((1, LANES), F32).at[0, :2 * N_HEADS].set(gate_bias.reshape(-1))
    gbr = jnp.broadcast_to(gate_bias.reshape(2 * N_HEADS, 1), (2 * N_HEADS, L)).astype(F32)
    tril, triu = _tri_consts(L)

    def blk(b_idx):
        return pl.BlockSpec((L, W), lambda b, c: (b * nc + c, b_idx))

    const = lambda shape: pl.BlockSpec(shape, lambda b, c: (0,) * len(shape))
    return pl.pallas_call(
        _mlstm_kernel, grid=(B, nc),
        in_specs=[blk(BLK_AQ), blk(BLK_AK), blk(BLK_AV), blk(BLK_AO),
                  pl.BlockSpec((L, LANES), lambda b, c: (b * nc + c, 0)),
                  pl.BlockSpec((2 * N_HEADS, L), lambda b, c: (0, b * nc + c)),
                  const((CONV_WIDTH, 2 * W)), const((1, LANES)), const((2 * N_HEADS, L)), const((1, W)),
                  const((L, L)), const((L, L))],
        out_specs=pl.BlockSpec((L, W), lambda b, c: (b * nc + c, 0)),
        out_shape=jax.ShapeDtypeStruct((n, W), BF16),
        scratch_shapes=[pltpu.VMEM((L + 8, 2 * W), F32), pltpu.VMEM((L, 2 * W), F32),
                        pltpu.VMEM((N_HEADS, HEAD_DIM, HEAD_DIM), F32), pltpu.VMEM((N_HEADS, HEAD_DIM), F32),
                        pltpu.VMEM((N_HEADS, LANES), F32)],
        compiler_params=_cparams(("parallel", "arbitrary")), name="mlstm",
    )(zm, zm, zm, zm, zs, zst, conv_w, gbc, gbr, gain.reshape(1, W), tril, triu)


def _ret_kernel(q_ref, k_ref, v_ref, g_ref, dec_ref, qd_ref, kd_ref, gain_ref, o_ref, r_s, *, chunk_decay):
    L, d, H = RET_CHUNK, HEAD_DIM, N_HEADS
    c = pl.program_id(1)

    @pl.when(c == 0)
    def _():
        r_s[...] = jnp.zeros_like(r_s)

    scale = d ** -0.5
    for h in range(H):
        sl = slice(h * d, (h + 1) * d)
        q = q_ref[:, sl].astype(F32)
        k = k_ref[:, sl].astype(F32) * scale
        v = v_ref[:, sl]
        s = _dot_nt(q.astype(BF16), k.astype(BF16)) * dec_ref[h]
        o = _dot(s.astype(BF16), v) + _dot((q * qd_ref[:, sl]).astype(BF16), r_s[h].astype(BF16))
        kd = (k * kd_ref[:, sl]).T.astype(BF16)
        r_s[h] = chunk_decay[h] * r_s[h] + _dot(kd, v)
        on = o * lax.rsqrt(jnp.mean(o * o, axis=-1, keepdims=True) + RMS_EPS)
        gate = g_ref[:, sl].astype(F32)
        o_ref[:, sl] = (on * gain_ref[:, sl] * (gate * jax.nn.sigmoid(gate))).astype(BF16)


def _retention(zm, gain, B, T):
    L, W, H, d = RET_CHUNK, MIX_WIDTH, N_HEADS, HEAD_DIM
    nc = T // L
    log_gamma = np.log(np.float32(1.0) - np.float32(2.0) ** (-5.0 - np.arange(H, dtype=np.float32))).astype(np.float32)
    pos = np.arange(L, dtype=np.float32)
    diff = pos[:, None] - pos[None, :]
    decay = np.where(diff >= 0, np.exp(log_gamma[:, None, None] * np.maximum(diff, 0.0)), 0.0).astype(np.float32)
    q_decay = np.exp(log_gamma[:, None] * (pos + 1.0)).astype(np.float32)
    k_decay = np.exp(log_gamma[:, None] * (L - 1.0 - pos)).astype(np.float32)
    chunk_decay = tuple(float(v) for v in np.exp(log_gamma * np.float32(L)).astype(np.float32))
    qd = jnp.asarray(np.repeat(q_decay.T, d, axis=1))
    kd = jnp.asarray(np.repeat(k_decay.T, d, axis=1))

    def blk(b_idx):
        return pl.BlockSpec((L, W), lambda b, c: (b * nc + c, b_idx))

    const = lambda shape: pl.BlockSpec(shape, lambda b, c: (0,) * len(shape))
    return pl.pallas_call(
        functools.partial(_ret_kernel, chunk_decay=chunk_decay), grid=(B, nc),
        in_specs=[blk(BLK_BQ), blk(BLK_BK), blk(BLK_BV), blk(BLK_BG),
                  const((H, L, L)), const((L, W)), const((L, W)), const((1, W))],
        out_specs=pl.BlockSpec((L, W), lambda b, c: (b * nc + c, 0)),
        out_shape=jax.ShapeDtypeStruct((B * T, W), BF16),
        scratch_shapes=[pltpu.VMEM((H, d, d), F32)],
        compiler_params=_cparams(("parallel", "arbitrary")), name="retention",
    )(zm, zm, zm, zm, jnp.asarray(decay), qd, kd, gain.reshape(1, W))


def _gelu(x):
    return 0.5 * x * (1.0 + jnp.tanh(0.7978845608028654 * (x + 0.044715 * (x * x * x))))


def _sgu_kernel(u_ref, v_ref, ng_ref, w_ref, b_ref, o_ref):
    L, d, H = SGU_CHUNK, HEAD_DIM, N_HEADS
    v = _gelu(v_ref[...].astype(F32))
    vn = (v * lax.rsqrt(jnp.mean(v * v, axis=-1, keepdims=True) + RMS_EPS) * ng_ref[...]).astype(BF16)
    u = _gelu(u_ref[...].astype(F32))
    row = lax.broadcasted_iota(jnp.int32, (L, L), 0)
    col = lax.broadcasted_iota(jnp.int32, (L, L), 1)
    for g in range(H):
        sl = slice(g * d, (g + 1) * d)
        wm = jnp.where(row >= col, w_ref[g], 0.0).astype(BF16)
        mixed = _dot(wm, vn[:, sl]) + b_ref[:, sl]
        o_ref[:, sl] = (u[:, sl] * mixed).astype(BF16)


def _sgu(zm, norm_g, w_s, b_s, n):
    L, W, H, d = SGU_CHUNK, MIX_WIDTH, N_HEADS, HEAD_DIM
    bsb = jnp.repeat(b_s.T, d, axis=1)
    const = lambda shape: pl.BlockSpec(shape, lambda i: (0,) * len(shape))
    return pl.pallas_call(
        _sgu_kernel, grid=(n // L,),
        in_specs=[pl.BlockSpec((L, W), lambda i: (i, BLK_DU)), pl.BlockSpec((L, W), lambda i: (i, BLK_DV)),
                  const((1, W)), const((H, L, L)), const((L, W))],
        out_specs=pl.BlockSpec((L, W), lambda i: (i, 0)),
        out_shape=jax.ShapeDtypeStruct((n, W), BF16),
        compiler_params=_cparams(("parallel",)), name="sgu",
    )(zm, zm, norm_g.reshape(1, W), w_s, bsb)


def _compress_kernel(k_ref, v_ref, w_ref, pe_ref, ko_ref, vo_ref, buf):
    t = k_ref.shape[0]
    n16 = t // CMP_STRIDE
    buf[t:t + CMP_STRIDE, :] = jnp.zeros((CMP_STRIDE, HEAD_DIM), F32)
    for which, (src, dst) in enumerate(((k_ref, ko_ref), (v_ref, vo_ref))):
        buf[0:t, :] = src[...].astype(F32)
        acc = jnp.zeros((n16, HEAD_DIM), F32)
        for l in range(CMP_BLOCK):
            x = buf[pl.ds(l, n16, stride=CMP_STRIDE), :]
            acc = acc + _dot((x + pe_ref[which, l:l + 1, :]).astype(BF16), w_ref[which, l].astype(BF16))
        dst[...] = (acc.T if which else acc).astype(BF16)


def _compress(zm, cmp_w, cmp_pe, B, T):
    G, d = NSA_KV_GROUPS, HEAD_DIM
    n16 = T // CMP_STRIDE
    out = jax.ShapeDtypeStruct((B, G, n16, d), BF16)
    kblk = KV_BASE
    vblk = KV_BASE + G
    return pl.pallas_call(
        _compress_kernel, grid=(B, G),
        in_specs=[pl.BlockSpec((T, d), lambda b, g: (b, kblk + g)), pl.BlockSpec((T, d), lambda b, g: (b, vblk + g)),
                  pl.BlockSpec((2, CMP_BLOCK, d, d), lambda b, g: (0, 0, 0, 0)),
                  pl.BlockSpec((2, CMP_BLOCK, d), lambda b, g: (0, 0, 0))],
        out_specs=[pl.BlockSpec((None, None, n16, d), lambda b, g: (b, g, 0, 0)),
                   pl.BlockSpec((None, None, d, n16), lambda b, g: (b, g, 0, 0))],
        out_shape=[out, jax.ShapeDtypeStruct((B, G, d, n16), BF16)],
        scratch_shapes=[pltpu.VMEM((T + CMP_STRIDE, d), F32)],
        compiler_params=_cparams(("parallel", "parallel")), name="nsa_compress",
    )(zm, zm, cmp_w, cmp_pe)


RANK_CHUNK = 8


def _lanes_x(x, times):
    return jnp.concatenate([x] * times, axis=1)


def _nsa_a_kernel(q_ref, kc_ref, vct_ref, *rest, n_back):
    nw = n_back + 1
    kw_refs, vwt_refs = rest[:nw], rest[nw:2 * nw]
    cgt_ref, ovt_ref, acmp_ref, awin_ref, part_ref, sel_ref, q_s, key_s, rank_s = rest[2 * nw:]
    tq, d, J = Q_BLOCK, HEAD_DIM, NSA_GROUP_SIZE
    qi = pl.program_id(2)
    t0 = qi * tq
    ncmp = kc_ref.shape[0]
    scale = d ** -0.5
    for j in range(J):
        q_s[j * tq:(j + 1) * tq, :] = (q_ref[:, j * d:(j + 1) * d].astype(F32) * scale).astype(BF16)
    sg_t = jax.nn.sigmoid(cgt_ref[...])
    t_row = t0 + lax.broadcasted_iota(jnp.int32, (1, tq), 1)

    n_i = lax.broadcasted_iota(jnp.int32, (ncmp, tq), 0)
    r_i = lax.broadcasted_iota(jnp.int32, (ncmp, tq), 1)
    bias_c = jnp.where(t0 + r_i - CMP_STRIDE * n_i - (CMP_BLOCK - 1) >= 0, 0.0, NEG_INF)
    s = _dot_nt(kc_ref[...], q_s[...]) - acmp_ref[...] + _lanes_x(bias_c, J)
    p = jnp.exp(s - jnp.max(s, axis=0, keepdims=True))
    has_c = jnp.where(t_row >= CMP_BLOCK - 1, 1.0, 0.0)
    p = p * (_lanes_x(has_c, J) / jnp.sum(p, axis=0, keepdims=True))
    o_cmp_t = _dot(vct_ref[...], p.astype(BF16))
    psum = p[:, 0:tq]
    for j in range(1, J):
        psum = psum + p[:, j * tq:(j + 1) * tq]

    hi, mid, lo = _split3(psum)
    ovt = ovt_ref[...]
    imp_t = _dot(ovt, hi) + _dot(ovt, mid) + _dot(ovt, lo)
    blk_t = lax.broadcasted_iota(jnp.int32, (LANES, tq), 0)
    cur_t = lax.shift_right_logical(t0 + lax.broadcasted_iota(jnp.int32, (LANES, tq), 1), SEL_BLOCK.bit_length() - 1)
    valid_t = blk_t <= cur_t
    forced = (blk_t == 0) | (blk_t == cur_t) | (blk_t == cur_t - 1)
    key_s[...] = jnp.where(valid_t, imp_t + jnp.where(forced, FORCE_SCORE, 0.0), NEG_INF)
    rank_s[...] = jnp.zeros_like(rank_s)
    sub = lax.broadcasted_iota(jnp.int32, (8, tq), 0)
    groups = LANES // 8
    last_valid = lax.shift_right_logical(t0 + tq - 1, SEL_BLOCK.bit_length() - 1)
    for chunk in range(LANES // RANK_CHUNK):
        @pl.when(chunk * RANK_CHUNK <= last_valid)
        def _(chunk=chunk):
            key_g = [key_s[8 * i:8 * i + 8, :] for i in range(groups)]
            rank = [rank_s[8 * i:8 * i + 8, :] for i in range(groups)]
            for mp in range(chunk * RANK_CHUNK, (chunk + 1) * RANK_CHUNK):
                rowv = jnp.broadcast_to(key_s[mp:mp + 1, :], (8, tq))
                for i in range(groups):
                    if 8 * i > mp:
                        ahead = rowv >= key_g[i]
                    elif 8 * i + 7 <= mp:
                        ahead = rowv > key_g[i]
                    else:
                        ahead = (rowv > key_g[i]) | ((sub > mp - 8 * i) & (rowv == key_g[i]))
                    rank[i] = rank[i] + jnp.where(ahead, 1.0, 0.0)
            for i in range(groups):
                rank_s[8 * i:8 * i + 8, :] = rank[i]
    sel_ref[...] = jnp.where((rank_s[...] < float(SEL_TOPN)) & valid_t, 1.0, 0.0).astype(BF16)

    span = nw * tq
    kw = jnp.concatenate([kr[...] for kr in kw_refs], axis=0)
    vw_t = jnp.concatenate([vr[...] for vr in vwt_refs], axis=1)
    c_i = lax.broadcasted_iota(jnp.int32, (span, tq), 0)
    bias_w = jnp.where(t0 - n_back * tq + c_i >= 0, 0.0, NEG_INF)
    s = _dot_nt(kw, q_s[...]) - awin_ref[...] + _lanes_x(bias_w, J)
    p = jnp.exp(s - jnp.max(s, axis=0, keepdims=True))
    inv_l = 1.0 / jnp.sum(p, axis=0, keepdims=True)
    o_win_t = _dot(vw_t, p.astype(BF16))
    g_cmp = jnp.concatenate([sg_t[j:j + 1, :] for j in range(J)], axis=1)
    g_win = jnp.concatenate([sg_t[2 * J + j:2 * J + j + 1, :] for j in range(J)], axis=1)
    part_ref[...] = g_cmp * o_cmp_t + (g_win * inv_l) * o_win_t


def _alibi_slopes():
    return (2.0 ** (-8.0 * (np.arange(N_HEADS, dtype=np.float32) + 1.0) / N_HEADS)).astype(np.float32)


def _nsa_consts(T):
    G, J, tq = NSA_KV_GROUPS, NSA_GROUP_SIZE, Q_BLOCK
    n16 = T // CMP_STRIDE
    n_sel = T // SEL_BLOCK
    n = np.arange(n16)[None, :]
    m = np.arange(LANES)[:, None]
    c_start, s_start = n * CMP_STRIDE, m * SEL_BLOCK
    overlap_t = ((c_start < s_start + SEL_BLOCK) & (c_start + CMP_BLOCK > s_start) & (n < n16 - 1) & (m < n_sel))
    slopes = _alibi_slopes().reshape(G, 1, J, 1)
    r = np.arange(tq, dtype=np.float32)[None, None, None, :]
    dist_c = r - CMP_STRIDE * np.arange(n16, dtype=np.float32)[None, :, None, None] - (CMP_BLOCK - 1)
    a_cmp = (slopes * dist_c).reshape(G, n16, J * tq)
    span = WINDOW + tq
    dist_w = r + WINDOW - np.arange(span, dtype=np.float32)[None, :, None, None]
    a_win = np.where((dist_w >= 0) & (dist_w < WINDOW), slopes * dist_w, -NEG_INF).reshape(G, span, J * tq)
    return (jnp.asarray(overlap_t.astype(np.float32), BF16), jnp.asarray(a_cmp.astype(np.float32)),
            jnp.asarray(a_win.astype(np.float32)))


def _nsa_a(zm, k_cmp, v_cmp_t, vw_t, cg_t, B, T):
    G, J, d, tq = NSA_KV_GROUPS, NSA_GROUP_SIZE, HEAD_DIM, Q_BLOCK
    nq = T // tq
    n16 = T // CMP_STRIDE
    n_back = WINDOW // tq
    span = WINDOW + tq
    qblk = BLK_CQ * (MIX_WIDTH // (J * d))
    kwblk = KV_BASE + 4 * G
    overlap_t, a_cmp, a_win = _nsa_consts(T)

    def past(qi, i):
        return jnp.maximum(qi - n_back + i, 0)

    in_specs = ([pl.BlockSpec((tq, J * d), lambda b, g, qi: (b * nq + qi, qblk + g)),
                 pl.BlockSpec((None, None, n16, d), lambda b, g, qi: (b, g, 0, 0)),
                 pl.BlockSpec((None, None, d, n16), lambda b, g, qi: (b, g, 0, 0))]
                + [pl.BlockSpec((tq, d), functools.partial(lambda b, g, qi, i: (b * nq + past(qi, i), kwblk + g), i=i))
                   for i in range(n_back + 1)]
                + [pl.BlockSpec((None, None, d, tq), functools.partial(lambda b, g, qi, i: (b, g, 0, past(qi, i)), i=i))
                   for i in range(n_back + 1)]
                + [pl.BlockSpec((None, 16, tq), lambda b, g, qi: (g, 0, b * nq + qi)),
                   pl.BlockSpec((LANES, n16), lambda b, g, qi: (0, 0)),
                   pl.BlockSpec((None, n16, J * tq), lambda b, g, qi: (g, 0, 0)),
                   pl.BlockSpec((None, span, J * tq), lambda b, g, qi: (g, 0, 0))])
    return pl.pallas_call(
        functools.partial(_nsa_a_kernel, n_back=n_back), grid=(B, G, nq), in_specs=in_specs,
        out_specs=[pl.BlockSpec((None, None, None, d, J * tq), lambda b, g, qi: (b, g, qi, 0, 0)),
                   pl.BlockSpec((None, None, LANES, tq), lambda b, g, qi: (b, g, 0, qi))],
        out_shape=[jax.ShapeDtypeStruct((B, G, nq, d, J * tq), F32), jax.ShapeDtypeStruct((B, G, LANES, T), BF16)],
        scratch_shapes=[pltpu.VMEM((J * tq, d), BF16), pltpu.VMEM((LANES, tq), F32), pltpu.VMEM((LANES, tq), F32)],
        compiler_params=_cparams(("parallel", "parallel", "parallel")), name="nsa_cmp_win",
    )(zm, k_cmp, v_cmp_t, *([zm] * (n_back + 1)), *([vw_t] * (n_back + 1)), cg_t, overlap_t, a_cmp, a_win)


def _nsa_b_kernel(cnt_ref, lst_ref, q_ref, ks_ref, vt_ref, sel_ref, e_ref, a_ref, srow_ref, cgt_ref, part_ref, o_ref,
                  q_s, acc_s, *, tk, nk):
    tq, d, J = Q_BLOCK, HEAD_DIM, NSA_GROUP_SIZE
    b, g, qi = pl.program_id(0), pl.program_id(1), pl.program_id(2)
    lin = (b * pl.num_programs(1) + g) * pl.num_programs(2) + qi
    t0 = qi * tq
    scale = d ** -0.5
    for j in range(J):
        q_s[j * tq:(j + 1) * tq, :] = (q_ref[:, j * d:(j + 1) * d].astype(F32) * scale).astype(BF16)
    acc_s[...] = jnp.zeros_like(acc_s)
    r_minus_c = lax.broadcasted_iota(jnp.int32, (tk, tq), 1) - lax.broadcasted_iota(jnp.int32, (tk, tq), 0)

    def body(i, carry):
        m_prev, l_prev = carry
        kt = lst_ref[lin * nk + i]
        k_t = ks_ref[pl.ds(pl.multiple_of(kt * tk, tk), tk), :]
        off = t0 - kt * tk
        picked = _dot(e_ref[kt], sel_ref[...])
        bias = jnp.where((picked > 0.5) & (r_minus_c + off >= 0), 0.0, NEG_INF)
        s = _dot_nt(k_t, q_s[...]) - a_ref[...] + jnp.concatenate([bias] * J, axis=1)
        shift = srow_ref[...] * off.astype(F32)
        m_new = jnp.maximum(m_prev, jnp.max(s, axis=0, keepdims=True) - shift)
        p = jnp.exp(s - (m_new + shift))
        alpha = jnp.exp(m_prev - m_new)
        acc_s[...] = alpha * acc_s[...] + _dot(vt_ref[kt], p.astype(BF16))
        return m_new, alpha * l_prev + jnp.sum(p, axis=0, keepdims=True)

    init = (jnp.full((1, J * tq), NEG_INF, F32), jnp.zeros((1, J * tq), F32))
    _, l = lax.fori_loop(0, cnt_ref[lin], body, init)
    sg_t = jax.nn.sigmoid(cgt_ref[...])
    g_sel = jnp.concatenate([sg_t[J + j:J + j + 1, :] for j in range(J)], axis=1)
    total_t = part_ref[...] + (g_sel / l) * acc_s[...]
    for j in range(J):
        o_ref[:, j * d:(j + 1) * d] = total_t[:, j * tq:(j + 1) * tq].T.astype(BF16)


def _nsa_b(zm, sel_t, cg_t, part_t, B, T, tk=512):
    G, J, d, tq = NSA_KV_GROUPS, NSA_GROUP_SIZE, HEAD_DIM, Q_BLOCK
    tk = min(tk, T)
    nq, nk = T // tq, T // tk
    qblk = BLK_CQ * (MIX_WIDTH // (J * d))
    ksblk = KV_BASE + 2 * G
    vs_col = (KV_BASE + 3 * G) * d
    per_tile = tk // SEL_BLOCK
    kt = np.arange(nk)[:, None, None]
    cc = np.arange(tk)[None, :, None]
    m = np.arange(LANES)[None, None, :]
    expand_t = jnp.asarray((m == kt * per_tile + cc // SEL_BLOCK).astype(np.float32), BF16)
    slopes_np = _alibi_slopes()
    r_minus_c = (np.arange(tq)[None, :] - np.arange(tk)[:, None]).astype(np.float32)
    alibi_t = jnp.asarray((slopes_np.reshape(G, 1, J, 1) * r_minus_c[None, :, None, :]).reshape(G, tk, J * tq))
    srow = jnp.asarray(np.repeat(slopes_np.reshape(G, J), tq, axis=1).reshape(G, 1, J * tq))
    vs_t = zm[:, vs_col:vs_col + G * d].reshape(B, nk, tk, G, d).transpose(0, 3, 1, 4, 2)

    active = sel_t[:, :, :nk * per_tile].reshape(B, G, nk, per_tile, nq, tq).max(axis=(3, 5)) > 0
    active = active.transpose(0, 1, 3, 2)
    tiles = jnp.arange(nk, dtype=jnp.int32)
    order = jnp.sort(jnp.where(active, tiles, tiles + nk), axis=-1) % nk
    counts = active.sum(axis=-1).astype(jnp.int32)

    grid_spec = pltpu.PrefetchScalarGridSpec(
        num_scalar_prefetch=2, grid=(B, G, nq),
        in_specs=[pl.BlockSpec((tq, J * d), lambda b, g, qi, c, o: (b * nq + qi, qblk + g)),
                  pl.BlockSpec((T, d), lambda b, g, qi, c, o: (b, ksblk + g)),
                  pl.BlockSpec((None, None, nk, d, tk), lambda b, g, qi, c, o: (b, g, 0, 0, 0)),
                  pl.BlockSpec((None, None, LANES, tq), lambda b, g, qi, c, o: (b, g, 0, qi)),
                  pl.BlockSpec((nk, tk, LANES), lambda b, g, qi, c, o: (0, 0, 0)),
                  pl.BlockSpec((None, tk, J * tq), lambda b, g, qi, c, o: (g, 0, 0)),
                  pl.BlockSpec((None, 1, J * tq), lambda b, g, qi, c, o: (g, 0, 0)),
                  pl.BlockSpec((None, 16, tq), lambda b, g, qi, c, o: (g, 0, b * nq + qi)),
                  pl.BlockSpec((None, None, None, d, J * tq), lambda b, g, qi, c, o: (b, g, qi, 0, 0))],
        out_specs=pl.BlockSpec((tq, J * d), lambda b, g, qi, c, o: (b * nq + qi, g)),
        scratch_shapes=[pltpu.VMEM((J * tq, d), BF16), pltpu.VMEM((d, J * tq), F32)])
    return pl.pallas_call(
        functools.partial(_nsa_b_kernel, tk=tk, nk=nk), grid_spec=grid_spec,
        out_shape=jax.ShapeDtypeStruct((B * T, MIX_WIDTH), BF16),
        compiler_params=_cparams(("parallel", "parallel", "parallel")), name="nsa_selected",
    )(counts.reshape(-1), order.reshape(-1).astype(jnp.int32), zm, zm, vs_t, sel_t, expand_t, alibi_t, srow, cg_t, part_t)


def _nsa(zm, zs, cmp_w, cmp_pe, B, T):
    G, J = NSA_KV_GROUPS, NSA_GROUP_SIZE
    n = B * T
    d = HEAD_DIM
    cg = zs[:, 2 * N_HEADS:2 * N_HEADS + 3 * N_HEADS].reshape(n, 3, G, J)
    cg_t = jnp.pad(cg.transpose(2, 1, 3, 0).reshape(G, 3 * J, n), ((0, 0), (0, 16 - 3 * J), (0, 0)))
    vw_col = (KV_BASE + 5 * G) * d
    vw_t = zm[:, vw_col:vw_col + G * d].reshape(B, T, G, d).transpose(0, 2, 3, 1)
    k_cmp, v_cmp_t = _compress(zm, cmp_w, cmp_pe, B, T)
    part_t, sel_t = _nsa_a(zm, k_cmp, v_cmp_t, vw_t, cg_t, B, T)
    return _nsa_b(zm, sel_t, cg_t, part_t, B, T)


def _merge_kernel(h_ref, y0, y1, y2, y3, g0, g1, g2, g3, b0, b1, b2, b3, o_ref):
    h = h_ref[...]
    acc = None
    for y, gw, bw in ((y0, g0, b0), (y1, g1, b1), (y2, g2, b2), (y3, g3, b3)):
        term = jax.nn.sigmoid(_dot(h, gw[...])) * _dot(y[...], bw[...])
        acc = term if acc is None else acc + term
    o_ref[...] = acc.astype(BF16)


def _merge(h, ys, w_gate, w_branch, tm=512, tn=256):
    n, dm = h.shape
    tm = min(tm, n)
    nj = dm // tn
    y_spec = pl.BlockSpec((tm, MIX_WIDTH), lambda i, j: (i, 0))
    gate_specs = [pl.BlockSpec((dm, tn), functools.partial(lambda i, j, b: (0, b * nj + j), b=b)) for b in range(N_BRANCH)]
    br_specs = [pl.BlockSpec((None, MIX_WIDTH, tn), functools.partial(lambda i, j, b: (b, 0, j), b=b)) for b in range(N_BRANCH)]
    return pl.pallas_call(
        _merge_kernel, grid=(n // tm, nj),
        in_specs=[pl.BlockSpec((tm, dm), lambda i, j: (i, 0))] + [y_spec] * N_BRANCH + gate_specs + br_specs,
        out_specs=pl.BlockSpec((tm, tn), lambda i, j: (i, j)),
        out_shape=jax.ShapeDtypeStruct((n, dm), BF16),
        compiler_params=_cparams(("parallel", "parallel")), name="gated_merge",
    )(h, *ys, *([w_gate] * N_BRANCH), *([w_branch] * N_BRANCH))


def _pack_bf16_pairs(y):
    half = y.shape[1] // 2
    bits = lax.bitcast_convert_type(y.astype(BF16).astype(F32), jnp.int32)
    return bits[:, :half] | lax.shift_right_logical(bits[:, half:], 16)


def _unpack_bf16_pairs(w):
    first = lax.bitcast_convert_type(w & jnp.int32(-65536), F32).astype(BF16)
    second = lax.bitcast_convert_type(lax.shift_left(w, 16), F32).astype(BF16)
    return first, second


def _expert_kernel(be_ref, nact_ref, idx_ref, idx_next_ref, hp_hbm, w1_ref, w3_ref, w2_ref, o_ref,
                   xbuf, sems, w1b, w3b, w2b):
    i = pl.program_id(0)
    n_act = nact_ref[0]
    slot = lax.rem(i, 2)
    e = be_ref[i]
    prev = be_ref[jnp.maximum(i - 1, 0)]
    half = hp_hbm.shape[1]

    @pl.when((i == 0) & (n_act > 0))
    def _():
        _row_gather(idx_ref, hp_hbm, xbuf.at[0], sems.at[0], MOE_BLOCK)

    @pl.when(i + 1 < n_act)
    def _():
        _row_gather(idx_next_ref, hp_hbm, xbuf.at[1 - slot], sems.at[1 - slot], MOE_BLOCK)

    @pl.when((i == 0) | (prev != e))
    def _():
        w1b[...] = w1_ref[...].astype(BF16)
        w3b[...] = w3_ref[...].astype(BF16)
        w2b[...] = w2_ref[...].astype(BF16)

    @pl.when(i < n_act)
    def _():
        _row_gather_wait(hp_hbm, xbuf.at[slot], sems.at[slot], MOE_BLOCK)
        x_a, x_b = _unpack_bf16_pairs(xbuf[slot])
        a1 = _dot(x_a, w1b[0:half, :]) + _dot(x_b, w1b[half:, :])
        a3 = _dot(x_a, w3b[0:half, :]) + _dot(x_b, w3b[half:, :])
        a = (a1 * jax.nn.sigmoid(a1)) * a3
        o_ref[...] = _dot(a.astype(BF16), w2b[...])

    @pl.when(i >= n_act)
    def _():
        o_ref[...] = jnp.zeros_like(o_ref)


def _experts(hp, tok_pad, blk_exp, n_active, w1, w3, w2, layer):
    half = hp.shape[1]
    dm = 2 * half
    n_blk = tok_pad.shape[0] // MOE_BLOCK
    ff = w1.shape[-1]
    idx = tok_pad.reshape(n_blk, 1, MOE_BLOCK)
    smem_blk = lambda f: pl.BlockSpec((None, 1, MOE_BLOCK), f, memory_space=pltpu.SMEM)
    grid_spec = pltpu.PrefetchScalarGridSpec(
        num_scalar_prefetch=2, grid=(n_blk,),
        in_specs=[smem_blk(lambda i, be, na: (i, 0, 0)),
                  smem_blk(lambda i, be, na: (jnp.minimum(i + 1, n_blk - 1), 0, 0)),
                  pl.BlockSpec(memory_space=pl.ANY),
                  pl.BlockSpec((None, None, dm, ff), lambda i, be, na: (layer, be[i], 0, 0)),
                  pl.BlockSpec((None, None, dm, ff), lambda i, be, na: (layer, be[i], 0, 0)),
                  pl.BlockSpec((None, None, ff, dm), lambda i, be, na: (layer, be[i], 0, 0))],
        out_specs=pl.BlockSpec((MOE_BLOCK, dm), lambda i, be, na: (i, 0)),
        scratch_shapes=[pltpu.VMEM((2, MOE_BLOCK, half), jnp.int32), pltpu.SemaphoreType.DMA((2,)),
                        pltpu.VMEM((dm, ff), BF16), pltpu.VMEM((dm, ff), BF16), pltpu.VMEM((ff, dm), BF16)])
    return pl.pallas_call(
        _expert_kernel, grid_spec=grid_spec,
        out_shape=jax.ShapeDtypeStruct((n_blk * MOE_BLOCK, dm), F32),
        compiler_params=_cparams(("arbitrary",), vmem=MOE_VMEM_LIMIT), name="moe_experts",
    )(blk_exp, n_active, idx, idx, hp, w1, w3, w2)


def _dispatch_plan(e_idx):
    n, K = e_idx.shape
    nk = n * K
    i32 = jnp.int32
    flat_e = e_idx.reshape(-1)
    order = jnp.argsort(flat_e).astype(i32)
    rank = jnp.argsort(order).astype(i32)
    experts = jnp.arange(N_EXPERTS, dtype=i32)
    hot = flat_e[:, None] == experts[None, :]
    counts = jnp.sum(hot.astype(i32), axis=0)
    padded = (counts + MOE_BLOCK - 1) // MOE_BLOCK * MOE_BLOCK
    pad_end = jnp.cumsum(padded)
    pad_start = pad_end - padded
    start = jnp.cumsum(counts) - counts
    lookup = lambda table, onehot: jnp.sum(jnp.where(onehot, table[None, :], 0), axis=1)
    dest = lookup(pad_start - start, hot) + rank
    n_blk = (nk + MOE_BLOCK - 1) // MOE_BLOCK + N_EXPERTS
    blk_first = jnp.arange(n_blk, dtype=i32) * MOE_BLOCK
    blk_exp = jnp.minimum(jnp.sum((pad_end[None, :] <= blk_first[:, None]).astype(i32), axis=1), N_EXPERTS - 1)
    blk_hot = blk_exp[:, None] == experts[None, :]
    within = jnp.arange(MOE_BLOCK, dtype=i32)[None, :]
    j = (blk_first - lookup(pad_start, blk_hot))[:, None] + within
    sorted_pos = jnp.clip(lookup(start, blk_hot)[:, None] + j, 0, nk - 1)
    src = jnp.take(order, sorted_pos.reshape(-1), mode="clip") // K
    tok_pad = jnp.where((j < lookup(counts, blk_hot)[:, None]).reshape(-1), src, n - 1)
    n_active = (pad_end[-1] // MOE_BLOCK).astype(i32).reshape(1)
    return tok_pad, blk_exp.astype(i32), n_active, dest.reshape(n, K)


def _moe(h, route, w1, w3, w2, layer):
    e_idx = route[:, :TOP_K_IN_GROUP].astype(jnp.int32)
    tok_pad, blk_exp, n_active, dest = _dispatch_plan(e_idx)
    y = _experts(h, tok_pad, blk_exp, n_active, w1, w3, w2, layer)
    return y, dest


PACK_TILE = 512


def _pack_kernel(offs_ref, wt_ref, o_ref):
    del offs_ref
    o_ref[...] = wt_ref[0].T.astype(BF16)


def _pack_columns(w_in_t, layer, col_offsets):
    _, _, k = w_in_t.shape
    nt = len(col_offsets)
    grid_spec = pltpu.PrefetchScalarGridSpec(
        num_scalar_prefetch=1, grid=(nt,),
        in_specs=[pl.BlockSpec((pl.Element(1), pl.Element(PACK_TILE), pl.Element(k)),
                               lambda j, offs: (layer, pl.multiple_of(offs[j], 8), 0))],
        out_specs=pl.BlockSpec((k, PACK_TILE), lambda j, offs: (0, j)))
    return pl.pallas_call(
        _pack_kernel, grid_spec=grid_spec,
        out_shape=jax.ShapeDtypeStruct((k, nt * PACK_TILE), BF16),
        compiler_params=_cparams(("parallel",)), name="pack_w_in",
    )(jnp.asarray(col_offsets, jnp.int32), w_in_t)


def _tile_offsets(groups):
    offs = []
    for i in groups:
        assert COL_WIDTHS[i] % PACK_TILE == 0 or COL_WIDTHS[i] < PACK_TILE
        offs += list(range(COL_OFFSETS[i], COL_OFFSETS[i + 1], PACK_TILE))
    return offs


def _pack_w_in(w_in_t, layer):
    wide = [i for i in MAIN_ORDER if COL_WIDTHS[i] >= PACK_TILE]
    narrow = [i for i in MAIN_ORDER if COL_WIDTHS[i] < PACK_TILE]
    assert MAIN_ORDER == tuple(wide + narrow) and narrow == list(range(narrow[0], narrow[-1] + 1))
    kv_lo, kv_hi = COL_OFFSETS[narrow[0]], COL_OFFSETS[narrow[-1] + 1]
    assert (kv_hi - kv_lo) % PACK_TILE == 0
    main_offs = _tile_offsets(wide) + list(range(kv_lo, kv_hi, PACK_TILE))
    w_main = _pack_columns(w_in_t, layer, main_offs)
    w_gate = _pack_columns(w_in_t, layer, _tile_offsets([I_MERGE]))
    rows = lambda i: w_in_t[layer, COL_OFFSETS[i]:COL_OFFSETS[i + 1]]
    small_t = jnp.concatenate([rows(I_AI), rows(I_AF), rows(I_CG)], axis=0)
    small_t = jnp.pad(small_t, ((0, LANES - small_t.shape[0]), (0, 0)))
    return w_main, small_t, w_gate


def _router_weights(wg, bg, we, be):
    w = jnp.concatenate([wg, we], axis=1)
    w = jnp.pad(w, ((0, 0), (0, LANES - w.shape[1])))
    hi = w.astype(BF16)
    lo = (w - hi.astype(F32)).astype(BF16)
    b = jnp.concatenate([bg, be])
    b = jnp.pad(b, (0, LANES - b.shape[0])).reshape(1, LANES).astype(F32)
    return hi, lo, b


def kernel(x, w_in, mlstm_conv, mlstm_gate_bias, mlstm_norm, ret_norm, nsa_cmp_w, nsa_cmp_pe, sgu_norm, sgu_w, sgu_b, w_branch, w_out, norm_mix, norm_ffn, router_group_w, router_group_b, router_expert_w, router_expert_b, expert_w1, expert_w3, expert_w2, norm_final):
    B, T, D = x.shape
    n = B * T
    xs, moe = x.reshape(n, D), None
    w_in_t = jnp.swapaxes(w_in, 1, 2)
    for l in range(DEPTH):
        w_main, w_small_t, w_gate = _pack_w_in(w_in_t, l)
        if moe is None:
            h, = _norm(xs, norm_mix[l])
        else:
            xs, h = _norm(xs, norm_mix[l], moe=moe)
        zm = _matmul(h, w_main, out_dtype=BF16, name="proj_main")
        zs = _matmul(h, w_small_t, out_dtype=F32, w_rows_are_outputs=True, name="proj_small")
        zst = zs[:, :2 * N_HEADS].T
        ya = _mlstm(zm, zs, zst, mlstm_conv[l], mlstm_gate_bias[l], mlstm_norm[l], B, T)
        yb = _retention(zm, ret_norm[l], B, T)
        yc = _nsa(zm, zs, nsa_cmp_w[l], nsa_cmp_pe[l], B, T)
        yd = _sgu(zm, sgu_norm[l], sgu_w[l], sgu_b[l], n)
        merged = _merge(h, (ya, yb, yc, yd), w_gate, w_branch[l].astype(BF16))
        x_mid = _matmul(merged, w_out[l].astype(BF16), out_dtype=F32, residual=xs, name="proj_out")
        h2, route = _norm(x_mid, norm_ffn[l],
                          router_w=_router_weights(router_group_w[l], router_group_b[l],
                                                   router_expert_w[l], router_expert_b[l]))
        y_exp, dest = _moe(h2, route, expert_w1, expert_w3, expert_w2, l)
        xs, moe = x_mid, (y_exp, dest, route)
    out, = _norm(xs, norm_final, moe=moe, final=True)
    return out.reshape(B, T, D)
```

```python
import functools

import numpy as np
import jax
import jax.numpy as jnp
from jax import lax
from jax.experimental import pallas as pl
from jax.experimental.pallas import tpu as pltpu

F32 = jnp.float32
BF16 = jnp.bfloat16

D_MODEL = 4096
DEPTH = 2
HEAD_DIM = 128
N_BRANCH = 4
MIX_WIDTH = D_MODEL // N_BRANCH
N_HEADS = MIX_WIDTH // HEAD_DIM
CONV_WIDTH = 4
CONV_TAIL = 16
MLSTM_CHUNK = 128
RET_CHUNK = 128
NSA_KV_GROUPS = 2
NSA_GROUP_SIZE = N_HEADS // NSA_KV_GROUPS
KV_WIDTH = NSA_KV_GROUPS * HEAD_DIM
CMP_BLOCK = 32
CMP_STRIDE = 16
SEL_BLOCK = 64
SEL_TOPN = 16
WINDOW = 512
Q_BLOCK = 128
SGU_CHUNK = 128
N_GROUPS = 8
EXPERTS_PER_GROUP = 8
N_EXPERTS = N_GROUPS * EXPERTS_PER_GROUP
TOP_K_IN_GROUP = 2
EXPERT_FF = 256
MOE_BLOCK = 256
GATHER_UNROLL = 8
RMS_EPS = 1e-6
NEG_INF = -1e30
FORCE_SCORE = 1e9

COL_WIDTHS = (
    MIX_WIDTH, MIX_WIDTH, MIX_WIDTH, MIX_WIDTH, N_HEADS, N_HEADS,
    MIX_WIDTH, MIX_WIDTH, MIX_WIDTH, MIX_WIDTH,
    MIX_WIDTH, KV_WIDTH, KV_WIDTH, KV_WIDTH, KV_WIDTH, KV_WIDTH, KV_WIDTH, 3 * N_HEADS,
    MIX_WIDTH, MIX_WIDTH,
    N_BRANCH * D_MODEL,
)
COL_OFFSETS = tuple(int(v) for v in np.concatenate([[0], np.cumsum(COL_WIDTHS)]))
(I_AQ, I_AK, I_AV, I_AO, I_AI, I_AF, I_BQ, I_BK, I_BV, I_BG, I_CQ, I_CKC, I_CVC, I_CKS, I_CVS,
 I_CKW, I_CVW, I_CG, I_DU, I_DV, I_MERGE) = range(21)

MAIN_ORDER = (I_AQ, I_AK, I_AV, I_AO, I_BQ, I_BK, I_BV, I_BG, I_CQ, I_DU, I_DV,
              I_CKC, I_CVC, I_CKS, I_CVS, I_CKW, I_CVW)
MAIN_WIDTH = sum(COL_WIDTHS[i] for i in MAIN_ORDER)
LANES = 128
BLK_AQ, BLK_AK, BLK_AV, BLK_AO, BLK_BQ, BLK_BK, BLK_BV, BLK_BG, BLK_CQ, BLK_DU, BLK_DV = range(11)
KV_BASE = 11 * MIX_WIDTH // HEAD_DIM
VMEM_LIMIT = 48 * 1024 * 1024
MOE_VMEM_LIMIT = 56 * 1024 * 1024


def _cparams(sem, vmem=VMEM_LIMIT):
    return pltpu.CompilerParams(dimension_semantics=sem, vmem_limit_bytes=vmem)


def _dot(a, b):
    return jnp.dot(a, b, preferred_element_type=F32)


def _dot_nt(a, b):
    return lax.dot_general(a, b, (((1,), (1,)), ((), ())), preferred_element_type=F32)


def _split3(x):
    hi = x.astype(BF16)
    r1 = x - hi.astype(F32)
    mid = r1.astype(BF16)
    lo = (r1 - mid.astype(F32)).astype(BF16)
    return hi, mid, lo


def _dot_exact_rhs(x, m_bf16):
    hi, mid, lo = _split3(x)
    return _dot(hi, m_bf16) + _dot(mid, m_bf16) + _dot(lo, m_bf16)


def _log_sigmoid(x):
    return jnp.minimum(x, 0.0) - jnp.log1p(jnp.exp(-jnp.abs(x)))


def _row_gather(idx_ref, src_hbm, dst, sem, count):
    def issue(c, carry):
        for u in range(GATHER_UNROLL):
            r = c * GATHER_UNROLL + u
            pltpu.make_async_copy(src_hbm.at[pl.ds(idx_ref[0, r], 1)], dst.at[pl.ds(r, 1)], sem).start(priority=u % 2)
        return carry
    lax.fori_loop(0, count // GATHER_UNROLL, issue, 0)


def _row_gather_wait(src_hbm, dst, sem, count):
    for r in range(count):
        pltpu.make_async_copy(src_hbm.at[pl.ds(0, 1)], dst.at[pl.ds(r, 1)], sem).wait()


def _norm_kernel(*refs, moe, router, final):
    x = refs[0][...]
    rows = x.shape[0]
    pos = 1
    n_scratch = 0
    if moe:
        idx_ref, idx_next_ref, y_hbm, rt_ref = refs[pos:pos + 4]
        pos += 4
        gbuf, sems = refs[-2:]
        n_scratch = 2
        i, last = pl.program_id(0), pl.num_programs(0) - 1
        slot = lax.rem(i, 2)

        @pl.when(i == 0)
        def _():
            _row_gather(idx_ref, y_hbm, gbuf.at[0], sems.at[0], 2 * rows)

        @pl.when(i < last)
        def _():
            _row_gather(idx_next_ref, y_hbm, gbuf.at[1 - slot], sems.at[1 - slot], 2 * rows)

        _row_gather_wait(y_hbm, gbuf.at[slot], sems.at[slot], 2 * rows)
        x = x + (rt_ref[:, 2:3] * gbuf[slot, 0:rows, :] + rt_ref[:, 3:4] * gbuf[slot, rows:2 * rows, :])
    g_ref = refs[pos]
    pos += 1
    if router:
        whi_ref, wlo_ref, rb_ref = refs[pos:pos + 3]
        pos += 3
    outs = refs[pos:len(refs) - n_scratch]
    y = x * lax.rsqrt(jnp.mean(x * x, axis=-1, keepdims=True) + RMS_EPS) * g_ref[...]
    o = 0
    if moe and not final:
        outs[o][...] = x
        o += 1
    if final:
        outs[o][...] = y
    elif router:
        outs[o][...] = _pack_bf16_pairs(y)
    else:
        outs[o][...] = y.astype(BF16)
    o += 1
    if router:
        y_hi = y.astype(BF16)
        y_lo = (y - y_hi.astype(F32)).astype(BF16)
        lg = _dot(y_hi, whi_ref[...]) + (_dot(y_hi, wlo_ref[...]) + _dot(y_lo, whi_ref[...])) + rb_ref[...]
        outs[o][...] = _route_rows(lg)


def _first_max(v, lane):
    mx = jnp.max(v, axis=-1, keepdims=True)
    return mx, jnp.min(jnp.where(v == mx, lane, LANES), axis=-1, keepdims=True)


def _route_rows(lg):
    low = -3.0e38
    lane = lax.broadcasted_iota(jnp.int32, lg.shape, 1)
    is_grp = lane < N_GROUPS
    gmax, g_idx = _first_max(jnp.where(is_grp, lg, low), lane)
    g_w = 1.0 / jnp.sum(jnp.where(is_grp, jnp.exp(lg - gmax), 0.0), axis=-1, keepdims=True)
    assert EXPERTS_PER_GROUP & (EXPERTS_PER_GROUP - 1) == 0
    grp_of_lane = lax.shift_right_arithmetic(lane - N_GROUPS, EXPERTS_PER_GROUP.bit_length() - 1)
    in_grp = (lane >= N_GROUPS) & (lane < N_GROUPS + N_EXPERTS) & (grp_of_lane == g_idx)
    el = jnp.where(in_grp, lg, low)
    v0, i0 = _first_max(el, lane)
    v1, i1 = _first_max(jnp.where(lane == i0, low, el), lane)
    e1 = jnp.exp(v1 - v0)
    p0 = g_w / (1.0 + e1)
    vals = (i0 - N_GROUPS).astype(F32), (i1 - N_GROUPS).astype(F32), p0, p0 * e1
    out = jnp.zeros(lg.shape, F32)
    for k, v in enumerate(vals):
        out = jnp.where(lane == k, v, out)
    return out


def _norm(x, gain, *, moe=None, router_w=None, final=False, rows=128):
    n, d = x.shape
    rows = min(rows, n)
    steps = n // rows
    row_spec = pl.BlockSpec((rows, d), lambda i: (i, 0))
    lane_spec = pl.BlockSpec((rows, LANES), lambda i: (i, 0))
    in_specs, args, scratch = [row_spec], [x], []
    if moe is not None:
        y, dest, route = moe
        K = dest.shape[1]
        idx = dest.reshape(steps, rows, K).transpose(0, 2, 1).reshape(steps, 1, K * rows)
        smem_blk = lambda f: pl.BlockSpec((None, 1, K * rows), f, memory_space=pltpu.SMEM)
        in_specs += [smem_blk(lambda i: (i, 0, 0)), smem_blk(lambda i: (jnp.minimum(i + 1, steps - 1), 0, 0)),
                     pl.BlockSpec(memory_space=pl.ANY), lane_spec]
        args += [idx, idx, y, route]
        scratch = [pltpu.VMEM((2, K * rows, d), F32), pltpu.SemaphoreType.DMA((2,))]
    in_specs.append(pl.BlockSpec((1, d), lambda i: (0, 0)))
    args.append(gain.reshape(1, d))
    out_shape, out_specs = [], []
    if moe is not None and not final:
        out_shape.append(jax.ShapeDtypeStruct((n, d), F32))
        out_specs.append(row_spec)
    if router_w is not None:
        out_shape.append(jax.ShapeDtypeStruct((n, d // 2), jnp.int32))
        out_specs.append(pl.BlockSpec((rows, d // 2), lambda i: (i, 0)))
    else:
        out_shape.append(jax.ShapeDtypeStruct((n, d), F32 if final else BF16))
        out_specs.append(row_spec)
    if router_w is not None:
        whi, wlo, rb = router_w
        in_specs += [pl.BlockSpec((d, LANES), lambda i: (0, 0))] * 2 + [pl.BlockSpec((1, LANES), lambda i: (0, 0))]
        args += [whi, wlo, rb]
        out_shape.append(jax.ShapeDtypeStruct((n, LANES), F32))
        out_specs.append(lane_spec)
    return pl.pallas_call(
        functools.partial(_norm_kernel, moe=moe is not None, router=router_w is not None, final=final),
        grid=(steps,), in_specs=in_specs, out_specs=out_specs, out_shape=out_shape, scratch_shapes=scratch,
        compiler_params=_cparams(("arbitrary",)), name="rmsnorm",
    )(*args)


def _mm_kernel(a_ref, w_ref, *rest, has_res, w_rows_are_outputs):
    w = w_ref[...].astype(BF16)
    acc = _dot_nt(a_ref[...], w) if w_rows_are_outputs else _dot(a_ref[...], w)
    if has_res:
        acc = acc + rest[0][...]
    rest[-1][...] = acc.astype(rest[-1].dtype)


def _matmul(a, w, *, out_dtype, residual=None, tm=1024, tn=512, w_rows_are_outputs=False, name="matmul"):
    m, k = a.shape
    nc = w.shape[0] if w_rows_are_outputs else w.shape[1]
    tm, tn = min(tm, m), min(tn, nc)
    w_spec = pl.BlockSpec((tn, k), lambda i, j: (j, 0)) if w_rows_are_outputs else pl.BlockSpec((k, tn), lambda i, j: (0, j))
    in_specs = [pl.BlockSpec((tm, k), lambda i, j: (i, 0), pipeline_mode=pl.Buffered(1)), w_spec]
    args = [a, w]
    if residual is not None:
        in_specs.append(pl.BlockSpec((tm, tn), lambda i, j: (i, j)))
        args.append(residual)
    return pl.pallas_call(
        functools.partial(_mm_kernel, has_res=residual is not None, w_rows_are_outputs=w_rows_are_outputs),
        grid=(m // tm, nc // tn), in_specs=in_specs,
        out_specs=pl.BlockSpec((tm, tn), lambda i, j: (i, j)),
        out_shape=jax.ShapeDtypeStruct((m, nc), out_dtype),
        compiler_params=_cparams(("parallel", "parallel")), name=name,
    )(*args)


def _mlstm_kernel(aq_ref, ak_ref, av_ref, ao_ref, zs_ref, zst_ref, cw_ref, gbc_ref, gbr_ref, gain_ref,
                  tril_ref, triu_ref, shift_ref, o_ref, tail_s, qk_s, ct_s, n_s, m_s):
    L, d, H, W = MLSTM_CHUNK, HEAD_DIM, N_HEADS, MIX_WIDTH
    c = pl.program_id(1)

    @pl.when(c == 0)
    def _():
        tail_s[...] = jnp.zeros_like(tail_s)
        ct_s[...] = jnp.zeros_like(ct_s)
        n_s[...] = jnp.zeros_like(n_s)
        m_s[...] = jnp.zeros_like(m_s)

    x_cur = jnp.concatenate([aq_ref[...], ak_ref[...]], axis=1)
    x_ext = jnp.concatenate([tail_s[...], x_cur], axis=0)
    conv = cw_ref[CONV_WIDTH - 1:CONV_WIDTH, :] * x_cur.astype(F32)
    for j in range(CONV_WIDTH - 1):
        conv = conv + cw_ref[j:j + 1, :] * _dot(shift_ref[j], x_ext)
    tail_s[...] = x_cur[L - CONV_TAIL:, :]
    qk_s[...] = conv * jax.nn.sigmoid(conv)

    pre_c = zs_ref[...] + gbc_ref[...]
    ls_c = _log_sigmoid(pre_c)
    hi, mid, lo = _split3(ls_c)
    tril = tril_ref[...]
    bc = _dot(tril, hi) + _dot(tril, mid) + _dot(tril, lo)
    pre_r = zst_ref[...] + gbr_ref[...]
    br = _dot_exact_rhs(_log_sigmoid(pre_r), triu_ref[...])

    row = lax.broadcasted_iota(jnp.int32, (L, L), 0)
    col = lax.broadcasted_iota(jnp.int32, (L, L), 1)
    causal = row >= col
    scale = d ** -0.5
    for h in range(H):
        sl = slice(h * d, (h + 1) * d)
        q = qk_s[:, sl] * scale
        k = qk_s[:, W + h * d:W + (h + 1) * d]
        v = av_ref[:, sl]
        i_col, b_col = pre_c[:, h:h + 1], bc[:, H + h:H + h + 1]
        i_row, b_row = pre_r[h:h + 1, :], br[H + h:H + h + 1, :]
        m_prev = m_s[h:h + 1, 0:1]
        log_d = jnp.where(causal, b_col - b_row + i_row, NEG_INF)
        log_inter = b_col + m_prev
        m_row = jnp.maximum(jnp.max(log_d, axis=-1, keepdims=True), log_inter)
        qb, kb = q.astype(BF16), k.astype(BF16)
        s = _dot_nt(qb, kb) * jnp.exp(log_d - m_row)
        w_inter = jnp.exp(log_inter - m_row)
        num = _dot(s.astype(BF16), v) + w_inter * _dot(qb, ct_s[h].astype(BF16))
        den = jnp.sum(s, axis=-1, keepdims=True) + w_inter * jnp.sum(q * n_s[h:h + 1, :], axis=-1, keepdims=True)
        hh = num / jnp.maximum(jnp.abs(den), jnp.exp(-m_row))
        b_last = b_row[:, L - 1:L]
        log_w_row = b_last - b_row + i_row
        m_new = jnp.maximum(b_last + m_prev, jnp.max(log_w_row, axis=-1, keepdims=True))
        w_col = jnp.exp(b_last - b_col + i_col - m_new)
        decay = jnp.exp(b_last + m_prev - m_new)
        vw = (v.astype(F32) * w_col).astype(BF16)
        ct_s[h] = decay * ct_s[h] + _dot(k.T.astype(BF16), vw)
        n_s[h:h + 1, :] = decay * n_s[h:h + 1, :] + jnp.sum(k * w_col, axis=0, keepdims=True)
        m_s[h:h + 1, :] = jnp.broadcast_to(m_new, (1, LANES))
        hn = hh * lax.rsqrt(jnp.mean(hh * hh, axis=-1, keepdims=True) + RMS_EPS)
        o_ref[:, sl] = (hn * gain_ref[:, sl] * jax.nn.sigmoid(ao_ref[:, sl].astype(F32))).astype(BF16)


def _tri_consts(L):
    r = np.arange(L)
    tril = (r[:, None] >= r[None, :]).astype(np.float32)
    return jnp.asarray(tril, BF16), jnp.asarray(tril.T, BF16)


def _mlstm(zm, zs, zst, conv_w, gate_bias, gain, B, T):
    L, W = MLSTM_CHUNK, MIX_WIDTH
    nc = T // L
    n = B * T
    gbc = jnp.zeros((1, LANES), F32).at[0, :2 * N_HEADS].set(gate_bias.reshape(-1))
    gbr = jnp.broadcast_to(gate_bias.reshape(2 * N_HEADS, 1), (2 * N_HEADS, L)).astype(F32)
    tril, triu = _tri_consts(L)
    t_idx = np.arange(L)[:, None]
    c_idx = np.arange(CONV_TAIL + L)[None, :]
    shift = jnp.asarray(np.stack([(c_idx == CONV_TAIL + t_idx - (CONV_WIDTH - 1 - j)) for j in range(CONV_WIDTH - 1)])
                        .astype(np.float32), BF16)

    def blk(b_idx):
        return pl.BlockSpec((L, W), lambda b, c: (b * nc + c, b_idx))

    const = lambda shape: pl.BlockSpec(shape, lambda b, c: (0,) * len(shape))
    return pl.pallas_call(
        _mlstm_kernel, grid=(B, nc),
        in_specs=[blk(BLK_AQ), blk(BLK_AK), blk(BLK_AV), blk(BLK_AO),
                  pl.BlockSpec((L, LANES), lambda b, c: (b * nc + c, 0)),
                  pl.BlockSpec((2 * N_HEADS, L), lambda b, c: (0, b * nc + c)),
                  const((CONV_WIDTH, 2 * W)), const((1, LANES)), const((2 * N_HEADS, L)), const((1, W)),
                  const((L, L)), const((L, L)), const((CONV_WIDTH - 1, L, CONV_TAIL + L))],
        out_specs=pl.BlockSpec((L, W), lambda b, c: (b * nc + c, 0)),
        out_shape=jax.ShapeDtypeStruct((n, W), BF16),
        scratch_shapes=[pltpu.VMEM((CONV_TAIL, 2 * W), BF16), pltpu.VMEM((L, 2 * W), F32),
                        pltpu.VMEM((N_HEADS, HEAD_DIM, HEAD_DIM), F32), pltpu.VMEM((N_HEADS, HEAD_DIM), F32),
                        pltpu.VMEM((N_HEADS, LANES), F32)],
        compiler_params=_cparams(("parallel", "arbitrary")), name="mlstm",
    )(zm, zm, zm, zm, zs, zst, conv_w, gbc, gbr, gain.reshape(1, W), tril, triu, shift)


def _ret_kernel(q_ref, k_ref, v_ref, g_ref, dec_ref, qd_ref, kd_ref, gain_ref, o_ref, r_s, *, chunk_decay):
    L, d, H = RET_CHUNK, HEAD_DIM, N_HEADS
    c = pl.program_id(1)

    @pl.when(c == 0)
    def _():
        r_s[...] = jnp.zeros_like(r_s)

    scale = d ** -0.5
    for h in range(H):
        sl = slice(h * d, (h + 1) * d)
        q = q_ref[:, sl].astype(F32)
        k = k_ref[:, sl].astype(F32) * scale
        v = v_ref[:, sl]
        s = _dot_nt(q.astype(BF16), k.astype(BF16)) * dec_ref[h]
        o = _dot(s.astype(BF16), v) + _dot((q * qd_ref[:, sl]).astype(BF16), r_s[h].astype(BF16))
        kd = (k * kd_ref[:, sl]).T.astype(BF16)
        r_s[h] = chunk_decay[h] * r_s[h] + _dot(kd, v)
        on = o * lax.rsqrt(jnp.mean(o * o, axis=-1, keepdims=True) + RMS_EPS)
        gate = g_ref[:, sl].astype(F32)
        o_ref[:, sl] = (on * gain_ref[:, sl] * (gate * jax.nn.sigmoid(gate))).astype(BF16)


def _retention(zm, gain, B, T):
    L, W, H, d = RET_CHUNK, MIX_WIDTH, N_HEADS, HEAD_DIM
    nc = T // L
    log_gamma = np.log(np.float32(1.0) - np.float32(2.0) ** (-5.0 - np.arange(H, dtype=np.float32))).astype(np.float32)
    pos = np.arange(L, dtype=np.float32)
    diff = pos[:, None] - pos[None, :]
    decay = np.where(diff >= 0, np.exp(log_gamma[:, None, None] * np.maximum(diff, 0.0)), 0.0).astype(np.float32)
    q_decay = np.exp(log_gamma[:, None] * (pos + 1.0)).astype(np.float32)
    k_decay = np.exp(log_gamma[:, None] * (L - 1.0 - pos)).astype(np.float32)
    chunk_decay = tuple(float(v) for v in np.exp(log_gamma * np.float32(L)).astype(np.float32))
    qd = jnp.asarray(np.repeat(q_decay.T, d, axis=1))
    kd = jnp.asarray(np.repeat(k_decay.T, d, axis=1))

    def blk(b_idx):
        return pl.BlockSpec((L, W), lambda b, c: (b * nc + c, b_idx))

    const = lambda shape: pl.BlockSpec(shape, lambda b, c: (0,) * len(shape))
    return pl.pallas_call(
        functools.partial(_ret_kernel, chunk_decay=chunk_decay), grid=(B, nc),
        in_specs=[blk(BLK_BQ), blk(BLK_BK), blk(BLK_BV), blk(BLK_BG),
                  const((H, L, L)), const((L, W)), const((L, W)), const((1, W))],
        out_specs=pl.BlockSpec((L, W), lambda b, c: (b * nc + c, 0)),
        out_shape=jax.ShapeDtypeStruct((B * T, W), BF16),
        scratch_shapes=[pltpu.VMEM((H, d, d), F32)],
        compiler_params=_cparams(("parallel", "arbitrary")), name="retention",
    )(zm, zm, zm, zm, jnp.asarray(decay), qd, kd, gain.reshape(1, W))


def _gelu(x):
    return 0.5 * x * (1.0 + jnp.tanh(0.7978845608028654 * (x + 0.044715 * (x * x * x))))


def _sgu_kernel(u_ref, v_ref, ng_ref, w_ref, b_ref, o_ref):
    L, d, H = SGU_CHUNK, HEAD_DIM, N_HEADS
    v = _gelu(v_ref[...].astype(F32))
    vn = (v * lax.rsqrt(jnp.mean(v * v, axis=-1, keepdims=True) + RMS_EPS) * ng_ref[...]).astype(BF16)
    u = _gelu(u_ref[...].astype(F32))
    row = lax.broadcasted_iota(jnp.int32, (L, L), 0)
    col = lax.broadcasted_iota(jnp.int32, (L, L), 1)
    for g in range(H):
        sl = slice(g * d, (g + 1) * d)
        wm = jnp.where(row >= col, w_ref[g], 0.0).astype(BF16)
        mixed = _dot(wm, vn[:, sl]) + b_ref[:, sl]
        o_ref[:, sl] = (u[:, sl] * mixed).astype(BF16)


def _sgu(zm, norm_g, w_s, b_s, n):
    L, W, H, d = SGU_CHUNK, MIX_WIDTH, N_HEADS, HEAD_DIM
    bsb = jnp.repeat(b_s.T, d, axis=1)
    const = lambda shape: pl.BlockSpec(shape, lambda i: (0,) * len(shape))
    return pl.pallas_call(
        _sgu_kernel, grid=(n // L,),
        in_specs=[pl.BlockSpec((L, W), lambda i: (i, BLK_DU)), pl.BlockSpec((L, W), lambda i: (i, BLK_DV)),
                  const((1, W)), const((H, L, L)), const((L, W))],
        out_specs=pl.BlockSpec((L, W), lambda i: (i, 0)),
        out_shape=jax.ShapeDtypeStruct((n, W), BF16),
        compiler_params=_cparams(("parallel",)), name="sgu",
    )(zm, zm, norm_g.reshape(1, W), w_s, bsb)


def _compress_kernel(k_ref, v_ref, w_ref, pe_ref, ko_ref, vo_ref, buf):
    t = k_ref.shape[0]
    n16 = t // CMP_STRIDE
    buf[t:t + CMP_STRIDE, :] = jnp.zeros((CMP_STRIDE, HEAD_DIM), F32)
    for which, (src, dst) in enumerate(((k_ref, ko_ref), (v_ref, vo_ref))):
        buf[0:t, :] = src[...].astype(F32)
        acc = jnp.zeros((n16, HEAD_DIM), F32)
        for l in range(CMP_BLOCK):
            x = buf[pl.ds(l, n16, stride=CMP_STRIDE), :]
            acc = acc + _dot((x + pe_ref[which, l:l + 1, :]).astype(BF16), w_ref[which, l].astype(BF16))
        dst[...] = (acc.T if which else acc).astype(BF16)


def _compress(zm, cmp_w, cmp_pe, B, T):
    G, d = NSA_KV_GROUPS, HEAD_DIM
    n16 = T // CMP_STRIDE
    out = jax.ShapeDtypeStruct((B, G, n16, d), BF16)
    kblk = KV_BASE
    vblk = KV_BASE + G
    return pl.pallas_call(
        _compress_kernel, grid=(B, G),
        in_specs=[pl.BlockSpec((T, d), lambda b, g: (b, kblk + g)), pl.BlockSpec((T, d), lambda b, g: (b, vblk + g)),
                  pl.BlockSpec((2, CMP_BLOCK, d, d), lambda b, g: (0, 0, 0, 0)),
                  pl.BlockSpec((2, CMP_BLOCK, d), lambda b, g: (0, 0, 0))],
        out_specs=[pl.BlockSpec((None, None, n16, d), lambda b, g: (b, g, 0, 0)),
                   pl.BlockSpec((None, None, d, n16), lambda b, g: (b, g, 0, 0))],
        out_shape=[out, jax.ShapeDtypeStruct((B, G, d, n16), BF16)],
        scratch_shapes=[pltpu.VMEM((T + CMP_STRIDE, d), F32)],
        compiler_params=_cparams(("parallel", "parallel")), name="nsa_compress",
    )(zm, zm, cmp_w, cmp_pe)


RANK_CHUNK = 8


def _lanes_x(x, times):
    return jnp.concatenate([x] * times, axis=1)


def _nsa_a_kernel(q_ref, kc_ref, vct_ref, *rest, n_back):
    nw = n_back + 1
    kw_refs, vwt_refs = rest[:nw], rest[nw:2 * nw]
    cgt_ref, ovt_ref, acmp_ref, awin_ref, part_ref, sel_ref, q_s, key_s, rank_s = rest[2 * nw:]
    tq, d, J = Q_BLOCK, HEAD_DIM, NSA_GROUP_SIZE
    qi = pl.program_id(2)
    t0 = qi * tq
    ncmp = kc_ref.shape[0]
    scale = d ** -0.5
    for j in range(J):
        q_s[j * tq:(j + 1) * tq, :] = (q_ref[:, j * d:(j + 1) * d].astype(F32) * scale).astype(BF16)
    sg_t = jax.nn.sigmoid(cgt_ref[...])
    t_row = t0 + lax.broadcasted_iota(jnp.int32, (1, tq), 1)

    n_i = lax.broadcasted_iota(jnp.int32, (ncmp, tq), 0)
    r_i = lax.broadcasted_iota(jnp.int32, (ncmp, tq), 1)
    bias_c = jnp.where(t0 + r_i - CMP_STRIDE * n_i - (CMP_BLOCK - 1) >= 0, 0.0, NEG_INF)
    s = _dot_nt(kc_ref[...], q_s[...]) - acmp_ref[...] + _lanes_x(bias_c, J)
    p = jnp.exp(s - jnp.max(s, axis=0, keepdims=True))
    has_c = jnp.where(t_row >= CMP_BLOCK - 1, 1.0, 0.0)
    p = p * (_lanes_x(has_c, J) / jnp.sum(p, axis=0, keepdims=True))
    o_cmp_t = _dot(vct_ref[...], p.astype(BF16))
    psum = p[:, 0:tq]
    for j in range(1, J):
        psum = psum + p[:, j * tq:(j + 1) * tq]

    hi, mid, lo = _split3(psum)
    ovt = ovt_ref[...]
    imp_t = _dot(ovt, hi) + _dot(ovt, mid) + _dot(ovt, lo)
    blk_t = lax.broadcasted_iota(jnp.int32, (LANES, tq), 0)
    cur_t = lax.shift_right_logical(t0 + lax.broadcasted_iota(jnp.int32, (LANES, tq), 1), SEL_BLOCK.bit_length() - 1)
    valid_t = blk_t <= cur_t
    forced = (blk_t == 0) | (blk_t == cur_t) | (blk_t == cur_t - 1)
    key_s[...] = jnp.where(valid_t, imp_t + jnp.where(forced, FORCE_SCORE, 0.0), NEG_INF)
    rank_s[...] = jnp.zeros_like(rank_s)
    sub = lax.broadcasted_iota(jnp.int32, (8, tq), 0)
    groups = LANES // 8
    last_valid = lax.shift_right_logical(t0 + tq - 1, SEL_BLOCK.bit_length() - 1)
    for chunk in range(LANES // RANK_CHUNK):
        @pl.when(chunk * RANK_CHUNK <= last_valid)
        def _(chunk=chunk):
            key_g = [key_s[8 * i:8 * i + 8, :] for i in range(groups)]
            rank = [rank_s[8 * i:8 * i + 8, :] for i in range(groups)]
            for mp in range(chunk * RANK_CHUNK, (chunk + 1) * RANK_CHUNK):
                rowv = jnp.broadcast_to(key_s[mp:mp + 1, :], (8, tq))
                for i in range(groups):
                    if 8 * i > mp:
                        ahead = rowv >= key_g[i]
                    elif 8 * i + 7 <= mp:
                        ahead = rowv > key_g[i]
                    else:
                        ahead = (rowv > key_g[i]) | ((sub > mp - 8 * i) & (rowv == key_g[i]))
                    rank[i] = rank[i] + jnp.where(ahead, 1.0, 0.0)
            for i in range(groups):
                rank_s[8 * i:8 * i + 8, :] = rank[i]
    sel_ref[...] = jnp.where((rank_s[...] < float(SEL_TOPN)) & valid_t, 1.0, 0.0).astype(BF16)

    span = nw * tq
    kw = jnp.concatenate([kr[...] for kr in kw_refs], axis=0)
    vw_t = jnp.concatenate([vr[...] for vr in vwt_refs], axis=1)
    c_i = lax.broadcasted_iota(jnp.int32, (span, tq), 0)
    bias_w = jnp.where(t0 - n_back * tq + c_i >= 0, 0.0, NEG_INF)
    s = _dot_nt(kw, q_s[...]) - awin_ref[...] + _lanes_x(bias_w, J)
    p = jnp.exp(s - jnp.max(s, axis=0, keepdims=True))
    inv_l = 1.0 / jnp.sum(p, axis=0, keepdims=True)
    o_win_t = _dot(vw_t, p.astype(BF16))
    g_cmp = jnp.concatenate([sg_t[j:j + 1, :] for j in range(J)], axis=1)
    g_win = jnp.concatenate([sg_t[2 * J + j:2 * J + j + 1, :] for j in range(J)], axis=1)
    part_ref[...] = g_cmp * o_cmp_t + (g_win * inv_l) * o_win_t


def _alibi_slopes():
    return (2.0 ** (-8.0 * (np.arange(N_HEADS, dtype=np.float32) + 1.0) / N_HEADS)).astype(np.float32)


def _nsa_consts(T):
    G, J, tq = NSA_KV_GROUPS, NSA_GROUP_SIZE, Q_BLOCK
    n16 = T // CMP_STRIDE
    n_sel = T // SEL_BLOCK
    n = np.arange(n16)[None, :]
    m = np.arange(LANES)[:, None]
    c_start, s_start = n * CMP_STRIDE, m * SEL_BLOCK
    overlap_t = ((c_start < s_start + SEL_BLOCK) & (c_start + CMP_BLOCK > s_start) & (n < n16 - 1) & (m < n_sel))
    slopes = _alibi_slopes().reshape(G, 1, J, 1)
    r = np.arange(tq, dtype=np.float32)[None, None, None, :]
    dist_c = r - CMP_STRIDE * np.arange(n16, dtype=np.float32)[None, :, None, None] - (CMP_BLOCK - 1)
    a_cmp = (slopes * dist_c).reshape(G, n16, J * tq)
    span = WINDOW + tq
    dist_w = r + WINDOW - np.arange(span, dtype=np.float32)[None, :, None, None]
    a_win = np.where((dist_w >= 0) & (dist_w < WINDOW), slopes * dist_w, -NEG_INF).reshape(G, span, J * tq)
    return (jnp.asarray(overlap_t.astype(np.float32), BF16), jnp.asarray(a_cmp.astype(np.float32)),
            jnp.asarray(a_win.astype(np.float32)))


def _nsa_a(zm, k_cmp, v_cmp_t, vw_t, cg_t, B, T):
    G, J, d, tq = NSA_KV_GROUPS, NSA_GROUP_SIZE, HEAD_DIM, Q_BLOCK
    nq = T // tq
    n16 = T // CMP_STRIDE
    n_back = WINDOW // tq
    span = WINDOW + tq
    qblk = BLK_CQ * (MIX_WIDTH // (J * d))
    kwblk = KV_BASE + 4 * G
    overlap_t, a_cmp, a_win = _nsa_consts(T)

    def past(qi, i):
        return jnp.maximum(qi - n_back + i, 0)

    in_specs = ([pl.BlockSpec((tq, J * d), lambda b, g, qi: (b * nq + qi, qblk + g)),
                 pl.BlockSpec((None, None, n16, d), lambda b, g, qi: (b, g, 0, 0)),
                 pl.BlockSpec((None, None, d, n16), lambda b, g, qi: (b, g, 0, 0))]
                + [pl.BlockSpec((tq, d), functools.partial(lambda b, g, qi, i: (b * nq + past(qi, i), kwblk + g), i=i))
                   for i in range(n_back + 1)]
                + [pl.BlockSpec((None, None, d, tq), functools.partial(lambda b, g, qi, i: (b, g, 0, past(qi, i)), i=i))
                   for i in range(n_back + 1)]
                + [pl.BlockSpec((None, 16, tq), lambda b, g, qi: (g, 0, b * nq + qi)),
                   pl.BlockSpec((LANES, n16), lambda b, g, qi: (0, 0)),
                   pl.BlockSpec((None, n16, J * tq), lambda b, g, qi: (g, 0, 0)),
                   pl.BlockSpec((None, span, J * tq), lambda b, g, qi: (g, 0, 0))])
    return pl.pallas_call(
        functools.partial(_nsa_a_kernel, n_back=n_back), grid=(B, G, nq), in_specs=in_specs,
        out_specs=[pl.BlockSpec((None, None, None, d, J * tq), lambda b, g, qi: (b, g, qi, 0, 0)),
                   pl.BlockSpec((None, None, LANES, tq), lambda b, g, qi: (b, g, 0, qi))],
        out_shape=[jax.ShapeDtypeStruct((B, G, nq, d, J * tq), F32), jax.ShapeDtypeStruct((B, G, LANES, T), BF16)],
        scratch_shapes=[pltpu.VMEM((J * tq, d), BF16), pltpu.VMEM((LANES, tq), F32), pltpu.VMEM((LANES, tq), F32)],
        compiler_params=_cparams(("parallel", "parallel", "parallel")), name="nsa_cmp_win",
    )(zm, k_cmp, v_cmp_t, *([zm] * (n_back + 1)), *([vw_t] * (n_back + 1)), cg_t, overlap_t, a_cmp, a_win)


def _nsa_b_kernel(cnt_ref, lst_ref, q_ref, ks_ref, vt_ref, sel_ref, e_ref, a_ref, srow_ref, cgt_ref, part_ref, o_ref,
                  q_s, acc_s, *, tk, nk):
    tq, d, J = Q_BLOCK, HEAD_DIM, NSA_GROUP_SIZE
    b, g, qi = pl.program_id(0), pl.program_id(1), pl.program_id(2)
    lin = (b * pl.num_programs(1) + g) * pl.num_programs(2) + qi
    t0 = qi * tq
    scale = d ** -0.5
    for j in range(J):
        q_s[j * tq:(j + 1) * tq, :] = (q_ref[:, j * d:(j + 1) * d].astype(F32) * scale).astype(BF16)
    acc_s[...] = jnp.zeros_like(acc_s)
    r_minus_c = lax.broadcasted_iota(jnp.int32, (tk, tq), 1) - lax.broadcasted_iota(jnp.int32, (tk, tq), 0)

    def body(i, carry):
        m_prev, l_prev = carry
        kt = lst_ref[lin * nk + i]
        k_t = ks_ref[pl.ds(pl.multiple_of(kt * tk, tk), tk), :]
        off = t0 - kt * tk
        picked = _dot(e_ref[kt], sel_ref[...])
        bias = jnp.where((picked > 0.5) & (r_minus_c + off >= 0), 0.0, NEG_INF)
        s = _dot_nt(k_t, q_s[...]) - a_ref[...] + jnp.concatenate([bias] * J, axis=1)
        shift = srow_ref[...] * off.astype(F32)
        m_new = jnp.maximum(m_prev, jnp.max(s, axis=0, keepdims=True) - shift)
        p = jnp.exp(s - (m_new + shift))
        alpha = jnp.exp(m_prev - m_new)
        acc_s[...] = alpha * acc_s[...] + _dot(vt_ref[kt], p.astype(BF16))
        return m_new, alpha * l_prev + jnp.sum(p, axis=0, keepdims=True)

    init = (jnp.full((1, J * tq), NEG_INF, F32), jnp.zeros((1, J * tq), F32))
    _, l = lax.fori_loop(0, cnt_ref[lin], body, init)
    sg_t = jax.nn.sigmoid(cgt_ref[...])
    g_sel = jnp.concatenate([sg_t[J + j:J + j + 1, :] for j in range(J)], axis=1)
    total_t = part_ref[...] + (g_sel / l) * acc_s[...]
    for j in range(J):
        o_ref[:, j * d:(j + 1) * d] = total_t[:, j * tq:(j + 1) * tq].T.astype(BF16)


def _nsa_b(zm, sel_t, cg_t, part_t, B, T, tk=512):
    G, J, d, tq = NSA_KV_GROUPS, NSA_GROUP_SIZE, HEAD_DIM, Q_BLOCK
    tk = min(tk, T)
    nq, nk = T // tq, T // tk
    qblk = BLK_CQ * (MIX_WIDTH // (J * d))
    ksblk = KV_BASE + 2 * G
    vs_col = (KV_BASE + 3 * G) * d
    per_tile = tk // SEL_BLOCK
    kt = np.arange(nk)[:, None, None]
    cc = np.arange(tk)[None, :, None]
    m = np.arange(LANES)[None, None, :]
    expand_t = jnp.asarray((m == kt * per_tile + cc // SEL_BLOCK).astype(np.float32), BF16)
    slopes_np = _alibi_slopes()
    r_minus_c = (np.arange(tq)[None, :] - np.arange(tk)[:, None]).astype(np.float32)
    alibi_t = jnp.asarray((slopes_np.reshape(G, 1, J, 1) * r_minus_c[None, :, None, :]).reshape(G, tk, J * tq))
    srow = jnp.asarray(np.repeat(slopes_np.reshape(G, J), tq, axis=1).reshape(G, 1, J * tq))
    vs_t = zm[:, vs_col:vs_col + G * d].reshape(B, nk, tk, G, d).transpose(0, 3, 1, 4, 2)

    active = sel_t[:, :, :nk * per_tile].reshape(B, G, nk, per_tile, nq, tq).max(axis=(3, 5)) > 0
    active = active.transpose(0, 1, 3, 2)
    tiles = jnp.arange(nk, dtype=jnp.int32)
    order = jnp.sort(jnp.where(active, tiles, tiles + nk), axis=-1) % nk
    counts = active.sum(axis=-1).astype(jnp.int32)

    grid_spec = pltpu.PrefetchScalarGridSpec(
        num_scalar_prefetch=2, grid=(B, G, nq),
        in_specs=[pl.BlockSpec((tq, J * d), lambda b, g, qi, c, o: (b * nq + qi, qblk + g)),
                  pl.BlockSpec((T, d), lambda b, g, qi, c, o: (b, ksblk + g)),
                  pl.BlockSpec((None, None, nk, d, tk), lambda b, g, qi, c, o: (b, g, 0, 0, 0)),
                  pl.BlockSpec((None, None, LANES, tq), lambda b, g, qi, c, o: (b, g, 0, qi)),
                  pl.BlockSpec((nk, tk, LANES), lambda b, g, qi, c, o: (0, 0, 0)),
                  pl.BlockSpec((None, tk, J * tq), lambda b, g, qi, c, o: (g, 0, 0)),
                  pl.BlockSpec((None, 1, J * tq), lambda b, g, qi, c, o: (g, 0, 0)),
                  pl.BlockSpec((None, 16, tq), lambda b, g, qi, c, o: (g, 0, b * nq + qi)),
                  pl.BlockSpec((None, None, None, d, J * tq), lambda b, g, qi, c, o: (b, g, qi, 0, 0))],
        out_specs=pl.BlockSpec((tq, J * d), lambda b, g, qi, c, o: (b * nq + qi, g)),
        scratch_shapes=[pltpu.VMEM((J * tq, d), BF16), pltpu.VMEM((d, J * tq), F32)])
    return pl.pallas_call(
        functools.partial(_nsa_b_kernel, tk=tk, nk=nk), grid_spec=grid_spec,
        out_shape=jax.ShapeDtypeStruct((B * T, MIX_WIDTH), BF16),
        compiler_params=_cparams(("parallel", "parallel", "parallel")), name="nsa_selected",
    )(counts.reshape(-1), order.reshape(-1).astype(jnp.int32), zm, zm, vs_t, sel_t, expand_t, alibi_t, srow, cg_t, part_t)


def _nsa(zm, zs, cmp_w, cmp_pe, B, T):
    G, J = NSA_KV_GROUPS, NSA_GROUP_SIZE
    n = B * T
    d = HEAD_DIM
    cg = zs[:, 2 * N_HEADS:2 * N_HEADS + 3 * N_HEADS].reshape(n, 3, G, J)
    cg_t = jnp.pad(cg.transpose(2, 1, 3, 0).reshape(G, 3 * J, n), ((0, 0), (0, 16 - 3 * J), (0, 0)))
    vw_col = (KV_BASE + 5 * G) * d
    vw_t = zm[:, vw_col:vw_col + G * d].reshape(B, T, G, d).transpose(0, 2, 3, 1)
    k_cmp, v_cmp_t = _compress(zm, cmp_w, cmp_pe, B, T)
    part_t, sel_t = _nsa_a(zm, k_cmp, v_cmp_t, vw_t, cg_t, B, T)
    return _nsa_b(zm, sel_t, cg_t, part_t, B, T)


def _merge_kernel(h_ref, y0, y1, y2, y3, g0, g1, g2, g3, b0, b1, b2, b3, o_ref):
    h = h_ref[...]
    acc = None
    for y, gw, bw in ((y0, g0, b0), (y1, g1, b1), (y2, g2, b2), (y3, g3, b3)):
        term = jax.nn.sigmoid(_dot(h, gw[...])) * _dot(y[...], bw[...])
        acc = term if acc is None else acc + term
    o_ref[...] = acc.astype(BF16)


def _merge(h, ys, w_gate, w_branch, tm=1024, tn=256):
    n, dm = h.shape
    tm = min(tm, n)
    nj = dm // tn
    once = pl.Buffered(1)
    y_spec = pl.BlockSpec((tm, MIX_WIDTH), lambda i, j: (i, 0), pipeline_mode=once)
    gate_specs = [pl.BlockSpec((dm, tn), functools.partial(lambda i, j, b: (0, b * nj + j), b=b)) for b in range(N_BRANCH)]
    br_specs = [pl.BlockSpec((None, MIX_WIDTH, tn), functools.partial(lambda i, j, b: (b, 0, j), b=b)) for b in range(N_BRANCH)]
    return pl.pallas_call(
        _merge_kernel, grid=(n // tm, nj),
        in_specs=([pl.BlockSpec((tm, dm), lambda i, j: (i, 0), pipeline_mode=once)] + [y_spec] * N_BRANCH
                  + gate_specs + br_specs),
        out_specs=pl.BlockSpec((tm, tn), lambda i, j: (i, j)),
        out_shape=jax.ShapeDtypeStruct((n, dm), BF16),
        compiler_params=_cparams(("parallel", "parallel")), name="gated_merge",
    )(h, *ys, *([w_gate] * N_BRANCH), *([w_branch] * N_BRANCH))


def _pack_bf16_pairs(y):
    half = y.shape[1] // 2
    bits = lax.bitcast_convert_type(y.astype(BF16).astype(F32), jnp.int32)
    return bits[:, :half] | lax.shift_right_logical(bits[:, half:], 16)


def _unpack_bf16_pairs(w):
    first = lax.bitcast_convert_type(w & jnp.int32(-65536), F32).astype(BF16)
    second = lax.bitcast_convert_type(lax.shift_left(w, 16), F32).astype(BF16)
    return first, second


def _expert_kernel(be_ref, nact_ref, idx_ref, idx_next_ref, hp_hbm, w1_ref, w3_ref, w2_ref, o_ref,
                   xbuf, sems, w1b, w3b, w2b):
    i = pl.program_id(0)
    n_act = nact_ref[0]
    slot = lax.rem(i, 2)
    e = be_ref[i]
    prev = be_ref[jnp.maximum(i - 1, 0)]
    half = hp_hbm.shape[1]

    @pl.when((i == 0) & (n_act > 0))
    def _():
        _row_gather(idx_ref, hp_hbm, xbuf.at[0], sems.at[0], MOE_BLOCK)

    @pl.when(i + 1 < n_act)
    def _():
        _row_gather(idx_next_ref, hp_hbm, xbuf.at[1 - slot], sems.at[1 - slot], MOE_BLOCK)

    @pl.when((i == 0) | (prev != e))
    def _():
        w1b[...] = w1_ref[...].astype(BF16)
        w3b[...] = w3_ref[...].astype(BF16)
        w2b[...] = w2_ref[...].astype(BF16)

    @pl.when(i < n_act)
    def _():
        _row_gather_wait(hp_hbm, xbuf.at[slot], sems.at[slot], MOE_BLOCK)
        x_a, x_b = _unpack_bf16_pairs(xbuf[slot])
        a1 = _dot(x_a, w1b[0:half, :]) + _dot(x_b, w1b[half:, :])
        a3 = _dot(x_a, w3b[0:half, :]) + _dot(x_b, w3b[half:, :])
        a = (a1 * jax.nn.sigmoid(a1)) * a3
        o_ref[...] = _dot(a.astype(BF16), w2b[...])

    @pl.when(i >= n_act)
    def _():
        o_ref[...] = jnp.zeros_like(o_ref)


def _experts(hp, tok_pad, blk_exp, n_active, w1, w3, w2, layer):
    half = hp.shape[1]
    dm = 2 * half
    n_blk = tok_pad.shape[0] // MOE_BLOCK
    ff = w1.shape[-1]
    idx = tok_pad.reshape(n_blk, 1, MOE_BLOCK)
    smem_blk = lambda f: pl.BlockSpec((None, 1, MOE_BLOCK), f, memory_space=pltpu.SMEM)
    grid_spec = pltpu.PrefetchScalarGridSpec(
        num_scalar_prefetch=2, grid=(n_blk,),
        in_specs=[smem_blk(lambda i, be, na: (i, 0, 0)),
                  smem_blk(lambda i, be, na: (jnp.minimum(i + 1, n_blk - 1), 0, 0)),
                  pl.BlockSpec(memory_space=pl.ANY),
                  pl.BlockSpec((None, None, dm, ff), lambda i, be, na: (layer, be[i], 0, 0)),
                  pl.BlockSpec((None, None, dm, ff), lambda i, be, na: (layer, be[i], 0, 0)),
                  pl.BlockSpec((None, None, ff, dm), lambda i, be, na: (layer, be[i], 0, 0))],
        out_specs=pl.BlockSpec((MOE_BLOCK, dm), lambda i, be, na: (i, 0)),
        scratch_shapes=[pltpu.VMEM((2, MOE_BLOCK, half), jnp.int32), pltpu.SemaphoreType.DMA((2,)),
                        pltpu.VMEM((dm, ff), BF16), pltpu.VMEM((dm, ff), BF16), pltpu.VMEM((ff, dm), BF16)])
    return pl.pallas_call(
        _expert_kernel, grid_spec=grid_spec,
        out_shape=jax.ShapeDtypeStruct((n_blk * MOE_BLOCK, dm), F32),
        compiler_params=_cparams(("arbitrary",), vmem=MOE_VMEM_LIMIT), name="moe_experts",
    )(blk_exp, n_active, idx, idx, hp, w1, w3, w2)


def _dispatch_plan(e_idx):
    n, K = e_idx.shape
    nk = n * K
    i32 = jnp.int32
    flat_e = e_idx.reshape(-1)
    order = jnp.argsort(flat_e).astype(i32)
    rank = jnp.argsort(order).astype(i32)
    experts = jnp.arange(N_EXPERTS, dtype=i32)
    hot = flat_e[:, None] == experts[None, :]
    counts = jnp.sum(hot.astype(i32), axis=0)
    padded = (counts + MOE_BLOCK - 1) // MOE_BLOCK * MOE_BLOCK
    pad_end = jnp.cumsum(padded)
    pad_start = pad_end - padded
    start = jnp.cumsum(counts) - counts
    lookup = lambda table, onehot: jnp.sum(jnp.where(onehot, table[None, :], 0), axis=1)
    dest = lookup(pad_start - start, hot) + rank
    n_blk = (nk + MOE_BLOCK - 1) // MOE_BLOCK + N_EXPERTS
    blk_first = jnp.arange(n_blk, dtype=i32) * MOE_BLOCK
    blk_exp = jnp.minimum(jnp.sum((pad_end[None, :] <= blk_first[:, None]).astype(i32), axis=1), N_EXPERTS - 1)
    blk_hot = blk_exp[:, None] == experts[None, :]
    within = jnp.arange(MOE_BLOCK, dtype=i32)[None, :]
    j = (blk_first - lookup(pad_start, blk_hot))[:, None] + within
    sorted_pos = jnp.clip(lookup(start, blk_hot)[:, None] + j, 0, nk - 1)
    src = jnp.take(order, sorted_pos.reshape(-1), mode="clip") // K
    tok_pad = jnp.where((j < lookup(counts, blk_hot)[:, None]).reshape(-1), src, n - 1)
    n_active = (pad_end[-1] // MOE_BLOCK).astype(i32).reshape(1)
    return tok_pad, blk_exp.astype(i32), n_active, dest.reshape(n, K)


def _moe(h, route, w1, w3, w2, layer):
    e_idx = route[:, :TOP_K_IN_GROUP].astype(jnp.int32)
    tok_pad, blk_exp, n_active, dest = _dispatch_plan(e_idx)
    y = _experts(h, tok_pad, blk_exp, n_active, w1, w3, w2, layer)
    return y, dest


PACK_TILE = 512


def _pack_kernel(offs_ref, wt_ref, o_ref):
    del offs_ref
    o_ref[...] = wt_ref[0].T.astype(BF16)


def _pack_columns(w_in_t, layer, col_offsets):
    _, _, k = w_in_t.shape
    nt = len(col_offsets)
    grid_spec = pltpu.PrefetchScalarGridSpec(
        num_scalar_prefetch=1, grid=(nt,),
        in_specs=[pl.BlockSpec((pl.Element(1), pl.Element(PACK_TILE), pl.Element(k)),
                               lambda j, offs: (layer, pl.multiple_of(offs[j], 8), 0))],
        out_specs=pl.BlockSpec((k, PACK_TILE), lambda j, offs: (0, j)))
    return pl.pallas_call(
        _pack_kernel, grid_spec=grid_spec,
        out_shape=jax.ShapeDtypeStruct((k, nt * PACK_TILE), BF16),
        compiler_params=_cparams(("parallel",)), name="pack_w_in",
    )(jnp.asarray(col_offsets, jnp.int32), w_in_t)


def _tile_offsets(groups):
    offs = []
    for i in groups:
        assert COL_WIDTHS[i] % PACK_TILE == 0 or COL_WIDTHS[i] < PACK_TILE
        offs += list(range(COL_OFFSETS[i], COL_OFFSETS[i + 1], PACK_TILE))
    return offs


def _pack_w_in(w_in_t, layer):
    wide = [i for i in MAIN_ORDER if COL_WIDTHS[i] >= PACK_TILE]
    narrow = [i for i in MAIN_ORDER if COL_WIDTHS[i] < PACK_TILE]
    assert MAIN_ORDER == tuple(wide + narrow) and narrow == list(range(narrow[0], narrow[-1] + 1))
    kv_lo, kv_hi = COL_OFFSETS[narrow[0]], COL_OFFSETS[narrow[-1] + 1]
    assert (kv_hi - kv_lo) % PACK_TILE == 0
    main_offs = _tile_offsets(wide) + list(range(kv_lo, kv_hi, PACK_TILE))
    w_main = _pack_columns(w_in_t, layer, main_offs)
    w_gate = _pack_columns(w_in_t, layer, _tile_offsets([I_MERGE]))
    rows = lambda i: w_in_t[layer, COL_OFFSETS[i]:COL_OFFSETS[i + 1]]
    small_t = jnp.concatenate([rows(I_AI), rows(I_AF), rows(I_CG)], axis=0)
    small_t = jnp.pad(small_t, ((0, LANES - small_t.shape[0]), (0, 0)))
    return w_main, small_t, w_gate


def _router_weights(wg, bg, we, be):
    w = jnp.concatenate([wg, we], axis=1)
    w = jnp.pad(w, ((0, 0), (0, LANES - w.shape[1])))
    hi = w.astype(BF16)
    lo = (w - hi.astype(F32)).astype(BF16)
    b = jnp.concatenate([bg, be])
    b = jnp.pad(b, (0, LANES - b.shape[0])).reshape(1, LANES).astype(F32)
    return hi, lo, b


def kernel(x, w_in, mlstm_conv, mlstm_gate_bias, mlstm_norm, ret_norm, nsa_cmp_w, nsa_cmp_pe, sgu_norm, sgu_w, sgu_b, w_branch, w_out, norm_mix, norm_ffn, router_group_w, router_group_b, router_expert_w, router_expert_b, expert_w1, expert_w3, expert_w2, norm_final):
    B, T, D = x.shape
    n = B * T
    xs, moe = x.reshape(n, D), None
    w_in_t = jnp.swapaxes(w_in, 1, 2)
    for l in range(DEPTH):
        w_main, w_small_t, w_gate = _pack_w_in(w_in_t, l)
        if moe is None:
            h, = _norm(xs, norm_mix[l])
        else:
            xs, h = _norm(xs, norm_mix[l], moe=moe)
        zm = _matmul(h, w_main, out_dtype=BF16, tn=1280, name="proj_main")
        zs = _matmul(h, w_small_t, out_dtype=F32, w_rows_are_outputs=True, name="proj_small")
        zst = zs[:, :2 * N_HEADS].T
        ya = _mlstm(zm, zs, zst, mlstm_conv[l], mlstm_gate_bias[l], mlstm_norm[l], B, T)
        yb = _retention(zm, ret_norm[l], B, T)
        yc = _nsa(zm, zs, nsa_cmp_w[l], nsa_cmp_pe[l], B, T)
        yd = _sgu(zm, sgu_norm[l], sgu_w[l], sgu_b[l], n)
        merged = _merge(h, (ya, yb, yc, yd), w_gate, w_branch[l].astype(BF16))
        x_mid = _matmul(merged, w_out[l].astype(BF16), out_dtype=F32, residual=xs, tn=1024, name="proj_out")
        h2, route = _norm(x_mid, norm_ffn[l],
                          router_w=_router_weights(router_group_w[l], router_group_b[l],
                                                   router_expert_w[l], router_expert_b[l]))
        y_exp, dest = _moe(h2, route, expert_w1, expert_w3, expert_w2, l)
        xs, moe = x_mid, (y_exp, dest, route)
    out, = _norm(xs, norm_final, moe=moe, final=True)
    return out.reshape(B, T, D)
```

```python
import functools

import numpy as np
import jax
import jax.numpy as jnp
from jax import lax
from jax.experimental import pallas as pl
from jax.experimental.pallas import tpu as pltpu

F32 = jnp.float32
BF16 = jnp.bfloat16

D_MODEL = 4096
DEPTH = 2
HEAD_DIM = 128
N_BRANCH = 4
MIX_WIDTH = D_MODEL // N_BRANCH
N_HEADS = MIX_WIDTH // HEAD_DIM
CONV_WIDTH = 4
CONV_TAIL = 16
MLSTM_CHUNK = 128
RET_CHUNK = 128
NSA_KV_GROUPS = 2
NSA_GROUP_SIZE = N_HEADS // NSA_KV_GROUPS
KV_WIDTH = NSA_KV_GROUPS * HEAD_DIM
CMP_BLOCK = 32
CMP_STRIDE = 16
SEL_BLOCK = 64
SEL_TOPN = 16
WINDOW = 512
Q_BLOCK = 128
SGU_CHUNK = 128
N_GROUPS = 8
EXPERTS_PER_GROUP = 8
N_EXPERTS = N_GROUPS * EXPERTS_PER_GROUP
TOP_K_IN_GROUP = 2
EXPERT_FF = 256
MOE_BLOCK = 256
GATHER_UNROLL = 8
RMS_EPS = 1e-6
NEG_INF = -1e30
FORCE_SCORE = 1e9

COL_WIDTHS = (
    MIX_WIDTH, MIX_WIDTH, MIX_WIDTH, MIX_WIDTH, N_HEADS, N_HEADS,
    MIX_WIDTH, MIX_WIDTH, MIX_WIDTH, MIX_WIDTH,
    MIX_WIDTH, KV_WIDTH, KV_WIDTH, KV_WIDTH, KV_WIDTH, KV_WIDTH, KV_WIDTH, 3 * N_HEADS,
    MIX_WIDTH, MIX_WIDTH,
    N_BRANCH * D_MODEL,
)
COL_OFFSETS = tuple(int(v) for v in np.concatenate([[0], np.cumsum(COL_WIDTHS)]))
(I_AQ, I_AK, I_AV, I_AO, I_AI, I_AF, I_BQ, I_BK, I_BV, I_BG, I_CQ, I_CKC, I_CVC, I_CKS, I_CVS,
 I_CKW, I_CVW, I_CG, I_DU, I_DV, I_MERGE) = range(21)

MAIN_ORDER = (I_AQ, I_AK, I_AV, I_AO, I_BQ, I_BK, I_BV, I_BG, I_CQ, I_DU, I_DV,
              I_CKC, I_CVC, I_CKS, I_CVS, I_CKW, I_CVW)
MAIN_WIDTH = sum(COL_WIDTHS[i] for i in MAIN_ORDER)
LANES = 128
BLK_AQ, BLK_AK, BLK_AV, BLK_AO, BLK_BQ, BLK_BK, BLK_BV, BLK_BG, BLK_CQ, BLK_DU, BLK_DV = range(11)
KV_BASE = 11 * MIX_WIDTH // HEAD_DIM
VMEM_LIMIT = 48 * 1024 * 1024
MOE_VMEM_LIMIT = 56 * 1024 * 1024


def _cparams(sem, vmem=VMEM_LIMIT):
    return pltpu.CompilerParams(dimension_semantics=sem, vmem_limit_bytes=vmem)


def _dot(a, b):
    return jnp.dot(a, b, preferred_element_type=F32)


def _dot_nt(a, b):
    return lax.dot_general(a, b, (((1,), (1,)), ((), ())), preferred_element_type=F32)


def _split3(x):
    hi = x.astype(BF16)
    r1 = x - hi.astype(F32)
    mid = r1.astype(BF16)
    lo = (r1 - mid.astype(F32)).astype(BF16)
    return hi, mid, lo


def _dot_exact_rhs(x, m_bf16):
    hi, mid, lo = _split3(x)
    return _dot(hi, m_bf16) + _dot(mid, m_bf16) + _dot(lo, m_bf16)


def _log_sigmoid(x):
    return jnp.minimum(x, 0.0) - jnp.log1p(jnp.exp(-jnp.abs(x)))


def _row_gather(idx_ref, src_hbm, dst, sem, count):
    def issue(c, carry):
        for u in range(GATHER_UNROLL):
            r = c * GATHER_UNROLL + u
            pltpu.make_async_copy(src_hbm.at[pl.ds(idx_ref[0, r], 1)], dst.at[pl.ds(r, 1)], sem).start(priority=u % 2)
        return carry
    lax.fori_loop(0, count // GATHER_UNROLL, issue, 0)


def _row_gather_wait(src_hbm, dst, sem, count):
    for r in range(count):
        pltpu.make_async_copy(src_hbm.at[pl.ds(0, 1)], dst.at[pl.ds(r, 1)], sem).wait()


def _norm_kernel(*refs, moe, router, final):
    x = refs[0][...]
    rows = x.shape[0]
    pos = 1
    n_scratch = 0
    if moe:
        idx_ref, idx_next_ref, y_hbm, rt_ref = refs[pos:pos + 4]
        pos += 4
        gbuf, sems = refs[-2:]
        n_scratch = 2
        i, last = pl.program_id(0), pl.num_programs(0) - 1
        slot = lax.rem(i, 2)

        @pl.when(i == 0)
        def _():
            _row_gather(idx_ref, y_hbm, gbuf.at[0], sems.at[0], 2 * rows)

        @pl.when(i < last)
        def _():
            _row_gather(idx_next_ref, y_hbm, gbuf.at[1 - slot], sems.at[1 - slot], 2 * rows)

        _row_gather_wait(y_hbm, gbuf.at[slot], sems.at[slot], 2 * rows)
        x = x + (rt_ref[:, 2:3] * gbuf[slot, 0:rows, :] + rt_ref[:, 3:4] * gbuf[slot, rows:2 * rows, :])
    g_ref = refs[pos]
    pos += 1
    if router:
        whi_ref, wlo_ref, rb_ref = refs[pos:pos + 3]
        pos += 3
    outs = refs[pos:len(refs) - n_scratch]
    y = x * lax.rsqrt(jnp.mean(x * x, axis=-1, keepdims=True) + RMS_EPS) * g_ref[...]
    o = 0
    if moe and not final:
        outs[o][...] = x
        o += 1
    if final:
        outs[o][...] = y
    elif router:
        outs[o][...] = _pack_bf16_pairs(y)
    else:
        outs[o][...] = y.astype(BF16)
    o += 1
    if router:
        y_hi = y.astype(BF16)
        y_lo = (y - y_hi.astype(F32)).astype(BF16)
        lg = _dot(y_hi, whi_ref[...]) + (_dot(y_hi, wlo_ref[...]) + _dot(y_lo, whi_ref[...])) + rb_ref[...]
        outs[o][...] = _route_rows(lg)


def _first_max(v, lane):
    mx = jnp.max(v, axis=-1, keepdims=True)
    return mx, jnp.min(jnp.where(v == mx, lane, LANES), axis=-1, keepdims=True)


def _route_rows(lg):
    low = -3.0e38
    lane = lax.broadcasted_iota(jnp.int32, lg.shape, 1)
    is_grp = lane < N_GROUPS
    gmax, g_idx = _first_max(jnp.where(is_grp, lg, low), lane)
    g_w = 1.0 / jnp.sum(jnp.where(is_grp, jnp.exp(lg - gmax), 0.0), axis=-1, keepdims=True)
    assert EXPERTS_PER_GROUP & (EXPERTS_PER_GROUP - 1) == 0
    grp_of_lane = lax.shift_right_arithmetic(lane - N_GROUPS, EXPERTS_PER_GROUP.bit_length() - 1)
    in_grp = (lane >= N_GROUPS) & (lane < N_GROUPS + N_EXPERTS) & (grp_of_lane == g_idx)
    el = jnp.where(in_grp, lg, low)
    v0, i0 = _first_max(el, lane)
    v1, i1 = _first_max(jnp.where(lane == i0, low, el), lane)
    e1 = jnp.exp(v1 - v0)
    p0 = g_w / (1.0 + e1)
    vals = (i0 - N_GROUPS).astype(F32), (i1 - N_GROUPS).astype(F32), p0, p0 * e1
    out = jnp.zeros(lg.shape, F32)
    for k, v in enumerate(vals):
        out = jnp.where(lane == k, v, out)
    return out


def _norm(x, gain, *, moe=None, router_w=None, final=False, rows=128):
    n, d = x.shape
    rows = min(rows, n)
    steps = n // rows
    row_spec = pl.BlockSpec((rows, d), lambda i: (i, 0))
    lane_spec = pl.BlockSpec((rows, LANES), lambda i: (i, 0))
    in_specs, args, scratch = [row_spec], [x], []
    if moe is not None:
        y, dest, route = moe
        K = dest.shape[1]
        idx = dest.reshape(steps, rows, K).transpose(0, 2, 1).reshape(steps, 1, K * rows)
        smem_blk = lambda f: pl.BlockSpec((None, 1, K * rows), f, memory_space=pltpu.SMEM)
        in_specs += [smem_blk(lambda i: (i, 0, 0)), smem_blk(lambda i: (jnp.minimum(i + 1, steps - 1), 0, 0)),
                     pl.BlockSpec(memory_space=pl.ANY), lane_spec]
        args += [idx, idx, y, route]
        scratch = [pltpu.VMEM((2, K * rows, d), F32), pltpu.SemaphoreType.DMA((2,))]
    in_specs.append(pl.BlockSpec((1, d), lambda i: (0, 0)))
    args.append(gain.reshape(1, d))
    out_shape, out_specs = [], []
    if moe is not None and not final:
        out_shape.append(jax.ShapeDtypeStruct((n, d), F32))
        out_specs.append(row_spec)
    if router_w is not None:
        out_shape.append(jax.ShapeDtypeStruct((n, d // 2), jnp.int32))
        out_specs.append(pl.BlockSpec((rows, d // 2), lambda i: (i, 0)))
    else:
        out_shape.append(jax.ShapeDtypeStruct((n, d), F32 if final else BF16))
        out_specs.append(row_spec)
    if router_w is not None:
        whi, wlo, rb = router_w
        in_specs += [pl.BlockSpec((d, LANES), lambda i: (0, 0))] * 2 + [pl.BlockSpec((1, LANES), lambda i: (0, 0))]
        args += [whi, wlo, rb]
        out_shape.append(jax.ShapeDtypeStruct((n, LANES), F32))
        out_specs.append(lane_spec)
    return pl.pallas_call(
        functools.partial(_norm_kernel, moe=moe is not None, router=router_w is not None, final=final),
        grid=(steps,), in_specs=in_specs, out_specs=out_specs, out_shape=out_shape, scratch_shapes=scratch,
        compiler_params=_cparams(("arbitrary",)), name="rmsnorm",
    )(*args)


def _mm_kernel(a_ref, w_ref, *rest, has_res, w_rows_are_outputs):
    w = w_ref[...].astype(BF16)
    acc = _dot_nt(a_ref[...], w) if w_rows_are_outputs else _dot(a_ref[...], w)
    if has_res:
        acc = acc + rest[0][...]
    rest[-1][...] = acc.astype(rest[-1].dtype)


def _matmul(a, w, *, out_dtype, residual=None, tm=1024, tn=512, w_rows_are_outputs=False, name="matmul"):
    m, k = a.shape
    nc = w.shape[0] if w_rows_are_outputs else w.shape[1]
    tm, tn = min(tm, m), min(tn, nc)
    w_spec = pl.BlockSpec((tn, k), lambda i, j: (j, 0)) if w_rows_are_outputs else pl.BlockSpec((k, tn), lambda i, j: (0, j))
    in_specs = [pl.BlockSpec((tm, k), lambda i, j: (i, 0), pipeline_mode=pl.Buffered(1)), w_spec]
    args = [a, w]
    if residual is not None:
        in_specs.append(pl.BlockSpec((tm, tn), lambda i, j: (i, j)))
        args.append(residual)
    return pl.pallas_call(
        functools.partial(_mm_kernel, has_res=residual is not None, w_rows_are_outputs=w_rows_are_outputs),
        grid=(m // tm, nc // tn), in_specs=in_specs,
        out_specs=pl.BlockSpec((tm, tn), lambda i, j: (i, j)),
        out_shape=jax.ShapeDtypeStruct((m, nc), out_dtype),
        compiler_params=_cparams(("parallel", "parallel")), name=name,
    )(*args)


def _mlstm_kernel(aq_ref, ak_ref, av_ref, ao_ref, zs_ref, zst_ref, cw_ref, gbc_ref, gbr_ref, gain_ref,
                  tril_ref, triu_ref, shift_ref, o_ref, tail_s, qk_s, ct_s, n_s, m_s):
    L, d, H, W = MLSTM_CHUNK, HEAD_DIM, N_HEADS, MIX_WIDTH
    c = pl.program_id(1)

    @pl.when(c == 0)
    def _():
        tail_s[...] = jnp.zeros_like(tail_s)
        ct_s[...] = jnp.zeros_like(ct_s)
        n_s[...] = jnp.zeros_like(n_s)
        m_s[...] = jnp.zeros_like(m_s)

    x_cur = jnp.concatenate([aq_ref[...], ak_ref[...]], axis=1)
    x_ext = jnp.concatenate([tail_s[...], x_cur], axis=0)
    conv = cw_ref[CONV_WIDTH - 1:CONV_WIDTH, :] * x_cur.astype(F32)
    for j in range(CONV_WIDTH - 1):
        conv = conv + cw_ref[j:j + 1, :] * _dot(shift_ref[j], x_ext)
    tail_s[...] = x_cur[L - CONV_TAIL:, :]
    qk_s[...] = conv * jax.nn.sigmoid(conv)

    pre_c = zs_ref[...] + gbc_ref[...]
    ls_c = _log_sigmoid(pre_c)
    hi, mid, lo = _split3(ls_c)
    tril = tril_ref[...]
    bc = _dot(tril, hi) + _dot(tril, mid) + _dot(tril, lo)
    pre_r = zst_ref[...] + gbr_ref[...]
    br = _dot_exact_rhs(_log_sigmoid(pre_r), triu_ref[...])

    row = lax.broadcasted_iota(jnp.int32, (L, L), 0)
    col = lax.broadcasted_iota(jnp.int32, (L, L), 1)
    causal = row >= col
    scale = d ** -0.5
    for h in range(H):
        sl = slice(h * d, (h + 1) * d)
        q = qk_s[:, sl] * scale
        k = qk_s[:, W + h * d:W + (h + 1) * d]
        v = av_ref[:, sl]
        i_col, b_col = pre_c[:, h:h + 1], bc[:, H + h:H + h + 1]
        i_row, b_row = pre_r[h:h + 1, :], br[H + h:H + h + 1, :]
        m_prev = m_s[h:h + 1, 0:1]
        log_d = jnp.where(causal, b_col - b_row + i_row, NEG_INF)
        log_inter = b_col + m_prev
        m_row = jnp.maximum(jnp.max(log_d, axis=-1, keepdims=True), log_inter)
        qb, kb = q.astype(BF16), k.astype(BF16)
        s = _dot_nt(qb, kb) * jnp.exp(log_d - m_row)
        w_inter = jnp.exp(log_inter - m_row)
        num = _dot(s.astype(BF16), v) + w_inter * _dot(qb, ct_s[h].astype(BF16))
        den = jnp.sum(s, axis=-1, keepdims=True) + w_inter * jnp.sum(q * n_s[h:h + 1, :], axis=-1, keepdims=True)
        hh = num / jnp.maximum(jnp.abs(den), jnp.exp(-m_row))
        b_last = b_row[:, L - 1:L]
        log_w_row = b_last - b_row + i_row
        m_new = jnp.maximum(b_last + m_prev, jnp.max(log_w_row, axis=-1, keepdims=True))
        w_col = jnp.exp(b_last - b_col + i_col - m_new)
        decay = jnp.exp(b_last + m_prev - m_new)
        vw = (v.astype(F32) * w_col).astype(BF16)
        ct_s[h] = decay * ct_s[h] + _dot(k.T.astype(BF16), vw)
        n_s[h:h + 1, :] = decay * n_s[h:h + 1, :] + jnp.sum(k * w_col, axis=0, keepdims=True)
        m_s[h:h + 1, :] = jnp.broadcast_to(m_new, (1, LANES))
        hn = hh * lax.rsqrt(jnp.mean(hh * hh, axis=-1, keepdims=True) + RMS_EPS)
        o_ref[:, sl] = (hn * gain_ref[:, sl] * jax.nn.sigmoid(ao_ref[:, sl].astype(F32))).astype(BF16)


def _tri_consts(L):
    r = np.arange(L)
    tril = (r[:, None] >= r[None, :]).astype(np.float32)
    return jnp.asarray(tril, BF16), jnp.asarray(tril.T, BF16)


def _mlstm(zm, zs, zst, conv_w, gate_bias, gain, B, T):
    L, W = MLSTM_CHUNK, MIX_WIDTH
    nc = T // L
    n = B * T
    gbc = jnp.zeros((1, LANES), F32).at[0, :2 * N_HEADS].set(gate_bias.reshape(-1))
    gbr = jnp.broadcast_to(gate_bias.reshape(2 * N_HEADS, 1), (2 * N_HEADS, L)).astype(F32)
    tril, triu = _tri_consts(L)
    t_idx = np.arange(L)[:, None]
    c_idx = np.arange(CONV_TAIL + L)[None, :]
    shift = jnp.asarray(np.stack([(c_idx == CONV_TAIL + t_idx - (CONV_WIDTH - 1 - j)) for j in range(CONV_WIDTH - 1)])
                        .astype(np.float32), BF16)

    def blk(b_idx):
        return pl.BlockSpec((L, W), lambda b, c: (b * nc + c, b_idx))

    const = lambda shape: pl.BlockSpec(shape, lambda b, c: (0,) * len(shape))
    return pl.pallas_call(
        _mlstm_kernel, grid=(B, nc),
        in_specs=[blk(BLK_AQ), blk(BLK_AK), blk(BLK_AV), blk(BLK_AO),
                  pl.BlockSpec((L, LANES), lambda b, c: (b * nc + c, 0)),
                  pl.BlockSpec((2 * N_HEADS, L), lambda b, c: (0, b * nc + c)),
                  const((CONV_WIDTH, 2 * W)), const((1, LANES)), const((2 * N_HEADS, L)), const((1, W)),
                  const((L, L)), const((L, L)), const((CONV_WIDTH - 1, L, CONV_TAIL + L))],
        out_specs=pl.BlockSpec((L, W), lambda b, c: (b * nc + c, 0)),
        out_shape=jax.ShapeDtypeStruct((n, W), BF16),
        scratch_shapes=[pltpu.VMEM((CONV_TAIL, 2 * W), BF16), pltpu.VMEM((L, 2 * W), F32),
                        pltpu.VMEM((N_HEADS, HEAD_DIM, HEAD_DIM), F32), pltpu.VMEM((N_HEADS, HEAD_DIM), F32),
                        pltpu.VMEM((N_HEADS, LANES), F32)],
        compiler_params=_cparams(("parallel", "arbitrary")), name="mlstm",
    )(zm, zm, zm, zm, zs, zst, conv_w, gbc, gbr, gain.reshape(1, W), tril, triu, shift)


def _ret_kernel(q_ref, k_ref, v_ref, g_ref, dec_ref, qd_ref, kd_ref, gain_ref, o_ref, r_s, *, chunk_decay):
    L, d, H = RET_CHUNK, HEAD_DIM, N_HEADS
    c = pl.program_id(1)

    @pl.when(c == 0)
    def _():
        r_s[...] = jnp.zeros_like(r_s)

    scale = d ** -0.5
    for h in range(H):
        sl = slice(h * d, (h + 1) * d)
        q = q_ref[:, sl].astype(F32)
        k = k_ref[:, sl].astype(F32) * scale
        v = v_ref[:, sl]
        s = _dot_nt(q.astype(BF16), k.astype(BF16)) * dec_ref[h]
        o = _dot(s.astype(BF16), v) + _dot((q * qd_ref[:, sl]).astype(BF16), r_s[h].astype(BF16))
        kd = (k * kd_ref[:, sl]).T.astype(BF16)
        r_s[h] = chunk_decay[h] * r_s[h] + _dot(kd, v)
        on = o * lax.rsqrt(jnp.mean(o * o, axis=-1, keepdims=True) + RMS_EPS)
        gate = g_ref[:, sl].astype(F32)
        o_ref[:, sl] = (on * gain_ref[:, sl] * (gate * jax.nn.sigmoid(gate))).astype(BF16)


def _retention(zm, gain, B, T):
    L, W, H, d = RET_CHUNK, MIX_WIDTH, N_HEADS, HEAD_DIM
    nc = T // L
    log_gamma = np.log(np.float32(1.0) - np.float32(2.0) ** (-5.0 - np.arange(H, dtype=np.float32))).astype(np.float32)
    pos = np.arange(L, dtype=np.float32)
    diff = pos[:, None] - pos[None, :]
    decay = np.where(diff >= 0, np.exp(log_gamma[:, None, None] * np.maximum(diff, 0.0)), 0.0).astype(np.float32)
    q_decay = np.exp(log_gamma[:, None] * (pos + 1.0)).astype(np.float32)
    k_decay = np.exp(log_gamma[:, None] * (L - 1.0 - pos)).astype(np.float32)
    chunk_decay = tuple(float(v) for v in np.exp(log_gamma * np.float32(L)).astype(np.float32))
    qd = jnp.asarray(np.repeat(q_decay.T, d, axis=1))
    kd = jnp.asarray(np.repeat(k_decay.T, d, axis=1))

    def blk(b_idx):
        return pl.BlockSpec((L, W), lambda b, c: (b * nc + c, b_idx))

    const = lambda shape: pl.BlockSpec(shape, lambda b, c: (0,) * len(shape))
    return pl.pallas_call(
        functools.partial(_ret_kernel, chunk_decay=chunk_decay), grid=(B, nc),
        in_specs=[blk(BLK_BQ), blk(BLK_BK), blk(BLK_BV), blk(BLK_BG),
                  const((H, L, L)), const((L, W)), const((L, W)), const((1, W))],
        out_specs=pl.BlockSpec((L, W), lambda b, c: (b * nc + c, 0)),
        out_shape=jax.ShapeDtypeStruct((B * T, W), BF16),
        scratch_shapes=[pltpu.VMEM((H, d, d), F32)],
        compiler_params=_cparams(("parallel", "arbitrary")), name="retention",
    )(zm, zm, zm, zm, jnp.asarray(decay), qd, kd, gain.reshape(1, W))


def _gelu(x):
    return 0.5 * x * (1.0 + jnp.tanh(0.7978845608028654 * (x + 0.044715 * (x * x * x))))


def _sgu_kernel(u_ref, v_ref, ng_ref, w_ref, b_ref, o_ref):
    L, d, H = SGU_CHUNK, HEAD_DIM, N_HEADS
    v = _gelu(v_ref[...].astype(F32))
    vn = (v * lax.rsqrt(jnp.mean(v * v, axis=-1, keepdims=True) + RMS_EPS) * ng_ref[...]).astype(BF16)
    u = _gelu(u_ref[...].astype(F32))
    row = lax.broadcasted_iota(jnp.int32, (L, L), 0)
    col = lax.broadcasted_iota(jnp.int32, (L, L), 1)
    for g in range(H):
        sl = slice(g * d, (g + 1) * d)
        wm = jnp.where(row >= col, w_ref[g], 0.0).astype(BF16)
        mixed = _dot(wm, vn[:, sl]) + b_ref[:, sl]
        o_ref[:, sl] = (u[:, sl] * mixed).astype(BF16)


def _sgu(zm, norm_g, w_s, b_s, n):
    L, W, H, d = SGU_CHUNK, MIX_WIDTH, N_HEADS, HEAD_DIM
    bsb = jnp.repeat(b_s.T, d, axis=1)
    const = lambda shape: pl.BlockSpec(shape, lambda i: (0,) * len(shape))
    return pl.pallas_call(
        _sgu_kernel, grid=(n // L,),
        in_specs=[pl.BlockSpec((L, W), lambda i: (i, BLK_DU)), pl.BlockSpec((L, W), lambda i: (i, BLK_DV)),
                  const((1, W)), const((H, L, L)), const((L, W))],
        out_specs=pl.BlockSpec((L, W), lambda i: (i, 0)),
        out_shape=jax.ShapeDtypeStruct((n, W), BF16),
        compiler_params=_cparams(("parallel",)), name="sgu",
    )(zm, zm, norm_g.reshape(1, W), w_s, bsb)


def _compress_kernel(k_ref, v_ref, w_ref, pe_ref, ko_ref, vo_ref, buf):
    t = k_ref.shape[0]
    n16 = t // CMP_STRIDE
    buf[t:t + CMP_STRIDE, :] = jnp.zeros((CMP_STRIDE, HEAD_DIM), F32)
    for which, (src, dst) in enumerate(((k_ref, ko_ref), (v_ref, vo_ref))):
        buf[0:t, :] = src[...].astype(F32)
        acc = jnp.zeros((n16, HEAD_DIM), F32)
        for l in range(CMP_BLOCK):
            x = buf[pl.ds(l, n16, stride=CMP_STRIDE), :]
            acc = acc + _dot((x + pe_ref[which, l:l + 1, :]).astype(BF16), w_ref[which, l].astype(BF16))
        dst[...] = (acc.T if which else acc).astype(BF16)


def _compress(zm, cmp_w, cmp_pe, B, T):
    G, d = NSA_KV_GROUPS, HEAD_DIM
    n16 = T // CMP_STRIDE
    out = jax.ShapeDtypeStruct((B, G, n16, d), BF16)
    kblk = KV_BASE
    vblk = KV_BASE + G
    return pl.pallas_call(
        _compress_kernel, grid=(B, G),
        in_specs=[pl.BlockSpec((T, d), lambda b, g: (b, kblk + g)), pl.BlockSpec((T, d), lambda b, g: (b, vblk + g)),
                  pl.BlockSpec((2, CMP_BLOCK, d, d), lambda b, g: (0, 0, 0, 0)),
                  pl.BlockSpec((2, CMP_BLOCK, d), lambda b, g: (0, 0, 0))],
        out_specs=[pl.BlockSpec((None, None, n16, d), lambda b, g: (b, g, 0, 0)),
                   pl.BlockSpec((None, None, d, n16), lambda b, g: (b, g, 0, 0))],
        out_shape=[out, jax.ShapeDtypeStruct((B, G, d, n16), BF16)],
        scratch_shapes=[pltpu.VMEM((T + CMP_STRIDE, d), F32)],
        compiler_params=_cparams(("parallel", "parallel")), name="nsa_compress",
    )(zm, zm, cmp_w, cmp_pe)


RANK_CHUNK = 8


def _lanes_x(x, times):
    return jnp.concatenate([x] * times, axis=1)


def _nsa_a_kernel(q_ref, kc_ref, vct_ref, *rest, n_back):
    nw = n_back + 1
    kw_refs, vwt_refs = rest[:nw], rest[nw:2 * nw]
    cgt_ref, ovt_ref, acmp_ref, awin_ref, part_ref, sel_ref, q_s, key_s, rank_s = rest[2 * nw:]
    tq, d, J = Q_BLOCK, HEAD_DIM, NSA_GROUP_SIZE
    qi = pl.program_id(2)
    t0 = qi * tq
    ncmp = kc_ref.shape[0]
    scale = d ** -0.5
    for j in range(J):
        q_s[j * tq:(j + 1) * tq, :] = (q_ref[:, j * d:(j + 1) * d].astype(F32) * scale).astype(BF16)
    sg_t = jax.nn.sigmoid(cgt_ref[...])
    t_row = t0 + lax.broadcasted_iota(jnp.int32, (1, tq), 1)

    n_i = lax.broadcasted_iota(jnp.int32, (ncmp, tq), 0)
    r_i = lax.broadcasted_iota(jnp.int32, (ncmp, tq), 1)
    bias_c = jnp.where(t0 + r_i - CMP_STRIDE * n_i - (CMP_BLOCK - 1) >= 0, 0.0, NEG_INF)
    s = _dot_nt(kc_ref[...], q_s[...]) - acmp_ref[...] + _lanes_x(bias_c, J)
    p = jnp.exp(s - jnp.max(s, axis=0, keepdims=True))
    has_c = jnp.where(t_row >= CMP_BLOCK - 1, 1.0, 0.0)
    p = p * (_lanes_x(has_c, J) / jnp.sum(p, axis=0, keepdims=True))
    o_cmp_t = _dot(vct_ref[...], p.astype(BF16))
    psum = p[:, 0:tq]
    for j in range(1, J):
        psum = psum + p[:, j * tq:(j + 1) * tq]

    hi, mid, lo = _split3(psum)
    ovt = ovt_ref[...]
    imp_t = _dot(ovt, hi) + _dot(ovt, mid) + _dot(ovt, lo)
    blk_t = lax.broadcasted_iota(jnp.int32, (LANES, tq), 0)
    cur_t = lax.shift_right_logical(t0 + lax.broadcasted_iota(jnp.int32, (LANES, tq), 1), SEL_BLOCK.bit_length() - 1)
    valid_t = blk_t <= cur_t
    forced = (blk_t == 0) | (blk_t == cur_t) | (blk_t == cur_t - 1)
    key_s[...] = jnp.where(valid_t, imp_t + jnp.where(forced, FORCE_SCORE, 0.0), NEG_INF)
    rank_s[...] = jnp.zeros_like(rank_s)
    sub = lax.broadcasted_iota(jnp.int32, (8, tq), 0)
    groups = LANES // 8
    last_valid = lax.shift_right_logical(t0 + tq - 1, SEL_BLOCK.bit_length() - 1)
    for chunk in range(LANES // RANK_CHUNK):
        @pl.when(chunk * RANK_CHUNK <= last_valid)
        def _(chunk=chunk):
            key_g = [key_s[8 * i:8 * i + 8, :] for i in range(groups)]
            rank = [rank_s[8 * i:8 * i + 8, :] for i in range(groups)]
            for mp in range(chunk * RANK_CHUNK, (chunk + 1) * RANK_CHUNK):
                rowv = jnp.broadcast_to(key_s[mp:mp + 1, :], (8, tq))
                for i in range(groups):
                    if 8 * i > mp:
                        ahead = rowv >= key_g[i]
                    elif 8 * i + 7 <= mp:
                        ahead = rowv > key_g[i]
                    else:
                        ahead = (rowv > key_g[i]) | ((sub > mp - 8 * i) & (rowv == key_g[i]))
                    rank[i] = rank[i] + jnp.where(ahead, 1.0, 0.0)
            for i in range(groups):
                rank_s[8 * i:8 * i + 8, :] = rank[i]
    sel_ref[...] = jnp.where((rank_s[...] < float(SEL_TOPN)) & valid_t, 1.0, 0.0).astype(BF16)

    span = nw * tq
    kw = jnp.concatenate([kr[...] for kr in kw_refs], axis=0)
    vw_t = jnp.concatenate([vr[...] for vr in vwt_refs], axis=1)
    c_i = lax.broadcasted_iota(jnp.int32, (span, tq), 0)
    bias_w = jnp.where(t0 - n_back * tq + c_i >= 0, 0.0, NEG_INF)
    s = _dot_nt(kw, q_s[...]) - awin_ref[...] + _lanes_x(bias_w, J)
    p = jnp.exp(s - jnp.max(s, axis=0, keepdims=True))
    inv_l = 1.0 / jnp.sum(p, axis=0, keepdims=True)
    o_win_t = _dot(vw_t, p.astype(BF16))
    g_cmp = jnp.concatenate([sg_t[j:j + 1, :] for j in range(J)], axis=1)
    g_win = jnp.concatenate([sg_t[2 * J + j:2 * J + j + 1, :] for j in range(J)], axis=1)
    part_ref[...] = g_cmp * o_cmp_t + (g_win * inv_l) * o_win_t


def _alibi_slopes():
    return (2.0 ** (-8.0 * (np.arange(N_HEADS, dtype=np.float32) + 1.0) / N_HEADS)).astype(np.float32)


def _nsa_consts(T):
    G, J, tq = NSA_KV_GROUPS, NSA_GROUP_SIZE, Q_BLOCK
    n16 = T // CMP_STRIDE
    n_sel = T // SEL_BLOCK
    n = np.arange(n16)[None, :]
    m = np.arange(LANES)[:, None]
    c_start, s_start = n * CMP_STRIDE, m * SEL_BLOCK
    overlap_t = ((c_start < s_start + SEL_BLOCK) & (c_start + CMP_BLOCK > s_start) & (n < n16 - 1) & (m < n_sel))
    slopes = _alibi_slopes().reshape(G, 1, J, 1)
    r = np.arange(tq, dtype=np.float32)[None, None, None, :]
    dist_c = r - CMP_STRIDE * np.arange(n16, dtype=np.float32)[None, :, None, None] - (CMP_BLOCK - 1)
    a_cmp = (slopes * dist_c).reshape(G, n16, J * tq)
    span = WINDOW + tq
    dist_w = r + WINDOW - np.arange(span, dtype=np.float32)[None, :, None, None]
    a_win = np.where((dist_w >= 0) & (dist_w < WINDOW), slopes * dist_w, -NEG_INF).reshape(G, span, J * tq)
    return (jnp.asarray(overlap_t.astype(np.float32), BF16), jnp.asarray(a_cmp.astype(np.float32)),
            jnp.asarray(a_win.astype(np.float32)))


def _nsa_a(zm, k_cmp, v_cmp_t, vw_t, cg_t, B, T):
    G, J, d, tq = NSA_KV_GROUPS, NSA_GROUP_SIZE, HEAD_DIM, Q_BLOCK
    nq = T // tq
    n16 = T // CMP_STRIDE
    n_back = WINDOW // tq
    span = WINDOW + tq
    qblk = BLK_CQ * (MIX_WIDTH // (J * d))
    kwblk = KV_BASE + 4 * G
    overlap_t, a_cmp, a_win = _nsa_consts(T)

    def past(qi, i):
        return jnp.maximum(qi - n_back + i, 0)

    in_specs = ([pl.BlockSpec((tq, J * d), lambda b, g, qi: (b * nq + qi, qblk + g)),
                 pl.BlockSpec((None, None, n16, d), lambda b, g, qi: (b, g, 0, 0)),
                 pl.BlockSpec((None, None, d, n16), lambda b, g, qi: (b, g, 0, 0))]
                + [pl.BlockSpec((tq, d), functools.partial(lambda b, g, qi, i: (b * nq + past(qi, i), kwblk + g), i=i))
                   for i in range(n_back + 1)]
                + [pl.BlockSpec((None, None, d, tq), functools.partial(lambda b, g, qi, i: (b, g, 0, past(qi, i)), i=i))
                   for i in range(n_back + 1)]
                + [pl.BlockSpec((None, 16, tq), lambda b, g, qi: (g, 0, b * nq + qi)),
                   pl.BlockSpec((LANES, n16), lambda b, g, qi: (0, 0)),
                   pl.BlockSpec((None, n16, J * tq), lambda b, g, qi: (g, 0, 0)),
                   pl.BlockSpec((None, span, J * tq), lambda b, g, qi: (g, 0, 0))])
    return pl.pallas_call(
        functools.partial(_nsa_a_kernel, n_back=n_back), grid=(B, G, nq), in_specs=in_specs,
        out_specs=[pl.BlockSpec((None, None, None, d, J * tq), lambda b, g, qi: (b, g, qi, 0, 0)),
                   pl.BlockSpec((None, None, LANES, tq), lambda b, g, qi: (b, g, 0, qi))],
        out_shape=[jax.ShapeDtypeStruct((B, G, nq, d, J * tq), F32), jax.ShapeDtypeStruct((B, G, LANES, T), BF16)],
        scratch_shapes=[pltpu.VMEM((J * tq, d), BF16), pltpu.VMEM((LANES, tq), F32), pltpu.VMEM((LANES, tq), F32)],
        compiler_params=_cparams(("parallel", "parallel", "parallel")), name="nsa_cmp_win",
    )(zm, k_cmp, v_cmp_t, *([zm] * (n_back + 1)), *([vw_t] * (n_back + 1)), cg_t, overlap_t, a_cmp, a_win)


def _nsa_b_kernel(cnt_ref, lst_ref, q_ref, ks_ref, vt_ref, sel_ref, e_ref, a_ref, srow_ref, cgt_ref, part_ref, o_ref,
                  q_s, acc_s, *, tk, nk):
    tq, d, J = Q_BLOCK, HEAD_DIM, NSA_GROUP_SIZE
    b, g, qi = pl.program_id(0), pl.program_id(1), pl.program_id(2)
    lin = (b * pl.num_programs(1) + g) * pl.num_programs(2) + qi
    t0 = qi * tq
    scale = d ** -0.5
    for j in range(J):
        q_s[j * tq:(j + 1) * tq, :] = (q_ref[:, j * d:(j + 1) * d].astype(F32) * scale).astype(BF16)
    acc_s[...] = jnp.zeros_like(acc_s)
    r_minus_c = lax.broadcasted_iota(jnp.int32, (tk, tq), 1) - lax.broadcasted_iota(jnp.int32, (tk, tq), 0)

    count = cnt_ref[lin]

    def scores(pos):
        kt = lst_ref[lin * nk + pos]
        k_t = ks_ref[pl.ds(pl.multiple_of(kt * tk, tk), tk), :]
        off = t0 - kt * tk
        picked = _dot(e_ref[kt], sel_ref[...])
        causal_off = jnp.where(pos < count, off, -tk * nk)
        keep = (picked > 0.5) & (r_minus_c + causal_off >= 0)
        s = _dot_nt(k_t, q_s[...]) - a_ref[...] + _lanes_x(jnp.where(keep, 0.0, NEG_INF), J)
        return s, srow_ref[...] * off.astype(F32), kt

    def body(i, carry):
        m_prev, l_prev = carry
        s_a, shift_a, kt_a = scores(2 * i)
        s_b, shift_b, kt_b = scores(2 * i + 1)
        m_new = jnp.maximum(m_prev, jnp.maximum(jnp.max(s_a, axis=0, keepdims=True) - shift_a,
                                                jnp.max(s_b, axis=0, keepdims=True) - shift_b))
        p_a = jnp.exp(s_a - (m_new + shift_a))
        p_b = jnp.exp(s_b - (m_new + shift_b))
        alpha = jnp.exp(m_prev - m_new)
        acc_s[...] = alpha * acc_s[...] + (_dot(vt_ref[kt_a], p_a.astype(BF16)) + _dot(vt_ref[kt_b], p_b.astype(BF16)))
        return m_new, alpha * l_prev + (jnp.sum(p_a, axis=0, keepdims=True) + jnp.sum(p_b, axis=0, keepdims=True))

    init = (jnp.full((1, J * tq), NEG_INF, F32), jnp.zeros((1, J * tq), F32))
    _, l = lax.fori_loop(0, (count + 1) // 2, body, init)
    sg_t = jax.nn.sigmoid(cgt_ref[...])
    g_sel = jnp.concatenate([sg_t[J + j:J + j + 1, :] for j in range(J)], axis=1)
    total_t = part_ref[...] + (g_sel / l) * acc_s[...]
    for j in range(J):
        o_ref[:, j * d:(j + 1) * d] = total_t[:, j * tq:(j + 1) * tq].T.astype(BF16)


def _nsa_b(zm, sel_t, cg_t, part_t, B, T, tk=512):
    G, J, d, tq = NSA_KV_GROUPS, NSA_GROUP_SIZE, HEAD_DIM, Q_BLOCK
    tk = min(tk, T)
    nq, nk = T // tq, T // tk
    qblk = BLK_CQ * (MIX_WIDTH // (J * d))
    ksblk = KV_BASE + 2 * G
    vs_col = (KV_BASE + 3 * G) * d
    per_tile = tk // SEL_BLOCK
    kt = np.arange(nk)[:, None, None]
    cc = np.arange(tk)[None, :, None]
    m = np.arange(LANES)[None, None, :]
    expand_t = jnp.asarray((m == kt * per_tile + cc // SEL_BLOCK).astype(np.float32), BF16)
    slopes_np = _alibi_slopes()
    r_minus_c = (np.arange(tq)[None, :] - np.arange(tk)[:, None]).astype(np.float32)
    alibi_t = jnp.asarray((slopes_np.reshape(G, 1, J, 1) * r_minus_c[None, :, None, :]).reshape(G, tk, J * tq))
    srow = jnp.asarray(np.repeat(slopes_np.reshape(G, J), tq, axis=1).reshape(G, 1, J * tq))
    vs_t = zm[:, vs_col:vs_col + G * d].reshape(B, nk, tk, G, d).transpose(0, 3, 1, 4, 2)

    active = sel_t[:, :, :nk * per_tile].reshape(B, G, nk, per_tile, nq, tq).max(axis=(3, 5)) > 0
    active = active.transpose(0, 1, 3, 2)
    tiles = jnp.arange(nk, dtype=jnp.int32)
    order = jnp.sort(jnp.where(active, tiles, tiles + nk), axis=-1) % nk
    counts = active.sum(axis=-1).astype(jnp.int32)

    grid_spec = pltpu.PrefetchScalarGridSpec(
        num_scalar_prefetch=2, grid=(B, G, nq),
        in_specs=[pl.BlockSpec((tq, J * d), lambda b, g, qi, c, o: (b * nq + qi, qblk + g)),
                  pl.BlockSpec((T, d), lambda b, g, qi, c, o: (b, ksblk + g)),
                  pl.BlockSpec((None, None, nk, d, tk), lambda b, g, qi, c, o: (b, g, 0, 0, 0)),
                  pl.BlockSpec((None, None, LANES, tq), lambda b, g, qi, c, o: (b, g, 0, qi)),
                  pl.BlockSpec((nk, tk, LANES), lambda b, g, qi, c, o: (0, 0, 0)),
                  pl.BlockSpec((None, tk, J * tq), lambda b, g, qi, c, o: (g, 0, 0)),
                  pl.BlockSpec((None, 1, J * tq), lambda b, g, qi, c, o: (g, 0, 0)),
                  pl.BlockSpec((None, 16, tq), lambda b, g, qi, c, o: (g, 0, b * nq + qi)),
                  pl.BlockSpec((None, None, None, d, J * tq), lambda b, g, qi, c, o: (b, g, qi, 0, 0))],
        out_specs=pl.BlockSpec((tq, J * d), lambda b, g, qi, c, o: (b * nq + qi, g)),
        scratch_shapes=[pltpu.VMEM((J * tq, d), BF16), pltpu.VMEM((d, J * tq), F32)])
    return pl.pallas_call(
        functools.partial(_nsa_b_kernel, tk=tk, nk=nk), grid_spec=grid_spec,
        out_shape=jax.ShapeDtypeStruct((B * T, MIX_WIDTH), BF16),
        compiler_params=_cparams(("parallel", "parallel", "parallel")), name="nsa_selected",
    )(counts.reshape(-1), order.reshape(-1).astype(jnp.int32), zm, zm, vs_t, sel_t, expand_t, alibi_t, srow, cg_t, part_t)


def _nsa(zm, zs, cmp_w, cmp_pe, B, T):
    G, J = NSA_KV_GROUPS, NSA_GROUP_SIZE
    n = B * T
    d = HEAD_DIM
    cg = zs[:, 2 * N_HEADS:2 * N_HEADS + 3 * N_HEADS].reshape(n, 3, G, J)
    cg_t = jnp.pad(cg.transpose(2, 1, 3, 0).reshape(G, 3 * J, n), ((0, 0), (0, 16 - 3 * J), (0, 0)))
    vw_col = (KV_BASE + 5 * G) * d
    vw_t = zm[:, vw_col:vw_col + G * d].reshape(B, T, G, d).transpose(0, 2, 3, 1)
    k_cmp, v_cmp_t = _compress(zm, cmp_w, cmp_pe, B, T)
    part_t, sel_t = _nsa_a(zm, k_cmp, v_cmp_t, vw_t, cg_t, B, T)
    return _nsa_b(zm, sel_t, cg_t, part_t, B, T)


def _merge_kernel(h_ref, y0, y1, y2, y3, g0, g1, g2, g3, b0, b1, b2, b3, o_ref):
    h = h_ref[...]
    acc = None
    for y, gw, bw in ((y0, g0, b0), (y1, g1, b1), (y2, g2, b2), (y3, g3, b3)):
        term = jax.nn.sigmoid(_dot(h, gw[...])) * _dot(y[...], bw[...])
        acc = term if acc is None else acc + term
    o_ref[...] = acc.astype(BF16)


def _merge(h, ys, w_gate, w_branch, tm=1024, tn=256):
    n, dm = h.shape
    tm = min(tm, n)
    nj = dm // tn
    once = pl.Buffered(1)
    y_spec = pl.BlockSpec((tm, MIX_WIDTH), lambda i, j: (i, 0), pipeline_mode=once)
    gate_specs = [pl.BlockSpec((dm, tn), functools.partial(lambda i, j, b: (0, b * nj + j), b=b)) for b in range(N_BRANCH)]
    br_specs = [pl.BlockSpec((None, MIX_WIDTH, tn), functools.partial(lambda i, j, b: (b, 0, j), b=b)) for b in range(N_BRANCH)]
    return pl.pallas_call(
        _merge_kernel, grid=(n // tm, nj),
        in_specs=([pl.BlockSpec((tm, dm), lambda i, j: (i, 0), pipeline_mode=once)] + [y_spec] * N_BRANCH
                  + gate_specs + br_specs),
        out_specs=pl.BlockSpec((tm, tn), lambda i, j: (i, j)),
        out_shape=jax.ShapeDtypeStruct((n, dm), BF16),
        compiler_params=_cparams(("parallel", "parallel")), name="gated_merge",
    )(h, *ys, *([w_gate] * N_BRANCH), *([w_branch] * N_BRANCH))


def _pack_bf16_pairs(y):
    half = y.shape[1] // 2
    bits = lax.bitcast_convert_type(y.astype(BF16).astype(F32), jnp.int32)
    return bits[:, :half] | lax.shift_right_logical(bits[:, half:], 16)


def _unpack_bf16_pairs(w):
    first = lax.bitcast_convert_type(w & jnp.int32(-65536), F32).astype(BF16)
    second = lax.bitcast_convert_type(lax.shift_left(w, 16), F32).astype(BF16)
    return first, second


def _expert_kernel(be_ref, nact_ref, idx_ref, idx_next_ref, hp_hbm, w1_ref, w3_ref, w2_ref, o_ref,
                   xbuf, sems, w1b, w3b, w2b):
    i = pl.program_id(0)
    n_act = nact_ref[0]
    slot = lax.rem(i, 2)
    e = be_ref[i]
    prev = be_ref[jnp.maximum(i - 1, 0)]
    half = hp_hbm.shape[1]

    @pl.when((i == 0) & (n_act > 0))
    def _():
        _row_gather(idx_ref, hp_hbm, xbuf.at[0], sems.at[0], MOE_BLOCK)

    @pl.when(i + 1 < n_act)
    def _():
        _row_gather(idx_next_ref, hp_hbm, xbuf.at[1 - slot], sems.at[1 - slot], MOE_BLOCK)

    @pl.when((i == 0) | (prev != e))
    def _():
        w1b[...] = w1_ref[...].astype(BF16)
        w3b[...] = w3_ref[...].astype(BF16)
        w2b[...] = w2_ref[...].astype(BF16)

    @pl.when(i < n_act)
    def _():
        _row_gather_wait(hp_hbm, xbuf.at[slot], sems.at[slot], MOE_BLOCK)
        x_a, x_b = _unpack_bf16_pairs(xbuf[slot])
        a1 = _dot(x_a, w1b[0:half, :]) + _dot(x_b, w1b[half:, :])
        a3 = _dot(x_a, w3b[0:half, :]) + _dot(x_b, w3b[half:, :])
        a = (a1 * jax.nn.sigmoid(a1)) * a3
        o_ref[...] = _dot(a.astype(BF16), w2b[...])

    @pl.when(i >= n_act)
    def _():
        o_ref[...] = jnp.zeros_like(o_ref)


def _experts(hp, tok_pad, blk_exp, n_active, w1, w3, w2, layer):
    half = hp.shape[1]
    dm = 2 * half
    n_blk = tok_pad.shape[0] // MOE_BLOCK
    ff = w1.shape[-1]
    idx = tok_pad.reshape(n_blk, 1, MOE_BLOCK)
    smem_blk = lambda f: pl.BlockSpec((None, 1, MOE_BLOCK), f, memory_space=pltpu.SMEM)
    grid_spec = pltpu.PrefetchScalarGridSpec(
        num_scalar_prefetch=2, grid=(n_blk,),
        in_specs=[smem_blk(lambda i, be, na: (i, 0, 0)),
                  smem_blk(lambda i, be, na: (jnp.minimum(i + 1, n_blk - 1), 0, 0)),
                  pl.BlockSpec(memory_space=pl.ANY),
                  pl.BlockSpec((None, None, dm, ff), lambda i, be, na: (layer, be[i], 0, 0)),
                  pl.BlockSpec((None, None, dm, ff), lambda i, be, na: (layer, be[i], 0, 0)),
                  pl.BlockSpec((None, None, ff, dm), lambda i, be, na: (layer, be[i], 0, 0))],
        out_specs=pl.BlockSpec((MOE_BLOCK, dm), lambda i, be, na: (i, 0)),
        scratch_shapes=[pltpu.VMEM((2, MOE_BLOCK, half), jnp.int32), pltpu.SemaphoreType.DMA((2,)),
                        pltpu.VMEM((dm, ff), BF16), pltpu.VMEM((dm, ff), BF16), pltpu.VMEM((ff, dm), BF16)])
    return pl.pallas_call(
        _expert_kernel, grid_spec=grid_spec,
        out_shape=jax.ShapeDtypeStruct((n_blk * MOE_BLOCK, dm), F32),
        compiler_params=_cparams(("arbitrary",), vmem=MOE_VMEM_LIMIT), name="moe_experts",
    )(blk_exp, n_active, idx, idx, hp, w1, w3, w2)


def _dispatch_plan(e_idx):
    n, K = e_idx.shape
    nk = n * K
    i32 = jnp.int32
    flat_e = e_idx.reshape(-1)
    order = jnp.argsort(flat_e).astype(i32)
    rank = jnp.argsort(order).astype(i32)
    experts = jnp.arange(N_EXPERTS, dtype=i32)
    hot = flat_e[:, None] == experts[None, :]
    counts = jnp.sum(hot.astype(i32), axis=0)
    padded = (counts + MOE_BLOCK - 1) // MOE_BLOCK * MOE_BLOCK
    pad_end = jnp.cumsum(padded)
    pad_start = pad_end - padded
    start = jnp.cumsum(counts) - counts
    lookup = lambda table, onehot: jnp.sum(jnp.where(onehot, table[None, :], 0), axis=1)
    dest = lookup(pad_start - start, hot) + rank
    n_blk = (nk + MOE_BLOCK - 1) // MOE_BLOCK + N_EXPERTS
    blk_first = jnp.arange(n_blk, dtype=i32) * MOE_BLOCK
    blk_exp = jnp.minimum(jnp.sum((pad_end[None, :] <= blk_first[:, None]).astype(i32), axis=1), N_EXPERTS - 1)
    blk_hot = blk_exp[:, None] == experts[None, :]
    within = jnp.arange(MOE_BLOCK, dtype=i32)[None, :]
    j = (blk_first - lookup(pad_start, blk_hot))[:, None] + within
    sorted_pos = jnp.clip(lookup(start, blk_hot)[:, None] + j, 0, nk - 1)
    src = jnp.take(order, sorted_pos.reshape(-1), mode="clip") // K
    tok_pad = jnp.where((j < lookup(counts, blk_hot)[:, None]).reshape(-1), src, n - 1)
    n_active = (pad_end[-1] // MOE_BLOCK).astype(i32).reshape(1)
    return tok_pad, blk_exp.astype(i32), n_active, dest.reshape(n, K)


def _moe(h, route, w1, w3, w2, layer):
    e_idx = route[:, :TOP_K_IN_GROUP].astype(jnp.int32)
    tok_pad, blk_exp, n_active, dest = _dispatch_plan(e_idx)
    y = _experts(h, tok_pad, blk_exp, n_active, w1, w3, w2, layer)
    return y, dest


PACK_TILE = 512


def _pack_kernel(offs_ref, wt_ref, o_ref):
    del offs_ref
    o_ref[...] = wt_ref[0].T.astype(BF16)


def _pack_columns(w_in_t, layer, col_offsets):
    _, _, k = w_in_t.shape
    nt = len(col_offsets)
    grid_spec = pltpu.PrefetchScalarGridSpec(
        num_scalar_prefetch=1, grid=(nt,),
        in_specs=[pl.BlockSpec((pl.Element(1), pl.Element(PACK_TILE), pl.Element(k)),
                               lambda j, offs: (layer, pl.multiple_of(offs[j], 8), 0))],
        out_specs=pl.BlockSpec((k, PACK_TILE), lambda j, offs: (0, j)))
    return pl.pallas_call(
        _pack_kernel, grid_spec=grid_spec,
        out_shape=jax.ShapeDtypeStruct((k, nt * PACK_TILE), BF16),
        compiler_params=_cparams(("parallel",)), name="pack_w_in",
    )(jnp.asarray(col_offsets, jnp.int32), w_in_t)


def _tile_offsets(groups):
    offs = []
    for i in groups:
        assert COL_WIDTHS[i] % PACK_TILE == 0 or COL_WIDTHS[i] < PACK_TILE
        offs += list(range(COL_OFFSETS[i], COL_OFFSETS[i + 1], PACK_TILE))
    return offs


def _pack_w_in(w_in_t, layer):
    wide = [i for i in MAIN_ORDER if COL_WIDTHS[i] >= PACK_TILE]
    narrow = [i for i in MAIN_ORDER if COL_WIDTHS[i] < PACK_TILE]
    assert MAIN_ORDER == tuple(wide + narrow) and narrow == list(range(narrow[0], narrow[-1] + 1))
    kv_lo, kv_hi = COL_OFFSETS[narrow[0]], COL_OFFSETS[narrow[-1] + 1]
    assert (kv_hi - kv_lo) % PACK_TILE == 0
    main_offs = _tile_offsets(wide) + list(range(kv_lo, kv_hi, PACK_TILE))
    w_main = _pack_columns(w_in_t, layer, main_offs)
    w_gate = _pack_columns(w_in_t, layer, _tile_offsets([I_MERGE]))
    rows = lambda i: w_in_t[layer, COL_OFFSETS[i]:COL_OFFSETS[i + 1]]
    small_t = jnp.concatenate([rows(I_AI), rows(I_AF), rows(I_CG)], axis=0)
    small_t = jnp.pad(small_t, ((0, LANES - small_t.shape[0]), (0, 0)))
    return w_main, small_t, w_gate


def _router_weights(wg, bg, we, be):
    w = jnp.concatenate([wg, we], axis=1)
    w = jnp.pad(w, ((0, 0), (0, LANES - w.shape[1])))
    hi = w.astype(BF16)
    lo = (w - hi.astype(F32)).astype(BF16)
    b = jnp.concatenate([bg, be])
    b = jnp.pad(b, (0, LANES - b.shape[0])).reshape(1, LANES).astype(F32)
    return hi, lo, b


def kernel(x, w_in, mlstm_conv, mlstm_gate_bias, mlstm_norm, ret_norm, nsa_cmp_w, nsa_cmp_pe, sgu_norm, sgu_w, sgu_b, w_branch, w_out, norm_mix, norm_ffn, router_group_w, router_group_b, router_expert_w, router_expert_b, expert_w1, expert_w3, expert_w2, norm_final):
    B, T, D = x.shape
    n = B * T
    xs, moe = x.reshape(n, D), None
    w_in_t = jnp.swapaxes(w_in, 1, 2)
    for l in range(DEPTH):
        w_main, w_small_t, w_gate = _pack_w_in(w_in_t, l)
        if moe is None:
            h, = _norm(xs, norm_mix[l])
        else:
            xs, h = _norm(xs, norm_mix[l], moe=moe)
        zm = _matmul(h, w_main, out_dtype=BF16, tn=1280, name="proj_main")
        zs = _matmul(h, w_small_t, out_dtype=F32, w_rows_are_outputs=True, name="proj_small")
        zst = zs[:, :2 * N_HEADS].T
        ya = _mlstm(zm, zs, zst, mlstm_conv[l], mlstm_gate_bias[l], mlstm_norm[l], B, T)
        yb = _retention(zm, ret_norm[l], B, T)
        yc = _nsa(zm, zs, nsa_cmp_w[l], nsa_cmp_pe[l], B, T)
        yd = _sgu(zm, sgu_norm[l], sgu_w[l], sgu_b[l], n)
        merged = _merge(h, (ya, yb, yc, yd), w_gate, w_branch[l].astype(BF16))
        x_mid = _matmul(merged, w_out[l].astype(BF16), out_dtype=F32, residual=xs, name="proj_out")
        h2, route = _norm(x_mid, norm_ffn[l],
                          router_w=_router_weights(router_group_w[l], router_group_b[l],
                                                   router_expert_w[l], router_expert_b[l]))
        y_exp, dest = _moe(h2, route, expert_w1, expert_w3, expert_w2, l)
        xs, moe = x_mid, (y_exp, dest, route)
    out, = _norm(xs, norm_final, moe=moe, final=True)
    return out.reshape(B, T, D)
```

```python
import functools

import numpy as np
import jax
import jax.numpy as jnp
from jax import lax
from jax.experimental import pallas as pl
from jax.experimental.pallas import tpu as pltpu

F32 = jnp.float32
BF16 = jnp.bfloat16

D_MODEL = 4096
DEPTH = 2
HEAD_DIM = 128
N_BRANCH = 4
MIX_WIDTH = D_MODEL // N_BRANCH
N_HEADS = MIX_WIDTH // HEAD_DIM
CONV_WIDTH = 4
CONV_TAIL = 16
MLSTM_CHUNK = 128
RET_CHUNK = 128
NSA_KV_GROUPS = 2
NSA_GROUP_SIZE = N_HEADS // NSA_KV_GROUPS
KV_WIDTH = NSA_KV_GROUPS * HEAD_DIM
CMP_BLOCK = 32
CMP_STRIDE = 16
SEL_BLOCK = 64
SEL_TOPN = 16
WINDOW = 512
Q_BLOCK = 128
SGU_CHUNK = 128
N_GROUPS = 8
EXPERTS_PER_GROUP = 8
N_EXPERTS = N_GROUPS * EXPERTS_PER_GROUP
TOP_K_IN_GROUP = 2
EXPERT_FF = 256
MOE_BLOCK = 256
GATHER_UNROLL = 8
RMS_EPS = 1e-6
NEG_INF = -1e30
FORCE_SCORE = 1e9

COL_WIDTHS = (
    MIX_WIDTH, MIX_WIDTH, MIX_WIDTH, MIX_WIDTH, N_HEADS, N_HEADS,
    MIX_WIDTH, MIX_WIDTH, MIX_WIDTH, MIX_WIDTH,
    MIX_WIDTH, KV_WIDTH, KV_WIDTH, KV_WIDTH, KV_WIDTH, KV_WIDTH, KV_WIDTH, 3 * N_HEADS,
    MIX_WIDTH, MIX_WIDTH,
    N_BRANCH * D_MODEL,
)
COL_OFFSETS = tuple(int(v) for v in np.concatenate([[0], np.cumsum(COL_WIDTHS)]))
(I_AQ, I_AK, I_AV, I_AO, I_AI, I_AF, I_BQ, I_BK, I_BV, I_BG, I_CQ, I_CKC, I_CVC, I_CKS, I_CVS,
 I_CKW, I_CVW, I_CG, I_DU, I_DV, I_MERGE) = range(21)

MAIN_ORDER = (I_AQ, I_AK, I_AV, I_AO, I_BQ, I_BK, I_BV, I_BG, I_CQ, I_DU, I_DV,
              I_CKC, I_CVC, I_CKS, I_CVS, I_CKW, I_CVW)
MAIN_WIDTH = sum(COL_WIDTHS[i] for i in MAIN_ORDER)
LANES = 128
BLK_AQ, BLK_AK, BLK_AV, BLK_AO, BLK_BQ, BLK_BK, BLK_BV, BLK_BG, BLK_CQ, BLK_DU, BLK_DV = range(11)
KV_BASE = 11 * MIX_WIDTH // HEAD_DIM
VMEM_LIMIT = 48 * 1024 * 1024
MOE_VMEM_LIMIT = 56 * 1024 * 1024


def _cparams(sem, vmem=VMEM_LIMIT):
    return pltpu.CompilerParams(dimension_semantics=sem, vmem_limit_bytes=vmem)


def _dot(a, b):
    return jnp.dot(a, b, preferred_element_type=F32)


def _dot_nt(a, b):
    return lax.dot_general(a, b, (((1,), (1,)), ((), ())), preferred_element_type=F32)


def _split3(x):
    hi = x.astype(BF16)
    r1 = x - hi.astype(F32)
    mid = r1.astype(BF16)
    lo = (r1 - mid.astype(F32)).astype(BF16)
    return hi, mid, lo


def _dot_exact_rhs(x, m_bf16):
    hi, mid, lo = _split3(x)
    return _dot(hi, m_bf16) + _dot(mid, m_bf16) + _dot(lo, m_bf16)


def _log_sigmoid(x):
    return jnp.minimum(x, 0.0) - jnp.log1p(jnp.exp(-jnp.abs(x)))


def _row_gather(idx_ref, src_hbm, dst, sem, count):
    def issue(c, carry):
        for u in range(GATHER_UNROLL):
            r = c * GATHER_UNROLL + u
            pltpu.make_async_copy(src_hbm.at[pl.ds(idx_ref[0, r], 1)], dst.at[pl.ds(r, 1)], sem).start(priority=u % 2)
        return carry
    lax.fori_loop(0, count // GATHER_UNROLL, issue, 0)


def _row_gather_wait(src_hbm, dst, sem, count):
    for r in range(count):
        pltpu.make_async_copy(src_hbm.at[pl.ds(0, 1)], dst.at[pl.ds(r, 1)], sem).wait()


def _norm_kernel(*refs, moe, router, final):
    x = refs[0][...]
    rows = x.shape[0]
    pos = 1
    n_scratch = 0
    if moe:
        idx_ref, idx_next_ref, y_hbm, rt_ref = refs[pos:pos + 4]
        pos += 4
        gbuf, sems = refs[-2:]
        n_scratch = 2
        i, last = pl.program_id(0), pl.num_programs(0) - 1
        slot = lax.rem(i, 2)

        @pl.when(i == 0)
        def _():
            _row_gather(idx_ref, y_hbm, gbuf.at[0], sems.at[0], 2 * rows)

        @pl.when(i < last)
        def _():
            _row_gather(idx_next_ref, y_hbm, gbuf.at[1 - slot], sems.at[1 - slot], 2 * rows)

        _row_gather_wait(y_hbm, gbuf.at[slot], sems.at[slot], 2 * rows)
        x = x + (rt_ref[:, 2:3] * gbuf[slot, 0:rows, :] + rt_ref[:, 3:4] * gbuf[slot, rows:2 * rows, :])
    g_ref = refs[pos]
    pos += 1
    if router:
        whi_ref, wlo_ref, rb_ref = refs[pos:pos + 3]
        pos += 3
    outs = refs[pos:len(refs) - n_scratch]
    y = x * lax.rsqrt(jnp.mean(x * x, axis=-1, keepdims=True) + RMS_EPS) * g_ref[...]
    o = 0
    if moe and not final:
        outs[o][...] = x
        o += 1
    if final:
        outs[o][...] = y
    elif router:
        outs[o][...] = _pack_bf16_pairs(y)
    else:
        outs[o][...] = y.astype(BF16)
    o += 1
    if router:
        y_hi = y.astype(BF16)
        y_lo = (y - y_hi.astype(F32)).astype(BF16)
        lg = _dot(y_hi, whi_ref[...]) + (_dot(y_hi, wlo_ref[...]) + _dot(y_lo, whi_ref[...])) + rb_ref[...]
        outs[o][...] = _route_rows(lg)


def _first_max(v, lane):
    mx = jnp.max(v, axis=-1, keepdims=True)
    return mx, jnp.min(jnp.where(v == mx, lane, LANES), axis=-1, keepdims=True)


def _route_rows(lg):
    low = -3.0e38
    lane = lax.broadcasted_iota(jnp.int32, lg.shape, 1)
    is_grp = lane < N_GROUPS
    gmax, g_idx = _first_max(jnp.where(is_grp, lg, low), lane)
    g_w = 1.0 / jnp.sum(jnp.where(is_grp, jnp.exp(lg - gmax), 0.0), axis=-1, keepdims=True)
    assert EXPERTS_PER_GROUP & (EXPERTS_PER_GROUP - 1) == 0
    grp_of_lane = lax.shift_right_arithmetic(lane - N_GROUPS, EXPERTS_PER_GROUP.bit_length() - 1)
    in_grp = (lane >= N_GROUPS) & (lane < N_GROUPS + N_EXPERTS) & (grp_of_lane == g_idx)
    el = jnp.where(in_grp, lg, low)
    v0, i0 = _first_max(el, lane)
    v1, i1 = _first_max(jnp.where(lane == i0, low, el), lane)
    e1 = jnp.exp(v1 - v0)
    p0 = g_w / (1.0 + e1)
    vals = (i0 - N_GROUPS).astype(F32), (i1 - N_GROUPS).astype(F32), p0, p0 * e1
    out = jnp.zeros(lg.shape, F32)
    for k, v in enumerate(vals):
        out = jnp.where(lane == k, v, out)
    return out


def _norm(x, gain, *, moe=None, router_w=None, final=False, rows=128):
    n, d = x.shape
    rows = min(rows, n)
    steps = n // rows
    row_spec = pl.BlockSpec((rows, d), lambda i: (i, 0))
    lane_spec = pl.BlockSpec((rows, LANES), lambda i: (i, 0))
    in_specs, args, scratch = [row_spec], [x], []
    if moe is not None:
        y, dest, route = moe
        K = dest.shape[1]
        idx = dest.reshape(steps, rows, K).transpose(0, 2, 1).reshape(steps, 1, K * rows)
        smem_blk = lambda f: pl.BlockSpec((None, 1, K * rows), f, memory_space=pltpu.SMEM)
        in_specs += [smem_blk(lambda i: (i, 0, 0)), smem_blk(lambda i: (jnp.minimum(i + 1, steps - 1), 0, 0)),
                     pl.BlockSpec(memory_space=pl.ANY), lane_spec]
        args += [idx, idx, y, route]
        scratch = [pltpu.VMEM((2, K * rows, d), F32), pltpu.SemaphoreType.DMA((2,))]
    in_specs.append(pl.BlockSpec((1, d), lambda i: (0, 0)))
    args.append(gain.reshape(1, d))
    out_shape, out_specs = [], []
    if moe is not None and not final:
        out_shape.append(jax.ShapeDtypeStruct((n, d), F32))
        out_specs.append(row_spec)
    if router_w is not None:
        out_shape.append(jax.ShapeDtypeStruct((n, d // 2), jnp.int32))
        out_specs.append(pl.BlockSpec((rows, d // 2), lambda i: (i, 0)))
    else:
        out_shape.append(jax.ShapeDtypeStruct((n, d), F32 if final else BF16))
        out_specs.append(row_spec)
    if router_w is not None:
        whi, wlo, rb = router_w
        in_specs += [pl.BlockSpec((d, LANES), lambda i: (0, 0))] * 2 + [pl.BlockSpec((1, LANES), lambda i: (0, 0))]
        args += [whi, wlo, rb]
        out_shape.append(jax.ShapeDtypeStruct((n, LANES), F32))
        out_specs.append(lane_spec)
    return pl.pallas_call(
        functools.partial(_norm_kernel, moe=moe is not None, router=router_w is not None, final=final),
        grid=(steps,), in_specs=in_specs, out_specs=out_specs, out_shape=out_shape, scratch_shapes=scratch,
        compiler_params=_cparams(("arbitrary",)), name="rmsnorm",
    )(*args)


def _mm_kernel(a_ref, w_ref, *rest, has_res, w_rows_are_outputs):
    w = w_ref[...].astype(BF16)
    acc = _dot_nt(a_ref[...], w) if w_rows_are_outputs else _dot(a_ref[...], w)
    if has_res:
        acc = acc + rest[0][...]
    rest[-1][...] = acc.astype(rest[-1].dtype)


def _matmul(a, w, *, out_dtype, residual=None, tm=1024, tn=512, w_rows_are_outputs=False, name="matmul"):
    m, k = a.shape
    nc = w.shape[0] if w_rows_are_outputs else w.shape[1]
    tm, tn = min(tm, m), min(tn, nc)
    w_spec = pl.BlockSpec((tn, k), lambda i, j: (j, 0)) if w_rows_are_outputs else pl.BlockSpec((k, tn), lambda i, j: (0, j))
    in_specs = [pl.BlockSpec((tm, k), lambda i, j: (i, 0)), w_spec]
    args = [a, w]
    if residual is not None:
        in_specs.append(pl.BlockSpec((tm, tn), lambda i, j: (i, j)))
        args.append(residual)
    return pl.pallas_call(
        functools.partial(_mm_kernel, has_res=residual is not None, w_rows_are_outputs=w_rows_are_outputs),
        grid=(m // tm, nc // tn), in_specs=in_specs,
        out_specs=pl.BlockSpec((tm, tn), lambda i, j: (i, j)),
        out_shape=jax.ShapeDtypeStruct((m, nc), out_dtype),
        compiler_params=_cparams(("parallel", "parallel")), name=name,
    )(*args)


def _mlstm_kernel(aq_ref, ak_ref, av_ref, ao_ref, zs_ref, zst_ref, cw_ref, gbc_ref, gbr_ref, gain_ref,
                  tril_ref, triu_ref, shift_ref, o_ref, tail_s, qk_s, ct_s, n_s, m_s):
    L, d, H, W = MLSTM_CHUNK, HEAD_DIM, N_HEADS, MIX_WIDTH
    c = pl.program_id(1)

    @pl.when(c == 0)
    def _():
        tail_s[...] = jnp.zeros_like(tail_s)
        ct_s[...] = jnp.zeros_like(ct_s)
        n_s[...] = jnp.zeros_like(n_s)
        m_s[...] = jnp.zeros_like(m_s)

    x_cur = jnp.concatenate([aq_ref[...], ak_ref[...]], axis=1)
    x_ext = jnp.concatenate([tail_s[...], x_cur], axis=0)
    conv = cw_ref[CONV_WIDTH - 1:CONV_WIDTH, :] * x_cur.astype(F32)
    for j in range(CONV_WIDTH - 1):
        conv = conv + cw_ref[j:j + 1, :] * _dot(shift_ref[j], x_ext)
    tail_s[...] = x_cur[L - CONV_TAIL:, :]
    qk_s[...] = conv * jax.nn.sigmoid(conv)

    pre_c = zs_ref[...] + gbc_ref[...]
    ls_c = _log_sigmoid(pre_c)
    hi, mid, lo = _split3(ls_c)
    tril = tril_ref[...]
    bc = _dot(tril, hi) + _dot(tril, mid) + _dot(tril, lo)
    pre_r = zst_ref[...] + gbr_ref[...]
    br = _dot_exact_rhs(_log_sigmoid(pre_r), triu_ref[...])

    row = lax.broadcasted_iota(jnp.int32, (L, L), 0)
    col = lax.broadcasted_iota(jnp.int32, (L, L), 1)
    causal = row >= col
    scale = d ** -0.5
    for h in range(H):
        sl = slice(h * d, (h + 1) * d)
        q = qk_s[:, sl] * scale
        k = qk_s[:, W + h * d:W + (h + 1) * d]
        v = av_ref[:, sl]
        i_col, b_col = pre_c[:, h:h + 1], bc[:, H + h:H + h + 1]
        i_row, b_row = pre_r[h:h + 1, :], br[H + h:H + h + 1, :]
        m_prev = m_s[h:h + 1, 0:1]
        log_d = jnp.where(causal, b_col - b_row + i_row, NEG_INF)
        log_inter = b_col + m_prev
        m_row = jnp.maximum(jnp.max(log_d, axis=-1, keepdims=True), log_inter)
        qb, kb = q.astype(BF16), k.astype(BF16)
        s = _dot_nt(qb, kb) * jnp.exp(log_d - m_row)
        w_inter = jnp.exp(log_inter - m_row)
        num = _dot(s.astype(BF16), v) + w_inter * _dot(qb, ct_s[h].astype(BF16))
        den = jnp.sum(s, axis=-1, keepdims=True) + w_inter * jnp.sum(q * n_s[h:h + 1, :], axis=-1, keepdims=True)
        hh = num / jnp.maximum(jnp.abs(den), jnp.exp(-m_row))
        b_last = b_row[:, L - 1:L]
        log_w_row = b_last - b_row + i_row
        m_new = jnp.maximum(b_last + m_prev, jnp.max(log_w_row, axis=-1, keepdims=True))
        w_col = jnp.exp(b_last - b_col + i_col - m_new)
        decay = jnp.exp(b_last + m_prev - m_new)
        vw = (v.astype(F32) * w_col).astype(BF16)
        ct_s[h] = decay * ct_s[h] + _dot(k.T.astype(BF16), vw)
        n_s[h:h + 1, :] = decay * n_s[h:h + 1, :] + jnp.sum(k * w_col, axis=0, keepdims=True)
        m_s[h:h + 1, :] = jnp.broadcast_to(m_new, (1, LANES))
        hn = hh * lax.rsqrt(jnp.mean(hh * hh, axis=-1, keepdims=True) + RMS_EPS)
        o_ref[:, sl] = (hn * gain_ref[:, sl] * jax.nn.sigmoid(ao_ref[:, sl].astype(F32))).astype(BF16)


def _tri_consts(L):
    r = np.arange(L)
    tril = (r[:, None] >= r[None, :]).astype(np.float32)
    return jnp.asarray(tril, BF16), jnp.asarray(tril.T, BF16)


def _mlstm(zm, zs, zst, conv_w, gate_bias, gain, B, T):
    L, W = MLSTM_CHUNK, MIX_WIDTH
    nc = T // L
    n = B * T
    gbc = jnp.zeros((1, LANES), F32).at[0, :2 * N_HEADS].set(gate_bias.reshape(-1))
    gbr = jnp.broadcast_to(gate_bias.reshape(2 * N_HEADS, 1), (2 * N_HEADS, L)).astype(F32)
    tril, triu = _tri_consts(L)
    t_idx = np.arange(L)[:, None]
    c_idx = np.arange(CONV_TAIL + L)[None, :]
    shift = jnp.asarray(np.stack([(c_idx == CONV_TAIL + t_idx - (CONV_WIDTH - 1 - j)) for j in range(CONV_WIDTH - 1)])
                        .astype(np.float32), BF16)

    def blk(b_idx):
        return pl.BlockSpec((L, W), lambda b, c: (b * nc + c, b_idx))

    const = lambda shape: pl.BlockSpec(shape, lambda b, c: (0,) * len(shape))
    return pl.pallas_call(
        _mlstm_kernel, grid=(B, nc),
        in_specs=[blk(BLK_AQ), blk(BLK_AK), blk(BLK_AV), blk(BLK_AO),
                  pl.BlockSpec((L, LANES), lambda b, c: (b * nc + c, 0)),
                  pl.BlockSpec((2 * N_HEADS, L), lambda b, c: (0, b * nc + c)),
                  const((CONV_WIDTH, 2 * W)), const((1, LANES)), const((2 * N_HEADS, L)), const((1, W)),
                  const((L, L)), const((L, L)), const((CONV_WIDTH - 1, L, CONV_TAIL + L))],
        out_specs=pl.BlockSpec((L, W), lambda b, c: (b * nc + c, 0)),
        out_shape=jax.ShapeDtypeStruct((n, W), BF16),
        scratch_shapes=[pltpu.VMEM((CONV_TAIL, 2 * W), BF16), pltpu.VMEM((L, 2 * W), F32),
                        pltpu.VMEM((N_HEADS, HEAD_DIM, HEAD_DIM), F32), pltpu.VMEM((N_HEADS, HEAD_DIM), F32),
                        pltpu.VMEM((N_HEADS, LANES), F32)],
        compiler_params=_cparams(("parallel", "arbitrary")), name="mlstm",
    )(zm, zm, zm, zm, zs, zst, conv_w, gbc, gbr, gain.reshape(1, W), tril, triu, shift)


def _ret_kernel(q_ref, k_ref, v_ref, g_ref, dec_ref, qd_ref, kd_ref, gain_ref, o_ref, r_s, *, chunk_decay):
    L, d, H = RET_CHUNK, HEAD_DIM, N_HEADS
    c = pl.program_id(1)

    @pl.when(c == 0)
    def _():
        r_s[...] = jnp.zeros_like(r_s)

    scale = d ** -0.5
    for h in range(H):
        sl = slice(h * d, (h + 1) * d)
        q = q_ref[:, sl].astype(F32)
        k = k_ref[:, sl].astype(F32) * scale
        v = v_ref[:, sl]
        s = _dot_nt(q.astype(BF16), k.astype(BF16)) * dec_ref[h]
        o = _dot(s.astype(BF16), v) + _dot((q * qd_ref[:, sl]).astype(BF16), r_s[h].astype(BF16))
        kd = (k * kd_ref[:, sl]).T.astype(BF16)
        r_s[h] = chunk_decay[h] * r_s[h] + _dot(kd, v)
        on = o * lax.rsqrt(jnp.mean(o * o, axis=-1, keepdims=True) + RMS_EPS)
        gate = g_ref[:, sl].astype(F32)
        o_ref[:, sl] = (on * gain_ref[:, sl] * (gate * jax.nn.sigmoid(gate))).astype(BF16)


def _retention(zm, gain, B, T):
    L, W, H, d = RET_CHUNK, MIX_WIDTH, N_HEADS, HEAD_DIM
    nc = T // L
    log_gamma = np.log(np.float32(1.0) - np.float32(2.0) ** (-5.0 - np.arange(H, dtype=np.float32))).astype(np.float32)
    pos = np.arange(L, dtype=np.float32)
    diff = pos[:, None] - pos[None, :]
    decay = np.where(diff >= 0, np.exp(log_gamma[:, None, None] * np.maximum(diff, 0.0)), 0.0).astype(np.float32)
    q_decay = np.exp(log_gamma[:, None] * (pos + 1.0)).astype(np.float32)
    k_decay = np.exp(log_gamma[:, None] * (L - 1.0 - pos)).astype(np.float32)
    chunk_decay = tuple(float(v) for v in np.exp(log_gamma * np.float32(L)).astype(np.float32))
    qd = jnp.asarray(np.repeat(q_decay.T, d, axis=1))
    kd = jnp.asarray(np.repeat(k_decay.T, d, axis=1))

    def blk(b_idx):
        return pl.BlockSpec((L, W), lambda b, c: (b * nc + c, b_idx))

    const = lambda shape: pl.BlockSpec(shape, lambda b, c: (0,) * len(shape))
    return pl.pallas_call(
        functools.partial(_ret_kernel, chunk_decay=chunk_decay), grid=(B, nc),
        in_specs=[blk(BLK_BQ), blk(BLK_BK), blk(BLK_BV), blk(BLK_BG),
                  const((H, L, L)), const((L, W)), const((L, W)), const((1, W))],
        out_specs=pl.BlockSpec((L, W), lambda b, c: (b * nc + c, 0)),
        out_shape=jax.ShapeDtypeStruct((B * T, W), BF16),
        scratch_shapes=[pltpu.VMEM((H, d, d), F32)],
        compiler_params=_cparams(("parallel", "arbitrary")), name="retention",
    )(zm, zm, zm, zm, jnp.asarray(decay), qd, kd, gain.reshape(1, W))


def _gelu(x):
    return 0.5 * x * (1.0 + jnp.tanh(0.7978845608028654 * (x + 0.044715 * (x * x * x))))


def _sgu_kernel(u_ref, v_ref, ng_ref, w_ref, b_ref, o_ref):
    L, d, H = SGU_CHUNK, HEAD_DIM, N_HEADS
    v = _gelu(v_ref[...].astype(F32))
    vn = (v * lax.rsqrt(jnp.mean(v * v, axis=-1, keepdims=True) + RMS_EPS) * ng_ref[...]).astype(BF16)
    u = _gelu(u_ref[...].astype(F32))
    row = lax.broadcasted_iota(jnp.int32, (L, L), 0)
    col = lax.broadcasted_iota(jnp.int32, (L, L), 1)
    for g in range(H):
        sl = slice(g * d, (g + 1) * d)
        wm = jnp.where(row >= col, w_ref[g], 0.0).astype(BF16)
        mixed = _dot(wm, vn[:, sl]) + b_ref[:, sl]
        o_ref[:, sl] = (u[:, sl] * mixed).astype(BF16)


def _sgu(zm, norm_g, w_s, b_s, n):
    L, W, H, d = SGU_CHUNK, MIX_WIDTH, N_HEADS, HEAD_DIM
    bsb = jnp.repeat(b_s.T, d, axis=1)
    const = lambda shape: pl.BlockSpec(shape, lambda i: (0,) * len(shape))
    return pl.pallas_call(
        _sgu_kernel, grid=(n // L,),
        in_specs=[pl.BlockSpec((L, W), lambda i: (i, BLK_DU)), pl.BlockSpec((L, W), lambda i: (i, BLK_DV)),
                  const((1, W)), const((H, L, L)), const((L, W))],
        out_specs=pl.BlockSpec((L, W), lambda i: (i, 0)),
        out_shape=jax.ShapeDtypeStruct((n, W), BF16),
        compiler_params=_cparams(("parallel",)), name="sgu",
    )(zm, zm, norm_g.reshape(1, W), w_s, bsb)


def _compress_kernel(k_ref, v_ref, w_ref, pe_ref, ko_ref, vo_ref, buf):
    t = k_ref.shape[0]
    n16 = t // CMP_STRIDE
    buf[t:t + CMP_STRIDE, :] = jnp.zeros((CMP_STRIDE, HEAD_DIM), F32)
    for which, (src, dst) in enumerate(((k_ref, ko_ref), (v_ref, vo_ref))):
        buf[0:t, :] = src[...].astype(F32)
        acc = jnp.zeros((n16, HEAD_DIM), F32)
        for l in range(CMP_BLOCK):
            x = buf[pl.ds(l, n16, stride=CMP_STRIDE), :]
            acc = acc + _dot((x + pe_ref[which, l:l + 1, :]).astype(BF16), w_ref[which, l].astype(BF16))
        dst[...] = (acc.T if which else acc).astype(BF16)


def _compress(zm, cmp_w, cmp_pe, B, T):
    G, d = NSA_KV_GROUPS, HEAD_DIM
    n16 = T // CMP_STRIDE
    out = jax.ShapeDtypeStruct((B, G, n16, d), BF16)
    kblk = KV_BASE
    vblk = KV_BASE + G
    return pl.pallas_call(
        _compress_kernel, grid=(B, G),
        in_specs=[pl.BlockSpec((T, d), lambda b, g: (b, kblk + g)), pl.BlockSpec((T, d), lambda b, g: (b, vblk + g)),
                  pl.BlockSpec((2, CMP_BLOCK, d, d), lambda b, g: (0, 0, 0, 0)),
                  pl.BlockSpec((2, CMP_BLOCK, d), lambda b, g: (0, 0, 0))],
        out_specs=[pl.BlockSpec((None, None, n16, d), lambda b, g: (b, g, 0, 0)),
                   pl.BlockSpec((None, None, d, n16), lambda b, g: (b, g, 0, 0))],
        out_shape=[out, jax.ShapeDtypeStruct((B, G, d, n16), BF16)],
        scratch_shapes=[pltpu.VMEM((T + CMP_STRIDE, d), F32)],
        compiler_params=_cparams(("parallel", "parallel")), name="nsa_compress",
    )(zm, zm, cmp_w, cmp_pe)


RANK_CHUNK = 8


def _lanes_x(x, times):
    return jnp.concatenate([x] * times, axis=1)


def _nsa_a_kernel(q_ref, kc_ref, vct_ref, *rest, n_back):
    nw = n_back + 1
    kw_refs, vwt_refs = rest[:nw], rest[nw:2 * nw]
    cgt_ref, ovt_ref, acmp_ref, awin_ref, part_ref, sel_ref, q_s, key_s, rank_s = rest[2 * nw:]
    tq, d, J = Q_BLOCK, HEAD_DIM, NSA_GROUP_SIZE
    qi = pl.program_id(2)
    t0 = qi * tq
    ncmp = kc_ref.shape[0]
    scale = d ** -0.5
    for j in range(J):
        q_s[j * tq:(j + 1) * tq, :] = (q_ref[:, j * d:(j + 1) * d].astype(F32) * scale).astype(BF16)
    sg_t = jax.nn.sigmoid(cgt_ref[...])
    t_row = t0 + lax.broadcasted_iota(jnp.int32, (1, tq), 1)

    n_i = lax.broadcasted_iota(jnp.int32, (ncmp, tq), 0)
    r_i = lax.broadcasted_iota(jnp.int32, (ncmp, tq), 1)
    bias_c = jnp.where(t0 + r_i - CMP_STRIDE * n_i - (CMP_BLOCK - 1) >= 0, 0.0, NEG_INF)
    s = _dot_nt(kc_ref[...], q_s[...]) - acmp_ref[...] + _lanes_x(bias_c, J)
    p = jnp.exp(s - jnp.max(s, axis=0, keepdims=True))
    has_c = jnp.where(t_row >= CMP_BLOCK - 1, 1.0, 0.0)
    p = p * (_lanes_x(has_c, J) / jnp.sum(p, axis=0, keepdims=True))
    o_cmp_t = _dot(vct_ref[...], p.astype(BF16))
    psum = p[:, 0:tq]
    for j in range(1, J):
        psum = psum + p[:, j * tq:(j + 1) * tq]

    hi, mid, lo = _split3(psum)
    ovt = ovt_ref[...]
    imp_t = _dot(ovt, hi) + _dot(ovt, mid) + _dot(ovt, lo)
    blk_t = lax.broadcasted_iota(jnp.int32, (LANES, tq), 0)
    cur_t = lax.shift_right_logical(t0 + lax.broadcasted_iota(jnp.int32, (LANES, tq), 1), SEL_BLOCK.bit_length() - 1)
    valid_t = blk_t <= cur_t
    forced = (blk_t == 0) | (blk_t == cur_t) | (blk_t == cur_t - 1)
    key_s[...] = jnp.where(valid_t, imp_t + jnp.where(forced, FORCE_SCORE, 0.0), NEG_INF)
    rank_s[...] = jnp.zeros_like(rank_s)
    sub = lax.broadcasted_iota(jnp.int32, (8, tq), 0)
    groups = LANES // 8
    last_valid = lax.shift_right_logical(t0 + tq - 1, SEL_BLOCK.bit_length() - 1)
    for chunk in range(LANES // RANK_CHUNK):
        @pl.when(chunk * RANK_CHUNK <= last_valid)
        def _(chunk=chunk):
            key_g = [key_s[8 * i:8 * i + 8, :] for i in range(groups)]
            rank = [rank_s[8 * i:8 * i + 8, :] for i in range(groups)]
            for mp in range(chunk * RANK_CHUNK, (chunk + 1) * RANK_CHUNK):
                rowv = jnp.broadcast_to(key_s[mp:mp + 1, :], (8, tq))
                for i in range(groups):
                    if 8 * i > mp:
                        ahead = rowv >= key_g[i]
                    elif 8 * i + 7 <= mp:
                        ahead = rowv > key_g[i]
                    else:
                        ahead = (rowv > key_g[i]) | ((sub > mp - 8 * i) & (rowv == key_g[i]))
                    rank[i] = rank[i] + jnp.where(ahead, 1.0, 0.0)
            for i in range(groups):
                rank_s[8 * i:8 * i + 8, :] = rank[i]
    sel_ref[...] = jnp.where((rank_s[...] < float(SEL_TOPN)) & valid_t, 1.0, 0.0).astype(BF16)

    span = nw * tq
    kw = jnp.concatenate([kr[...] for kr in kw_refs], axis=0)
    vw_t = jnp.concatenate([vr[...] for vr in vwt_refs], axis=1)
    c_i = lax.broadcasted_iota(jnp.int32, (span, tq), 0)
    bias_w = jnp.where(t0 - n_back * tq + c_i >= 0, 0.0, NEG_INF)
    s = _dot_nt(kw, q_s[...]) - awin_ref[...] + _lanes_x(bias_w, J)
    p = jnp.exp(s - jnp.max(s, axis=0, keepdims=True))
    inv_l = 1.0 / jnp.sum(p, axis=0, keepdims=True)
    o_win_t = _dot(vw_t, p.astype(BF16))
    g_cmp = jnp.concatenate([sg_t[j:j + 1, :] for j in range(J)], axis=1)
    g_win = jnp.concatenate([sg_t[2 * J + j:2 * J + j + 1, :] for j in range(J)], axis=1)
    part_ref[...] = g_cmp * o_cmp_t + (g_win * inv_l) * o_win_t


def _alibi_slopes():
    return (2.0 ** (-8.0 * (np.arange(N_HEADS, dtype=np.float32) + 1.0) / N_HEADS)).astype(np.float32)


def _nsa_consts(T):
    G, J, tq = NSA_KV_GROUPS, NSA_GROUP_SIZE, Q_BLOCK
    n16 = T // CMP_STRIDE
    n_sel = T // SEL_BLOCK
    n = np.arange(n16)[None, :]
    m = np.arange(LANES)[:, None]
    c_start, s_start = n * CMP_STRIDE, m * SEL_BLOCK
    overlap_t = ((c_start < s_start + SEL_BLOCK) & (c_start + CMP_BLOCK > s_start) & (n < n16 - 1) & (m < n_sel))
    slopes = _alibi_slopes().reshape(G, 1, J, 1)
    r = np.arange(tq, dtype=np.float32)[None, None, None, :]
    dist_c = r - CMP_STRIDE * np.arange(n16, dtype=np.float32)[None, :, None, None] - (CMP_BLOCK - 1)
    a_cmp = (slopes * dist_c).reshape(G, n16, J * tq)
    span = WINDOW + tq
    dist_w = r + WINDOW - np.arange(span, dtype=np.float32)[None, :, None, None]
    a_win = np.where((dist_w >= 0) & (dist_w < WINDOW), slopes * dist_w, -NEG_INF).reshape(G, span, J * tq)
    return (jnp.asarray(overlap_t.astype(np.float32), BF16), jnp.asarray(a_cmp.astype(np.float32)),
            jnp.asarray(a_win.astype(np.float32)))


def _nsa_a(zm, k_cmp, v_cmp_t, vw_t, cg_t, B, T):
    G, J, d, tq = NSA_KV_GROUPS, NSA_GROUP_SIZE, HEAD_DIM, Q_BLOCK
    nq = T // tq
    n16 = T // CMP_STRIDE
    n_back = WINDOW // tq
    span = WINDOW + tq
    qblk = BLK_CQ * (MIX_WIDTH // (J * d))
    kwblk = KV_BASE + 4 * G
    overlap_t, a_cmp, a_win = _nsa_consts(T)

    def past(qi, i):
        return jnp.maximum(qi - n_back + i, 0)

    in_specs = ([pl.BlockSpec((tq, J * d), lambda b, g, qi: (b * nq + qi, qblk + g)),
                 pl.BlockSpec((None, None, n16, d), lambda b, g, qi: (b, g, 0, 0)),
                 pl.BlockSpec((None, None, d, n16), lambda b, g, qi: (b, g, 0, 0))]
                + [pl.BlockSpec((tq, d), functools.partial(lambda b, g, qi, i: (b * nq + past(qi, i), kwblk + g), i=i))
                   for i in range(n_back + 1)]
                + [pl.BlockSpec((None, None, d, tq), functools.partial(lambda b, g, qi, i: (b, g, 0, past(qi, i)), i=i))
                   for i in range(n_back + 1)]
                + [pl.BlockSpec((None, 16, tq), lambda b, g, qi: (g, 0, b * nq + qi)),
                   pl.BlockSpec((LANES, n16), lambda b, g, qi: (0, 0)),
                   pl.BlockSpec((None, n16, J * tq), lambda b, g, qi: (g, 0, 0)),
                   pl.BlockSpec((None, span, J * tq), lambda b, g, qi: (g, 0, 0))])
    return pl.pallas_call(
        functools.partial(_nsa_a_kernel, n_back=n_back), grid=(B, G, nq), in_specs=in_specs,
        out_specs=[pl.BlockSpec((None, None, None, d, J * tq), lambda b, g, qi: (b, g, qi, 0, 0)),
                   pl.BlockSpec((None, None, LANES, tq), lambda b, g, qi: (b, g, 0, qi))],
        out_shape=[jax.ShapeDtypeStruct((B, G, nq, d, J * tq), F32), jax.ShapeDtypeStruct((B, G, LANES, T), BF16)],
        scratch_shapes=[pltpu.VMEM((J * tq, d), BF16), pltpu.VMEM((LANES, tq), F32), pltpu.VMEM((LANES, tq), F32)],
        compiler_params=_cparams(("parallel", "parallel", "parallel")), name="nsa_cmp_win",
    )(zm, k_cmp, v_cmp_t, *([zm] * (n_back + 1)), *([vw_t] * (n_back + 1)), cg_t, overlap_t, a_cmp, a_win)


def _nsa_b_kernel(cnt_ref, lst_ref, q_ref, ks_ref, vt_ref, sel_ref, e_ref, a_ref, srow_ref, cgt_ref, part_ref, o_ref,
                  q_s, acc_s, *, tk, nk):
    tq, d, J = Q_BLOCK, HEAD_DIM, NSA_GROUP_SIZE
    b, g, qi = pl.program_id(0), pl.program_id(1), pl.program_id(2)
    lin = (b * pl.num_programs(1) + g) * pl.num_programs(2) + qi
    t0 = qi * tq
    scale = d ** -0.5
    for j in range(J):
        q_s[j * tq:(j + 1) * tq, :] = (q_ref[:, j * d:(j + 1) * d].astype(F32) * scale).astype(BF16)
    acc_s[...] = jnp.zeros_like(acc_s)
    r_minus_c = lax.broadcasted_iota(jnp.int32, (tk, tq), 1) - lax.broadcasted_iota(jnp.int32, (tk, tq), 0)

    count = cnt_ref[lin]

    def scores(pos):
        kt = lst_ref[lin * nk + pos]
        k_t = ks_ref[pl.ds(pl.multiple_of(kt * tk, tk), tk), :]
        off = t0 - kt * tk
        picked = _dot(e_ref[kt], sel_ref[...])
        causal_off = jnp.where(pos < count, off, -tk * nk)
        keep = (picked > 0.5) & (r_minus_c + causal_off >= 0)
        s = _dot_nt(k_t, q_s[...]) - a_ref[...] + _lanes_x(jnp.where(keep, 0.0, NEG_INF), J)
        return s, srow_ref[...] * off.astype(F32), kt

    def body(i, carry):
        m_prev, l_prev = carry
        s_a, shift_a, kt_a = scores(2 * i)
        s_b, shift_b, kt_b = scores(2 * i + 1)
        m_new = jnp.maximum(m_prev, jnp.maximum(jnp.max(s_a, axis=0, keepdims=True) - shift_a,
                                                jnp.max(s_b, axis=0, keepdims=True) - shift_b))
        p_a = jnp.exp(s_a - (m_new + shift_a))
        p_b = jnp.exp(s_b - (m_new + shift_b))
        alpha = jnp.exp(m_prev - m_new)
        acc_s[...] = alpha * acc_s[...] + (_dot(vt_ref[kt_a], p_a.astype(BF16)) + _dot(vt_ref[kt_b], p_b.astype(BF16)))
        return m_new, alpha * l_prev + (jnp.sum(p_a, axis=0, keepdims=True) + jnp.sum(p_b, axis=0, keepdims=True))

    init = (jnp.full((1, J * tq), NEG_INF, F32), jnp.zeros((1, J * tq), F32))
    _, l = lax.fori_loop(0, (count + 1) // 2, body, init)
    sg_t = jax.nn.sigmoid(cgt_ref[...])
    g_sel = jnp.concatenate([sg_t[J + j:J + j + 1, :] for j in range(J)], axis=1)
    total_t = part_ref[...] + (g_sel / l) * acc_s[...]
    for j in range(J):
        o_ref[:, j * d:(j + 1) * d] = total_t[:, j * tq:(j + 1) * tq].T.astype(BF16)


def _nsa_b(zm, sel_t, cg_t, part_t, B, T, tk=512):
    G, J, d, tq = NSA_KV_GROUPS, NSA_GROUP_SIZE, HEAD_DIM, Q_BLOCK
    tk = min(tk, T)
    nq, nk = T // tq, T // tk
    qblk = BLK_CQ * (MIX_WIDTH // (J * d))
    ksblk = KV_BASE + 2 * G
    vs_col = (KV_BASE + 3 * G) * d
    per_tile = tk // SEL_BLOCK
    kt = np.arange(nk)[:, None, None]
    cc = np.arange(tk)[None, :, None]
    m = np.arange(LANES)[None, None, :]
    expand_t = jnp.asarray((m == kt * per_tile + cc // SEL_BLOCK).astype(np.float32), BF16)
    slopes_np = _alibi_slopes()
    r_minus_c = (np.arange(tq)[None, :] - np.arange(tk)[:, None]).astype(np.float32)
    alibi_t = jnp.asarray((slopes_np.reshape(G, 1, J, 1) * r_minus_c[None, :, None, :]).reshape(G, tk, J * tq))
    srow = jnp.asarray(np.repeat(slopes_np.reshape(G, J), tq, axis=1).reshape(G, 1, J * tq))
    vs_t = zm[:, vs_col:vs_col + G * d].reshape(B, nk, tk, G, d).transpose(0, 3, 1, 4, 2)

    active = sel_t[:, :, :nk * per_tile].reshape(B, G, nk, per_tile, nq, tq).max(axis=(3, 5)) > 0
    active = active.transpose(0, 1, 3, 2)
    tiles = jnp.arange(nk, dtype=jnp.int32)
    order = jnp.sort(jnp.where(active, tiles, tiles + nk), axis=-1) % nk
    counts = active.sum(axis=-1).astype(jnp.int32)

    grid_spec = pltpu.PrefetchScalarGridSpec(
        num_scalar_prefetch=2, grid=(B, G, nq),
        in_specs=[pl.BlockSpec((tq, J * d), lambda b, g, qi, c, o: (b * nq + qi, qblk + g)),
                  pl.BlockSpec((T, d), lambda b, g, qi, c, o: (b, ksblk + g)),
                  pl.BlockSpec((None, None, nk, d, tk), lambda b, g, qi, c, o: (b, g, 0, 0, 0)),
                  pl.BlockSpec((None, None, LANES, tq), lambda b, g, qi, c, o: (b, g, 0, qi)),
                  pl.BlockSpec((nk, tk, LANES), lambda b, g, qi, c, o: (0, 0, 0)),
                  pl.BlockSpec((None, tk, J * tq), lambda b, g, qi, c, o: (g, 0, 0)),
                  pl.BlockSpec((None, 1, J * tq), lambda b, g, qi, c, o: (g, 0, 0)),
                  pl.BlockSpec((None, 16, tq), lambda b, g, qi, c, o: (g, 0, b * nq + qi)),
                  pl.BlockSpec((None, None, None, d, J * tq), lambda b, g, qi, c, o: (b, g, qi, 0, 0))],
        out_specs=pl.BlockSpec((tq, J * d), lambda b, g, qi, c, o: (b * nq + qi, g)),
        scratch_shapes=[pltpu.VMEM((J * tq, d), BF16), pltpu.VMEM((d, J * tq), F32)])
    return pl.pallas_call(
        functools.partial(_nsa_b_kernel, tk=tk, nk=nk), grid_spec=grid_spec,
        out_shape=jax.ShapeDtypeStruct((B * T, MIX_WIDTH), BF16),
        compiler_params=_cparams(("parallel", "parallel", "parallel")), name="nsa_selected",
    )(counts.reshape(-1), order.reshape(-1).astype(jnp.int32), zm, zm, vs_t, sel_t, expand_t, alibi_t, srow, cg_t, part_t)


def _nsa(zm, zs, cmp_w, cmp_pe, B, T):
    G, J = NSA_KV_GROUPS, NSA_GROUP_SIZE
    n = B * T
    d = HEAD_DIM
    cg = zs[:, 2 * N_HEADS:2 * N_HEADS + 3 * N_HEADS].reshape(n, 3, G, J)
    cg_t = jnp.pad(cg.transpose(2, 1, 3, 0).reshape(G, 3 * J, n), ((0, 0), (0, 16 - 3 * J), (0, 0)))
    vw_col = (KV_BASE + 5 * G) * d
    vw_t = zm[:, vw_col:vw_col + G * d].reshape(B, T, G, d).transpose(0, 2, 3, 1)
    k_cmp, v_cmp_t = _compress(zm, cmp_w, cmp_pe, B, T)
    part_t, sel_t = _nsa_a(zm, k_cmp, v_cmp_t, vw_t, cg_t, B, T)
    return _nsa_b(zm, sel_t, cg_t, part_t, B, T)


def _merge_kernel(h_ref, y0, y1, y2, y3, g0, g1, g2, g3, b0, b1, b2, b3, o_ref):
    h = h_ref[...]
    acc = None
    for y, gw, bw in ((y0, g0, b0), (y1, g1, b1), (y2, g2, b2), (y3, g3, b3)):
        term = jax.nn.sigmoid(_dot(h, gw[...])) * _dot(y[...], bw[...])
        acc = term if acc is None else acc + term
    o_ref[...] = acc.astype(BF16)


def _merge(h, ys, w_gate, w_branch, tm=1024, tn=256):
    n, dm = h.shape
    tm = min(tm, n)
    nj = dm // tn
    once = pl.Buffered(1)
    y_spec = pl.BlockSpec((tm, MIX_WIDTH), lambda i, j: (i, 0), pipeline_mode=once)
    gate_specs = [pl.BlockSpec((dm, tn), functools.partial(lambda i, j, b: (0, b * nj + j), b=b)) for b in range(N_BRANCH)]
    br_specs = [pl.BlockSpec((None, MIX_WIDTH, tn), functools.partial(lambda i, j, b: (b, 0, j), b=b)) for b in range(N_BRANCH)]
    return pl.pallas_call(
        _merge_kernel, grid=(n // tm, nj),
        in_specs=([pl.BlockSpec((tm, dm), lambda i, j: (i, 0), pipeline_mode=once)] + [y_spec] * N_BRANCH
                  + gate_specs + br_specs),
        out_specs=pl.BlockSpec((tm, tn), lambda i, j: (i, j)),
        out_shape=jax.ShapeDtypeStruct((n, dm), BF16),
        compiler_params=_cparams(("parallel", "parallel")), name="gated_merge",
    )(h, *ys, *([w_gate] * N_BRANCH), *([w_branch] * N_BRANCH))


def _pack_bf16_pairs(y):
    half = y.shape[1] // 2
    bits = lax.bitcast_convert_type(y.astype(BF16).astype(F32), jnp.int32)
    return bits[:, :half] | lax.shift_right_logical(bits[:, half:], 16)


def _unpack_bf16_pairs(w):
    first = lax.bitcast_convert_type(w & jnp.int32(-65536), F32).astype(BF16)
    second = lax.bitcast_convert_type(lax.shift_left(w, 16), F32).astype(BF16)
    return first, second


def _expert_kernel(be_ref, nact_ref, idx_ref, idx_next_ref, hp_hbm, w1_ref, w3_ref, w2_ref, o_ref,
                   xbuf, sems, w1b, w3b, w2b):
    i = pl.program_id(0)
    n_act = nact_ref[0]
    slot = lax.rem(i, 2)
    e = be_ref[i]
    prev = be_ref[jnp.maximum(i - 1, 0)]
    half = hp_hbm.shape[1]

    @pl.when((i == 0) & (n_act > 0))
    def _():
        _row_gather(idx_ref, hp_hbm, xbuf.at[0], sems.at[0], MOE_BLOCK)

    @pl.when(i + 1 < n_act)
    def _():
        _row_gather(idx_next_ref, hp_hbm, xbuf.at[1 - slot], sems.at[1 - slot], MOE_BLOCK)

    @pl.when((i == 0) | (prev != e))
    def _():
        w1b[...] = w1_ref[...].astype(BF16)
        w3b[...] = w3_ref[...].astype(BF16)
        w2b[...] = w2_ref[...].astype(BF16)

    @pl.when(i < n_act)
    def _():
        _row_gather_wait(hp_hbm, xbuf.at[slot], sems.at[slot], MOE_BLOCK)
        x_a, x_b = _unpack_bf16_pairs(xbuf[slot])
        a1 = _dot(x_a, w1b[0:half, :]) + _dot(x_b, w1b[half:, :])
        a3 = _dot(x_a, w3b[0:half, :]) + _dot(x_b, w3b[half:, :])
        a = (a1 * jax.nn.sigmoid(a1)) * a3
        o_ref[...] = _dot(a.astype(BF16), w2b[...])

    @pl.when(i >= n_act)
    def _():
        o_ref[...] = jnp.zeros_like(o_ref)


def _experts(hp, tok_pad, blk_exp, n_active, w1, w3, w2, layer):
    half = hp.shape[1]
    dm = 2 * half
    n_blk = tok_pad.shape[0] // MOE_BLOCK
    ff = w1.shape[-1]
    idx = tok_pad.reshape(n_blk, 1, MOE_BLOCK)
    smem_blk = lambda f: pl.BlockSpec((None, 1, MOE_BLOCK), f, memory_space=pltpu.SMEM)
    grid_spec = pltpu.PrefetchScalarGridSpec(
        num_scalar_prefetch=2, grid=(n_blk,),
        in_specs=[smem_blk(lambda i, be, na: (i, 0, 0)),
                  smem_blk(lambda i, be, na: (jnp.minimum(i + 1, n_blk - 1), 0, 0)),
                  pl.BlockSpec(memory_space=pl.ANY),
                  pl.BlockSpec((None, None, dm, ff), lambda i, be, na: (layer, be[i], 0, 0)),
                  pl.BlockSpec((None, None, dm, ff), lambda i, be, na: (layer, be[i], 0, 0)),
                  pl.BlockSpec((None, None, ff, dm), lambda i, be, na: (layer, be[i], 0, 0))],
        out_specs=pl.BlockSpec((MOE_BLOCK, dm), lambda i, be, na: (i, 0)),
        scratch_shapes=[pltpu.VMEM((2, MOE_BLOCK, half), jnp.int32), pltpu.SemaphoreType.DMA((2,)),
                        pltpu.VMEM((dm, ff), BF16), pltpu.VMEM((dm, ff), BF16), pltpu.VMEM((ff, dm), BF16)])
    return pl.pallas_call(
        _expert_kernel, grid_spec=grid_spec,
        out_shape=jax.ShapeDtypeStruct((n_blk * MOE_BLOCK, dm), F32),
        compiler_params=_cparams(("arbitrary",), vmem=MOE_VMEM_LIMIT), name="moe_experts",
    )(blk_exp, n_active, idx, idx, hp, w1, w3, w2)


def _dispatch_plan(e_idx):
    n, K = e_idx.shape
    nk = n * K
    i32 = jnp.int32
    flat_e = e_idx.reshape(-1)
    order = jnp.argsort(flat_e).astype(i32)
    rank = jnp.argsort(order).astype(i32)
    experts = jnp.arange(N_EXPERTS, dtype=i32)
    hot = flat_e[:, None] == experts[None, :]
    counts = jnp.sum(hot.astype(i32), axis=0)
    padded = (counts + MOE_BLOCK - 1) // MOE_BLOCK * MOE_BLOCK
    pad_end = jnp.cumsum(padded)
    pad_start = pad_end - padded
    start = jnp.cumsum(counts) - counts
    lookup = lambda table, onehot: jnp.sum(jnp.where(onehot, table[None, :], 0), axis=1)
    dest = lookup(pad_start - start, hot) + rank
    n_blk = (nk + MOE_BLOCK - 1) // MOE_BLOCK + N_EXPERTS
    blk_first = jnp.arange(n_blk, dtype=i32) * MOE_BLOCK
    blk_exp = jnp.minimum(jnp.sum((pad_end[None, :] <= blk_first[:, None]).astype(i32), axis=1), N_EXPERTS - 1)
    blk_hot = blk_exp[:, None] == experts[None, :]
    within = jnp.arange(MOE_BLOCK, dtype=i32)[None, :]
    j = (blk_first - lookup(pad_start, blk_hot))[:, None] + within
    sorted_pos = jnp.clip(lookup(start, blk_hot)[:, None] + j, 0, nk - 1)
    src = jnp.take(order, sorted_pos.reshape(-1), mode="clip") // K
    tok_pad = jnp.where((j < lookup(counts, blk_hot)[:, None]).reshape(-1), src, n - 1)
    n_active = (pad_end[-1] // MOE_BLOCK).astype(i32).reshape(1)
    return tok_pad, blk_exp.astype(i32), n_active, dest.reshape(n, K)


def _moe(h, route, w1, w3, w2, layer):
    e_idx = route[:, :TOP_K_IN_GROUP].astype(jnp.int32)
    tok_pad, blk_exp, n_active, dest = _dispatch_plan(e_idx)
    y = _experts(h, tok_pad, blk_exp, n_active, w1, w3, w2, layer)
    return y, dest


PACK_TILE = 512


def _pack_kernel(offs_ref, wt_ref, o_ref):
    del offs_ref
    o_ref[...] = wt_ref[0].T.astype(BF16)


def _pack_columns(w_in_t, layer, col_offsets):
    _, _, k = w_in_t.shape
    nt = len(col_offsets)
    grid_spec = pltpu.PrefetchScalarGridSpec(
        num_scalar_prefetch=1, grid=(nt,),
        in_specs=[pl.BlockSpec((pl.Element(1), pl.Element(PACK_TILE), pl.Element(k)),
                               lambda j, offs: (layer, pl.multiple_of(offs[j], 8), 0))],
        out_specs=pl.BlockSpec((k, PACK_TILE), lambda j, offs: (0, j)))
    return pl.pallas_call(
        _pack_kernel, grid_spec=grid_spec,
        out_shape=jax.ShapeDtypeStruct((k, nt * PACK_TILE), BF16),
        compiler_params=_cparams(("parallel",)), name="pack_w_in",
    )(jnp.asarray(col_offsets, jnp.int32), w_in_t)


def _tile_offsets(groups):
    offs = []
    for i in groups:
        assert COL_WIDTHS[i] % PACK_TILE == 0 or COL_WIDTHS[i] < PACK_TILE
        offs += list(range(COL_OFFSETS[i], COL_OFFSETS[i + 1], PACK_TILE))
    return offs


def _pack_w_in(w_in_t, layer):
    wide = [i for i in MAIN_ORDER if COL_WIDTHS[i] >= PACK_TILE]
    narrow = [i for i in MAIN_ORDER if COL_WIDTHS[i] < PACK_TILE]
    assert MAIN_ORDER == tuple(wide + narrow) and narrow == list(range(narrow[0], narrow[-1] + 1))
    kv_lo, kv_hi = COL_OFFSETS[narrow[0]], COL_OFFSETS[narrow[-1] + 1]
    assert (kv_hi - kv_lo) % PACK_TILE == 0
    main_offs = _tile_offsets(wide) + list(range(kv_lo, kv_hi, PACK_TILE))
    w_main = _pack_columns(w_in_t, layer, main_offs)
    w_gate = _pack_columns(w_in_t, layer, _tile_offsets([I_MERGE]))
    rows = lambda i: w_in_t[layer, COL_OFFSETS[i]:COL_OFFSETS[i + 1]]
    small_t = jnp.concatenate([rows(I_AI), rows(I_AF), rows(I_CG)], axis=0)
    small_t = jnp.pad(small_t, ((0, LANES - small_t.shape[0]), (0, 0)))
    return w_main, small_t, w_gate


def _router_weights(wg, bg, we, be):
    w = jnp.concatenate([wg, we], axis=1)
    w = jnp.pad(w, ((0, 0), (0, LANES - w.shape[1])))
    hi = w.astype(BF16)
    lo = (w - hi.astype(F32)).astype(BF16)
    b = jnp.concatenate([bg, be])
    b = jnp.pad(b, (0, LANES - b.shape[0])).reshape(1, LANES).astype(F32)
    return hi, lo, b


def kernel(x, w_in, mlstm_conv, mlstm_gate_bias, mlstm_norm, ret_norm, nsa_cmp_w, nsa_cmp_pe, sgu_norm, sgu_w, sgu_b, w_branch, w_out, norm_mix, norm_ffn, router_group_w, router_group_b, router_expert_w, router_expert_b, expert_w1, expert_w3, expert_w2, norm_final):
    B, T, D = x.shape
    n = B * T
    xs, moe = x.reshape(n, D), None
    w_in_t = jnp.swapaxes(w_in, 1, 2)
    for l in range(DEPTH):
        w_main, w_small_t, w_gate = _pack_w_in(w_in_t, l)
        if moe is None:
            h, = _norm(xs, norm_mix[l])
        else:
            xs, h = _norm(xs, norm_mix[l], moe=moe)
        zm = _matmul(h, w_main, out_dtype=BF16, name="proj_main")
        zs = _matmul(h, w_small_t, out_dtype=F32, w_rows_are_outputs=True, name="proj_small")
        zst = zs[:, :2 * N_HEADS].T
        ya = _mlstm(zm, zs, zst, mlstm_conv[l], mlstm_gate_bias[l], mlstm_norm[l], B, T)
        yb = _retention(zm, ret_norm[l], B, T)
        yc = _nsa(zm, zs, nsa_cmp_w[l], nsa_cmp_pe[l], B, T)
        yd = _sgu(zm, sgu_norm[l], sgu_w[l], sgu_b[l], n)
        merged = _merge(h, (ya, yb, yc, yd), w_gate, w_branch[l].astype(BF16))
        x_mid = _matmul(merged, w_out[l].astype(BF16), out_dtype=F32, residual=xs, name="proj_out")
        h2, route = _norm(x_mid, norm_ffn[l], rows=256,
                          router_w=_router_weights(router_group_w[l], router_group_b[l],
                                                   router_expert_w[l], router_expert_b[l]))
        y_exp, dest = _moe(h2, route, expert_w1, expert_w3, expert_w2, l)
        xs, moe = x_mid, (y_exp, dest, route)
    out, = _norm(xs, norm_final, moe=moe, final=True)
    return out.reshape(B, T, D)
```

```python
import functools

import numpy as np
import jax
import jax.numpy as jnp
from jax import lax
from jax.experimental import pallas as pl
from jax.experimental.pallas import tpu as pltpu

F32 = jnp.float32
BF16 = jnp.bfloat16

D_MODEL = 4096
DEPTH = 2
HEAD_DIM = 128
N_BRANCH = 4
MIX_WIDTH = D_MODEL // N_BRANCH
N_HEADS = MIX_WIDTH // HEAD_DIM
CONV_WIDTH = 4
CONV_TAIL = 16
MLSTM_CHUNK = 128
RET_CHUNK = 128
NSA_KV_GROUPS = 2
NSA_GROUP_SIZE = N_HEADS // NSA_KV_GROUPS
KV_WIDTH = NSA_KV_GROUPS * HEAD_DIM
CMP_BLOCK = 32
CMP_STRIDE = 16
SEL_BLOCK = 64
SEL_TOPN = 16
WINDOW = 512
Q_BLOCK = 128
SGU_CHUNK = 128
N_GROUPS = 8
EXPERTS_PER_GROUP = 8
N_EXPERTS = N_GROUPS * EXPERTS_PER_GROUP
TOP_K_IN_GROUP = 2
EXPERT_FF = 256
MOE_BLOCK = 256
RMS_EPS = 1e-6
NEG_INF = -1e30
FORCE_SCORE = 1e9

COL_WIDTHS = (
    MIX_WIDTH, MIX_WIDTH, MIX_WIDTH, MIX_WIDTH, N_HEADS, N_HEADS,
    MIX_WIDTH, MIX_WIDTH, MIX_WIDTH, MIX_WIDTH,
    MIX_WIDTH, KV_WIDTH, KV_WIDTH, KV_WIDTH, KV_WIDTH, KV_WIDTH, KV_WIDTH, 3 * N_HEADS,
    MIX_WIDTH, MIX_WIDTH,
    N_BRANCH * D_MODEL,
)
COL_OFFSETS = tuple(int(v) for v in np.concatenate([[0], np.cumsum(COL_WIDTHS)]))
(I_AQ, I_AK, I_AV, I_AO, I_AI, I_AF, I_BQ, I_BK, I_BV, I_BG, I_CQ, I_CKC, I_CVC, I_CKS, I_CVS,
 I_CKW, I_CVW, I_CG, I_DU, I_DV, I_MERGE) = range(21)

MAIN_ORDER = (I_AQ, I_AK, I_AV, I_AO, I_BQ, I_BK, I_BV, I_BG, I_CQ, I_DU, I_DV,
              I_CKC, I_CVC, I_CKS, I_CVS, I_CKW, I_CVW)
MAIN_WIDTH = sum(COL_WIDTHS[i] for i in MAIN_ORDER)
LANES = 128
BLK_AQ, BLK_AK, BLK_AV, BLK_AO, BLK_BQ, BLK_BK, BLK_BV, BLK_BG, BLK_CQ, BLK_DU, BLK_DV = range(11)
KV_BASE = 11 * MIX_WIDTH // HEAD_DIM
VMEM_LIMIT = 48 * 1024 * 1024
MOE_VMEM_LIMIT = 56 * 1024 * 1024


def _cparams(sem, vmem=VMEM_LIMIT):
    return pltpu.CompilerParams(dimension_semantics=sem, vmem_limit_bytes=vmem)


def _dot(a, b):
    return jnp.dot(a, b, preferred_element_type=F32)


def _dot_nt(a, b):
    return lax.dot_general(a, b, (((1,), (1,)), ((), ())), preferred_element_type=F32)


def _split3(x):
    hi = x.astype(BF16)
    r1 = x - hi.astype(F32)
    mid = r1.astype(BF16)
    lo = (r1 - mid.astype(F32)).astype(BF16)
    return hi, mid, lo


def _dot_exact_rhs(x, m_bf16):
    hi, mid, lo = _split3(x)
    return _dot(hi, m_bf16) + _dot(mid, m_bf16) + _dot(lo, m_bf16)


def _log_sigmoid(x):
    return jnp.minimum(x, 0.0) - jnp.log1p(jnp.exp(-jnp.abs(x)))


def _row_gather(idx_ref, src_hbm, dst, sem, count):
    for r in range(count):
        pltpu.make_async_copy(src_hbm.at[pl.ds(idx_ref[0, r], 1)], dst.at[pl.ds(r, 1)], sem).start(priority=r % 2)


def _row_gather_wait(src_hbm, dst, sem, count):
    for r in range(count):
        pltpu.make_async_copy(src_hbm.at[pl.ds(0, 1)], dst.at[pl.ds(r, 1)], sem).wait()


def _norm_kernel(*refs, moe, router, final):
    x = refs[0][...]
    rows = x.shape[0]
    pos = 1
    n_scratch = 0
    if moe:
        idx_ref, idx_next_ref, y_hbm, rt_ref = refs[pos:pos + 4]
        pos += 4
        gbuf, sems = refs[-2:]
        n_scratch = 2
        i, last = pl.program_id(0), pl.num_programs(0) - 1
        slot = lax.rem(i, 2)

        @pl.when(i == 0)
        def _():
            _row_gather(idx_ref, y_hbm, gbuf.at[0], sems.at[0], 2 * rows)

        @pl.when(i < last)
        def _():
            _row_gather(idx_next_ref, y_hbm, gbuf.at[1 - slot], sems.at[1 - slot], 2 * rows)

        _row_gather_wait(y_hbm, gbuf.at[slot], sems.at[slot], 2 * rows)
        x = x + (rt_ref[:, 2:3] * gbuf[slot, 0:rows, :] + rt_ref[:, 3:4] * gbuf[slot, rows:2 * rows, :])
    g_ref = refs[pos]
    pos += 1
    if router:
        whi_ref, wlo_ref, rb_ref = refs[pos:pos + 3]
        pos += 3
    outs = refs[pos:len(refs) - n_scratch]
    y = x * lax.rsqrt(jnp.mean(x * x, axis=-1, keepdims=True) + RMS_EPS) * g_ref[...]
    o = 0
    if moe and not final:
        outs[o][...] = x
        o += 1
    if final:
        outs[o][...] = y
    elif router:
        outs[o][...] = _pack_bf16_pairs(y)
    else:
        outs[o][...] = y.astype(BF16)
    o += 1
    if router:
        y_hi = y.astype(BF16)
        y_lo = (y - y_hi.astype(F32)).astype(BF16)
        lg = _dot(y_hi, whi_ref[...]) + (_dot(y_hi, wlo_ref[...]) + _dot(y_lo, whi_ref[...])) + rb_ref[...]
        outs[o][...] = _route_rows(lg)


def _first_max(v, lane):
    mx = jnp.max(v, axis=-1, keepdims=True)
    return mx, jnp.min(jnp.where(v == mx, lane, LANES), axis=-1, keepdims=True)


def _route_rows(lg):
    low = -3.0e38
    lane = lax.broadcasted_iota(jnp.int32, lg.shape, 1)
    is_grp = lane < N_GROUPS
    gmax, g_idx = _first_max(jnp.where(is_grp, lg, low), lane)
    g_w = 1.0 / jnp.sum(jnp.where(is_grp, jnp.exp(lg - gmax), 0.0), axis=-1, keepdims=True)
    assert EXPERTS_PER_GROUP & (EXPERTS_PER_GROUP - 1) == 0
    grp_of_lane = lax.shift_right_arithmetic(lane - N_GROUPS, EXPERTS_PER_GROUP.bit_length() - 1)
    in_grp = (lane >= N_GROUPS) & (lane < N_GROUPS + N_EXPERTS) & (grp_of_lane == g_idx)
    el = jnp.where(in_grp, lg, low)
    v0, i0 = _first_max(el, lane)
    v1, i1 = _first_max(jnp.where(lane == i0, low, el), lane)
    e1 = jnp.exp(v1 - v0)
    p0 = g_w / (1.0 + e1)
    vals = (i0 - N_GROUPS).astype(F32), (i1 - N_GROUPS).astype(F32), p0, p0 * e1
    out = jnp.zeros(lg.shape, F32)
    for k, v in enumerate(vals):
        out = jnp.where(lane == k, v, out)
    return out


def _norm(x, gain, *, moe=None, router_w=None, final=False, rows=128):
    n, d = x.shape
    rows = min(rows, n)
    steps = n // rows
    row_spec = pl.BlockSpec((rows, d), lambda i: (i, 0))
    lane_spec = pl.BlockSpec((rows, LANES), lambda i: (i, 0))
    in_specs, args, scratch = [row_spec], [x], []
    if moe is not None:
        y, dest, route = moe
        K = dest.shape[1]
        idx = dest.reshape(steps, rows, K).transpose(0, 2, 1).reshape(steps, 1, K * rows)
        smem_blk = lambda f: pl.BlockSpec((None, 1, K * rows), f, memory_space=pltpu.SMEM)
        in_specs += [smem_blk(lambda i: (i, 0, 0)), smem_blk(lambda i: (jnp.minimum(i + 1, steps - 1), 0, 0)),
                     pl.BlockSpec(memory_space=pl.ANY), lane_spec]
        args += [idx, idx, y, route]
        scratch = [pltpu.VMEM((2, K * rows, d), F32), pltpu.SemaphoreType.DMA((2,))]
    in_specs.append(pl.BlockSpec((1, d), lambda i: (0, 0)))
    args.append(gain.reshape(1, d))
    out_shape, out_specs = [], []
    if moe is not None and not final:
        out_shape.append(jax.ShapeDtypeStruct((n, d), F32))
        out_specs.append(row_spec)
    if router_w is not None:
        out_shape.append(jax.ShapeDtypeStruct((n, d // 2), jnp.int32))
        out_specs.append(pl.BlockSpec((rows, d // 2), lambda i: (i, 0)))
    else:
        out_shape.append(jax.ShapeDtypeStruct((n, d), F32 if final else BF16))
        out_specs.append(row_spec)
    if router_w is not None:
        whi, wlo, rb = router_w
        in_specs += [pl.BlockSpec((d, LANES), lambda i: (0, 0))] * 2 + [pl.BlockSpec((1, LANES), lambda i: (0, 0))]
        args += [whi, wlo, rb]
        out_shape.append(jax.ShapeDtypeStruct((n, LANES), F32))
        out_specs.append(lane_spec)
    return pl.pallas_call(
        functools.partial(_norm_kernel, moe=moe is not None, router=router_w is not None, final=final),
        grid=(steps,), in_specs=in_specs, out_specs=out_specs, out_shape=out_shape, scratch_shapes=scratch,
        compiler_params=_cparams(("arbitrary",)), name="rmsnorm",
    )(*args)


def _mm_kernel(a_ref, w_ref, *rest, has_res, w_rows_are_outputs):
    w = w_ref[...].astype(BF16)
    acc = _dot_nt(a_ref[...], w) if w_rows_are_outputs else _dot(a_ref[...], w)
    if has_res:
        acc = acc + rest[0][...]
    rest[-1][...] = acc.astype(rest[-1].dtype)


def _matmul(a, w, *, out_dtype, residual=None, tm=1024, tn=512, w_rows_are_outputs=False, name="matmul"):
    m, k = a.shape
    nc = w.shape[0] if w_rows_are_outputs else w.shape[1]
    tm, tn = min(tm, m), min(tn, nc)
    w_spec = pl.BlockSpec((tn, k), lambda i, j: (j, 0)) if w_rows_are_outputs else pl.BlockSpec((k, tn), lambda i, j: (0, j))
    in_specs = [pl.BlockSpec((tm, k), lambda i, j: (i, 0)), w_spec]
    args = [a, w]
    if residual is not None:
        in_specs.append(pl.BlockSpec((tm, tn), lambda i, j: (i, j)))
        args.append(residual)
    return pl.pallas_call(
        functools.partial(_mm_kernel, has_res=residual is not None, w_rows_are_outputs=w_rows_are_outputs),
        grid=(m // tm, nc // tn), in_specs=in_specs,
        out_specs=pl.BlockSpec((tm, tn), lambda i, j: (i, j)),
        out_shape=jax.ShapeDtypeStruct((m, nc), out_dtype),
        compiler_params=_cparams(("parallel", "parallel")), name=name,
    )(*args)


def _mlstm_kernel(aq_ref, ak_ref, av_ref, ao_ref, zs_ref, zst_ref, cw_ref, gbc_ref, gbr_ref, gain_ref,
                  tril_ref, triu_ref, shift_ref, o_ref, tail_s, qk_s, ct_s, n_s, m_s):
    L, d, H, W = MLSTM_CHUNK, HEAD_DIM, N_HEADS, MIX_WIDTH
    c = pl.program_id(1)

    @pl.when(c == 0)
    def _():
        tail_s[...] = jnp.zeros_like(tail_s)
        ct_s[...] = jnp.zeros_like(ct_s)
        n_s[...] = jnp.zeros_like(n_s)
        m_s[...] = jnp.zeros_like(m_s)

    x_cur = jnp.concatenate([aq_ref[...], ak_ref[...]], axis=1)
    x_ext = jnp.concatenate([tail_s[...], x_cur], axis=0)
    conv = cw_ref[CONV_WIDTH - 1:CONV_WIDTH, :] * x_cur.astype(F32)
    for j in range(CONV_WIDTH - 1):
        conv = conv + cw_ref[j:j + 1, :] * _dot(shift_ref[j], x_ext)
    tail_s[...] = x_cur[L - CONV_TAIL:, :]
    qk_s[...] = conv * jax.nn.sigmoid(conv)

    pre_c = zs_ref[...] + gbc_ref[...]
    ls_c = _log_sigmoid(pre_c)
    hi, mid, lo = _split3(ls_c)
    tril = tril_ref[...]
    bc = _dot(tril, hi) + _dot(tril, mid) + _dot(tril, lo)
    pre_r = zst_ref[...] + gbr_ref[...]
    br = _dot_exact_rhs(_log_sigmoid(pre_r), triu_ref[...])

    row = lax.broadcasted_iota(jnp.int32, (L, L), 0)
    col = lax.broadcasted_iota(jnp.int32, (L, L), 1)
    causal = row >= col
    scale = d ** -0.5
    for h in range(H):
        sl = slice(h * d, (h + 1) * d)
        q = qk_s[:, sl] * scale
        k = qk_s[:, W + h * d:W + (h + 1) * d]
        v = av_ref[:, sl]
        i_col, b_col = pre_c[:, h:h + 1], bc[:, H + h:H + h + 1]
        i_row, b_row = pre_r[h:h + 1, :], br[H + h:H + h + 1, :]
        m_prev = m_s[h:h + 1, 0:1]
        log_d = jnp.where(causal, b_col - b_row + i_row, NEG_INF)
        log_inter = b_col + m_prev
        m_row = jnp.maximum(jnp.max(log_d, axis=-1, keepdims=True), log_inter)
        qb, kb = q.astype(BF16), k.astype(BF16)
        s = _dot_nt(qb, kb) * jnp.exp(log_d - m_row)
        w_inter = jnp.exp(log_inter - m_row)
        num = _dot(s.astype(BF16), v) + w_inter * _dot(qb, ct_s[h].astype(BF16))
        den = jnp.sum(s, axis=-1, keepdims=True) + w_inter * jnp.sum(q * n_s[h:h + 1, :], axis=-1, keepdims=True)
        hh = num / jnp.maximum(jnp.abs(den), jnp.exp(-m_row))
        b_last = b_row[:, L - 1:L]
        log_w_row = b_last - b_row + i_row
        m_new = jnp.maximum(b_last + m_prev, jnp.max(log_w_row, axis=-1, keepdims=True))
        w_col = jnp.exp(b_last - b_col + i_col - m_new)
        decay = jnp.exp(b_last + m_prev - m_new)
        vw = (v.astype(F32) * w_col).astype(BF16)
        ct_s[h] = decay * ct_s[h] + _dot(k.T.astype(BF16), vw)
        n_s[h:h + 1, :] = decay * n_s[h:h + 1, :] + jnp.sum(k * w_col, axis=0, keepdims=True)
        m_s[h:h + 1, :] = jnp.broadcast_to(m_new, (1, LANES))
        hn = hh * lax.rsqrt(jnp.mean(hh * hh, axis=-1, keepdims=True) + RMS_EPS)
        o_ref[:, sl] = (hn * gain_ref[:, sl] * jax.nn.sigmoid(ao_ref[:, sl].astype(F32))).astype(BF16)


def _tri_consts(L):
    r = np.arange(L)
    tril = (r[:, None] >= r[None, :]).astype(np.float32)
    return jnp.asarray(tril, BF16), jnp.asarray(tril.T, BF16)


def _mlstm(zm, zs, zst, conv_w, gate_bias, gain, B, T):
    L, W = MLSTM_CHUNK, MIX_WIDTH
    nc = T // L
    n = B * T
    gbc = jnp.zeros((1, LANES), F32).at[0, :2 * N_HEADS].set(gate_bias.reshape(-1))
    gbr = jnp.broadcast_to(gate_bias.reshape(2 * N_HEADS, 1), (2 * N_HEADS, L)).astype(F32)
    tril, triu = _tri_consts(L)
    t_idx = np.arange(L)[:, None]
    c_idx = np.arange(CONV_TAIL + L)[None, :]
    shift = jnp.asarray(np.stack([(c_idx == CONV_TAIL + t_idx - (CONV_WIDTH - 1 - j)) for j in range(CONV_WIDTH - 1)])
                        .astype(np.float32), BF16)

    def blk(b_idx):
        return pl.BlockSpec((L, W), lambda b, c: (b * nc + c, b_idx))

    const = lambda shape: pl.BlockSpec(shape, lambda b, c: (0,) * len(shape))
    return pl.pallas_call(
        _mlstm_kernel, grid=(B, nc),
        in_specs=[blk(BLK_AQ), blk(BLK_AK), blk(BLK_AV), blk(BLK_AO),
                  pl.BlockSpec((L, LANES), lambda b, c: (b * nc + c, 0)),
                  pl.BlockSpec((2 * N_HEADS, L), lambda b, c: (0, b * nc + c)),
                  const((CONV_WIDTH, 2 * W)), const((1, LANES)), const((2 * N_HEADS, L)), const((1, W)),
                  const((L, L)), const((L, L)), const((CONV_WIDTH - 1, L, CONV_TAIL + L))],
        out_specs=pl.BlockSpec((L, W), lambda b, c: (b * nc + c, 0)),
        out_shape=jax.ShapeDtypeStruct((n, W), BF16),
        scratch_shapes=[pltpu.VMEM((CONV_TAIL, 2 * W), BF16), pltpu.VMEM((L, 2 * W), F32),
                        pltpu.VMEM((N_HEADS, HEAD_DIM, HEAD_DIM), F32), pltpu.VMEM((N_HEADS, HEAD_DIM), F32),
                        pltpu.VMEM((N_HEADS, LANES), F32)],
        compiler_params=_cparams(("parallel", "arbitrary")), name="mlstm",
    )(zm, zm, zm, zm, zs, zst, conv_w, gbc, gbr, gain.reshape(1, W), tril, triu, shift)


def _ret_kernel(q_ref, k_ref, v_ref, g_ref, dec_ref, qd_ref, kd_ref, gain_ref, o_ref, r_s, *, chunk_decay):
    L, d, H = RET_CHUNK, HEAD_DIM, N_HEADS
    c = pl.program_id(1)

    @pl.when(c == 0)
    def _():
        r_s[...] = jnp.zeros_like(r_s)

    scale = d ** -0.5
    for h in range(H):
        sl = slice(h * d, (h + 1) * d)
        q = q_ref[:, sl].astype(F32)
        k = k_ref[:, sl].astype(F32) * scale
        v = v_ref[:, sl]
        s = _dot_nt(q.astype(BF16), k.astype(BF16)) * dec_ref[h]
        o = _dot(s.astype(BF16), v) + _dot((q * qd_ref[:, sl]).astype(BF16), r_s[h].astype(BF16))
        kd = (k * kd_ref[:, sl]).T.astype(BF16)
        r_s[h] = chunk_decay[h] * r_s[h] + _dot(kd, v)
        on = o * lax.rsqrt(jnp.mean(o * o, axis=-1, keepdims=True) + RMS_EPS)
        gate = g_ref[:, sl].astype(F32)
        o_ref[:, sl] = (on * gain_ref[:, sl] * (gate * jax.nn.sigmoid(gate))).astype(BF16)


def _retention(zm, gain, B, T):
    L, W, H, d = RET_CHUNK, MIX_WIDTH, N_HEADS, HEAD_DIM
    nc = T // L
    log_gamma = np.log(np.float32(1.0) - np.float32(2.0) ** (-5.0 - np.arange(H, dtype=np.float32))).astype(np.float32)
    pos = np.arange(L, dtype=np.float32)
    diff = pos[:, None] - pos[None, :]
    decay = np.where(diff >= 0, np.exp(log_gamma[:, None, None] * np.maximum(diff, 0.0)), 0.0).astype(np.float32)
    q_decay = np.exp(log_gamma[:, None] * (pos + 1.0)).astype(np.float32)
    k_decay = np.exp(log_gamma[:, None] * (L - 1.0 - pos)).astype(np.float32)
    chunk_decay = tuple(float(v) for v in np.exp(log_gamma * np.float32(L)).astype(np.float32))
    qd = jnp.asarray(np.repeat(q_decay.T, d, axis=1))
    kd = jnp.asarray(np.repeat(k_decay.T, d, axis=1))

    def blk(b_idx):
        return pl.BlockSpec((L, W), lambda b, c: (b * nc + c, b_idx))

    const = lambda shape: pl.BlockSpec(shape, lambda b, c: (0,) * len(shape))
    return pl.pallas_call(
        functools.partial(_ret_kernel, chunk_decay=chunk_decay), grid=(B, nc),
        in_specs=[blk(BLK_BQ), blk(BLK_BK), blk(BLK_BV), blk(BLK_BG),
                  const((H, L, L)), const((L, W)), const((L, W)), const((1, W))],
        out_specs=pl.BlockSpec((L, W), lambda b, c: (b * nc + c, 0)),
        out_shape=jax.ShapeDtypeStruct((B * T, W), BF16),
        scratch_shapes=[pltpu.VMEM((H, d, d), F32)],
        compiler_params=_cparams(("parallel", "arbitrary")), name="retention",
    )(zm, zm, zm, zm, jnp.asarray(decay), qd, kd, gain.reshape(1, W))


def _gelu(x):
    return 0.5 * x * (1.0 + jnp.tanh(0.7978845608028654 * (x + 0.044715 * (x * x * x))))


def _sgu_kernel(u_ref, v_ref, ng_ref, w_ref, b_ref, o_ref):
    L, d, H = SGU_CHUNK, HEAD_DIM, N_HEADS
    v = _gelu(v_ref[...].astype(F32))
    vn = (v * lax.rsqrt(jnp.mean(v * v, axis=-1, keepdims=True) + RMS_EPS) * ng_ref[...]).astype(BF16)
    u = _gelu(u_ref[...].astype(F32))
    row = lax.broadcasted_iota(jnp.int32, (L, L), 0)
    col = lax.broadcasted_iota(jnp.int32, (L, L), 1)
    for g in range(H):
        sl = slice(g * d, (g + 1) * d)
        wm = jnp.where(row >= col, w_ref[g], 0.0).astype(BF16)
        mixed = _dot(wm, vn[:, sl]) + b_ref[:, sl]
        o_ref[:, sl] = (u[:, sl] * mixed).astype(BF16)


def _sgu(zm, norm_g, w_s, b_s, n):
    L, W, H, d = SGU_CHUNK, MIX_WIDTH, N_HEADS, HEAD_DIM
    bsb = jnp.repeat(b_s.T, d, axis=1)
    const = lambda shape: pl.BlockSpec(shape, lambda i: (0,) * len(shape))
    return pl.pallas_call(
        _sgu_kernel, grid=(n // L,),
        in_specs=[pl.BlockSpec((L, W), lambda i: (i, BLK_DU)), pl.BlockSpec((L, W), lambda i: (i, BLK_DV)),
                  const((1, W)), const((H, L, L)), const((L, W))],
        out_specs=pl.BlockSpec((L, W), lambda i: (i, 0)),
        out_shape=jax.ShapeDtypeStruct((n, W), BF16),
        compiler_params=_cparams(("parallel",)), name="sgu",
    )(zm, zm, norm_g.reshape(1, W), w_s, bsb)


def _compress_kernel(k_ref, v_ref, w_ref, pe_ref, ko_ref, vo_ref, buf):
    t = k_ref.shape[0]
    n16 = t // CMP_STRIDE
    buf[t:t + CMP_STRIDE, :] = jnp.zeros((CMP_STRIDE, HEAD_DIM), F32)
    for which, (src, dst) in enumerate(((k_ref, ko_ref), (v_ref, vo_ref))):
        buf[0:t, :] = src[...].astype(F32)
        acc = jnp.zeros((n16, HEAD_DIM), F32)
        for l in range(CMP_BLOCK):
            x = buf[pl.ds(l, n16, stride=CMP_STRIDE), :]
            acc = acc + _dot((x + pe_ref[which, l:l + 1, :]).astype(BF16), w_ref[which, l].astype(BF16))
        dst[...] = (acc.T if which else acc).astype(BF16)


def _compress(zm, cmp_w, cmp_pe, B, T):
    G, d = NSA_KV_GROUPS, HEAD_DIM
    n16 = T // CMP_STRIDE
    out = jax.ShapeDtypeStruct((B, G, n16, d), BF16)
    kblk = KV_BASE
    vblk = KV_BASE + G
    return pl.pallas_call(
        _compress_kernel, grid=(B, G),
        in_specs=[pl.BlockSpec((T, d), lambda b, g: (b, kblk + g)), pl.BlockSpec((T, d), lambda b, g: (b, vblk + g)),
                  pl.BlockSpec((2, CMP_BLOCK, d, d), lambda b, g: (0, 0, 0, 0)),
                  pl.BlockSpec((2, CMP_BLOCK, d), lambda b, g: (0, 0, 0))],
        out_specs=[pl.BlockSpec((None, None, n16, d), lambda b, g: (b, g, 0, 0)),
                   pl.BlockSpec((None, None, d, n16), lambda b, g: (b, g, 0, 0))],
        out_shape=[out, jax.ShapeDtypeStruct((B, G, d, n16), BF16)],
        scratch_shapes=[pltpu.VMEM((T + CMP_STRIDE, d), F32)],
        compiler_params=_cparams(("parallel", "parallel")), name="nsa_compress",
    )(zm, zm, cmp_w, cmp_pe)


RANK_CHUNK = 8


def _lanes_x(x, times):
    return jnp.concatenate([x] * times, axis=1)


def _nsa_a_kernel(q_ref, kc_ref, vct_ref, *rest, n_back):
    nw = n_back + 1
    kw_refs, vwt_refs = rest[:nw], rest[nw:2 * nw]
    cgt_ref, ovt_ref, acmp_ref, awin_ref, part_ref, sel_ref, q_s, key_s, rank_s = rest[2 * nw:]
    tq, d, J = Q_BLOCK, HEAD_DIM, NSA_GROUP_SIZE
    qi = pl.program_id(2)
    t0 = qi * tq
    ncmp = kc_ref.shape[0]
    scale = d ** -0.5
    for j in range(J):
        q_s[j * tq:(j + 1) * tq, :] = (q_ref[:, j * d:(j + 1) * d].astype(F32) * scale).astype(BF16)
    sg_t = jax.nn.sigmoid(cgt_ref[...])
    t_row = t0 + lax.broadcasted_iota(jnp.int32, (1, tq), 1)

    n_i = lax.broadcasted_iota(jnp.int32, (ncmp, tq), 0)
    r_i = lax.broadcasted_iota(jnp.int32, (ncmp, tq), 1)
    bias_c = jnp.where(t0 + r_i - CMP_STRIDE * n_i - (CMP_BLOCK - 1) >= 0, 0.0, NEG_INF)
    s = _dot_nt(kc_ref[...], q_s[...]) - acmp_ref[...] + _lanes_x(bias_c, J)
    p = jnp.exp(s - jnp.max(s, axis=0, keepdims=True))
    has_c = jnp.where(t_row >= CMP_BLOCK - 1, 1.0, 0.0)
    p = p * (_lanes_x(has_c, J) / jnp.sum(p, axis=0, keepdims=True))
    o_cmp_t = _dot(vct_ref[...], p.astype(BF16))
    psum = p[:, 0:tq]
    for j in range(1, J):
        psum = psum + p[:, j * tq:(j + 1) * tq]

    hi, mid, lo = _split3(psum)
    ovt = ovt_ref[...]
    imp_t = _dot(ovt, hi) + _dot(ovt, mid) + _dot(ovt, lo)
    blk_t = lax.broadcasted_iota(jnp.int32, (LANES, tq), 0)
    cur_t = lax.shift_right_logical(t0 + lax.broadcasted_iota(jnp.int32, (LANES, tq), 1), SEL_BLOCK.bit_length() - 1)
    valid_t = blk_t <= cur_t
    forced = (blk_t == 0) | (blk_t == cur_t) | (blk_t == cur_t - 1)
    key_s[...] = jnp.where(valid_t, imp_t + jnp.where(forced, FORCE_SCORE, 0.0), NEG_INF)
    rank_s[...] = jnp.zeros_like(rank_s)
    sub = lax.broadcasted_iota(jnp.int32, (8, tq), 0)
    groups = LANES // 8
    last_valid = lax.shift_right_logical(t0 + tq - 1, SEL_BLOCK.bit_length() - 1)
    for chunk in range(LANES // RANK_CHUNK):
        @pl.when(chunk * RANK_CHUNK <= last_valid)
        def _(chunk=chunk):
            key_g = [key_s[8 * i:8 * i + 8, :] for i in range(groups)]
            rank = [rank_s[8 * i:8 * i + 8, :] for i in range(groups)]
            for mp in range(chunk * RANK_CHUNK, (chunk + 1) * RANK_CHUNK):
                rowv = jnp.broadcast_to(key_s[mp:mp + 1, :], (8, tq))
                for i in range(groups):
                    if 8 * i > mp:
                        ahead = rowv >= key_g[i]
                    elif 8 * i + 7 <= mp:
                        ahead = rowv > key_g[i]
                    else:
                        ahead = (rowv > key_g[i]) | ((sub > mp - 8 * i) & (rowv == key_g[i]))
                    rank[i] = rank[i] + jnp.where(ahead, 1.0, 0.0)
            for i in range(groups):
                rank_s[8 * i:8 * i + 8, :] = rank[i]
    sel_ref[...] = jnp.where((rank_s[...] < float(SEL_TOPN)) & valid_t, 1.0, 0.0).astype(BF16)

    span = nw * tq
    kw = jnp.concatenate([kr[...] for kr in kw_refs], axis=0)
    vw_t = jnp.concatenate([vr[...] for vr in vwt_refs], axis=1)
    c_i = lax.broadcasted_iota(jnp.int32, (span, tq), 0)
    bias_w = jnp.where(t0 - n_back * tq + c_i >= 0, 0.0, NEG_INF)
    s = _dot_nt(kw, q_s[...]) - awin_ref[...] + _lanes_x(bias_w, J)
    p = jnp.exp(s - jnp.max(s, axis=0, keepdims=True))
    inv_l = 1.0 / jnp.sum(p, axis=0, keepdims=True)
    o_win_t = _dot(vw_t, p.astype(BF16))
    g_cmp = jnp.concatenate([sg_t[j:j + 1, :] for j in range(J)], axis=1)
    g_win = jnp.concatenate([sg_t[2 * J + j:2 * J + j + 1, :] for j in range(J)], axis=1)
    part_ref[...] = g_cmp * o_cmp_t + (g_win * inv_l) * o_win_t


def _alibi_slopes():
    return (2.0 ** (-8.0 * (np.arange(N_HEADS, dtype=np.float32) + 1.0) / N_HEADS)).astype(np.float32)


def _nsa_consts(T):
    G, J, tq = NSA_KV_GROUPS, NSA_GROUP_SIZE, Q_BLOCK
    n16 = T // CMP_STRIDE
    n_sel = T // SEL_BLOCK
    n = np.arange(n16)[None, :]
    m = np.arange(LANES)[:, None]
    c_start, s_start = n * CMP_STRIDE, m * SEL_BLOCK
    overlap_t = ((c_start < s_start + SEL_BLOCK) & (c_start + CMP_BLOCK > s_start) & (n < n16 - 1) & (m < n_sel))
    slopes = _alibi_slopes().reshape(G, 1, J, 1)
    r = np.arange(tq, dtype=np.float32)[None, None, None, :]
    dist_c = r - CMP_STRIDE * np.arange(n16, dtype=np.float32)[None, :, None, None] - (CMP_BLOCK - 1)
    a_cmp = (slopes * dist_c).reshape(G, n16, J * tq)
    span = WINDOW + tq
    dist_w = r + WINDOW - np.arange(span, dtype=np.float32)[None, :, None, None]
    a_win = np.where((dist_w >= 0) & (dist_w < WINDOW), slopes * dist_w, -NEG_INF).reshape(G, span, J * tq)
    return (jnp.asarray(overlap_t.astype(np.float32), BF16), jnp.asarray(a_cmp.astype(np.float32)),
            jnp.asarray(a_win.astype(np.float32)))


def _nsa_a(zm, k_cmp, v_cmp_t, vw_t, cg_t, B, T):
    G, J, d, tq = NSA_KV_GROUPS, NSA_GROUP_SIZE, HEAD_DIM, Q_BLOCK
    nq = T // tq
    n16 = T // CMP_STRIDE
    n_back = WINDOW // tq
    span = WINDOW + tq
    qblk = BLK_CQ * (MIX_WIDTH // (J * d))
    kwblk = KV_BASE + 4 * G
    overlap_t, a_cmp, a_win = _nsa_consts(T)

    def past(qi, i):
        return jnp.maximum(qi - n_back + i, 0)

    in_specs = ([pl.BlockSpec((tq, J * d), lambda b, g, qi: (b * nq + qi, qblk + g)),
                 pl.BlockSpec((None, None, n16, d), lambda b, g, qi: (b, g, 0, 0)),
                 pl.BlockSpec((None, None, d, n16), lambda b, g, qi: (b, g, 0, 0))]
                + [pl.BlockSpec((tq, d), functools.partial(lambda b, g, qi, i: (b * nq + past(qi, i), kwblk + g), i=i))
                   for i in range(n_back + 1)]
                + [pl.BlockSpec((None, None, d, tq), functools.partial(lambda b, g, qi, i: (b, g, 0, past(qi, i)), i=i))
                   for i in range(n_back + 1)]
                + [pl.BlockSpec((None, 16, tq), lambda b, g, qi: (g, 0, b * nq + qi)),
                   pl.BlockSpec((LANES, n16), lambda b, g, qi: (0, 0)),
                   pl.BlockSpec((None, n16, J * tq), lambda b, g, qi: (g, 0, 0)),
                   pl.BlockSpec((None, span, J * tq), lambda b, g, qi: (g, 0, 0))])
    return pl.pallas_call(
        functools.partial(_nsa_a_kernel, n_back=n_back), grid=(B, G, nq), in_specs=in_specs,
        out_specs=[pl.BlockSpec((None, None, None, d, J * tq), lambda b, g, qi: (b, g, qi, 0, 0)),
                   pl.BlockSpec((None, None, LANES, tq), lambda b, g, qi: (b, g, 0, qi))],
        out_shape=[jax.ShapeDtypeStruct((B, G, nq, d, J * tq), F32), jax.ShapeDtypeStruct((B, G, LANES, T), BF16)],
        scratch_shapes=[pltpu.VMEM((J * tq, d), BF16), pltpu.VMEM((LANES, tq), F32), pltpu.VMEM((LANES, tq), F32)],
        compiler_params=_cparams(("parallel", "parallel", "parallel")), name="nsa_cmp_win",
    )(zm, k_cmp, v_cmp_t, *([zm] * (n_back + 1)), *([vw_t] * (n_back + 1)), cg_t, overlap_t, a_cmp, a_win)


def _nsa_b_kernel(cnt_ref, lst_ref, q_ref, ks_ref, vt_ref, sel_ref, e_ref, a_ref, srow_ref, cgt_ref, part_ref, o_ref,
                  q_s, acc_s, *, tk, nk):
    tq, d, J = Q_BLOCK, HEAD_DIM, NSA_GROUP_SIZE
    b, g, qi = pl.program_id(0), pl.program_id(1), pl.program_id(2)
    lin = (b * pl.num_programs(1) + g) * pl.num_programs(2) + qi
    t0 = qi * tq
    scale = d ** -0.5
    for j in range(J):
        q_s[j * tq:(j + 1) * tq, :] = (q_ref[:, j * d:(j + 1) * d].astype(F32) * scale).astype(BF16)
    acc_s[...] = jnp.zeros_like(acc_s)
    r_minus_c = lax.broadcasted_iota(jnp.int32, (tk, tq), 1) - lax.broadcasted_iota(jnp.int32, (tk, tq), 0)

    count = cnt_ref[lin]

    def scores(pos):
        kt = lst_ref[lin * nk + pos]
        k_t = ks_ref[pl.ds(pl.multiple_of(kt * tk, tk), tk), :]
        off = t0 - kt * tk
        picked = _dot(e_ref[kt], sel_ref[...])
        causal_off = jnp.where(pos < count, off, -tk * nk)
        keep = (picked > 0.5) & (r_minus_c + causal_off >= 0)
        s = _dot_nt(k_t, q_s[...]) - a_ref[...] + _lanes_x(jnp.where(keep, 0.0, NEG_INF), J)
        return s, srow_ref[...] * off.astype(F32), kt

    def body(i, carry):
        m_prev, l_prev = carry
        s_a, shift_a, kt_a = scores(2 * i)
        s_b, shift_b, kt_b = scores(2 * i + 1)
        m_new = jnp.maximum(m_prev, jnp.maximum(jnp.max(s_a, axis=0, keepdims=True) - shift_a,
                                                jnp.max(s_b, axis=0, keepdims=True) - shift_b))
        p_a = jnp.exp(s_a - (m_new + shift_a))
        p_b = jnp.exp(s_b - (m_new + shift_b))
        alpha = jnp.exp(m_prev - m_new)
        acc_s[...] = alpha * acc_s[...] + (_dot(vt_ref[kt_a], p_a.astype(BF16)) + _dot(vt_ref[kt_b], p_b.astype(BF16)))
        return m_new, alpha * l_prev + (jnp.sum(p_a, axis=0, keepdims=True) + jnp.sum(p_b, axis=0, keepdims=True))

    init = (jnp.full((1, J * tq), NEG_INF, F32), jnp.zeros((1, J * tq), F32))
    _, l = lax.fori_loop(0, (count + 1) // 2, body, init)
    sg_t = jax.nn.sigmoid(cgt_ref[...])
    g_sel = jnp.concatenate([sg_t[J + j:J + j + 1, :] for j in range(J)], axis=1)
    total_t = part_ref[...] + (g_sel / l) * acc_s[...]
    for j in range(J):
        o_ref[:, j * d:(j + 1) * d] = total_t[:, j * tq:(j + 1) * tq].T.astype(BF16)


def _nsa_b(zm, sel_t, cg_t, part_t, B, T, tk=512):
    G, J, d, tq = NSA_KV_GROUPS, NSA_GROUP_SIZE, HEAD_DIM, Q_BLOCK
    tk = min(tk, T)
    nq, nk = T // tq, T // tk
    qblk = BLK_CQ * (MIX_WIDTH // (J * d))
    ksblk = KV_BASE + 2 * G
    vs_col = (KV_BASE + 3 * G) * d
    per_tile = tk // SEL_BLOCK
    kt = np.arange(nk)[:, None, None]
    cc = np.arange(tk)[None, :, None]
    m = np.arange(LANES)[None, None, :]
    expand_t = jnp.asarray((m == kt * per_tile + cc // SEL_BLOCK).astype(np.float32), BF16)
    slopes_np = _alibi_slopes()
    r_minus_c = (np.arange(tq)[None, :] - np.arange(tk)[:, None]).astype(np.float32)
    alibi_t = jnp.asarray((slopes_np.reshape(G, 1, J, 1) * r_minus_c[None, :, None, :]).reshape(G, tk, J * tq))
    srow = jnp.asarray(np.repeat(slopes_np.reshape(G, J), tq, axis=1).reshape(G, 1, J * tq))
    vs_t = zm[:, vs_col:vs_col + G * d].reshape(B, nk, tk, G, d).transpose(0, 3, 1, 4, 2)

    active = sel_t[:, :, :nk * per_tile].reshape(B, G, nk, per_tile, nq, tq).max(axis=(3, 5)) > 0
    active = active.transpose(0, 1, 3, 2)
    tiles = jnp.arange(nk, dtype=jnp.int32)
    order = jnp.sort(jnp.where(active, tiles, tiles + nk), axis=-1) % nk
    counts = active.sum(axis=-1).astype(jnp.int32)

    grid_spec = pltpu.PrefetchScalarGridSpec(
        num_scalar_prefetch=2, grid=(B, G, nq),
        in_specs=[pl.BlockSpec((tq, J * d), lambda b, g, qi, c, o: (b * nq + qi, qblk + g)),
                  pl.BlockSpec((T, d), lambda b, g, qi, c, o: (b, ksblk + g)),
                  pl.BlockSpec((None, None, nk, d, tk), lambda b, g, qi, c, o: (b, g, 0, 0, 0)),
                  pl.BlockSpec((None, None, LANES, tq), lambda b, g, qi, c, o: (b, g, 0, qi)),
                  pl.BlockSpec((nk, tk, LANES), lambda b, g, qi, c, o: (0, 0, 0)),
                  pl.BlockSpec((None, tk, J * tq), lambda b, g, qi, c, o: (g, 0, 0)),
                  pl.BlockSpec((None, 1, J * tq), lambda b, g, qi, c, o: (g, 0, 0)),
                  pl.BlockSpec((None, 16, tq), lambda b, g, qi, c, o: (g, 0, b * nq + qi)),
                  pl.BlockSpec((None, None, None, d, J * tq), lambda b, g, qi, c, o: (b, g, qi, 0, 0))],
        out_specs=pl.BlockSpec((tq, J * d), lambda b, g, qi, c, o: (b * nq + qi, g)),
        scratch_shapes=[pltpu.VMEM((J * tq, d), BF16), pltpu.VMEM((d, J * tq), F32)])
    return pl.pallas_call(
        functools.partial(_nsa_b_kernel, tk=tk, nk=nk), grid_spec=grid_spec,
        out_shape=jax.ShapeDtypeStruct((B * T, MIX_WIDTH), BF16),
        compiler_params=_cparams(("parallel", "parallel", "parallel")), name="nsa_selected",
    )(counts.reshape(-1), order.reshape(-1).astype(jnp.int32), zm, zm, vs_t, sel_t, expand_t, alibi_t, srow, cg_t, part_t)


def _nsa(zm, zs, cmp_w, cmp_pe, B, T):
    G, J = NSA_KV_GROUPS, NSA_GROUP_SIZE
    n = B * T
    d = HEAD_DIM
    cg = zs[:, 2 * N_HEADS:2 * N_HEADS + 3 * N_HEADS].reshape(n, 3, G, J)
    cg_t = jnp.pad(cg.transpose(2, 1, 3, 0).reshape(G, 3 * J, n), ((0, 0), (0, 16 - 3 * J), (0, 0)))
    vw_col = (KV_BASE + 5 * G) * d
    vw_t = zm[:, vw_col:vw_col + G * d].reshape(B, T, G, d).transpose(0, 2, 3, 1)
    k_cmp, v_cmp_t = _compress(zm, cmp_w, cmp_pe, B, T)
    part_t, sel_t = _nsa_a(zm, k_cmp, v_cmp_t, vw_t, cg_t, B, T)
    return _nsa_b(zm, sel_t, cg_t, part_t, B, T)


def _merge_kernel(h_ref, y0, y1, y2, y3, g0, g1, g2, g3, b0, b1, b2, b3, o_ref):
    h = h_ref[...]
    acc = None
    for y, gw, bw in ((y0, g0, b0), (y1, g1, b1), (y2, g2, b2), (y3, g3, b3)):
        term = jax.nn.sigmoid(_dot(h, gw[...])) * _dot(y[...], bw[...])
        acc = term if acc is None else acc + term
    o_ref[...] = acc.astype(BF16)


def _merge(h, ys, w_gate, w_branch, tm=1024, tn=256):
    n, dm = h.shape
    tm = min(tm, n)
    nj = dm // tn
    once = pl.Buffered(1)
    y_spec = pl.BlockSpec((tm, MIX_WIDTH), lambda i, j: (i, 0), pipeline_mode=once)
    gate_specs = [pl.BlockSpec((dm, tn), functools.partial(lambda i, j, b: (0, b * nj + j), b=b)) for b in range(N_BRANCH)]
    br_specs = [pl.BlockSpec((None, MIX_WIDTH, tn), functools.partial(lambda i, j, b: (b, 0, j), b=b)) for b in range(N_BRANCH)]
    return pl.pallas_call(
        _merge_kernel, grid=(n // tm, nj),
        in_specs=([pl.BlockSpec((tm, dm), lambda i, j: (i, 0), pipeline_mode=once)] + [y_spec] * N_BRANCH
                  + gate_specs + br_specs),
        out_specs=pl.BlockSpec((tm, tn), lambda i, j: (i, j)),
        out_shape=jax.ShapeDtypeStruct((n, dm), BF16),
        compiler_params=_cparams(("parallel", "parallel")), name="gated_merge",
    )(h, *ys, *([w_gate] * N_BRANCH), *([w_branch] * N_BRANCH))


def _pack_bf16_pairs(y):
    half = y.shape[1] // 2
    bits = lax.bitcast_convert_type(y.astype(BF16).astype(F32), jnp.int32)
    return bits[:, :half] | lax.shift_right_logical(bits[:, half:], 16)


def _unpack_bf16_pairs(w):
    first = lax.bitcast_convert_type(w & jnp.int32(-65536), F32).astype(BF16)
    second = lax.bitcast_convert_type(lax.shift_left(w, 16), F32).astype(BF16)
    return first, second


def _expert_kernel(be_ref, nact_ref, idx_ref, idx_next_ref, hp_hbm, w1_ref, w3_ref, w2_ref, o_ref,
                   xbuf, sems, w1b, w3b, w2b):
    i = pl.program_id(0)
    n_act = nact_ref[0]
    slot = lax.rem(i, 2)
    e = be_ref[i]
    prev = be_ref[jnp.maximum(i - 1, 0)]
    half = hp_hbm.shape[1]

    @pl.when((i == 0) & (n_act > 0))
    def _():
        _row_gather(idx_ref, hp_hbm, xbuf.at[0], sems.at[0], MOE_BLOCK)

    @pl.when(i + 1 < n_act)
    def _():
        _row_gather(idx_next_ref, hp_hbm, xbuf.at[1 - slot], sems.at[1 - slot], MOE_BLOCK)

    @pl.when((i == 0) | (prev != e))
    def _():
        w1b[...] = w1_ref[...].astype(BF16)
        w3b[...] = w3_ref[...].astype(BF16)
        w2b[...] = w2_ref[...].astype(BF16)

    @pl.when(i < n_act)
    def _():
        _row_gather_wait(hp_hbm, xbuf.at[slot], sems.at[slot], MOE_BLOCK)
        x_a, x_b = _unpack_bf16_pairs(xbuf[slot])
        a1 = _dot(x_a, w1b[0:half, :]) + _dot(x_b, w1b[half:, :])
        a3 = _dot(x_a, w3b[0:half, :]) + _dot(x_b, w3b[half:, :])
        a = (a1 * jax.nn.sigmoid(a1)) * a3
        o_ref[...] = _dot(a.astype(BF16), w2b[...])

    @pl.when(i >= n_act)
    def _():
        o_ref[...] = jnp.zeros_like(o_ref)


def _experts(hp, tok_pad, blk_exp, n_active, w1, w3, w2, layer):
    half = hp.shape[1]
    dm = 2 * half
    n_blk = tok_pad.shape[0] // MOE_BLOCK
    ff = w1.shape[-1]
    idx = tok_pad.reshape(n_blk, 1, MOE_BLOCK)
    smem_blk = lambda f: pl.BlockSpec((None, 1, MOE_BLOCK), f, memory_space=pltpu.SMEM)
    grid_spec = pltpu.PrefetchScalarGridSpec(
        num_scalar_prefetch=2, grid=(n_blk,),
        in_specs=[smem_blk(lambda i, be, na: (i, 0, 0)),
                  smem_blk(lambda i, be, na: (jnp.minimum(i + 1, n_blk - 1), 0, 0)),
                  pl.BlockSpec(memory_space=pl.ANY),
                  pl.BlockSpec((None, None, dm, ff), lambda i, be, na: (layer, be[i], 0, 0)),
                  pl.BlockSpec((None, None, dm, ff), lambda i, be, na: (layer, be[i], 0, 0)),
                  pl.BlockSpec((None, None, ff, dm), lambda i, be, na: (layer, be[i], 0, 0))],
        out_specs=pl.BlockSpec((MOE_BLOCK, dm), lambda i, be, na: (i, 0)),
        scratch_shapes=[pltpu.VMEM((2, MOE_BLOCK, half), jnp.int32), pltpu.SemaphoreType.DMA((2,)),
                        pltpu.VMEM((dm, ff), BF16), pltpu.VMEM((dm, ff), BF16), pltpu.VMEM((ff, dm), BF16)])
    return pl.pallas_call(
        _expert_kernel, grid_spec=grid_spec,
        out_shape=jax.ShapeDtypeStruct((n_blk * MOE_BLOCK, dm), F32),
        compiler_params=_cparams(("arbitrary",), vmem=MOE_VMEM_LIMIT), name="moe_experts",
    )(blk_exp, n_active, idx, idx, hp, w1, w3, w2)


def _dispatch_plan(e_idx):
    n, K = e_idx.shape
    nk = n * K
    i32 = jnp.int32
    flat_e = e_idx.reshape(-1)
    order = jnp.argsort(flat_e).astype(i32)
    rank = jnp.argsort(order).astype(i32)
    experts = jnp.arange(N_EXPERTS, dtype=i32)
    hot = flat_e[:, None] == experts[None, :]
    counts = jnp.sum(hot.astype(i32), axis=0)
    padded = (counts + MOE_BLOCK - 1) // MOE_BLOCK * MOE_BLOCK
    pad_end = jnp.cumsum(padded)
    pad_start = pad_end - padded
    start = jnp.cumsum(counts) - counts
    lookup = lambda table, onehot: jnp.sum(jnp.where(onehot, table[None, :], 0), axis=1)
    dest = lookup(pad_start - start, hot) + rank
    n_blk = (nk + MOE_BLOCK - 1) // MOE_BLOCK + N_EXPERTS
    blk_first = jnp.arange(n_blk, dtype=i32) * MOE_BLOCK
    blk_exp = jnp.minimum(jnp.sum((pad_end[None, :] <= blk_first[:, None]).astype(i32), axis=1), N_EXPERTS - 1)
    blk_hot = blk_exp[:, None] == experts[None, :]
    within = jnp.arange(MOE_BLOCK, dtype=i32)[None, :]
    j = (blk_first - lookup(pad_start, blk_hot))[:, None] + within
    sorted_pos = jnp.clip(lookup(start, blk_hot)[:, None] + j, 0, nk - 1)
    src = jnp.take(order, sorted_pos.reshape(-1), mode="clip") // K
    tok_pad = jnp.where((j < lookup(counts, blk_hot)[:, None]).reshape(-1), src, n - 1)
    n_active = (pad_end[-1] // MOE_BLOCK).astype(i32).reshape(1)
    return tok_pad, blk_exp.astype(i32), n_active, dest.reshape(n, K)


def _moe(h, route, w1, w3, w2, layer):
    e_idx = route[:, :TOP_K_IN_GROUP].astype(jnp.int32)
    tok_pad, blk_exp, n_active, dest = _dispatch_plan(e_idx)
    y = _experts(h, tok_pad, blk_exp, n_active, w1, w3, w2, layer)
    return y, dest


PACK_TILE = 512


def _pack_kernel(offs_ref, wt_ref, o_ref):
    del offs_ref
    o_ref[...] = wt_ref[0].T.astype(BF16)


def _pack_columns(w_in_t, layer, col_offsets):
    _, _, k = w_in_t.shape
    nt = len(col_offsets)
    grid_spec = pltpu.PrefetchScalarGridSpec(
        num_scalar_prefetch=1, grid=(nt,),
        in_specs=[pl.BlockSpec((pl.Element(1), pl.Element(PACK_TILE), pl.Element(k)),
                               lambda j, offs: (layer, pl.multiple_of(offs[j], 8), 0))],
        out_specs=pl.BlockSpec((k, PACK_TILE), lambda j, offs: (0, j)))
    return pl.pallas_call(
        _pack_kernel, grid_spec=grid_spec,
        out_shape=jax.ShapeDtypeStruct((k, nt * PACK_TILE), BF16),
        compiler_params=_cparams(("parallel",)), name="pack_w_in",
    )(jnp.asarray(col_offsets, jnp.int32), w_in_t)


def _tile_offsets(groups):
    offs = []
    for i in groups:
        assert COL_WIDTHS[i] % PACK_TILE == 0 or COL_WIDTHS[i] < PACK_TILE
        offs += list(range(COL_OFFSETS[i], COL_OFFSETS[i + 1], PACK_TILE))
    return offs


def _pack_w_in(w_in_t, layer):
    wide = [i for i in MAIN_ORDER if COL_WIDTHS[i] >= PACK_TILE]
    narrow = [i for i in MAIN_ORDER if COL_WIDTHS[i] < PACK_TILE]
    assert MAIN_ORDER == tuple(wide + narrow) and narrow == list(range(narrow[0], narrow[-1] + 1))
    kv_lo, kv_hi = COL_OFFSETS[narrow[0]], COL_OFFSETS[narrow[-1] + 1]
    assert (kv_hi - kv_lo) % PACK_TILE == 0
    main_offs = _tile_offsets(wide) + list(range(kv_lo, kv_hi, PACK_TILE))
    w_main = _pack_columns(w_in_t, layer, main_offs)
    w_gate = _pack_columns(w_in_t, layer, _tile_offsets([I_MERGE]))
    rows = lambda i: w_in_t[layer, COL_OFFSETS[i]:COL_OFFSETS[i + 1]]
    small_t = jnp.concatenate([rows(I_AI), rows(I_AF), rows(I_CG)], axis=0)
    small_t = jnp.pad(small_t, ((0, LANES - small_t.shape[0]), (0, 0)))
    return w_main, small_t, w_gate


def _router_weights(wg, bg, we, be):
    w = jnp.concatenate([wg, we], axis=1)
    w = jnp.pad(w, ((0, 0), (0, LANES - w.shape[1])))
    hi = w.astype(BF16)
    lo = (w - hi.astype(F32)).astype(BF16)
    b = jnp.concatenate([bg, be])
    b = jnp.pad(b, (0, LANES - b.shape[0])).reshape(1, LANES).astype(F32)
    return hi, lo, b


def kernel(x, w_in, mlstm_conv, mlstm_gate_bias, mlstm_norm, ret_norm, nsa_cmp_w, nsa_cmp_pe, sgu_norm, sgu_w, sgu_b, w_branch, w_out, norm_mix, norm_ffn, router_group_w, router_group_b, router_expert_w, router_expert_b, expert_w1, expert_w3, expert_w2, norm_final):
    B, T, D = x.shape
    n = B * T
    xs, moe = x.reshape(n, D), None
    w_in_t = jnp.swapaxes(w_in, 1, 2)
    for l in range(DEPTH):
        w_main, w_small_t, w_gate = _pack_w_in(w_in_t, l)
        if moe is None:
            h, = _norm(xs, norm_mix[l])
        else:
            xs, h = _norm(xs, norm_mix[l], moe=moe)
        zm = _matmul(h, w_main, out_dtype=BF16, name="proj_main")
        zs = _matmul(h, w_small_t, out_dtype=F32, w_rows_are_outputs=True, name="proj_small")
        zst = zs[:, :2 * N_HEADS].T
        ya = _mlstm(zm, zs, zst, mlstm_conv[l], mlstm_gate_bias[l], mlstm_norm[l], B, T)
        yb = _retention(zm, ret_norm[l], B, T)
        yc = _nsa(zm, zs, nsa_cmp_w[l], nsa_cmp_pe[l], B, T)
        yd = _sgu(zm, sgu_norm[l], sgu_w[l], sgu_b[l], n)
        merged = _merge(h, (ya, yb, yc, yd), w_gate, w_branch[l].astype(BF16))
        x_mid = _matmul(merged, w_out[l].astype(BF16), out_dtype=F32, residual=xs, name="proj_out")
        h2, route = _norm(x_mid, norm_ffn[l], rows=256,
                          router_w=_router_weights(router_group_w[l], router_group_b[l],
                                                   router_expert_w[l], router_expert_b[l]))
        y_exp, dest = _moe(h2, route, expert_w1, expert_w3, expert_w2, l)
        xs, moe = x_mid, (y_exp, dest, route)
    out, = _norm(xs, norm_final, moe=moe, final=True)
    return out.reshape(B, T, D)
```

```python
import functools

import numpy as np
import jax
import jax.numpy as jnp
from jax import lax
from jax.experimental import pallas as pl
from jax.experimental.pallas import tpu as pltpu

F32 = jnp.float32
BF16 = jnp.bfloat16

D_MODEL = 4096
DEPTH = 2
HEAD_DIM = 128
N_BRANCH = 4
MIX_WIDTH = D_MODEL // N_BRANCH
N_HEADS = MIX_WIDTH // HEAD_DIM
CONV_WIDTH = 4
CONV_TAIL = 16
MLSTM_CHUNK = 128
RET_CHUNK = 128
NSA_KV_GROUPS = 2
NSA_GROUP_SIZE = N_HEADS // NSA_KV_GROUPS
KV_WIDTH = NSA_KV_GROUPS * HEAD_DIM
CMP_BLOCK = 32
CMP_STRIDE = 16
SEL_BLOCK = 64
SEL_TOPN = 16
WINDOW = 512
Q_BLOCK = 128
SGU_CHUNK = 128
N_GROUPS = 8
EXPERTS_PER_GROUP = 8
N_EXPERTS = N_GROUPS * EXPERTS_PER_GROUP
TOP_K_IN_GROUP = 2
EXPERT_FF = 256
MOE_BLOCK = 256
RMS_EPS = 1e-6
NEG_INF = -1e30
FORCE_SCORE = 1e9

COL_WIDTHS = (
    MIX_WIDTH, MIX_WIDTH, MIX_WIDTH, MIX_WIDTH, N_HEADS, N_HEADS,
    MIX_WIDTH, MIX_WIDTH, MIX_WIDTH, MIX_WIDTH,
    MIX_WIDTH, KV_WIDTH, KV_WIDTH, KV_WIDTH, KV_WIDTH, KV_WIDTH, KV_WIDTH, 3 * N_HEADS,
    MIX_WIDTH, MIX_WIDTH,
    N_BRANCH * D_MODEL,
)
COL_OFFSETS = tuple(int(v) for v in np.concatenate([[0], np.cumsum(COL_WIDTHS)]))
(I_AQ, I_AK, I_AV, I_AO, I_AI, I_AF, I_BQ, I_BK, I_BV, I_BG, I_CQ, I_CKC, I_CVC, I_CKS, I_CVS,
 I_CKW, I_CVW, I_CG, I_DU, I_DV, I_MERGE) = range(21)

MAIN_ORDER = (I_AQ, I_AK, I_AV, I_AO, I_BQ, I_BK, I_BV, I_BG, I_CQ, I_DU, I_DV,
              I_CKC, I_CVC, I_CKS, I_CVS, I_CKW, I_CVW)
MAIN_WIDTH = sum(COL_WIDTHS[i] for i in MAIN_ORDER)
LANES = 128
BLK_AQ, BLK_AK, BLK_AV, BLK_AO, BLK_BQ, BLK_BK, BLK_BV, BLK_BG, BLK_CQ, BLK_DU, BLK_DV = range(11)
KV_BASE = 11 * MIX_WIDTH // HEAD_DIM
VMEM_LIMIT = 48 * 1024 * 1024
MOE_VMEM_LIMIT = 56 * 1024 * 1024


def _cparams(sem, vmem=VMEM_LIMIT):
    return pltpu.CompilerParams(dimension_semantics=sem, vmem_limit_bytes=vmem)


def _dot(a, b):
    return jnp.dot(a, b, preferred_element_type=F32)


def _dot_nt(a, b):
    return lax.dot_general(a, b, (((1,), (1,)), ((), ())), preferred_element_type=F32)


def _split3(x):
    hi = x.astype(BF16)
    r1 = x - hi.astype(F32)
    mid = r1.astype(BF16)
    lo = (r1 - mid.astype(F32)).astype(BF16)
    return hi, mid, lo


def _dot_exact_rhs(x, m_bf16):
    hi, mid, lo = _split3(x)
    return _dot(hi, m_bf16) + _dot(mid, m_bf16) + _dot(lo, m_bf16)


def _log_sigmoid(x):
    return jnp.minimum(x, 0.0) - jnp.log1p(jnp.exp(-jnp.abs(x)))


def _row_gather(idx_ref, src_hbm, dst, sem, count):
    for r in range(count):
        pltpu.make_async_copy(src_hbm.at[pl.ds(idx_ref[0, r], 1)], dst.at[pl.ds(r, 1)], sem).start(priority=r % 2)


def _row_gather_wait(src_hbm, dst, sem, count):
    for r in range(count):
        pltpu.make_async_copy(src_hbm.at[pl.ds(0, 1)], dst.at[pl.ds(r, 1)], sem).wait()


def _norm_kernel(*refs, moe, router, final):
    x = refs[0][...]
    rows = x.shape[0]
    pos = 1
    n_scratch = 0
    if moe:
        idx_ref, idx_next_ref, y_hbm, rt_ref = refs[pos:pos + 4]
        pos += 4
        gbuf, sems = refs[-2:]
        n_scratch = 2
        i, last = pl.program_id(0), pl.num_programs(0) - 1
        slot = lax.rem(i, 2)

        @pl.when(i == 0)
        def _():
            _row_gather(idx_ref, y_hbm, gbuf.at[0], sems.at[0], 2 * rows)

        @pl.when(i < last)
        def _():
            _row_gather(idx_next_ref, y_hbm, gbuf.at[1 - slot], sems.at[1 - slot], 2 * rows)

        _row_gather_wait(y_hbm, gbuf.at[slot], sems.at[slot], 2 * rows)
        x = x + (rt_ref[:, 2:3] * gbuf[slot, 0:rows, :] + rt_ref[:, 3:4] * gbuf[slot, rows:2 * rows, :])
    g_ref = refs[pos]
    pos += 1
    if router:
        whi_ref, wlo_ref, rb_ref = refs[pos:pos + 3]
        pos += 3
    outs = refs[pos:len(refs) - n_scratch]
    y = x * lax.rsqrt(jnp.mean(x * x, axis=-1, keepdims=True) + RMS_EPS) * g_ref[...]
    o = 0
    if moe and not final:
        outs[o][...] = x
        o += 1
    if final:
        outs[o][...] = y
    elif router:
        outs[o][...] = _pack_bf16_pairs(y)
    else:
        outs[o][...] = y.astype(BF16)
    o += 1
    if router:
        y_hi = y.astype(BF16)
        y_lo = (y - y_hi.astype(F32)).astype(BF16)
        lg = _dot(y_hi, whi_ref[...]) + (_dot(y_hi, wlo_ref[...]) + _dot(y_lo, whi_ref[...])) + rb_ref[...]
        outs[o][...] = _route_rows(lg)


def _first_max(v, lane):
    mx = jnp.max(v, axis=-1, keepdims=True)
    return mx, jnp.min(jnp.where(v == mx, lane, LANES), axis=-1, keepdims=True)


def _route_rows(lg):
    low = -3.0e38
    lane = lax.broadcasted_iota(jnp.int32, lg.shape, 1)
    is_grp = lane < N_GROUPS
    gmax, g_idx = _first_max(jnp.where(is_grp, lg, low), lane)
    g_w = 1.0 / jnp.sum(jnp.where(is_grp, jnp.exp(lg - gmax), 0.0), axis=-1, keepdims=True)
    assert EXPERTS_PER_GROUP & (EXPERTS_PER_GROUP - 1) == 0
    grp_of_lane = lax.shift_right_arithmetic(lane - N_GROUPS, EXPERTS_PER_GROUP.bit_length() - 1)
    in_grp = (lane >= N_GROUPS) & (lane < N_GROUPS + N_EXPERTS) & (grp_of_lane == g_idx)
    el = jnp.where(in_grp, lg, low)
    v0, i0 = _first_max(el, lane)
    v1, i1 = _first_max(jnp.where(lane == i0, low, el), lane)
    e1 = jnp.exp(v1 - v0)
    p0 = g_w / (1.0 + e1)
    vals = (i0 - N_GROUPS).astype(F32), (i1 - N_GROUPS).astype(F32), p0, p0 * e1
    out = jnp.zeros(lg.shape, F32)
    for k, v in enumerate(vals):
        out = jnp.where(lane == k, v, out)
    return out


def _norm(x, gain, *, moe=None, router_w=None, final=False, rows=128):
    n, d = x.shape
    rows = min(rows, n)
    steps = n // rows
    row_spec = pl.BlockSpec((rows, d), lambda i: (i, 0))
    lane_spec = pl.BlockSpec((rows, LANES), lambda i: (i, 0))
    in_specs, args, scratch = [row_spec], [x], []
    if moe is not None:
        y, dest, route = moe
        K = dest.shape[1]
        idx = dest.reshape(steps, rows, K).transpose(0, 2, 1).reshape(steps, 1, K * rows)
        smem_blk = lambda f: pl.BlockSpec((None, 1, K * rows), f, memory_space=pltpu.SMEM)
        in_specs += [smem_blk(lambda i: (i, 0, 0)), smem_blk(lambda i: (jnp.minimum(i + 1, steps - 1), 0, 0)),
                     pl.BlockSpec(memory_space=pl.ANY), lane_spec]
        args += [idx, idx, y, route]
        scratch = [pltpu.VMEM((2, K * rows, d), F32), pltpu.SemaphoreType.DMA((2,))]
    in_specs.append(pl.BlockSpec((1, d), lambda i: (0, 0)))
    args.append(gain.reshape(1, d))
    out_shape, out_specs = [], []
    if moe is not None and not final:
        out_shape.append(jax.ShapeDtypeStruct((n, d), F32))
        out_specs.append(row_spec)
    if router_w is not None:
        out_shape.append(jax.ShapeDtypeStruct((n, d // 2), jnp.int32))
        out_specs.append(pl.BlockSpec((rows, d // 2), lambda i: (i, 0)))
    else:
        out_shape.append(jax.ShapeDtypeStruct((n, d), F32 if final else BF16))
        out_specs.append(row_spec)
    if router_w is not None:
        whi, wlo, rb = router_w
        in_specs += [pl.BlockSpec((d, LANES), lambda i: (0, 0))] * 2 + [pl.BlockSpec((1, LANES), lambda i: (0, 0))]
        args += [whi, wlo, rb]
        out_shape.append(jax.ShapeDtypeStruct((n, LANES), F32))
        out_specs.append(lane_spec)
    return pl.pallas_call(
        functools.partial(_norm_kernel, moe=moe is not None, router=router_w is not None, final=final),
        grid=(steps,), in_specs=in_specs, out_specs=out_specs, out_shape=out_shape, scratch_shapes=scratch,
        compiler_params=_cparams(("arbitrary",)), name="rmsnorm",
    )(*args)


def _mm_kernel(a_ref, w_ref, *rest, has_res, w_rows_are_outputs):
    w = w_ref[...].astype(BF16)
    acc = _dot_nt(a_ref[...], w) if w_rows_are_outputs else _dot(a_ref[...], w)
    if has_res:
        acc = acc + rest[0][...]
    rest[-1][...] = acc.astype(rest[-1].dtype)


def _matmul(a, w, *, out_dtype, residual=None, tm=1024, tn=512, w_rows_are_outputs=False, name="matmul"):
    m, k = a.shape
    nc = w.shape[0] if w_rows_are_outputs else w.shape[1]
    tm, tn = min(tm, m), min(tn, nc)
    w_spec = pl.BlockSpec((tn, k), lambda i, j: (j, 0)) if w_rows_are_outputs else pl.BlockSpec((k, tn), lambda i, j: (0, j))
    in_specs = [pl.BlockSpec((tm, k), lambda i, j: (i, 0)), w_spec]
    args = [a, w]
    if residual is not None:
        in_specs.append(pl.BlockSpec((tm, tn), lambda i, j: (i, j)))
        args.append(residual)
    return pl.pallas_call(
        functools.partial(_mm_kernel, has_res=residual is not None, w_rows_are_outputs=w_rows_are_outputs),
        grid=(m // tm, nc // tn), in_specs=in_specs,
        out_specs=pl.BlockSpec((tm, tn), lambda i, j: (i, j)),
        out_shape=jax.ShapeDtypeStruct((m, nc), out_dtype),
        compiler_params=_cparams(("parallel", "parallel")), name=name,
    )(*args)


def _mlstm_kernel(aq_ref, ak_ref, av_ref, ao_ref, zs_ref, zst_ref, cw_ref, gbc_ref, gbr_ref, gain_ref,
                  tril_ref, triu_ref, shift_ref, o_ref, tail_s, qk_s, ct_s, n_s, m_s):
    L, d, H, W = MLSTM_CHUNK, HEAD_DIM, N_HEADS, MIX_WIDTH
    c = pl.program_id(1)

    @pl.when(c == 0)
    def _():
        tail_s[...] = jnp.zeros_like(tail_s)
        ct_s[...] = jnp.zeros_like(ct_s)
        n_s[...] = jnp.zeros_like(n_s)
        m_s[...] = jnp.zeros_like(m_s)

    x_cur = jnp.concatenate([aq_ref[...], ak_ref[...]], axis=1)
    x_ext = jnp.concatenate([tail_s[...], x_cur], axis=0)
    conv = cw_ref[CONV_WIDTH - 1:CONV_WIDTH, :] * x_cur.astype(F32)
    for j in range(CONV_WIDTH - 1):
        conv = conv + cw_ref[j:j + 1, :] * _dot(shift_ref[j], x_ext)
    tail_s[...] = x_cur[L - CONV_TAIL:, :]
    qk_s[...] = conv * jax.nn.sigmoid(conv)

    pre_c = zs_ref[...] + gbc_ref[...]
    ls_c = _log_sigmoid(pre_c)
    hi, mid, lo = _split3(ls_c)
    tril = tril_ref[...]
    bc = _dot(tril, hi) + _dot(tril, mid) + _dot(tril, lo)
    pre_r = zst_ref[...] + gbr_ref[...]
    br = _dot_exact_rhs(_log_sigmoid(pre_r), triu_ref[...])

    row = lax.broadcasted_iota(jnp.int32, (L, L), 0)
    col = lax.broadcasted_iota(jnp.int32, (L, L), 1)
    causal = row >= col
    scale = d ** -0.5
    for h in range(H):
        sl = slice(h * d, (h + 1) * d)
        q = qk_s[:, sl] * scale
        k = qk_s[:, W + h * d:W + (h + 1) * d]
        v = av_ref[:, sl]
        i_col, b_col = pre_c[:, h:h + 1], bc[:, H + h:H + h + 1]
        i_row, b_row = pre_r[h:h + 1, :], br[H + h:H + h + 1, :]
        m_prev = m_s[h:h + 1, 0:1]
        log_d = jnp.where(causal, b_col - b_row + i_row, NEG_INF)
        log_inter = b_col + m_prev
        m_row = jnp.maximum(jnp.max(log_d, axis=-1, keepdims=True), log_inter)
        qb, kb = q.astype(BF16), k.astype(BF16)
        s = _dot_nt(qb, kb) * jnp.exp(log_d - m_row)
        w_inter = jnp.exp(log_inter - m_row)
        num = _dot(s.astype(BF16), v) + w_inter * _dot(qb, ct_s[h].astype(BF16))
        den = jnp.sum(s, axis=-1, keepdims=True) + w_inter * jnp.sum(q * n_s[h:h + 1, :], axis=-1, keepdims=True)
        hh = num / jnp.maximum(jnp.abs(den), jnp.exp(-m_row))
        b_last = b_row[:, L - 1:L]
        log_w_row = b_last - b_row + i_row
        m_new = jnp.maximum(b_last + m_prev, jnp.max(log_w_row, axis=-1, keepdims=True))
        w_col = jnp.exp(b_last - b_col + i_col - m_new)
        decay = jnp.exp(b_last + m_prev - m_new)
        vw = (v.astype(F32) * w_col).astype(BF16)
        ct_s[h] = decay * ct_s[h] + _dot(k.T.astype(BF16), vw)
        n_s[h:h + 1, :] = decay * n_s[h:h + 1, :] + jnp.sum(k * w_col, axis=0, keepdims=True)
        m_s[h:h + 1, :] = jnp.broadcast_to(m_new, (1, LANES))
        hn = hh * lax.rsqrt(jnp.mean(hh * hh, axis=-1, keepdims=True) + RMS_EPS)
        o_ref[:, sl] = (hn * gain_ref[:, sl] * jax.nn.sigmoid(ao_ref[:, sl].astype(F32))).astype(BF16)


def _tri_consts(L):
    r = np.arange(L)
    tril = (r[:, None] >= r[None, :]).astype(np.float32)
    return jnp.asarray(tril, BF16), jnp.asarray(tril.T, BF16)


def _mlstm(zm, zs, zst, conv_w, gate_bias, gain, B, T):
    L, W = MLSTM_CHUNK, MIX_WIDTH
    nc = T // L
    n = B * T
    gbc = jnp.zeros((1, LANES), F32).at[0, :2 * N_HEADS].set(gate_bias.reshape(-1))
    gbr = jnp.broadcast_to(gate_bias.reshape(2 * N_HEADS, 1), (2 * N_HEADS, L)).astype(F32)
    tril, triu = _tri_consts(L)
    t_idx = np.arange(L)[:, None]
    c_idx = np.arange(CONV_TAIL + L)[None, :]
    shift = jnp.asarray(np.stack([(c_idx == CONV_TAIL + t_idx - (CONV_WIDTH - 1 - j)) for j in range(CONV_WIDTH - 1)])
                        .astype(np.float32), BF16)

    def blk(b_idx):
        return pl.BlockSpec((L, W), lambda b, c: (b * nc + c, b_idx))

    const = lambda shape: pl.BlockSpec(shape, lambda b, c: (0,) * len(shape))
    return pl.pallas_call(
        _mlstm_kernel, grid=(B, nc),
        in_specs=[blk(BLK_AQ), blk(BLK_AK), blk(BLK_AV), blk(BLK_AO),
                  pl.BlockSpec((L, LANES), lambda b, c: (b * nc + c, 0)),
                  pl.BlockSpec((2 * N_HEADS, L), lambda b, c: (0, b * nc + c)),
                  const((CONV_WIDTH, 2 * W)), const((1, LANES)), const((2 * N_HEADS, L)), const((1, W)),
                  const((L, L)), const((L, L)), const((CONV_WIDTH - 1, L, CONV_TAIL + L))],
        out_specs=pl.BlockSpec((L, W), lambda b, c: (b * nc + c, 0)),
        out_shape=jax.ShapeDtypeStruct((n, W), BF16),
        scratch_shapes=[pltpu.VMEM((CONV_TAIL, 2 * W), BF16), pltpu.VMEM((L, 2 * W), F32),
                        pltpu.VMEM((N_HEADS, HEAD_DIM, HEAD_DIM), F32), pltpu.VMEM((N_HEADS, HEAD_DIM), F32),
                        pltpu.VMEM((N_HEADS, LANES), F32)],
        compiler_params=_cparams(("parallel", "arbitrary")), name="mlstm",
    )(zm, zm, zm, zm, zs, zst, conv_w, gbc, gbr, gain.reshape(1, W), tril, triu, shift)


def _ret_kernel(q_ref, k_ref, v_ref, g_ref, dec_ref, qd_ref, kd_ref, gain_ref, o_ref, r_s, *, chunk_decay):
    L, d, H = RET_CHUNK, HEAD_DIM, N_HEADS
    c = pl.program_id(1)

    @pl.when(c == 0)
    def _():
        r_s[...] = jnp.zeros_like(r_s)

    scale = d ** -0.5
    for h in range(H):
        sl = slice(h * d, (h + 1) * d)
        q = q_ref[:, sl].astype(F32)
        k = k_ref[:, sl].astype(F32) * scale
        v = v_ref[:, sl]
        s = _dot_nt(q.astype(BF16), k.astype(BF16)) * dec_ref[h]
        o = _dot(s.astype(BF16), v) + _dot((q * qd_ref[:, sl]).astype(BF16), r_s[h].astype(BF16))
        kd = (k * kd_ref[:, sl]).T.astype(BF16)
        r_s[h] = chunk_decay[h] * r_s[h] + _dot(kd, v)
        on = o * lax.rsqrt(jnp.mean(o * o, axis=-1, keepdims=True) + RMS_EPS)
        gate = g_ref[:, sl].astype(F32)
        o_ref[:, sl] = (on * gain_ref[:, sl] * (gate * jax.nn.sigmoid(gate))).astype(BF16)


def _retention(zm, gain, B, T):
    L, W, H, d = RET_CHUNK, MIX_WIDTH, N_HEADS, HEAD_DIM
    nc = T // L
    log_gamma = np.log(np.float32(1.0) - np.float32(2.0) ** (-5.0 - np.arange(H, dtype=np.float32))).astype(np.float32)
    pos = np.arange(L, dtype=np.float32)
    diff = pos[:, None] - pos[None, :]
    decay = np.where(diff >= 0, np.exp(log_gamma[:, None, None] * np.maximum(diff, 0.0)), 0.0).astype(np.float32)
    q_decay = np.exp(log_gamma[:, None] * (pos + 1.0)).astype(np.float32)
    k_decay = np.exp(log_gamma[:, None] * (L - 1.0 - pos)).astype(np.float32)
    chunk_decay = tuple(float(v) for v in np.exp(log_gamma * np.float32(L)).astype(np.float32))
    qd = jnp.asarray(np.repeat(q_decay.T, d, axis=1))
    kd = jnp.asarray(np.repeat(k_decay.T, d, axis=1))

    def blk(b_idx):
        return pl.BlockSpec((L, W), lambda b, c: (b * nc + c, b_idx))

    const = lambda shape: pl.BlockSpec(shape, lambda b, c: (0,) * len(shape))
    return pl.pallas_call(
        functools.partial(_ret_kernel, chunk_decay=chunk_decay), grid=(B, nc),
        in_specs=[blk(BLK_BQ), blk(BLK_BK), blk(BLK_BV), blk(BLK_BG),
                  const((H, L, L)), const((L, W)), const((L, W)), const((1, W))],
        out_specs=pl.BlockSpec((L, W), lambda b, c: (b * nc + c, 0)),
        out_shape=jax.ShapeDtypeStruct((B * T, W), BF16),
        scratch_shapes=[pltpu.VMEM((H, d, d), F32)],
        compiler_params=_cparams(("parallel", "arbitrary")), name="retention",
    )(zm, zm, zm, zm, jnp.asarray(decay), qd, kd, gain.reshape(1, W))


def _gelu(x):
    return 0.5 * x * (1.0 + jnp.tanh(0.7978845608028654 * (x + 0.044715 * (x * x * x))))


def _sgu_kernel(u_ref, v_ref, ng_ref, w_ref, b_ref, o_ref):
    L, d, H = SGU_CHUNK, HEAD_DIM, N_HEADS
    v = _gelu(v_ref[...].astype(F32))
    vn = (v * lax.rsqrt(jnp.mean(v * v, axis=-1, keepdims=True) + RMS_EPS) * ng_ref[...]).astype(BF16)
    u = _gelu(u_ref[...].astype(F32))
    row = lax.broadcasted_iota(jnp.int32, (L, L), 0)
    col = lax.broadcasted_iota(jnp.int32, (L, L), 1)
    for g in range(H):
        sl = slice(g * d, (g + 1) * d)
        wm = jnp.where(row >= col, w_ref[g], 0.0).astype(BF16)
        mixed = _dot(wm, vn[:, sl]) + b_ref[:, sl]
        o_ref[:, sl] = (u[:, sl] * mixed).astype(BF16)


def _sgu(zm, norm_g, w_s, b_s, n):
    L, W, H, d = SGU_CHUNK, MIX_WIDTH, N_HEADS, HEAD_DIM
    bsb = jnp.repeat(b_s.T, d, axis=1)
    const = lambda shape: pl.BlockSpec(shape, lambda i: (0,) * len(shape))
    return pl.pallas_call(
        _sgu_kernel, grid=(n // L,),
        in_specs=[pl.BlockSpec((L, W), lambda i: (i, BLK_DU)), pl.BlockSpec((L, W), lambda i: (i, BLK_DV)),
                  const((1, W)), const((H, L, L)), const((L, W))],
        out_specs=pl.BlockSpec((L, W), lambda i: (i, 0)),
        out_shape=jax.ShapeDtypeStruct((n, W), BF16),
        compiler_params=_cparams(("parallel",)), name="sgu",
    )(zm, zm, norm_g.reshape(1, W), w_s, bsb)


def _compress_kernel(k_ref, v_ref, w_ref, pe_ref, ko_ref, vo_ref, buf):
    t = k_ref.shape[0]
    n16 = t // CMP_STRIDE
    buf[t:t + CMP_STRIDE, :] = jnp.zeros((CMP_STRIDE, HEAD_DIM), F32)
    for which, (src, dst) in enumerate(((k_ref, ko_ref), (v_ref, vo_ref))):
        buf[0:t, :] = src[...].astype(F32)
        acc = jnp.zeros((n16, HEAD_DIM), F32)
        for l in range(CMP_BLOCK):
            x = buf[pl.ds(l, n16, stride=CMP_STRIDE), :]
            acc = acc + _dot((x + pe_ref[which, l:l + 1, :]).astype(BF16), w_ref[which, l].astype(BF16))
        dst[...] = (acc.T if which else acc).astype(BF16)


def _compress(zm, cmp_w, cmp_pe, B, T):
    G, d = NSA_KV_GROUPS, HEAD_DIM
    n16 = T // CMP_STRIDE
    out = jax.ShapeDtypeStruct((B, G, n16, d), BF16)
    kblk = KV_BASE
    vblk = KV_BASE + G
    return pl.pallas_call(
        _compress_kernel, grid=(B, G),
        in_specs=[pl.BlockSpec((T, d), lambda b, g: (b, kblk + g)), pl.BlockSpec((T, d), lambda b, g: (b, vblk + g)),
                  pl.BlockSpec((2, CMP_BLOCK, d, d), lambda b, g: (0, 0, 0, 0)),
                  pl.BlockSpec((2, CMP_BLOCK, d), lambda b, g: (0, 0, 0))],
        out_specs=[pl.BlockSpec((None, None, n16, d), lambda b, g: (b, g, 0, 0)),
                   pl.BlockSpec((None, None, d, n16), lambda b, g: (b, g, 0, 0))],
        out_shape=[out, jax.ShapeDtypeStruct((B, G, d, n16), BF16)],
        scratch_shapes=[pltpu.VMEM((T + CMP_STRIDE, d), F32)],
        compiler_params=_cparams(("parallel", "parallel")), name="nsa_compress",
    )(zm, zm, cmp_w, cmp_pe)


RANK_CHUNK = 8


def _lanes_x(x, times):
    return jnp.concatenate([x] * times, axis=1)


def _nsa_a_kernel(q_ref, kc_ref, vct_ref, *rest, n_back):
    nw = n_back + 1
    kw_refs, vwt_refs = rest[:nw], rest[nw:2 * nw]
    cgt_ref, ovt_ref, acmp_ref, awin_ref, part_ref, sel_ref, q_s, key_s, rank_s = rest[2 * nw:]
    tq, d, J = Q_BLOCK, HEAD_DIM, NSA_GROUP_SIZE
    qi = pl.program_id(2)
    t0 = qi * tq
    ncmp = kc_ref.shape[0]
    scale = d ** -0.5
    for j in range(J):
        q_s[j * tq:(j + 1) * tq, :] = (q_ref[:, j * d:(j + 1) * d].astype(F32) * scale).astype(BF16)
    sg_t = jax.nn.sigmoid(cgt_ref[...])
    t_row = t0 + lax.broadcasted_iota(jnp.int32, (1, tq), 1)

    n_i = lax.broadcasted_iota(jnp.int32, (ncmp, tq), 0)
    r_i = lax.broadcasted_iota(jnp.int32, (ncmp, tq), 1)
    bias_c = jnp.where(t0 + r_i - CMP_STRIDE * n_i - (CMP_BLOCK - 1) >= 0, 0.0, NEG_INF)
    s = _dot_nt(kc_ref[...], q_s[...]) - acmp_ref[...] + _lanes_x(bias_c, J)
    p = jnp.exp(s - jnp.max(s, axis=0, keepdims=True))
    has_c = jnp.where(t_row >= CMP_BLOCK - 1, 1.0, 0.0)
    p = p * (_lanes_x(has_c, J) / jnp.sum(p, axis=0, keepdims=True))
    o_cmp_t = _dot(vct_ref[...], p.astype(BF16))
    psum = p[:, 0:tq]
    for j in range(1, J):
        psum = psum + p[:, j * tq:(j + 1) * tq]

    hi, mid, lo = _split3(psum)
    ovt = ovt_ref[...]
    imp_t = _dot(ovt, hi) + _dot(ovt, mid) + _dot(ovt, lo)
    blk_t = lax.broadcasted_iota(jnp.int32, (LANES, tq), 0)
    cur_t = lax.shift_right_logical(t0 + lax.broadcasted_iota(jnp.int32, (LANES, tq), 1), SEL_BLOCK.bit_length() - 1)
    valid_t = blk_t <= cur_t
    forced = (blk_t == 0) | (blk_t == cur_t) | (blk_t == cur_t - 1)
    key_s[...] = jnp.where(valid_t, imp_t + jnp.where(forced, FORCE_SCORE, 0.0), NEG_INF)
    rank_s[...] = jnp.zeros_like(rank_s)
    sub = lax.broadcasted_iota(jnp.int32, (8, tq), 0)
    groups = LANES // 8
    last_valid = lax.shift_right_logical(t0 + tq - 1, SEL_BLOCK.bit_length() - 1)
    for chunk in range(LANES // RANK_CHUNK):
        @pl.when(chunk * RANK_CHUNK <= last_valid)
        def _(chunk=chunk):
            key_g = [key_s[8 * i:8 * i + 8, :] for i in range(groups)]
            rank = [rank_s[8 * i:8 * i + 8, :] for i in range(groups)]
            for mp in range(chunk * RANK_CHUNK, (chunk + 1) * RANK_CHUNK):
                rowv = jnp.broadcast_to(key_s[mp:mp + 1, :], (8, tq))
                for i in range(groups):
                    if 8 * i > mp:
                        ahead = rowv >= key_g[i]
                    elif 8 * i + 7 <= mp:
                        ahead = rowv > key_g[i]
                    else:
                        ahead = (rowv > key_g[i]) | ((sub > mp - 8 * i) & (rowv == key_g[i]))
                    rank[i] = rank[i] + jnp.where(ahead, 1.0, 0.0)
            for i in range(groups):
                rank_s[8 * i:8 * i + 8, :] = rank[i]
    sel_ref[...] = jnp.where((rank_s[...] < float(SEL_TOPN)) & valid_t, 1.0, 0.0).astype(BF16)

    span = nw * tq
    kw = jnp.concatenate([kr[...] for kr in kw_refs], axis=0)
    vw_t = jnp.concatenate([vr[...] for vr in vwt_refs], axis=1)
    c_i = lax.broadcasted_iota(jnp.int32, (span, tq), 0)
    bias_w = jnp.where(t0 - n_back * tq + c_i >= 0, 0.0, NEG_INF)
    s = _dot_nt(kw, q_s[...]) - awin_ref[...] + _lanes_x(bias_w, J)
    p = jnp.exp(s - jnp.max(s, axis=0, keepdims=True))
    inv_l = 1.0 / jnp.sum(p, axis=0, keepdims=True)
    o_win_t = _dot(vw_t, p.astype(BF16))
    g_cmp = jnp.concatenate([sg_t[j:j + 1, :] for j in range(J)], axis=1)
    g_win = jnp.concatenate([sg_t[2 * J + j:2 * J + j + 1, :] for j in range(J)], axis=1)
    part_ref[...] = g_cmp * o_cmp_t + (g_win * inv_l) * o_win_t


def _alibi_slopes():
    return (2.0 ** (-8.0 * (np.arange(N_HEADS, dtype=np.float32) + 1.0) / N_HEADS)).astype(np.float32)


def _nsa_consts(T):
    G, J, tq = NSA_KV_GROUPS, NSA_GROUP_SIZE, Q_BLOCK
    n16 = T // CMP_STRIDE
    n_sel = T // SEL_BLOCK
    n = np.arange(n16)[None, :]
    m = np.arange(LANES)[:, None]
    c_start, s_start = n * CMP_STRIDE, m * SEL_BLOCK
    overlap_t = ((c_start < s_start + SEL_BLOCK) & (c_start + CMP_BLOCK > s_start) & (n < n16 - 1) & (m < n_sel))
    slopes = _alibi_slopes().reshape(G, 1, J, 1)
    r = np.arange(tq, dtype=np.float32)[None, None, None, :]
    dist_c = r - CMP_STRIDE * np.arange(n16, dtype=np.float32)[None, :, None, None] - (CMP_BLOCK - 1)
    a_cmp = (slopes * dist_c).reshape(G, n16, J * tq)
    span = WINDOW + tq
    dist_w = r + WINDOW - np.arange(span, dtype=np.float32)[None, :, None, None]
    a_win = np.where((dist_w >= 0) & (dist_w < WINDOW), slopes * dist_w, -NEG_INF).reshape(G, span, J * tq)
    return (jnp.asarray(overlap_t.astype(np.float32), BF16), jnp.asarray(a_cmp.astype(np.float32)),
            jnp.asarray(a_win.astype(np.float32)))


def _nsa_a(zm, k_cmp, v_cmp_t, vw_t, cg_t, B, T):
    G, J, d, tq = NSA_KV_GROUPS, NSA_GROUP_SIZE, HEAD_DIM, Q_BLOCK
    nq = T // tq
    n16 = T // CMP_STRIDE
    n_back = WINDOW // tq
    span = WINDOW + tq
    qblk = BLK_CQ * (MIX_WIDTH // (J * d))
    kwblk = KV_BASE + 4 * G
    overlap_t, a_cmp, a_win = _nsa_consts(T)

    def past(qi, i):
        return jnp.maximum(qi - n_back + i, 0)

    in_specs = ([pl.BlockSpec((tq, J * d), lambda b, g, qi: (b * nq + qi, qblk + g)),
                 pl.BlockSpec((None, None, n16, d), lambda b, g, qi: (b, g, 0, 0)),
                 pl.BlockSpec((None, None, d, n16), lambda b, g, qi: (b, g, 0, 0))]
                + [pl.BlockSpec((tq, d), functools.partial(lambda b, g, qi, i: (b * nq + past(qi, i), kwblk + g), i=i))
                   for i in range(n_back + 1)]
                + [pl.BlockSpec((None, None, d, tq), functools.partial(lambda b, g, qi, i: (b, g, 0, past(qi, i)), i=i))
                   for i in range(n_back + 1)]
                + [pl.BlockSpec((None, 16, tq), lambda b, g, qi: (g, 0, b * nq + qi)),
                   pl.BlockSpec((LANES, n16), lambda b, g, qi: (0, 0)),
                   pl.BlockSpec((None, n16, J * tq), lambda b, g, qi: (g, 0, 0)),
                   pl.BlockSpec((None, span, J * tq), lambda b, g, qi: (g, 0, 0))])
    return pl.pallas_call(
        functools.partial(_nsa_a_kernel, n_back=n_back), grid=(B, G, nq), in_specs=in_specs,
        out_specs=[pl.BlockSpec((None, None, None, d, J * tq), lambda b, g, qi: (b, g, qi, 0, 0)),
                   pl.BlockSpec((None, None, LANES, tq), lambda b, g, qi: (b, g, 0, qi))],
        out_shape=[jax.ShapeDtypeStruct((B, G, nq, d, J * tq), F32), jax.ShapeDtypeStruct((B, G, LANES, T), BF16)],
        scratch_shapes=[pltpu.VMEM((J * tq, d), BF16), pltpu.VMEM((LANES, tq), F32), pltpu.VMEM((LANES, tq), F32)],
        compiler_params=_cparams(("parallel", "parallel", "parallel")), name="nsa_cmp_win",
    )(zm, k_cmp, v_cmp_t, *([zm] * (n_back + 1)), *([vw_t] * (n_back + 1)), cg_t, overlap_t, a_cmp, a_win)


def _nsa_b_kernel(cnt_ref, lst_ref, q_ref, ks_ref, vt_ref, sel_ref, e_ref, a_ref, srow_ref, cgt_ref, part_ref, o_ref,
                  q_s, acc_s, *, tk, nk):
    tq, d, J = Q_BLOCK, HEAD_DIM, NSA_GROUP_SIZE
    b, g, qi = pl.program_id(0), pl.program_id(1), pl.program_id(2)
    lin = (b * pl.num_programs(1) + g) * pl.num_programs(2) + qi
    t0 = qi * tq
    scale = d ** -0.5
    for j in range(J):
        q_s[j * tq:(j + 1) * tq, :] = (q_ref[:, j * d:(j + 1) * d].astype(F32) * scale).astype(BF16)
    acc_s[...] = jnp.zeros_like(acc_s)
    r_minus_c = lax.broadcasted_iota(jnp.int32, (tk, tq), 1) - lax.broadcasted_iota(jnp.int32, (tk, tq), 0)

    count = cnt_ref[lin]

    def scores(pos):
        kt = lst_ref[lin * nk + pos]
        k_t = ks_ref[pl.ds(pl.multiple_of(kt * tk, tk), tk), :]
        off = t0 - kt * tk
        picked = _dot(e_ref[kt], sel_ref[...])
        causal_off = jnp.where(pos < count, off, -tk * nk)
        keep = (picked > 0.5) & (r_minus_c + causal_off >= 0)
        s = _dot_nt(k_t, q_s[...]) - a_ref[...] + _lanes_x(jnp.where(keep, 0.0, NEG_INF), J)
        return s, srow_ref[...] * off.astype(F32), kt

    def body(i, carry):
        m_prev, l_prev = carry
        s_a, shift_a, kt_a = scores(2 * i)
        s_b, shift_b, kt_b = scores(2 * i + 1)
        m_new = jnp.maximum(m_prev, jnp.maximum(jnp.max(s_a, axis=0, keepdims=True) - shift_a,
                                                jnp.max(s_b, axis=0, keepdims=True) - shift_b))
        p_a = jnp.exp(s_a - (m_new + shift_a))
        p_b = jnp.exp(s_b - (m_new + shift_b))
        alpha = jnp.exp(m_prev - m_new)
        acc_s[...] = alpha * acc_s[...] + (_dot(vt_ref[kt_a], p_a.astype(BF16)) + _dot(vt_ref[kt_b], p_b.astype(BF16)))
        return m_new, alpha * l_prev + (jnp.sum(p_a, axis=0, keepdims=True) + jnp.sum(p_b, axis=0, keepdims=True))

    init = (jnp.full((1, J * tq), NEG_INF, F32), jnp.zeros((1, J * tq), F32))
    _, l = lax.fori_loop(0, (count + 1) // 2, body, init)
    sg_t = jax.nn.sigmoid(cgt_ref[...])
    g_sel = jnp.concatenate([sg_t[J + j:J + j + 1, :] for j in range(J)], axis=1)
    total_t = part_ref[...] + (g_sel / l) * acc_s[...]
    for j in range(J):
        o_ref[:, j * d:(j + 1) * d] = total_t[:, j * tq:(j + 1) * tq].T.astype(BF16)


def _nsa_b(zm, sel_t, cg_t, part_t, B, T, tk=512):
    G, J, d, tq = NSA_KV_GROUPS, NSA_GROUP_SIZE, HEAD_DIM, Q_BLOCK
    tk = min(tk, T)
    nq, nk = T // tq, T // tk
    qblk = BLK_CQ * (MIX_WIDTH // (J * d))
    ksblk = KV_BASE + 2 * G
    vs_col = (KV_BASE + 3 * G) * d
    per_tile = tk // SEL_BLOCK
    kt = np.arange(nk)[:, None, None]
    cc = np.arange(tk)[None, :, None]
    m = np.arange(LANES)[None, None, :]
    expand_t = jnp.asarray((m == kt * per_tile + cc // SEL_BLOCK).astype(np.float32), BF16)
    slopes_np = _alibi_slopes()
    r_minus_c = (np.arange(tq)[None, :] - np.arange(tk)[:, None]).astype(np.float32)
    alibi_t = jnp.asarray((slopes_np.reshape(G, 1, J, 1) * r_minus_c[None, :, None, :]).reshape(G, tk, J * tq))
    srow = jnp.asarray(np.repeat(slopes_np.reshape(G, J), tq, axis=1).reshape(G, 1, J * tq))
    vs_t = zm[:, vs_col:vs_col + G * d].reshape(B, nk, tk, G, d).transpose(0, 3, 1, 4, 2)

    active = sel_t[:, :, :nk * per_tile].reshape(B, G, nk, per_tile, nq, tq).max(axis=(3, 5)) > 0
    active = active.transpose(0, 1, 3, 2)
    tiles = jnp.arange(nk, dtype=jnp.int32)
    order = jnp.sort(jnp.where(active, tiles, tiles + nk), axis=-1) % nk
    counts = active.sum(axis=-1).astype(jnp.int32)

    grid_spec = pltpu.PrefetchScalarGridSpec(
        num_scalar_prefetch=2, grid=(B, G, nq),
        in_specs=[pl.BlockSpec((tq, J * d), lambda b, g, qi, c, o: (b * nq + qi, qblk + g)),
                  pl.BlockSpec((T, d), lambda b, g, qi, c, o: (b, ksblk + g)),
                  pl.BlockSpec((None, None, nk, d, tk), lambda b, g, qi, c, o: (b, g, 0, 0, 0)),
                  pl.BlockSpec((None, None, LANES, tq), lambda b, g, qi, c, o: (b, g, 0, qi)),
                  pl.BlockSpec((nk, tk, LANES), lambda b, g, qi, c, o: (0, 0, 0)),
                  pl.BlockSpec((None, tk, J * tq), lambda b, g, qi, c, o: (g, 0, 0)),
                  pl.BlockSpec((None, 1, J * tq), lambda b, g, qi, c, o: (g, 0, 0)),
                  pl.BlockSpec((None, 16, tq), lambda b, g, qi, c, o: (g, 0, b * nq + qi)),
                  pl.BlockSpec((None, None, None, d, J * tq), lambda b, g, qi, c, o: (b, g, qi, 0, 0))],
        out_specs=pl.BlockSpec((tq, J * d), lambda b, g, qi, c, o: (b * nq + qi, g)),
        scratch_shapes=[pltpu.VMEM((J * tq, d), BF16), pltpu.VMEM((d, J * tq), F32)])
    return pl.pallas_call(
        functools.partial(_nsa_b_kernel, tk=tk, nk=nk), grid_spec=grid_spec,
        out_shape=jax.ShapeDtypeStruct((B * T, MIX_WIDTH), BF16),
        compiler_params=_cparams(("parallel", "parallel", "parallel")), name="nsa_selected",
    )(counts.reshape(-1), order.reshape(-1).astype(jnp.int32), zm, zm, vs_t, sel_t, expand_t, alibi_t, srow, cg_t, part_t)


def _nsa(zm, zs, cmp_w, cmp_pe, B, T):
    G, J = NSA_KV_GROUPS, NSA_GROUP_SIZE
    n = B * T
    d = HEAD_DIM
    cg = zs[:, 2 * N_HEADS:2 * N_HEADS + 3 * N_HEADS].reshape(n, 3, G, J)
    cg_t = jnp.pad(cg.transpose(2, 1, 3, 0).reshape(G, 3 * J, n), ((0, 0), (0, 16 - 3 * J), (0, 0)))
    vw_col = (KV_BASE + 5 * G) * d
    vw_t = zm[:, vw_col:vw_col + G * d].reshape(B, T, G, d).transpose(0, 2, 3, 1)
    k_cmp, v_cmp_t = _compress(zm, cmp_w, cmp_pe, B, T)
    part_t, sel_t = _nsa_a(zm, k_cmp, v_cmp_t, vw_t, cg_t, B, T)
    return _nsa_b(zm, sel_t, cg_t, part_t, B, T)


def _merge_kernel(h_ref, y0, y1, y2, y3, g0, g1, g2, g3, b0, b1, b2, b3, o_ref):
    h = h_ref[...]
    acc = None
    for y, gw, bw in ((y0, g0, b0), (y1, g1, b1), (y2, g2, b2), (y3, g3, b3)):
        term = jax.nn.sigmoid(_dot(h, gw[...])) * _dot(y[...], bw[...])
        acc = term if acc is None else acc + term
    o_ref[...] = acc.astype(BF16)


def _merge(h, ys, w_gate, w_branch, tm=512, tn=256):
    n, dm = h.shape
    tm = min(tm, n)
    nj = dm // tn
    y_spec = pl.BlockSpec((tm, MIX_WIDTH), lambda i, j: (i, 0))
    gate_specs = [pl.BlockSpec((dm, tn), functools.partial(lambda i, j, b: (0, b * nj + j), b=b)) for b in range(N_BRANCH)]
    br_specs = [pl.BlockSpec((None, MIX_WIDTH, tn), functools.partial(lambda i, j, b: (b, 0, j), b=b)) for b in range(N_BRANCH)]
    return pl.pallas_call(
        _merge_kernel, grid=(n // tm, nj),
        in_specs=[pl.BlockSpec((tm, dm), lambda i, j: (i, 0))] + [y_spec] * N_BRANCH + gate_specs + br_specs,
        out_specs=pl.BlockSpec((tm, tn), lambda i, j: (i, j)),
        out_shape=jax.ShapeDtypeStruct((n, dm), BF16),
        compiler_params=_cparams(("parallel", "parallel")), name="gated_merge",
    )(h, *ys, *([w_gate] * N_BRANCH), *([w_branch] * N_BRANCH))


def _pack_bf16_pairs(y):
    half = y.shape[1] // 2
    bits = lax.bitcast_convert_type(y.astype(BF16).astype(F32), jnp.int32)
    return bits[:, :half] | lax.shift_right_logical(bits[:, half:], 16)


def _unpack_bf16_pairs(w):
    first = lax.bitcast_convert_type(w & jnp.int32(-65536), F32).astype(BF16)
    second = lax.bitcast_convert_type(lax.shift_left(w, 16), F32).astype(BF16)
    return first, second


def _expert_kernel(be_ref, nact_ref, idx_ref, idx_next_ref, hp_hbm, w1_ref, w3_ref, w2_ref, o_ref,
                   xbuf, sems, w1b, w3b, w2b):
    i = pl.program_id(0)
    n_act = nact_ref[0]
    slot = lax.rem(i, 2)
    e = be_ref[i]
    prev = be_ref[jnp.maximum(i - 1, 0)]
    half = hp_hbm.shape[1]

    @pl.when((i == 0) & (n_act > 0))
    def _():
        _row_gather(idx_ref, hp_hbm, xbuf.at[0], sems.at[0], MOE_BLOCK)

    @pl.when(i + 1 < n_act)
    def _():
        _row_gather(idx_next_ref, hp_hbm, xbuf.at[1 - slot], sems.at[1 - slot], MOE_BLOCK)

    @pl.when((i == 0) | (prev != e))
    def _():
        w1b[...] = w1_ref[...].astype(BF16)
        w3b[...] = w3_ref[...].astype(BF16)
        w2b[...] = w2_ref[...].astype(BF16)

    @pl.when(i < n_act)
    def _():
        _row_gather_wait(hp_hbm, xbuf.at[slot], sems.at[slot], MOE_BLOCK)
        x_a, x_b = _unpack_bf16_pairs(xbuf[slot])
        a1 = _dot(x_a, w1b[0:half, :]) + _dot(x_b, w1b[half:, :])
        a3 = _dot(x_a, w3b[0:half, :]) + _dot(x_b, w3b[half:, :])
        a = (a1 * jax.nn.sigmoid(a1)) * a3
        o_ref[...] = _dot(a.astype(BF16), w2b[...])

    @pl.when(i >= n_act)
    def _():
        o_ref[...] = jnp.zeros_like(o_ref)


def _experts(hp, tok_pad, blk_exp, n_active, w1, w3, w2, layer):
    half = hp.shape[1]
    dm = 2 * half
    n_blk = tok_pad.shape[0] // MOE_BLOCK
    ff = w1.shape[-1]
    idx = tok_pad.reshape(n_blk, 1, MOE_BLOCK)
    smem_blk = lambda f: pl.BlockSpec((None, 1, MOE_BLOCK), f, memory_space=pltpu.SMEM)
    grid_spec = pltpu.PrefetchScalarGridSpec(
        num_scalar_prefetch=2, grid=(n_blk,),
        in_specs=[smem_blk(lambda i, be, na: (i, 0, 0)),
                  smem_blk(lambda i, be, na: (jnp.minimum(i + 1, n_blk - 1), 0, 0)),
                  pl.BlockSpec(memory_space=pl.ANY),
                  pl.BlockSpec((None, None, dm, ff), lambda i, be, na: (layer, be[i], 0, 0)),
                  pl.BlockSpec((None, None, dm, ff), lambda i, be, na: (layer, be[i], 0, 0)),
                  pl.BlockSpec((None, None, ff, dm), lambda i, be, na: (layer, be[i], 0, 0))],
        out_specs=pl.BlockSpec((MOE_BLOCK, dm), lambda i, be, na: (i, 0)),
        scratch_shapes=[pltpu.VMEM((2, MOE_BLOCK, half), jnp.int32), pltpu.SemaphoreType.DMA((2,)),
                        pltpu.VMEM((dm, ff), BF16), pltpu.VMEM((dm, ff), BF16), pltpu.VMEM((ff, dm), BF16)])
    return pl.pallas_call(
        _expert_kernel, grid_spec=grid_spec,
        out_shape=jax.ShapeDtypeStruct((n_blk * MOE_BLOCK, dm), F32),
        compiler_params=_cparams(("arbitrary",), vmem=MOE_VMEM_LIMIT), name="moe_experts",
    )(blk_exp, n_active, idx, idx, hp, w1, w3, w2)


def _dispatch_plan(e_idx):
    n, K = e_idx.shape
    nk = n * K
    i32 = jnp.int32
    flat_e = e_idx.reshape(-1)
    order = jnp.argsort(flat_e).astype(i32)
    rank = jnp.argsort(order).astype(i32)
    experts = jnp.arange(N_EXPERTS, dtype=i32)
    hot = flat_e[:, None] == experts[None, :]
    counts = jnp.sum(hot.astype(i32), axis=0)
    padded = (counts + MOE_BLOCK - 1) // MOE_BLOCK * MOE_BLOCK
    pad_end = jnp.cumsum(padded)
    pad_start = pad_end - padded
    start = jnp.cumsum(counts) - counts
    lookup = lambda table, onehot: jnp.sum(jnp.where(onehot, table[None, :], 0), axis=1)
    dest = lookup(pad_start - start, hot) + rank
    n_blk = (nk + MOE_BLOCK - 1) // MOE_BLOCK + N_EXPERTS
    blk_first = jnp.arange(n_blk, dtype=i32) * MOE_BLOCK
    blk_exp = jnp.minimum(jnp.sum((pad_end[None, :] <= blk_first[:, None]).astype(i32), axis=1), N_EXPERTS - 1)
    blk_hot = blk_exp[:, None] == experts[None, :]
    within = jnp.arange(MOE_BLOCK, dtype=i32)[None, :]
    j = (blk_first - lookup(pad_start, blk_hot))[:, None] + within
    sorted_pos = jnp.clip(lookup(start, blk_hot)[:, None] + j, 0, nk - 1)
    src = jnp.take(order, sorted_pos.reshape(-1), mode="clip") // K
    tok_pad = jnp.where((j < lookup(counts, blk_hot)[:, None]).reshape(-1), src, n - 1)
    n_active = (pad_end[-1] // MOE_BLOCK).astype(i32).reshape(1)
    return tok_pad, blk_exp.astype(i32), n_active, dest.reshape(n, K)


def _moe(h, route, w1, w3, w2, layer):
    e_idx = route[:, :TOP_K_IN_GROUP].astype(jnp.int32)
    tok_pad, blk_exp, n_active, dest = _dispatch_plan(e_idx)
    y = _experts(h, tok_pad, blk_exp, n_active, w1, w3, w2, layer)
    return y, dest


PACK_TILE = 512


def _pack_kernel(offs_ref, wt_ref, o_ref):
    del offs_ref
    o_ref[...] = wt_ref[0].T.astype(BF16)


def _pack_columns(w_in_t, layer, col_offsets):
    _, _, k = w_in_t.shape
    nt = len(col_offsets)
    grid_spec = pltpu.PrefetchScalarGridSpec(
        num_scalar_prefetch=1, grid=(nt,),
        in_specs=[pl.BlockSpec((pl.Element(1), pl.Element(PACK_TILE), pl.Element(k)),
                               lambda j, offs: (layer, pl.multiple_of(offs[j], 8), 0))],
        out_specs=pl.BlockSpec((k, PACK_TILE), lambda j, offs: (0, j)))
    return pl.pallas_call(
        _pack_kernel, grid_spec=grid_spec,
        out_shape=jax.ShapeDtypeStruct((k, nt * PACK_TILE), BF16),
        compiler_params=_cparams(("parallel",)), name="pack_w_in",
    )(jnp.asarray(col_offsets, jnp.int32), w_in_t)


def _tile_offsets(groups):
    offs = []
    for i in groups:
        assert COL_WIDTHS[i] % PACK_TILE == 0 or COL_WIDTHS[i] < PACK_TILE
        offs += list(range(COL_OFFSETS[i], COL_OFFSETS[i + 1], PACK_TILE))
    return offs


def _pack_w_in(w_in_t, layer):
    wide = [i for i in MAIN_ORDER if COL_WIDTHS[i] >= PACK_TILE]
    narrow = [i for i in MAIN_ORDER if COL_WIDTHS[i] < PACK_TILE]
    assert MAIN_ORDER == tuple(wide + narrow) and narrow == list(range(narrow[0], narrow[-1] + 1))
    kv_lo, kv_hi = COL_OFFSETS[narrow[0]], COL_OFFSETS[narrow[-1] + 1]
    assert (kv_hi - kv_lo) % PACK_TILE == 0
    main_offs = _tile_offsets(wide) + list(range(kv_lo, kv_hi, PACK_TILE))
    w_main = _pack_columns(w_in_t, layer, main_offs)
    w_gate = _pack_columns(w_in_t, layer, _tile_offsets([I_MERGE]))
    rows = lambda i: w_in_t[layer, COL_OFFSETS[i]:COL_OFFSETS[i + 1]]
    small_t = jnp.concatenate([rows(I_AI), rows(I_AF), rows(I_CG)], axis=0)
    small_t = jnp.pad(small_t, ((0, LANES - small_t.shape[0]), (0, 0)))
    return w_main, small_t, w_gate


def _router_weights(wg, bg, we, be):
    w = jnp.concatenate([wg, we], axis=1)
    w = jnp.pad(w, ((0, 0), (0, LANES - w.shape[1])))
    hi = w.astype(BF16)
    lo = (w - hi.astype(F32)).astype(BF16)
    b = jnp.concatenate([bg, be])
    b = jnp.pad(b, (0, LANES - b.shape[0])).reshape(1, LANES).astype(F32)
    return hi, lo, b


def kernel(x, w_in, mlstm_conv, mlstm_gate_bias, mlstm_norm, ret_norm, nsa_cmp_w, nsa_cmp_pe, sgu_norm, sgu_w, sgu_b, w_branch, w_out, norm_mix, norm_ffn, router_group_w, router_group_b, router_expert_w, router_expert_b, expert_w1, expert_w3, expert_w2, norm_final):
    B, T, D = x.shape
    n = B * T
    xs, moe = x.reshape(n, D), None
    w_in_t = jnp.swapaxes(w_in, 1, 2)
    for l in range(DEPTH):
        w_main, w_small_t, w_gate = _pack_w_in(w_in_t, l)
        if moe is None:
            h, = _norm(xs, norm_mix[l])
        else:
            xs, h = _norm(xs, norm_mix[l], moe=moe)
        zm = _matmul(h, w_main, out_dtype=BF16, name="proj_main")
        zs = _matmul(h, w_small_t, out_dtype=F32, w_rows_are_outputs=True, name="proj_small")
        zst = zs[:, :2 * N_HEADS].T
        ya = _mlstm(zm, zs, zst, mlstm_conv[l], mlstm_gate_bias[l], mlstm_norm[l], B, T)
        yb = _retention(zm, ret_norm[l], B, T)
        yc = _nsa(zm, zs, nsa_cmp_w[l], nsa_cmp_pe[l], B, T)
        yd = _sgu(zm, sgu_norm[l], sgu_w[l], sgu_b[l], n)
        merged = _merge(h, (ya, yb, yc, yd), w_gate, w_branch[l].astype(BF16))
        x_mid = _matmul(merged, w_out[l].astype(BF16), out_dtype=F32, residual=xs, name="proj_out")
        h2, route = _norm(x_mid, norm_ffn[l], rows=256,
                          router_w=_router_weights(router_group_w[l], router_group_b[l],
                                                   router_expert_w[l], router_expert_b[l]))
        y_exp, dest = _moe(h2, route, expert_w1, expert_w3, expert_w2, l)
        xs, moe = x_mid, (y_exp, dest, route)
    out, = _norm(xs, norm_final, moe=moe, final=True)
    return out.reshape(B, T, D)
```

```python
import functools

import numpy as np
import jax
import jax.numpy as jnp
from jax import lax
from jax.experimental import pallas as pl
from jax.experimental.pallas import tpu as pltpu

F32 = jnp.float32
BF16 = jnp.bfloat16

D_MODEL = 4096
DEPTH = 2
HEAD_DIM = 128
N_BRANCH = 4
MIX_WIDTH = D_MODEL // N_BRANCH
N_HEADS = MIX_WIDTH // HEAD_DIM
CONV_WIDTH = 4
CONV_TAIL = 16
MLSTM_CHUNK = 128
RET_CHUNK = 128
NSA_KV_GROUPS = 2
NSA_GROUP_SIZE = N_HEADS // NSA_KV_GROUPS
KV_WIDTH = NSA_KV_GROUPS * HEAD_DIM
CMP_BLOCK = 32
CMP_STRIDE = 16
SEL_BLOCK = 64
SEL_TOPN = 16
WINDOW = 512
Q_BLOCK = 128
SGU_CHUNK = 128
N_GROUPS = 8
EXPERTS_PER_GROUP = 8
N_EXPERTS = N_GROUPS * EXPERTS_PER_GROUP
TOP_K_IN_GROUP = 2
EXPERT_FF = 256
MOE_BLOCK = 256
RMS_EPS = 1e-6
NEG_INF = -1e30
FORCE_SCORE = 1e9

COL_WIDTHS = (
    MIX_WIDTH, MIX_WIDTH, MIX_WIDTH, MIX_WIDTH, N_HEADS, N_HEADS,
    MIX_WIDTH, MIX_WIDTH, MIX_WIDTH, MIX_WIDTH,
    MIX_WIDTH, KV_WIDTH, KV_WIDTH, KV_WIDTH, KV_WIDTH, KV_WIDTH, KV_WIDTH, 3 * N_HEADS,
    MIX_WIDTH, MIX_WIDTH,
    N_BRANCH * D_MODEL,
)
COL_OFFSETS = tuple(int(v) for v in np.concatenate([[0], np.cumsum(COL_WIDTHS)]))
(I_AQ, I_AK, I_AV, I_AO, I_AI, I_AF, I_BQ, I_BK, I_BV, I_BG, I_CQ, I_CKC, I_CVC, I_CKS, I_CVS,
 I_CKW, I_CVW, I_CG, I_DU, I_DV, I_MERGE) = range(21)

MAIN_ORDER = (I_AQ, I_AK, I_AV, I_AO, I_BQ, I_BK, I_BV, I_BG, I_CQ, I_DU, I_DV,
              I_CKC, I_CVC, I_CKS, I_CVS, I_CKW, I_CVW)
MAIN_WIDTH = sum(COL_WIDTHS[i] for i in MAIN_ORDER)
LANES = 128
BLK_AQ, BLK_AK, BLK_AV, BLK_AO, BLK_BQ, BLK_BK, BLK_BV, BLK_BG, BLK_CQ, BLK_DU, BLK_DV = range(11)
KV_BASE = 11 * MIX_WIDTH // HEAD_DIM
VMEM_LIMIT = 48 * 1024 * 1024
MOE_VMEM_LIMIT = 56 * 1024 * 1024


def _cparams(sem, vmem=VMEM_LIMIT):
    return pltpu.CompilerParams(dimension_semantics=sem, vmem_limit_bytes=vmem)


def _dot(a, b):
    return jnp.dot(a, b, preferred_element_type=F32)


def _dot_nt(a, b):
    return lax.dot_general(a, b, (((1,), (1,)), ((), ())), preferred_element_type=F32)


def _split3(x):
    hi = x.astype(BF16)
    r1 = x - hi.astype(F32)
    mid = r1.astype(BF16)
    lo = (r1 - mid.astype(F32)).astype(BF16)
    return hi, mid, lo


def _dot_exact_rhs(x, m_bf16):
    hi, mid, lo = _split3(x)
    return _dot(hi, m_bf16) + _dot(mid, m_bf16) + _dot(lo, m_bf16)


def _log_sigmoid(x):
    return jnp.minimum(x, 0.0) - jnp.log1p(jnp.exp(-jnp.abs(x)))


def _row_gather(idx_ref, src_hbm, dst, sem, count):
    for r in range(count):
        pltpu.make_async_copy(src_hbm.at[pl.ds(idx_ref[0, r], 1)], dst.at[pl.ds(r, 1)], sem).start(priority=r % 2)


def _row_gather_wait(src_hbm, dst, sem, count):
    for r in range(count):
        pltpu.make_async_copy(src_hbm.at[pl.ds(0, 1)], dst.at[pl.ds(r, 1)], sem).wait()


def _norm_kernel(*refs, moe, router, final):
    x = refs[0][...]
    rows = x.shape[0]
    pos = 1
    n_scratch = 0
    if moe:
        idx_ref, idx_next_ref, y_hbm, rt_ref = refs[pos:pos + 4]
        pos += 4
        gbuf, sems = refs[-2:]
        n_scratch = 2
        i, last = pl.program_id(0), pl.num_programs(0) - 1
        slot = lax.rem(i, 2)

        @pl.when(i == 0)
        def _():
            _row_gather(idx_ref, y_hbm, gbuf.at[0], sems.at[0], 2 * rows)

        @pl.when(i < last)
        def _():
            _row_gather(idx_next_ref, y_hbm, gbuf.at[1 - slot], sems.at[1 - slot], 2 * rows)

        _row_gather_wait(y_hbm, gbuf.at[slot], sems.at[slot], 2 * rows)
        x = x + (rt_ref[:, 2:3] * gbuf[slot, 0:rows, :] + rt_ref[:, 3:4] * gbuf[slot, rows:2 * rows, :])
    g_ref = refs[pos]
    pos += 1
    if router:
        whi_ref, wlo_ref, rb_ref = refs[pos:pos + 3]
        pos += 3
    outs = refs[pos:len(refs) - n_scratch]
    y = x * lax.rsqrt(jnp.mean(x * x, axis=-1, keepdims=True) + RMS_EPS) * g_ref[...]
    o = 0
    if moe and not final:
        outs[o][...] = x
        o += 1
    if final:
        outs[o][...] = y
    elif router:
        outs[o][...] = _pack_bf16_pairs(y)
    else:
        outs[o][...] = y.astype(BF16)
    o += 1
    if router:
        y_hi = y.astype(BF16)
        y_lo = (y - y_hi.astype(F32)).astype(BF16)
        lg = _dot(y_hi, whi_ref[...]) + (_dot(y_hi, wlo_ref[...]) + _dot(y_lo, whi_ref[...])) + rb_ref[...]
        outs[o][...] = _route_rows(lg)


def _first_max(v, lane):
    mx = jnp.max(v, axis=-1, keepdims=True)
    return mx, jnp.min(jnp.where(v == mx, lane, LANES), axis=-1, keepdims=True)


def _route_rows(lg):
    low = -3.0e38
    lane = lax.broadcasted_iota(jnp.int32, lg.shape, 1)
    is_grp = lane < N_GROUPS
    gmax, g_idx = _first_max(jnp.where(is_grp, lg, low), lane)
    g_w = 1.0 / jnp.sum(jnp.where(is_grp, jnp.exp(lg - gmax), 0.0), axis=-1, keepdims=True)
    assert EXPERTS_PER_GROUP & (EXPERTS_PER_GROUP - 1) == 0
    grp_of_lane = lax.shift_right_arithmetic(lane - N_GROUPS, EXPERTS_PER_GROUP.bit_length() - 1)
    in_grp = (lane >= N_GROUPS) & (lane < N_GROUPS + N_EXPERTS) & (grp_of_lane == g_idx)
    el = jnp.where(in_grp, lg, low)
    v0, i0 = _first_max(el, lane)
    v1, i1 = _first_max(jnp.where(lane == i0, low, el), lane)
    e1 = jnp.exp(v1 - v0)
    p0 = g_w / (1.0 + e1)
    vals = (i0 - N_GROUPS).astype(F32), (i1 - N_GROUPS).astype(F32), p0, p0 * e1
    out = jnp.zeros(lg.shape, F32)
    for k, v in enumerate(vals):
        out = jnp.where(lane == k, v, out)
    return out


def _norm(x, gain, *, moe=None, router_w=None, final=False, rows=128):
    n, d = x.shape
    rows = min(rows, n)
    steps = n // rows
    row_spec = pl.BlockSpec((rows, d), lambda i: (i, 0))
    lane_spec = pl.BlockSpec((rows, LANES), lambda i: (i, 0))
    in_specs, args, scratch = [row_spec], [x], []
    if moe is not None:
        y, dest, route = moe
        K = dest.shape[1]
        idx = dest.reshape(steps, rows, K).transpose(0, 2, 1).reshape(steps, 1, K * rows)
        smem_blk = lambda f: pl.BlockSpec((None, 1, K * rows), f, memory_space=pltpu.SMEM)
        in_specs += [smem_blk(lambda i: (i, 0, 0)), smem_blk(lambda i: (jnp.minimum(i + 1, steps - 1), 0, 0)),
                     pl.BlockSpec(memory_space=pl.ANY), lane_spec]
        args += [idx, idx, y, route]
        scratch = [pltpu.VMEM((2, K * rows, d), F32), pltpu.SemaphoreType.DMA((2,))]
    in_specs.append(pl.BlockSpec((1, d), lambda i: (0, 0)))
    args.append(gain.reshape(1, d))
    out_shape, out_specs = [], []
    if moe is not None and not final:
        out_shape.append(jax.ShapeDtypeStruct((n, d), F32))
        out_specs.append(row_spec)
    if router_w is not None:
        out_shape.append(jax.ShapeDtypeStruct((n, d // 2), jnp.int32))
        out_specs.append(pl.BlockSpec((rows, d // 2), lambda i: (i, 0)))
    else:
        out_shape.append(jax.ShapeDtypeStruct((n, d), F32 if final else BF16))
        out_specs.append(row_spec)
    if router_w is not None:
        whi, wlo, rb = router_w
        in_specs += [pl.BlockSpec((d, LANES), lambda i: (0, 0))] * 2 + [pl.BlockSpec((1, LANES), lambda i: (0, 0))]
        args += [whi, wlo, rb]
        out_shape.append(jax.ShapeDtypeStruct((n, LANES), F32))
        out_specs.append(lane_spec)
    return pl.pallas_call(
        functools.partial(_norm_kernel, moe=moe is not None, router=router_w is not None, final=final),
        grid=(steps,), in_specs=in_specs, out_specs=out_specs, out_shape=out_shape, scratch_shapes=scratch,
        compiler_params=_cparams(("arbitrary",)), name="rmsnorm",
    )(*args)


def _mm_kernel(a_ref, w_ref, *rest, has_res, w_rows_are_outputs):
    w = w_ref[...].astype(BF16)
    acc = _dot_nt(a_ref[...], w) if w_rows_are_outputs else _dot(a_ref[...], w)
    if has_res:
        acc = acc + rest[0][...]
    rest[-1][...] = acc.astype(rest[-1].dtype)


def _matmul(a, w, *, out_dtype, residual=None, tm=1024, tn=512, w_rows_are_outputs=False, name="matmul"):
    m, k = a.shape
    nc = w.shape[0] if w_rows_are_outputs else w.shape[1]
    tm, tn = min(tm, m), min(tn, nc)
    w_spec = pl.BlockSpec((tn, k), lambda i, j: (j, 0)) if w_rows_are_outputs else pl.BlockSpec((k, tn), lambda i, j: (0, j))
    in_specs = [pl.BlockSpec((tm, k), lambda i, j: (i, 0)), w_spec]
    args = [a, w]
    if residual is not None:
        in_specs.append(pl.BlockSpec((tm, tn), lambda i, j: (i, j)))
        args.append(residual)
    return pl.pallas_call(
        functools.partial(_mm_kernel, has_res=residual is not None, w_rows_are_outputs=w_rows_are_outputs),
        grid=(m // tm, nc // tn), in_specs=in_specs,
        out_specs=pl.BlockSpec((tm, tn), lambda i, j: (i, j)),
        out_shape=jax.ShapeDtypeStruct((m, nc), out_dtype),
        compiler_params=_cparams(("parallel", "parallel")), name=name,
    )(*args)


def _mlstm_kernel(aq_ref, ak_ref, av_ref, ao_ref, zs_ref, zst_ref, cw_ref, gbc_ref, gbr_ref, gain_ref,
                  tril_ref, triu_ref, shift_ref, o_ref, tail_s, qk_s, ct_s, n_s, m_s):
    L, d, H, W = MLSTM_CHUNK, HEAD_DIM, N_HEADS, MIX_WIDTH
    c = pl.program_id(1)

    @pl.when(c == 0)
    def _():
        tail_s[...] = jnp.zeros_like(tail_s)
        ct_s[...] = jnp.zeros_like(ct_s)
        n_s[...] = jnp.zeros_like(n_s)
        m_s[...] = jnp.zeros_like(m_s)

    x_cur = jnp.concatenate([aq_ref[...], ak_ref[...]], axis=1)
    x_ext = jnp.concatenate([tail_s[...], x_cur], axis=0)
    conv = cw_ref[CONV_WIDTH - 1:CONV_WIDTH, :] * x_cur.astype(F32)
    for j in range(CONV_WIDTH - 1):
        conv = conv + cw_ref[j:j + 1, :] * _dot(shift_ref[j], x_ext)
    tail_s[...] = x_cur[L - CONV_TAIL:, :]
    qk_s[...] = conv * jax.nn.sigmoid(conv)

    pre_c = zs_ref[...] + gbc_ref[...]
    ls_c = _log_sigmoid(pre_c)
    hi, mid, lo = _split3(ls_c)
    tril = tril_ref[...]
    bc = _dot(tril, hi) + _dot(tril, mid) + _dot(tril, lo)
    pre_r = zst_ref[...] + gbr_ref[...]
    br = _dot_exact_rhs(_log_sigmoid(pre_r), triu_ref[...])

    row = lax.broadcasted_iota(jnp.int32, (L, L), 0)
    col = lax.broadcasted_iota(jnp.int32, (L, L), 1)
    causal = row >= col
    scale = d ** -0.5
    for h in range(H):
        sl = slice(h * d, (h + 1) * d)
        q = qk_s[:, sl] * scale
        k = qk_s[:, W + h * d:W + (h + 1) * d]
        v = av_ref[:, sl]
        i_col, b_col = pre_c[:, h:h + 1], bc[:, H + h:H + h + 1]
        i_row, b_row = pre_r[h:h + 1, :], br[H + h:H + h + 1, :]
        m_prev = m_s[h:h + 1, 0:1]
        log_d = jnp.where(causal, b_col - b_row + i_row, NEG_INF)
        log_inter = b_col + m_prev
        m_row = jnp.maximum(jnp.max(log_d, axis=-1, keepdims=True), log_inter)
        qb, kb = q.astype(BF16), k.astype(BF16)
        s = _dot_nt(qb, kb) * jnp.exp(log_d - m_row)
        w_inter = jnp.exp(log_inter - m_row)
        num = _dot(s.astype(BF16), v) + w_inter * _dot(qb, ct_s[h].astype(BF16))
        den = jnp.sum(s, axis=-1, keepdims=True) + w_inter * jnp.sum(q * n_s[h:h + 1, :], axis=-1, keepdims=True)
        hh = num / jnp.maximum(jnp.abs(den), jnp.exp(-m_row))
        b_last = b_row[:, L - 1:L]
        log_w_row = b_last - b_row + i_row
        m_new = jnp.maximum(b_last + m_prev, jnp.max(log_w_row, axis=-1, keepdims=True))
        w_col = jnp.exp(b_last - b_col + i_col - m_new)
        decay = jnp.exp(b_last + m_prev - m_new)
        vw = (v.astype(F32) * w_col).astype(BF16)
        ct_s[h] = decay * ct_s[h] + _dot(k.T.astype(BF16), vw)
        n_s[h:h + 1, :] = decay * n_s[h:h + 1, :] + jnp.sum(k * w_col, axis=0, keepdims=True)
        m_s[h:h + 1, :] = jnp.broadcast_to(m_new, (1, LANES))
        hn = hh * lax.rsqrt(jnp.mean(hh * hh, axis=-1, keepdims=True) + RMS_EPS)
        o_ref[:, sl] = (hn * gain_ref[:, sl] * jax.nn.sigmoid(ao_ref[:, sl].astype(F32))).astype(BF16)


def _tri_consts(L):
    r = np.arange(L)
    tril = (r[:, None] >= r[None, :]).astype(np.float32)
    return jnp.asarray(tril, BF16), jnp.asarray(tril.T, BF16)


def _mlstm(zm, zs, zst, conv_w, gate_bias, gain, B, T):
    L, W = MLSTM_CHUNK, MIX_WIDTH
    nc = T // L
    n = B * T
    gbc = jnp.zeros((1, LANES), F32).at[0, :2 * N_HEADS].set(gate_bias.reshape(-1))
    gbr = jnp.broadcast_to(gate_bias.reshape(2 * N_HEADS, 1), (2 * N_HEADS, L)).astype(F32)
    tril, triu = _tri_consts(L)
    t_idx = np.arange(L)[:, None]
    c_idx = np.arange(CONV_TAIL + L)[None, :]
    shift = jnp.asarray(np.stack([(c_idx == CONV_TAIL + t_idx - (CONV_WIDTH - 1 - j)) for j in range(CONV_WIDTH - 1)])
                        .astype(np.float32), BF16)

    def blk(b_idx):
        return pl.BlockSpec((L, W), lambda b, c: (b * nc + c, b_idx))

    const = lambda shape: pl.BlockSpec(shape, lambda b, c: (0,) * len(shape))
    return pl.pallas_call(
        _mlstm_kernel, grid=(B, nc),
        in_specs=[blk(BLK_AQ), blk(BLK_AK), blk(BLK_AV), blk(BLK_AO),
                  pl.BlockSpec((L, LANES), lambda b, c: (b * nc + c, 0)),
                  pl.BlockSpec((2 * N_HEADS, L), lambda b, c: (0, b * nc + c)),
                  const((CONV_WIDTH, 2 * W)), const((1, LANES)), const((2 * N_HEADS, L)), const((1, W)),
                  const((L, L)), const((L, L)), const((CONV_WIDTH - 1, L, CONV_TAIL + L))],
        out_specs=pl.BlockSpec((L, W), lambda b, c: (b * nc + c, 0)),
        out_shape=jax.ShapeDtypeStruct((n, W), BF16),
        scratch_shapes=[pltpu.VMEM((CONV_TAIL, 2 * W), BF16), pltpu.VMEM((L, 2 * W), F32),
                        pltpu.VMEM((N_HEADS, HEAD_DIM, HEAD_DIM), F32), pltpu.VMEM((N_HEADS, HEAD_DIM), F32),
                        pltpu.VMEM((N_HEADS, LANES), F32)],
        compiler_params=_cparams(("parallel", "arbitrary")), name="mlstm",
    )(zm, zm, zm, zm, zs, zst, conv_w, gbc, gbr, gain.reshape(1, W), tril, triu, shift)


def _ret_kernel(q_ref, k_ref, v_ref, g_ref, dec_ref, qd_ref, kd_ref, gain_ref, o_ref, r_s, *, chunk_decay):
    L, d, H = RET_CHUNK, HEAD_DIM, N_HEADS
    c = pl.program_id(1)

    @pl.when(c == 0)
    def _():
        r_s[...] = jnp.zeros_like(r_s)

    scale = d ** -0.5
    for h in range(H):
        sl = slice(h * d, (h + 1) * d)
        q = q_ref[:, sl].astype(F32)
        k = k_ref[:, sl].astype(F32) * scale
        v = v_ref[:, sl]
        s = _dot_nt(q.astype(BF16), k.astype(BF16)) * dec_ref[h]
        o = _dot(s.astype(BF16), v) + _dot((q * qd_ref[:, sl]).astype(BF16), r_s[h].astype(BF16))
        kd = (k * kd_ref[:, sl]).T.astype(BF16)
        r_s[h] = chunk_decay[h] * r_s[h] + _dot(kd, v)
        on = o * lax.rsqrt(jnp.mean(o * o, axis=-1, keepdims=True) + RMS_EPS)
        gate = g_ref[:, sl].astype(F32)
        o_ref[:, sl] = (on * gain_ref[:, sl] * (gate * jax.nn.sigmoid(gate))).astype(BF16)


def _retention(zm, gain, B, T):
    L, W, H, d = RET_CHUNK, MIX_WIDTH, N_HEADS, HEAD_DIM
    nc = T // L
    log_gamma = np.log(np.float32(1.0) - np.float32(2.0) ** (-5.0 - np.arange(H, dtype=np.float32))).astype(np.float32)
    pos = np.arange(L, dtype=np.float32)
    diff = pos[:, None] - pos[None, :]
    decay = np.where(diff >= 0, np.exp(log_gamma[:, None, None] * np.maximum(diff, 0.0)), 0.0).astype(np.float32)
    q_decay = np.exp(log_gamma[:, None] * (pos + 1.0)).astype(np.float32)
    k_decay = np.exp(log_gamma[:, None] * (L - 1.0 - pos)).astype(np.float32)
    chunk_decay = tuple(float(v) for v in np.exp(log_gamma * np.float32(L)).astype(np.float32))
    qd = jnp.asarray(np.repeat(q_decay.T, d, axis=1))
    kd = jnp.asarray(np.repeat(k_decay.T, d, axis=1))

    def blk(b_idx):
        return pl.BlockSpec((L, W), lambda b, c: (b * nc + c, b_idx))

    const = lambda shape: pl.BlockSpec(shape, lambda b, c: (0,) * len(shape))
    return pl.pallas_call(
        functools.partial(_ret_kernel, chunk_decay=chunk_decay), grid=(B, nc),
        in_specs=[blk(BLK_BQ), blk(BLK_BK), blk(BLK_BV), blk(BLK_BG),
                  const((H, L, L)), const((L, W)), const((L, W)), const((1, W))],
        out_specs=pl.BlockSpec((L, W), lambda b, c: (b * nc + c, 0)),
        out_shape=jax.ShapeDtypeStruct((B * T, W), BF16),
        scratch_shapes=[pltpu.VMEM((H, d, d), F32)],
        compiler_params=_cparams(("parallel", "arbitrary")), name="retention",
    )(zm, zm, zm, zm, jnp.asarray(decay), qd, kd, gain.reshape(1, W))


def _gelu(x):
    return 0.5 * x * (1.0 + jnp.tanh(0.7978845608028654 * (x + 0.044715 * (x * x * x))))


def _sgu_kernel(u_ref, v_ref, ng_ref, w_ref, b_ref, o_ref):
    L, d, H = SGU_CHUNK, HEAD_DIM, N_HEADS
    v = _gelu(v_ref[...].astype(F32))
    vn = (v * lax.rsqrt(jnp.mean(v * v, axis=-1, keepdims=True) + RMS_EPS) * ng_ref[...]).astype(BF16)
    u = _gelu(u_ref[...].astype(F32))
    row = lax.broadcasted_iota(jnp.int32, (L, L), 0)
    col = lax.broadcasted_iota(jnp.int32, (L, L), 1)
    for g in range(H):
        sl = slice(g * d, (g + 1) * d)
        wm = jnp.where(row >= col, w_ref[g], 0.0).astype(BF16)
        mixed = _dot(wm, vn[:, sl]) + b_ref[:, sl]
        o_ref[:, sl] = (u[:, sl] * mixed).astype(BF16)


def _sgu(zm, norm_g, w_s, b_s, n):
    L, W, H, d = SGU_CHUNK, MIX_WIDTH, N_HEADS, HEAD_DIM
    bsb = jnp.repeat(b_s.T, d, axis=1)
    const = lambda shape: pl.BlockSpec(shape, lambda i: (0,) * len(shape))
    return pl.pallas_call(
        _sgu_kernel, grid=(n // L,),
        in_specs=[pl.BlockSpec((L, W), lambda i: (i, BLK_DU)), pl.BlockSpec((L, W), lambda i: (i, BLK_DV)),
                  const((1, W)), const((H, L, L)), const((L, W))],
        out_specs=pl.BlockSpec((L, W), lambda i: (i, 0)),
        out_shape=jax.ShapeDtypeStruct((n, W), BF16),
        compiler_params=_cparams(("parallel",)), name="sgu",
    )(zm, zm, norm_g.reshape(1, W), w_s, bsb)


def _compress_kernel(k_ref, v_ref, w_ref, pe_ref, ko_ref, vo_ref, buf):
    t = k_ref.shape[0]
    n16 = t // CMP_STRIDE
    buf[t:t + CMP_STRIDE, :] = jnp.zeros((CMP_STRIDE, HEAD_DIM), F32)
    for which, (src, dst) in enumerate(((k_ref, ko_ref), (v_ref, vo_ref))):
        buf[0:t, :] = src[...].astype(F32)
        acc = jnp.zeros((n16, HEAD_DIM), F32)
        for l in range(CMP_BLOCK):
            x = buf[pl.ds(l, n16, stride=CMP_STRIDE), :]
            acc = acc + _dot((x + pe_ref[which, l:l + 1, :]).astype(BF16), w_ref[which, l].astype(BF16))
        dst[...] = (acc.T if which else acc).astype(BF16)


def _compress(zm, cmp_w, cmp_pe, B, T):
    G, d = NSA_KV_GROUPS, HEAD_DIM
    n16 = T // CMP_STRIDE
    out = jax.ShapeDtypeStruct((B, G, n16, d), BF16)
    kblk = KV_BASE
    vblk = KV_BASE + G
    return pl.pallas_call(
        _compress_kernel, grid=(B, G),
        in_specs=[pl.BlockSpec((T, d), lambda b, g: (b, kblk + g)), pl.BlockSpec((T, d), lambda b, g: (b, vblk + g)),
                  pl.BlockSpec((2, CMP_BLOCK, d, d), lambda b, g: (0, 0, 0, 0)),
                  pl.BlockSpec((2, CMP_BLOCK, d), lambda b, g: (0, 0, 0))],
        out_specs=[pl.BlockSpec((None, None, n16, d), lambda b, g: (b, g, 0, 0)),
                   pl.BlockSpec((None, None, d, n16), lambda b, g: (b, g, 0, 0))],
        out_shape=[out, jax.ShapeDtypeStruct((B, G, d, n16), BF16)],
        scratch_shapes=[pltpu.VMEM((T + CMP_STRIDE, d), F32)],
        compiler_params=_cparams(("parallel", "parallel")), name="nsa_compress",
    )(zm, zm, cmp_w, cmp_pe)


RANK_CHUNK = 8


def _lanes_x(x, times):
    return jnp.concatenate([x] * times, axis=1)


def _nsa_a_kernel(q_ref, kc_ref, vct_ref, *rest, n_back):
    nw = n_back + 1
    kw_refs, vwt_refs = rest[:nw], rest[nw:2 * nw]
    cgt_ref, ovt_ref, acmp_ref, awin_ref, part_ref, sel_ref, q_s, key_s, rank_s = rest[2 * nw:]
    tq, d, J = Q_BLOCK, HEAD_DIM, NSA_GROUP_SIZE
    qi = pl.program_id(2)
    t0 = qi * tq
    ncmp = kc_ref.shape[0]
    scale = d ** -0.5
    for j in range(J):
        q_s[j * tq:(j + 1) * tq, :] = (q_ref[:, j * d:(j + 1) * d].astype(F32) * scale).astype(BF16)
    sg_t = jax.nn.sigmoid(cgt_ref[...])
    t_row = t0 + lax.broadcasted_iota(jnp.int32, (1, tq), 1)

    n_i = lax.broadcasted_iota(jnp.int32, (ncmp, tq), 0)
    r_i = lax.broadcasted_iota(jnp.int32, (ncmp, tq), 1)
    bias_c = jnp.where(t0 + r_i - CMP_STRIDE * n_i - (CMP_BLOCK - 1) >= 0, 0.0, NEG_INF)
    s = _dot_nt(kc_ref[...], q_s[...]) - acmp_ref[...] + _lanes_x(bias_c, J)
    p = jnp.exp(s - jnp.max(s, axis=0, keepdims=True))
    has_c = jnp.where(t_row >= CMP_BLOCK - 1, 1.0, 0.0)
    p = p * (_lanes_x(has_c, J) / jnp.sum(p, axis=0, keepdims=True))
    o_cmp_t = _dot(vct_ref[...], p.astype(BF16))
    psum = p[:, 0:tq]
    for j in range(1, J):
        psum = psum + p[:, j * tq:(j + 1) * tq]

    hi, mid, lo = _split3(psum)
    ovt = ovt_ref[...]
    imp_t = _dot(ovt, hi) + _dot(ovt, mid) + _dot(ovt, lo)
    blk_t = lax.broadcasted_iota(jnp.int32, (LANES, tq), 0)
    cur_t = lax.shift_right_logical(t0 + lax.broadcasted_iota(jnp.int32, (LANES, tq), 1), SEL_BLOCK.bit_length() - 1)
    valid_t = blk_t <= cur_t
    forced = (blk_t == 0) | (blk_t == cur_t) | (blk_t == cur_t - 1)
    key_s[...] = jnp.where(valid_t, imp_t + jnp.where(forced, FORCE_SCORE, 0.0), NEG_INF)
    rank_s[...] = jnp.zeros_like(rank_s)
    sub = lax.broadcasted_iota(jnp.int32, (8, tq), 0)
    groups = LANES // 8
    last_valid = lax.shift_right_logical(t0 + tq - 1, SEL_BLOCK.bit_length() - 1)
    for chunk in range(LANES // RANK_CHUNK):
        @pl.when(chunk * RANK_CHUNK <= last_valid)
        def _(chunk=chunk):
            key_g = [key_s[8 * i:8 * i + 8, :] for i in range(groups)]
            rank = [rank_s[8 * i:8 * i + 8, :] for i in range(groups)]
            for mp in range(chunk * RANK_CHUNK, (chunk + 1) * RANK_CHUNK):
                rowv = jnp.broadcast_to(key_s[mp:mp + 1, :], (8, tq))
                for i in range(groups):
                    if 8 * i > mp:
                        ahead = rowv >= key_g[i]
                    elif 8 * i + 7 <= mp:
                        ahead = rowv > key_g[i]
                    else:
                        ahead = (rowv > key_g[i]) | ((sub > mp - 8 * i) & (rowv == key_g[i]))
                    rank[i] = rank[i] + jnp.where(ahead, 1.0, 0.0)
            for i in range(groups):
                rank_s[8 * i:8 * i + 8, :] = rank[i]
    sel_ref[...] = jnp.where((rank_s[...] < float(SEL_TOPN)) & valid_t, 1.0, 0.0).astype(BF16)

    span = nw * tq
    kw = jnp.concatenate([kr[...] for kr in kw_refs], axis=0)
    vw_t = jnp.concatenate([vr[...] for vr in vwt_refs], axis=1)
    c_i = lax.broadcasted_iota(jnp.int32, (span, tq), 0)
    bias_w = jnp.where(t0 - n_back * tq + c_i >= 0, 0.0, NEG_INF)
    s = _dot_nt(kw, q_s[...]) - awin_ref[...] + _lanes_x(bias_w, J)
    p = jnp.exp(s - jnp.max(s, axis=0, keepdims=True))
    inv_l = 1.0 / jnp.sum(p, axis=0, keepdims=True)
    o_win_t = _dot(vw_t, p.astype(BF16))
    g_cmp = jnp.concatenate([sg_t[j:j + 1, :] for j in range(J)], axis=1)
    g_win = jnp.concatenate([sg_t[2 * J + j:2 * J + j + 1, :] for j in range(J)], axis=1)
    part_ref[...] = g_cmp * o_cmp_t + (g_win * inv_l) * o_win_t


def _alibi_slopes():
    return (2.0 ** (-8.0 * (np.arange(N_HEADS, dtype=np.float32) + 1.0) / N_HEADS)).astype(np.float32)


def _nsa_consts(T):
    G, J, tq = NSA_KV_GROUPS, NSA_GROUP_SIZE, Q_BLOCK
    n16 = T // CMP_STRIDE
    n_sel = T // SEL_BLOCK
    n = np.arange(n16)[None, :]
    m = np.arange(LANES)[:, None]
    c_start, s_start = n * CMP_STRIDE, m * SEL_BLOCK
    overlap_t = ((c_start < s_start + SEL_BLOCK) & (c_start + CMP_BLOCK > s_start) & (n < n16 - 1) & (m < n_sel))
    slopes = _alibi_slopes().reshape(G, 1, J, 1)
    r = np.arange(tq, dtype=np.float32)[None, None, None, :]
    dist_c = r - CMP_STRIDE * np.arange(n16, dtype=np.float32)[None, :, None, None] - (CMP_BLOCK - 1)
    a_cmp = (slopes * dist_c).reshape(G, n16, J * tq)
    span = WINDOW + tq
    dist_w = r + WINDOW - np.arange(span, dtype=np.float32)[None, :, None, None]
    a_win = np.where((dist_w >= 0) & (dist_w < WINDOW), slopes * dist_w, -NEG_INF).reshape(G, span, J * tq)
    return (jnp.asarray(overlap_t.astype(np.float32), BF16), jnp.asarray(a_cmp.astype(np.float32)),
            jnp.asarray(a_win.astype(np.float32)))


def _nsa_a(zm, k_cmp, v_cmp_t, vw_t, cg_t, B, T):
    G, J, d, tq = NSA_KV_GROUPS, NSA_GROUP_SIZE, HEAD_DIM, Q_BLOCK
    nq = T // tq
    n16 = T // CMP_STRIDE
    n_back = WINDOW // tq
    span = WINDOW + tq
    qblk = BLK_CQ * (MIX_WIDTH // (J * d))
    kwblk = KV_BASE + 4 * G
    overlap_t, a_cmp, a_win = _nsa_consts(T)

    def past(qi, i):
        return jnp.maximum(qi - n_back + i, 0)

    in_specs = ([pl.BlockSpec((tq, J * d), lambda b, g, qi: (b * nq + qi, qblk + g)),
                 pl.BlockSpec((None, None, n16, d), lambda b, g, qi: (b, g, 0, 0)),
                 pl.BlockSpec((None, None, d, n16), lambda b, g, qi: (b, g, 0, 0))]
                + [pl.BlockSpec((tq, d), functools.partial(lambda b, g, qi, i: (b * nq + past(qi, i), kwblk + g), i=i))
                   for i in range(n_back + 1)]
                + [pl.BlockSpec((None, None, d, tq), functools.partial(lambda b, g, qi, i: (b, g, 0, past(qi, i)), i=i))
                   for i in range(n_back + 1)]
                + [pl.BlockSpec((None, 16, tq), lambda b, g, qi: (g, 0, b * nq + qi)),
                   pl.BlockSpec((LANES, n16), lambda b, g, qi: (0, 0)),
                   pl.BlockSpec((None, n16, J * tq), lambda b, g, qi: (g, 0, 0)),
                   pl.BlockSpec((None, span, J * tq), lambda b, g, qi: (g, 0, 0))])
    return pl.pallas_call(
        functools.partial(_nsa_a_kernel, n_back=n_back), grid=(B, G, nq), in_specs=in_specs,
        out_specs=[pl.BlockSpec((None, None, None, d, J * tq), lambda b, g, qi: (b, g, qi, 0, 0)),
                   pl.BlockSpec((None, None, LANES, tq), lambda b, g, qi: (b, g, 0, qi))],
        out_shape=[jax.ShapeDtypeStruct((B, G, nq, d, J * tq), F32), jax.ShapeDtypeStruct((B, G, LANES, T), BF16)],
        scratch_shapes=[pltpu.VMEM((J * tq, d), BF16), pltpu.VMEM((LANES, tq), F32), pltpu.VMEM((LANES, tq), F32)],
        compiler_params=_cparams(("parallel", "parallel", "parallel")), name="nsa_cmp_win",
    )(zm, k_cmp, v_cmp_t, *([zm] * (n_back + 1)), *([vw_t] * (n_back + 1)), cg_t, overlap_t, a_cmp, a_win)


def _nsa_b_kernel(cnt_ref, lst_ref, q_ref, ks_ref, vt_ref, sel_ref, e_ref, a_ref, srow_ref, cgt_ref, part_ref, o_ref,
                  q_s, acc_s, *, tk, nk):
    tq, d, J = Q_BLOCK, HEAD_DIM, NSA_GROUP_SIZE
    b, g, qi = pl.program_id(0), pl.program_id(1), pl.program_id(2)
    lin = (b * pl.num_programs(1) + g) * pl.num_programs(2) + qi
    t0 = qi * tq
    scale = d ** -0.5
    for j in range(J):
        q_s[j * tq:(j + 1) * tq, :] = (q_ref[:, j * d:(j + 1) * d].astype(F32) * scale).astype(BF16)
    acc_s[...] = jnp.zeros_like(acc_s)
    r_minus_c = lax.broadcasted_iota(jnp.int32, (tk, tq), 1) - lax.broadcasted_iota(jnp.int32, (tk, tq), 0)

    count = cnt_ref[lin]

    def scores(pos):
        kt = lst_ref[lin * nk + pos]
        k_t = ks_ref[pl.ds(pl.multiple_of(kt * tk, tk), tk), :]
        off = t0 - kt * tk
        picked = _dot(e_ref[kt], sel_ref[...])
        causal_off = jnp.where(pos < count, off, -tk * nk)
        keep = (picked > 0.5) & (r_minus_c + causal_off >= 0)
        s = _dot_nt(k_t, q_s[...]) - a_ref[...] + _lanes_x(jnp.where(keep, 0.0, NEG_INF), J)
        return s, srow_ref[...] * off.astype(F32), kt

    def body(i, carry):
        m_prev, l_prev = carry
        s_a, shift_a, kt_a = scores(2 * i)
        s_b, shift_b, kt_b = scores(2 * i + 1)
        m_new = jnp.maximum(m_prev, jnp.maximum(jnp.max(s_a, axis=0, keepdims=True) - shift_a,
                                                jnp.max(s_b, axis=0, keepdims=True) - shift_b))
        p_a = jnp.exp(s_a - (m_new + shift_a))
        p_b = jnp.exp(s_b - (m_new + shift_b))
        alpha = jnp.exp(m_prev - m_new)
        acc_s[...] = alpha * acc_s[...] + (_dot(vt_ref[kt_a], p_a.astype(BF16)) + _dot(vt_ref[kt_b], p_b.astype(BF16)))
        return m_new, alpha * l_prev + (jnp.sum(p_a, axis=0, keepdims=True) + jnp.sum(p_b, axis=0, keepdims=True))

    init = (jnp.full((1, J * tq), NEG_INF, F32), jnp.zeros((1, J * tq), F32))
    _, l = lax.fori_loop(0, (count + 1) // 2, body, init)
    sg_t = jax.nn.sigmoid(cgt_ref[...])
    g_sel = jnp.concatenate([sg_t[J + j:J + j + 1, :] for j in range(J)], axis=1)
    total_t = part_ref[...] + (g_sel / l) * acc_s[...]
    for j in range(J):
        o_ref[:, j * d:(j + 1) * d] = total_t[:, j * tq:(j + 1) * tq].T.astype(BF16)


def _nsa_b(zm, sel_t, cg_t, part_t, B, T, tk=512):
    G, J, d, tq = NSA_KV_GROUPS, NSA_GROUP_SIZE, HEAD_DIM, Q_BLOCK
    tk = min(tk, T)
    nq, nk = T // tq, T // tk
    qblk = BLK_CQ * (MIX_WIDTH // (J * d))
    ksblk = KV_BASE + 2 * G
    vs_col = (KV_BASE + 3 * G) * d
    per_tile = tk // SEL_BLOCK
    kt = np.arange(nk)[:, None, None]
    cc = np.arange(tk)[None, :, None]
    m = np.arange(LANES)[None, None, :]
    expand_t = jnp.asarray((m == kt * per_tile + cc // SEL_BLOCK).astype(np.float32), BF16)
    slopes_np = _alibi_slopes()
    r_minus_c = (np.arange(tq)[None, :] - np.arange(tk)[:, None]).astype(np.float32)
    alibi_t = jnp.asarray((slopes_np.reshape(G, 1, J, 1) * r_minus_c[None, :, None, :]).reshape(G, tk, J * tq))
    srow = jnp.asarray(np.repeat(slopes_np.reshape(G, J), tq, axis=1).reshape(G, 1, J * tq))
    vs_t = zm[:, vs_col:vs_col + G * d].reshape(B, nk, tk, G, d).transpose(0, 3, 1, 4, 2)

    active = sel_t[:, :, :nk * per_tile].reshape(B, G, nk, per_tile, nq, tq).max(axis=(3, 5)) > 0
    active = active.transpose(0, 1, 3, 2)
    tiles = jnp.arange(nk, dtype=jnp.int32)
    order = jnp.sort(jnp.where(active, tiles, tiles + nk), axis=-1) % nk
    counts = active.sum(axis=-1).astype(jnp.int32)

    grid_spec = pltpu.PrefetchScalarGridSpec(
        num_scalar_prefetch=2, grid=(B, G, nq),
        in_specs=[pl.BlockSpec((tq, J * d), lambda b, g, qi, c, o: (b * nq + qi, qblk + g)),
                  pl.BlockSpec((T, d), lambda b, g, qi, c, o: (b, ksblk + g)),
                  pl.BlockSpec((None, None, nk, d, tk), lambda b, g, qi, c, o: (b, g, 0, 0, 0)),
                  pl.BlockSpec((None, None, LANES, tq), lambda b, g, qi, c, o: (b, g, 0, qi)),
                  pl.BlockSpec((nk, tk, LANES), lambda b, g, qi, c, o: (0, 0, 0)),
                  pl.BlockSpec((None, tk, J * tq), lambda b, g, qi, c, o: (g, 0, 0)),
                  pl.BlockSpec((None, 1, J * tq), lambda b, g, qi, c, o: (g, 0, 0)),
                  pl.BlockSpec((None, 16, tq), lambda b, g, qi, c, o: (g, 0, b * nq + qi)),
                  pl.BlockSpec((None, None, None, d, J * tq), lambda b, g, qi, c, o: (b, g, qi, 0, 0))],
        out_specs=pl.BlockSpec((tq, J * d), lambda b, g, qi, c, o: (b * nq + qi, g)),
        scratch_shapes=[pltpu.VMEM((J * tq, d), BF16), pltpu.VMEM((d, J * tq), F32)])
    return pl.pallas_call(
        functools.partial(_nsa_b_kernel, tk=tk, nk=nk), grid_spec=grid_spec,
        out_shape=jax.ShapeDtypeStruct((B * T, MIX_WIDTH), BF16),
        compiler_params=_cparams(("parallel", "parallel", "parallel")), name="nsa_selected",
    )(counts.reshape(-1), order.reshape(-1).astype(jnp.int32), zm, zm, vs_t, sel_t, expand_t, alibi_t, srow, cg_t, part_t)


def _nsa(zm, zs, cmp_w, cmp_pe, B, T):
    G, J = NSA_KV_GROUPS, NSA_GROUP_SIZE
    n = B * T
    d = HEAD_DIM
    cg = zs[:, 2 * N_HEADS:2 * N_HEADS + 3 * N_HEADS].reshape(n, 3, G, J)
    cg_t = jnp.pad(cg.transpose(2, 1, 3, 0).reshape(G, 3 * J, n), ((0, 0), (0, 16 - 3 * J), (0, 0)))
    vw_col = (KV_BASE + 5 * G) * d
    vw_t = zm[:, vw_col:vw_col + G * d].reshape(B, T, G, d).transpose(0, 2, 3, 1)
    k_cmp, v_cmp_t = _compress(zm, cmp_w, cmp_pe, B, T)
    part_t, sel_t = _nsa_a(zm, k_cmp, v_cmp_t, vw_t, cg_t, B, T)
    return _nsa_b(zm, sel_t, cg_t, part_t, B, T)


def _merge_kernel(h_ref, y0, y1, y2, y3, g0, g1, g2, g3, b0, b1, b2, b3, o_ref):
    h = h_ref[...]
    acc = None
    for y, gw, bw in ((y0, g0, b0), (y1, g1, b1), (y2, g2, b2), (y3, g3, b3)):
        term = jax.nn.sigmoid(_dot(h, gw[...])) * _dot(y[...], bw[...])
        acc = term if acc is None else acc + term
    o_ref[...] = acc.astype(BF16)


def _merge(h, ys, w_gate, w_branch, tm=1024, tn=256):
    n, dm = h.shape
    tm = min(tm, n)
    nj = dm // tn
    once = pl.Buffered(1)
    y_spec = pl.BlockSpec((tm, MIX_WIDTH), lambda i, j: (i, 0), pipeline_mode=once)
    gate_specs = [pl.BlockSpec((dm, tn), functools.partial(lambda i, j, b: (0, b * nj + j), b=b)) for b in range(N_BRANCH)]
    br_specs = [pl.BlockSpec((None, MIX_WIDTH, tn), functools.partial(lambda i, j, b: (b, 0, j), b=b)) for b in range(N_BRANCH)]
    return pl.pallas_call(
        _merge_kernel, grid=(n // tm, nj),
        in_specs=([pl.BlockSpec((tm, dm), lambda i, j: (i, 0), pipeline_mode=once)] + [y_spec] * N_BRANCH
                  + gate_specs + br_specs),
        out_specs=pl.BlockSpec((tm, tn), lambda i, j: (i, j)),
        out_shape=jax.ShapeDtypeStruct((n, dm), BF16),
        compiler_params=_cparams(("parallel", "parallel")), name="gated_merge",
    )(h, *ys, *([w_gate] * N_BRANCH), *([w_branch] * N_BRANCH))


def _pack_bf16_pairs(y):
    half = y.shape[1] // 2
    bits = lax.bitcast_convert_type(y.astype(BF16).astype(F32), jnp.int32)
    return bits[:, :half] | lax.shift_right_logical(bits[:, half:], 16)


def _unpack_bf16_pairs(w):
    first = lax.bitcast_convert_type(w & jnp.int32(-65536), F32).astype(BF16)
    second = lax.bitcast_convert_type(lax.shift_left(w, 16), F32).astype(BF16)
    return first, second


def _expert_kernel(be_ref, nact_ref, idx_ref, idx_next_ref, hp_hbm, w1_ref, w3_ref, w2_ref, o_ref,
                   xbuf, sems, w1b, w3b, w2b):
    i = pl.program_id(0)
    n_act = nact_ref[0]
    slot = lax.rem(i, 2)
    e = be_ref[i]
    prev = be_ref[jnp.maximum(i - 1, 0)]
    half = hp_hbm.shape[1]

    @pl.when((i == 0) | (prev != e))
    def _():
        w1b[...] = w1_ref[...].astype(BF16)
        w3b[...] = w3_ref[...].astype(BF16)
        w2b[...] = w2_ref[...].astype(BF16)

    @pl.when((i == 0) & (n_act > 0))
    def _():
        _row_gather(idx_ref, hp_hbm, xbuf.at[0], sems.at[0], MOE_BLOCK)

    @pl.when(i + 1 < n_act)
    def _():
        _row_gather(idx_next_ref, hp_hbm, xbuf.at[1 - slot], sems.at[1 - slot], MOE_BLOCK)

    @pl.when(i < n_act)
    def _():
        _row_gather_wait(hp_hbm, xbuf.at[slot], sems.at[slot], MOE_BLOCK)
        x_a, x_b = _unpack_bf16_pairs(xbuf[slot])
        a1 = _dot(x_a, w1b[0:half, :]) + _dot(x_b, w1b[half:, :])
        a3 = _dot(x_a, w3b[0:half, :]) + _dot(x_b, w3b[half:, :])
        a = (a1 * jax.nn.sigmoid(a1)) * a3
        o_ref[...] = _dot(a.astype(BF16), w2b[...])

    @pl.when(i >= n_act)
    def _():
        o_ref[...] = jnp.zeros_like(o_ref)


def _experts(hp, tok_pad, blk_exp, n_active, w1, w3, w2, layer):
    half = hp.shape[1]
    dm = 2 * half
    n_blk = tok_pad.shape[0] // MOE_BLOCK
    ff = w1.shape[-1]
    idx = tok_pad.reshape(n_blk, 1, MOE_BLOCK)
    smem_blk = lambda f: pl.BlockSpec((None, 1, MOE_BLOCK), f, memory_space=pltpu.SMEM)
    grid_spec = pltpu.PrefetchScalarGridSpec(
        num_scalar_prefetch=2, grid=(n_blk,),
        in_specs=[smem_blk(lambda i, be, na: (i, 0, 0)),
                  smem_blk(lambda i, be, na: (jnp.minimum(i + 1, n_blk - 1), 0, 0)),
                  pl.BlockSpec(memory_space=pl.ANY),
                  pl.BlockSpec((None, None, dm, ff), lambda i, be, na: (layer, be[i], 0, 0)),
                  pl.BlockSpec((None, None, dm, ff), lambda i, be, na: (layer, be[i], 0, 0)),
                  pl.BlockSpec((None, None, ff, dm), lambda i, be, na: (layer, be[i], 0, 0))],
        out_specs=pl.BlockSpec((MOE_BLOCK, dm), lambda i, be, na: (i, 0)),
        scratch_shapes=[pltpu.VMEM((2, MOE_BLOCK, half), jnp.int32), pltpu.SemaphoreType.DMA((2,)),
                        pltpu.VMEM((dm, ff), BF16), pltpu.VMEM((dm, ff), BF16), pltpu.VMEM((ff, dm), BF16)])
    return pl.pallas_call(
        _expert_kernel, grid_spec=grid_spec,
        out_shape=jax.ShapeDtypeStruct((n_blk * MOE_BLOCK, dm), F32),
        compiler_params=_cparams(("arbitrary",), vmem=MOE_VMEM_LIMIT), name="moe_experts",
    )(blk_exp, n_active, idx, idx, hp, w1, w3, w2)


def _dispatch_plan(e_idx):
    n, K = e_idx.shape
    nk = n * K
    i32 = jnp.int32
    flat_e = e_idx.reshape(-1)
    order = jnp.argsort(flat_e).astype(i32)
    rank = jnp.argsort(order).astype(i32)
    experts = jnp.arange(N_EXPERTS, dtype=i32)
    hot = flat_e[:, None] == experts[None, :]
    counts = jnp.sum(hot.astype(i32), axis=0)
    padded = (counts + MOE_BLOCK - 1) // MOE_BLOCK * MOE_BLOCK
    pad_end = jnp.cumsum(padded)
    pad_start = pad_end - padded
    start = jnp.cumsum(counts) - counts
    lookup = lambda table, onehot: jnp.sum(jnp.where(onehot, table[None, :], 0), axis=1)
    dest = lookup(pad_start - start, hot) + rank
    n_blk = (nk + MOE_BLOCK - 1) // MOE_BLOCK + N_EXPERTS
    blk_first = jnp.arange(n_blk, dtype=i32) * MOE_BLOCK
    blk_exp = jnp.minimum(jnp.sum((pad_end[None, :] <= blk_first[:, None]).astype(i32), axis=1), N_EXPERTS - 1)
    blk_hot = blk_exp[:, None] == experts[None, :]
    within = jnp.arange(MOE_BLOCK, dtype=i32)[None, :]
    j = (blk_first - lookup(pad_start, blk_hot))[:, None] + within
    sorted_pos = jnp.clip(lookup(start, blk_hot)[:, None] + j, 0, nk - 1)
    src = jnp.take(order, sorted_pos.reshape(-1), mode="clip") // K
    tok_pad = jnp.where((j < lookup(counts, blk_hot)[:, None]).reshape(-1), src, n - 1)
    n_active = (pad_end[-1] // MOE_BLOCK).astype(i32).reshape(1)
    return tok_pad, blk_exp.astype(i32), n_active, dest.reshape(n, K)


def _moe(h, route, w1, w3, w2, layer):
    e_idx = route[:, :TOP_K_IN_GROUP].astype(jnp.int32)
    tok_pad, blk_exp, n_active, dest = _dispatch_plan(e_idx)
    y = _experts(h, tok_pad, blk_exp, n_active, w1, w3, w2, layer)
    return y, dest


PACK_TILE = 512


def _pack_kernel(offs_ref, wt_ref, o_ref):
    del offs_ref
    o_ref[...] = wt_ref[0].T.astype(BF16)


def _pack_columns(w_in_t, layer, col_offsets):
    _, _, k = w_in_t.shape
    nt = len(col_offsets)
    grid_spec = pltpu.PrefetchScalarGridSpec(
        num_scalar_prefetch=1, grid=(nt,),
        in_specs=[pl.BlockSpec((pl.Element(1), pl.Element(PACK_TILE), pl.Element(k)),
                               lambda j, offs: (layer, pl.multiple_of(offs[j], 8), 0))],
        out_specs=pl.BlockSpec((k, PACK_TILE), lambda j, offs: (0, j)))
    return pl.pallas_call(
        _pack_kernel, grid_spec=grid_spec,
        out_shape=jax.ShapeDtypeStruct((k, nt * PACK_TILE), BF16),
        compiler_params=_cparams(("parallel",)), name="pack_w_in",
    )(jnp.asarray(col_offsets, jnp.int32), w_in_t)


def _tile_offsets(groups):
    offs = []
    for i in groups:
        assert COL_WIDTHS[i] % PACK_TILE == 0 or COL_WIDTHS[i] < PACK_TILE
        offs += list(range(COL_OFFSETS[i], COL_OFFSETS[i + 1], PACK_TILE))
    return offs


def _pack_w_in(w_in_t, layer):
    wide = [i for i in MAIN_ORDER if COL_WIDTHS[i] >= PACK_TILE]
    narrow = [i for i in MAIN_ORDER if COL_WIDTHS[i] < PACK_TILE]
    assert MAIN_ORDER == tuple(wide + narrow) and narrow == list(range(narrow[0], narrow[-1] + 1))
    kv_lo, kv_hi = COL_OFFSETS[narrow[0]], COL_OFFSETS[narrow[-1] + 1]
    assert (kv_hi - kv_lo) % PACK_TILE == 0
    main_offs = _tile_offsets(wide) + list(range(kv_lo, kv_hi, PACK_TILE))
    w_main = _pack_columns(w_in_t, layer, main_offs)
    w_gate = _pack_columns(w_in_t, layer, _tile_offsets([I_MERGE]))
    rows = lambda i: w_in_t[layer, COL_OFFSETS[i]:COL_OFFSETS[i + 1]]
    small_t = jnp.concatenate([rows(I_AI), rows(I_AF), rows(I_CG)], axis=0)
    small_t = jnp.pad(small_t, ((0, LANES - small_t.shape[0]), (0, 0)))
    return w_main, small_t, w_gate


def _router_weights(wg, bg, we, be):
    w = jnp.concatenate([wg, we], axis=1)
    w = jnp.pad(w, ((0, 0), (0, LANES - w.shape[1])))
    hi = w.astype(BF16)
    lo = (w - hi.astype(F32)).astype(BF16)
    b = jnp.concatenate([bg, be])
    b = jnp.pad(b, (0, LANES - b.shape[0])).reshape(1, LANES).astype(F32)
    return hi, lo, b


def kernel(x, w_in, mlstm_conv, mlstm_gate_bias, mlstm_norm, ret_norm, nsa_cmp_w, nsa_cmp_pe, sgu_norm, sgu_w, sgu_b, w_branch, w_out, norm_mix, norm_ffn, router_group_w, router_group_b, router_expert_w, router_expert_b, expert_w1, expert_w3, expert_w2, norm_final):
    B, T, D = x.shape
    n = B * T
    xs, moe = x.reshape(n, D), None
    w_in_t = jnp.swapaxes(w_in, 1, 2)
    for l in range(DEPTH):
        w_main, w_small_t, w_gate = _pack_w_in(w_in_t, l)
        if moe is None:
            h, = _norm(xs, norm_mix[l])
        else:
            xs, h = _norm(xs, norm_mix[l], moe=moe)
        zm = _matmul(h, w_main, out_dtype=BF16, name="proj_main")
        zs = _matmul(h, w_small_t, out_dtype=F32, w_rows_are_outputs=True, name="proj_small")
        zst = zs[:, :2 * N_HEADS].T
        ya = _mlstm(zm, zs, zst, mlstm_conv[l], mlstm_gate_bias[l], mlstm_norm[l], B, T)
        yb = _retention(zm, ret_norm[l], B, T)
        yc = _nsa(zm, zs, nsa_cmp_w[l], nsa_cmp_pe[l], B, T)
        yd = _sgu(zm, sgu_norm[l], sgu_w[l], sgu_b[l], n)
        merged = _merge(h, (ya, yb, yc, yd), w_gate, w_branch[l].astype(BF16))
        x_mid = _matmul(merged, w_out[l].astype(BF16), out_dtype=F32, residual=xs, name="proj_out")
        h2, route = _norm(x_mid, norm_ffn[l], rows=256,
                          router_w=_router_weights(router_group_w[l], router_group_b[l],
                                                   router_expert_w[l], router_expert_b[l]))
        y_exp, dest = _moe(h2, route, expert_w1, expert_w3, expert_w2, l)
        xs, moe = x_mid, (y_exp, dest, route)
    out, = _norm(xs, norm_final, moe=moe, final=True)
    return out.reshape(B, T, D)
```
